```python
import jax, jax.numpy as jnp
from jax import lax
import numpy as np

D_MODEL = 2048
BATCH = 8
SEQ = 4096
DEPTH = 4

N_MEM = 256
N_MIXERS = 3
HEAD_DIM = 128
BLOCK = 128
NORM_EPS = 1e-6
ROPE_THETA = 500000.0
ROPE_FRACTION = 4
SB_HEADS = D_MODEL // HEAD_DIM
DIL_PATTERNS = ((128, 1), (512, 4), (2048, 16))
DIL_HEADS_PER_GROUP = D_MODEL // (4 * HEAD_DIM)
DIL_HEADS = DIL_HEADS_PER_GROUP * len(DIL_PATTERNS)
SWA_HEAD_DIM = 64
SWA_Q_HEADS = D_MODEL // SWA_HEAD_DIM
SWA_KV_HEADS = SWA_Q_HEADS // 8
SWA_GROUP = SWA_Q_HEADS // SWA_KV_HEADS
SWA_WINDOW = 128
XA_HEADS = 4
XA_HEAD_DIM = 128
D_FF = 5632
N_LAYERS_A = (DEPTH + 2) // 3
N_LAYERS_B = (DEPTH + 1) // 3
N_LAYERS_C = DEPTH // 3

kernel_name = "hybrid_sb_dilated_swa_macaron"

F32 = jnp.float32


def rmsnorm(x, g):
    xf = x.astype(F32)
    y = xf * lax.rsqrt(jnp.mean(xf * xf, axis=-1, keepdims=True) + NORM_EPS)
    return (y * g.astype(F32)).astype(x.dtype)


def swiglu(x, w_gate_up, w_down):
    gate, up = jnp.split(x @ w_gate_up, 2, axis=-1)
    return (jax.nn.silu(gate) * up) @ w_down


def partial_rotary(x, positions):
    d = x.shape[-1]
    rot = d // ROPE_FRACTION
    half = rot // 2
    inv_freq = jnp.power(F32(ROPE_THETA), -jnp.arange(half, dtype=F32) * 2.0 / rot)
    ang = positions.astype(F32)[..., None] * inv_freq
    cos = jnp.cos(ang)[:, :, None, :]
    sin = jnp.sin(ang)[:, :, None, :]
    xf = x.astype(F32)
    x1, x2, rest = xf[..., :half], xf[..., half:rot], xf[..., rot:]
    out = jnp.concatenate([x1 * cos - x2 * sin, x2 * cos + x1 * sin, rest], axis=-1)
    return out.astype(x.dtype)


def banded_window_attention(q, k, v, max_dist, sinks=None):
    n, L, hk, g, d = q.shape
    nb = L // BLOCK
    qb = q.reshape(n, nb, BLOCK, hk, g, d)

    def with_prev(t):
        t = t.reshape(n, nb, BLOCK, hk, d)
        prev = jnp.pad(t, ((0, 0), (1, 0), (0, 0), (0, 0), (0, 0)))[:, :-1]
        return jnp.concatenate([prev, t], axis=2)

    kk, vv = with_prev(k), with_prev(v)
    scores = jnp.einsum('bnqhgd,bnkhd->bnhgqk', qb, kk, preferred_element_type=F32) * (d ** -0.5)
    dist = (BLOCK + jnp.arange(BLOCK))[:, None] - jnp.arange(2 * BLOCK)[None, :]
    in_band = (dist >= 0) & (dist <= max_dist)
    has_prev = (jnp.arange(nb)[:, None] > 0) | (jnp.arange(2 * BLOCK)[None, :] >= BLOCK)
    mask = in_band[None] & has_prev[:, None, :]
    scores = jnp.where(mask[None, :, None, None], scores, -jnp.inf)
    lse = jax.nn.logsumexp(scores, axis=-1)
    if sinks is not None:
        lse = jnp.logaddexp(lse, sinks.astype(F32)[None, None, :, :, None])
    p = jnp.exp(scores - lse[..., None])
    o = jnp.einsum('bnhgqk,bnkhd->bnqhgd', p, vv.astype(F32)).astype(q.dtype)
    return o.reshape(n, L, hk, g, d), lse.transpose(0, 1, 4, 2, 3).reshape(n, L, hk, g)


def stick_breaking_attention(q, k, v):
    b, s, h, d = q.shape
    nb = s // BLOCK
    qb = q.reshape(b, nb, BLOCK, h, d).transpose(1, 0, 2, 3, 4)
    kpos = jnp.arange(s)
    vf = v.astype(F32)

    def block_fn(args):
        q_blk, blk = args
        z = jnp.einsum('bqhd,bkhd->bhqk', q_blk, k, preferred_element_type=F32) * (d ** -0.5)
        qpos = blk * BLOCK + jnp.arange(BLOCK)
        causal = kpos[None, :] < qpos[:, None]
        log_keep = jnp.where(causal, -jax.nn.softplus(z), 0.0)
        tail = lax.cumsum(log_keep, axis=3, reverse=True) - log_keep
        a = jnp.where(causal, jnp.exp(jax.nn.log_sigmoid(z) + tail), 0.0)
        return jnp.einsum('bhqk,bkhd->bqhd', a, vf).astype(q.dtype)

    out = lax.map(block_fn, (qb, jnp.arange(nb, dtype=jnp.int32)))
    return out.transpose(1, 0, 2, 3, 4).reshape(b, s, h, d)


def stick_breaking_mixer(h, w_qkv, w_o):
    b, s, _ = h.shape
    qkv = (h @ w_qkv).reshape(b, s, 3, SB_HEADS, HEAD_DIM)
    o = stick_breaking_attention(qkv[:, :, 0], qkv[:, :, 1], qkv[:, :, 2])
    return o.reshape(b, s, SB_HEADS * HEAD_DIM) @ w_o


def dilated_group_attention(q, k, v, window, dil):
    b, s, h, d = q.shape
    L = s // dil
    Lp = -(-L // BLOCK) * BLOCK

    def to_classes(t):
        t = t.reshape(b, L, dil, h, d).transpose(0, 2, 1, 3, 4).reshape(b * dil, L, h, d)
        return jnp.pad(t, ((0, 0), (0, Lp - L), (0, 0), (0, 0)))

    qc, kc, vc = to_classes(q), to_classes(k), to_classes(v)
    o, lse = banded_window_attention(qc[:, :, :, None], kc, vc, window // dil)
    o = o[:, :L, :, 0].reshape(b, dil, L, h, d).transpose(0, 2, 1, 3, 4).reshape(b, s, h, d)
    lse = lse[:, :L, :, 0].reshape(b, dil, L, h).transpose(0, 2, 1, 3).reshape(b, s, h)
    return o, lse


def dilated_mixer(h, positions, w_qkv, w_o):
    b, s, _ = h.shape
    hg = DIL_HEADS_PER_GROUP
    qkv = (h @ w_qkv).reshape(b, s, 3, DIL_HEADS, HEAD_DIM)
    q = partial_rotary(qkv[:, :, 0], positions)
    k = partial_rotary(qkv[:, :, 1], positions)
    v = qkv[:, :, 2]
    outs, lses = [], []
    for g, (window, dil) in enumerate(DIL_PATTERNS):
        sl = slice(g * hg, (g + 1) * hg)
        o, lse = dilated_group_attention(q[:, :, sl], k[:, :, sl], v[:, :, sl], window, dil)
        outs.append(o)
        lses.append(lse)
    alpha = jax.nn.softmax(jnp.stack(lses, axis=0), axis=0)
    out = jnp.concatenate(
        [(o.astype(F32) * a[..., None]).astype(h.dtype) for o, a in zip(outs, alpha)], axis=2)
    return out.reshape(b, s, DIL_HEADS * HEAD_DIM) @ w_o


def swa_sink_mixer(h, positions, w_qkv, b_qkv, sinks, w_o, b_o):
    b, s, _ = h.shape
    nq = SWA_Q_HEADS * SWA_HEAD_DIM
    nk = SWA_KV_HEADS * SWA_HEAD_DIM
    qkv = h @ w_qkv + b_qkv
    q = partial_rotary(qkv[..., :nq].reshape(b, s, SWA_Q_HEADS, SWA_HEAD_DIM), positions)
    q = q.reshape(b, s, SWA_KV_HEADS, SWA_GROUP, SWA_HEAD_DIM)
    k = partial_rotary(qkv[..., nq:nq + nk].reshape(b, s, SWA_KV_HEADS, SWA_HEAD_DIM), positions)
    v = qkv[..., nq + nk:].reshape(b, s, SWA_KV_HEADS, SWA_HEAD_DIM)
    o, _ = banded_window_attention(q, k, v, SWA_WINDOW - 1, sinks.reshape(SWA_KV_HEADS, SWA_GROUP))
    return o.reshape(b, s, nq) @ w_o + b_o


def memory_cross_attention(h, mem_h, w_q, w_kv, w_o):
    b, s, _ = h.shape
    m = mem_h.shape[1]
    q = (h @ w_q).reshape(b, s, XA_HEADS, XA_HEAD_DIM)
    kv = (mem_h @ w_kv).reshape(b, m, 2, XA_HEADS, XA_HEAD_DIM)
    scores = jnp.einsum('bqhd,bkhd->bhqk', q, kv[:, :, 0], preferred_element_type=F32) * (XA_HEAD_DIM ** -0.5)
    p = jax.nn.softmax(scores, axis=-1)
    o = jnp.einsum('bhqk,bkhd->bqhd', p, kv[:, :, 1].astype(F32)).astype(h.dtype)
    return o.reshape(b, s, XA_HEADS * XA_HEAD_DIM) @ w_o


def _fwd_setup_inputs(seed: int = 0) -> dict:
    key = jax.random.key(seed)
    ks = iter(jax.random.split(key, 32))
    D, F = D_MODEL, D_FF

    def normal(shape, scale):
        return jax.random.normal(next(ks), shape, F32) * scale

    def dense(shape, fan_in):
        return normal(shape, fan_in ** -0.5)

    def gain(shape):
        return 1.0 + normal(shape, 0.02)

    sb_w = SB_HEADS * HEAD_DIM
    dil_w = DIL_HEADS * HEAD_DIM
    swa_q = SWA_Q_HEADS * SWA_HEAD_DIM
    swa_qkv = swa_q + 2 * SWA_KV_HEADS * SWA_HEAD_DIM
    xa_w = XA_HEADS * XA_HEAD_DIM
    return {
        "x": normal((BATCH, SEQ, D), 1.0),
        "mem": normal((BATCH, N_MEM, D), 1.0),
        "positions": (jax.random.randint(next(ks), (BATCH, 1), 0, 1024, dtype=jnp.int32)
                      + jnp.arange(SEQ, dtype=jnp.int32)[None, :]),
        "ffn1_norm": gain((DEPTH, D)),
        "ffn1_w_gate_up": dense((DEPTH, D, 2 * F), D),
        "ffn1_w_down": dense((DEPTH, F, D), F),
        "mix_norm": gain((DEPTH, D)),
        "sb_w_qkv": dense((N_LAYERS_A, D, 3 * sb_w), D),
        "sb_w_o": dense((N_LAYERS_A, sb_w, D), sb_w),
        "dil_w_qkv": dense((N_LAYERS_B, D, 3 * dil_w), D),
        "dil_w_o": dense((N_LAYERS_B, dil_w, D), dil_w),
        "swa_w_qkv": dense((N_LAYERS_C, D, swa_qkv), D),
        "swa_b_qkv": normal((N_LAYERS_C, swa_qkv), 0.02),
        "swa_sinks": normal((N_LAYERS_C, SWA_Q_HEADS), 1.0),
        "swa_w_o": dense((N_LAYERS_C, swa_q, D), swa_q),
        "swa_b_o": normal((N_LAYERS_C, D), 0.02),
        "xattn_norm": gain((DEPTH, D)),
        "mem_norm": gain((DEPTH, D)),
        "xattn_w_q": dense((DEPTH, D, xa_w), D),
        "xattn_w_kv": dense((DEPTH, D, 2 * xa_w), D),
        "xattn_w_o": dense((DEPTH, xa_w, D), xa_w),
        "ffn2_norm": gain((DEPTH, D)),
        "ffn2_w_gate_up": dense((DEPTH, D, 2 * F), D),
        "ffn2_w_down": dense((DEPTH, F, D), F),
        "final_norm": gain((D,)),
    }


def _fwd_reference(x, mem, positions, ffn1_norm, ffn1_w_gate_up, ffn1_w_down, mix_norm,
              sb_w_qkv, sb_w_o, dil_w_qkv, dil_w_o,
              swa_w_qkv, swa_b_qkv, swa_sinks, swa_w_o, swa_b_o,
              xattn_norm, mem_norm, xattn_w_q, xattn_w_kv, xattn_w_o,
              ffn2_norm, ffn2_w_gate_up, ffn2_w_down, final_norm):
    for i in range(DEPTH):
        x = x + 0.5 * swiglu(rmsnorm(x, ffn1_norm[i]), ffn1_w_gate_up[i], ffn1_w_down[i])
        h = rmsnorm(x, mix_norm[i])
        kind, j = i % N_MIXERS, i // N_MIXERS
        if kind == 0:
            y = stick_breaking_mixer(h, sb_w_qkv[j], sb_w_o[j])
        elif kind == 1:
            y = dilated_mixer(h, positions, dil_w_qkv[j], dil_w_o[j])
        else:
            y = swa_sink_mixer(h, positions, swa_w_qkv[j], swa_b_qkv[j], swa_sinks[j],
                               swa_w_o[j], swa_b_o[j])
        x = x + y
        x = x + memory_cross_attention(rmsnorm(x, xattn_norm[i]), rmsnorm(mem, mem_norm[i]),
                                       xattn_w_q[i], xattn_w_kv[i], xattn_w_o[i])
        x = x + 0.5 * swiglu(rmsnorm(x, ffn2_norm[i]), ffn2_w_gate_up[i], ffn2_w_down[i])
    return rmsnorm(x, final_norm)


import jax as _jax
import jax.numpy as _jnp

TWIN_FORMAT = 'train_step'
FWD_PARAMS = ['x', 'mem', 'positions', 'ffn1_norm', 'ffn1_w_gate_up', 'ffn1_w_down', 'mix_norm', 'sb_w_qkv', 'sb_w_o', 'dil_w_qkv', 'dil_w_o', 'swa_w_qkv', 'swa_b_qkv', 'swa_sinks', 'swa_w_o', 'swa_b_o', 'xattn_norm', 'mem_norm', 'xattn_w_q', 'xattn_w_kv', 'xattn_w_o', 'ffn2_norm', 'ffn2_w_gate_up', 'ffn2_w_down', 'final_norm']
TWIN_WEIGHTS = ['ffn1_norm', 'ffn1_w_gate_up', 'ffn1_w_down', 'mix_norm', 'sb_w_qkv', 'sb_w_o', 'dil_w_qkv', 'dil_w_o', 'swa_w_qkv', 'swa_b_qkv', 'swa_sinks', 'swa_w_o', 'swa_b_o', 'xattn_norm', 'mem_norm', 'xattn_w_q', 'xattn_w_kv', 'xattn_w_o', 'ffn2_norm', 'ffn2_w_gate_up', 'ffn2_w_down', 'final_norm']
TWIN_DIFF_INPUT = 'x'
TWIN_INPUTS = ['x', 'mem', 'positions', 'ffn1_norm', 'ffn1_w_gate_up', 'ffn1_w_down', 'mix_norm', 'sb_w_qkv', 'sb_w_o', 'dil_w_qkv', 'dil_w_o', 'swa_w_qkv', 'swa_b_qkv', 'swa_sinks', 'swa_w_o', 'swa_b_o', 'xattn_norm', 'mem_norm', 'xattn_w_q', 'xattn_w_kv', 'xattn_w_o', 'ffn2_norm', 'ffn2_w_gate_up', 'ffn2_w_down', 'final_norm', 'loss_target', 'm_ffn1_norm', 'm_ffn1_w_gate_up', 'm_ffn1_w_down', 'm_mix_norm', 'm_sb_w_qkv', 'm_sb_w_o', 'm_dil_w_qkv', 'm_dil_w_o', 'm_swa_w_qkv', 'm_swa_b_qkv', 'm_swa_sinks', 'm_swa_w_o', 'm_swa_b_o', 'm_xattn_norm', 'm_mem_norm', 'm_xattn_w_q', 'm_xattn_w_kv', 'm_xattn_w_o', 'm_ffn2_norm', 'm_ffn2_w_gate_up', 'm_ffn2_w_down', 'm_final_norm', 'v_ffn1_norm', 'v_ffn1_w_gate_up', 'v_ffn1_w_down', 'v_mix_norm', 'v_sb_w_qkv', 'v_sb_w_o', 'v_dil_w_qkv', 'v_dil_w_o', 'v_swa_w_qkv', 'v_swa_b_qkv', 'v_swa_sinks', 'v_swa_w_o', 'v_swa_b_o', 'v_xattn_norm', 'v_mem_norm', 'v_xattn_w_q', 'v_xattn_w_kv', 'v_xattn_w_o', 'v_ffn2_norm', 'v_ffn2_w_gate_up', 'v_ffn2_w_down', 'v_final_norm']
TWIN_OUTPUTS = ['loss', 'grad_x', 'grad_ffn1_norm', 'grad_ffn1_w_gate_up', 'grad_ffn1_w_down', 'grad_mix_norm', 'grad_sb_w_qkv', 'grad_sb_w_o', 'grad_dil_w_qkv', 'grad_dil_w_o', 'grad_swa_w_qkv', 'grad_swa_b_qkv', 'grad_swa_sinks', 'grad_swa_w_o', 'grad_swa_b_o', 'grad_xattn_norm', 'grad_mem_norm', 'grad_xattn_w_q', 'grad_xattn_w_kv', 'grad_xattn_w_o', 'grad_ffn2_norm', 'grad_ffn2_w_gate_up', 'grad_ffn2_w_down', 'grad_final_norm', 'delta_ffn1_norm', 'delta_ffn1_w_gate_up', 'delta_ffn1_w_down', 'delta_mix_norm', 'delta_sb_w_qkv', 'delta_sb_w_o', 'delta_dil_w_qkv', 'delta_dil_w_o', 'delta_swa_w_qkv', 'delta_swa_b_qkv', 'delta_swa_sinks', 'delta_swa_w_o', 'delta_swa_b_o', 'delta_xattn_norm', 'delta_mem_norm', 'delta_xattn_w_q', 'delta_xattn_w_kv', 'delta_xattn_w_o', 'delta_ffn2_norm', 'delta_ffn2_w_gate_up', 'delta_ffn2_w_down', 'delta_final_norm', 'new_m_ffn1_norm', 'new_m_ffn1_w_gate_up', 'new_m_ffn1_w_down', 'new_m_mix_norm', 'new_m_sb_w_qkv', 'new_m_sb_w_o', 'new_m_dil_w_qkv', 'new_m_dil_w_o', 'new_m_swa_w_qkv', 'new_m_swa_b_qkv', 'new_m_swa_sinks', 'new_m_swa_w_o', 'new_m_swa_b_o', 'new_m_xattn_norm', 'new_m_mem_norm', 'new_m_xattn_w_q', 'new_m_xattn_w_kv', 'new_m_xattn_w_o', 'new_m_ffn2_norm', 'new_m_ffn2_w_gate_up', 'new_m_ffn2_w_down', 'new_m_final_norm', 'new_v_ffn1_norm', 'new_v_ffn1_w_gate_up', 'new_v_ffn1_w_down', 'new_v_mix_norm', 'new_v_sb_w_qkv', 'new_v_sb_w_o', 'new_v_dil_w_qkv', 'new_v_dil_w_o', 'new_v_swa_w_qkv', 'new_v_swa_b_qkv', 'new_v_swa_sinks', 'new_v_swa_w_o', 'new_v_swa_b_o', 'new_v_xattn_norm', 'new_v_mem_norm', 'new_v_xattn_w_q', 'new_v_xattn_w_kv', 'new_v_xattn_w_o', 'new_v_ffn2_norm', 'new_v_ffn2_w_gate_up', 'new_v_ffn2_w_down', 'new_v_final_norm']
TWIN_LEAF_KINDS = {'loss': 'loss', 'grad_x': 'grad_x', 'grad_ffn1_norm': 'grad_w', 'grad_ffn1_w_gate_up': 'grad_w', 'grad_ffn1_w_down': 'grad_w', 'grad_mix_norm': 'grad_w', 'grad_sb_w_qkv': 'grad_w', 'grad_sb_w_o': 'grad_w', 'grad_dil_w_qkv': 'grad_w', 'grad_dil_w_o': 'grad_w', 'grad_swa_w_qkv': 'grad_w', 'grad_swa_b_qkv': 'grad_w', 'grad_swa_sinks': 'grad_w', 'grad_swa_w_o': 'grad_w', 'grad_swa_b_o': 'grad_w', 'grad_xattn_norm': 'grad_w', 'grad_mem_norm': 'grad_w', 'grad_xattn_w_q': 'grad_w', 'grad_xattn_w_kv': 'grad_w', 'grad_xattn_w_o': 'grad_w', 'grad_ffn2_norm': 'grad_w', 'grad_ffn2_w_gate_up': 'grad_w', 'grad_ffn2_w_down': 'grad_w', 'grad_final_norm': 'grad_w', 'delta_ffn1_norm': 'delta_w', 'delta_ffn1_w_gate_up': 'delta_w', 'delta_ffn1_w_down': 'delta_w', 'delta_mix_norm': 'delta_w', 'delta_sb_w_qkv': 'delta_w', 'delta_sb_w_o': 'delta_w', 'delta_dil_w_qkv': 'delta_w', 'delta_dil_w_o': 'delta_w', 'delta_swa_w_qkv': 'delta_w', 'delta_swa_b_qkv': 'delta_w', 'delta_swa_sinks': 'delta_w', 'delta_swa_w_o': 'delta_w', 'delta_swa_b_o': 'delta_w', 'delta_xattn_norm': 'delta_w', 'delta_mem_norm': 'delta_w', 'delta_xattn_w_q': 'delta_w', 'delta_xattn_w_kv': 'delta_w', 'delta_xattn_w_o': 'delta_w', 'delta_ffn2_norm': 'delta_w', 'delta_ffn2_w_gate_up': 'delta_w', 'delta_ffn2_w_down': 'delta_w', 'delta_final_norm': 'delta_w', 'new_m_ffn1_norm': 'new_m', 'new_m_ffn1_w_gate_up': 'new_m', 'new_m_ffn1_w_down': 'new_m', 'new_m_mix_norm': 'new_m', 'new_m_sb_w_qkv': 'new_m', 'new_m_sb_w_o': 'new_m', 'new_m_dil_w_qkv': 'new_m', 'new_m_dil_w_o': 'new_m', 'new_m_swa_w_qkv': 'new_m', 'new_m_swa_b_qkv': 'new_m', 'new_m_swa_sinks': 'new_m', 'new_m_swa_w_o': 'new_m', 'new_m_swa_b_o': 'new_m', 'new_m_xattn_norm': 'new_m', 'new_m_mem_norm': 'new_m', 'new_m_xattn_w_q': 'new_m', 'new_m_xattn_w_kv': 'new_m', 'new_m_xattn_w_o': 'new_m', 'new_m_ffn2_norm': 'new_m', 'new_m_ffn2_w_gate_up': 'new_m', 'new_m_ffn2_w_down': 'new_m', 'new_m_final_norm': 'new_m', 'new_v_ffn1_norm': 'new_v', 'new_v_ffn1_w_gate_up': 'new_v', 'new_v_ffn1_w_down': 'new_v', 'new_v_mix_norm': 'new_v', 'new_v_sb_w_qkv': 'new_v', 'new_v_sb_w_o': 'new_v', 'new_v_dil_w_qkv': 'new_v', 'new_v_dil_w_o': 'new_v', 'new_v_swa_w_qkv': 'new_v', 'new_v_swa_b_qkv': 'new_v', 'new_v_swa_sinks': 'new_v', 'new_v_swa_w_o': 'new_v', 'new_v_swa_b_o': 'new_v', 'new_v_xattn_norm': 'new_v', 'new_v_mem_norm': 'new_v', 'new_v_xattn_w_q': 'new_v', 'new_v_xattn_w_kv': 'new_v', 'new_v_xattn_w_o': 'new_v', 'new_v_ffn2_norm': 'new_v', 'new_v_ffn2_w_gate_up': 'new_v', 'new_v_ffn2_w_down': 'new_v', 'new_v_final_norm': 'new_v'}


def _forward(args):
    return _fwd_reference(*[args[k] for k in FWD_PARAMS])


def _output_shape():
    def fwd():
        inp = _fwd_setup_inputs(0)
        return _fwd_reference(*[inp[k] for k in FWD_PARAMS])
    out = _jax.eval_shape(fwd)
    return out.shape, out.dtype

N_MICROBATCH = 1
ADAM_LR = 0.001
ADAM_B1 = 0.9
ADAM_B2 = 0.999
ADAM_EPS = 1e-08
ADAM_WD = 0.01
ADAM_STEP = 10
PER_EXAMPLE_BATCH_AXIS = {'x': 0, 'mem': 0, 'positions': 0, 'loss_target': 0}
SHARED_INPUTS = []
_WEIGHT_DTYPES = {'ffn1_norm': _jnp.float32, 'ffn1_w_gate_up': _jnp.float32, 'ffn1_w_down': _jnp.float32, 'mix_norm': _jnp.float32, 'sb_w_qkv': _jnp.float32, 'sb_w_o': _jnp.float32, 'dil_w_qkv': _jnp.float32, 'dil_w_o': _jnp.float32, 'swa_w_qkv': _jnp.float32, 'swa_b_qkv': _jnp.float32, 'swa_sinks': _jnp.float32, 'swa_w_o': _jnp.float32, 'swa_b_o': _jnp.float32, 'xattn_norm': _jnp.float32, 'mem_norm': _jnp.float32, 'xattn_w_q': _jnp.float32, 'xattn_w_kv': _jnp.float32, 'xattn_w_o': _jnp.float32, 'ffn2_norm': _jnp.float32, 'ffn2_w_gate_up': _jnp.float32, 'ffn2_w_down': _jnp.float32, 'final_norm': _jnp.float32}
MOMENT_SCALE = {'ffn1_norm': 3.574046e-02, 'ffn1_w_gate_up': 1.520030e-02, 'ffn1_w_down': 2.480821e-02, 'mix_norm': 4.287830e-02, 'sb_w_qkv': 3.355947e-02, 'sb_w_o': 4.899310e-02, 'dil_w_qkv': 1.003674e-02, 'dil_w_o': 1.040413e-02, 'swa_w_qkv': 2.090475e-02, 'swa_b_qkv': 7.871623e-02, 'swa_sinks': 1.335112e-02, 'swa_w_o': 1.636823e-02, 'swa_b_o': 8.186157e-02, 'xattn_norm': 7.536714e-03, 'mem_norm': 1.159172e-02, 'xattn_w_q': 1.526509e-02, 'xattn_w_kv': 1.598764e-02, 'xattn_w_o': 8.381666e-03, 'ffn2_norm': 3.069266e-02, 'ffn2_w_gate_up': 1.289891e-02, 'ffn2_w_down': 2.104775e-02, 'final_norm': 1.601211e+01}


def _to_microbatches(a, axis):
    t = _jnp.moveaxis(a, axis, 0)
    t = t.reshape((N_MICROBATCH, t.shape[0] // N_MICROBATCH) + t.shape[1:])
    return _jnp.moveaxis(t, 1, axis + 1)


def setup_inputs(seed: int = 0) -> dict:
    inp = _fwd_setup_inputs(seed)
    key = _jax.random.fold_in(_jax.random.key(seed), 7919)
    shape, _ = _output_shape()
    out = dict(inp)
    out["loss_target"] = _jax.random.normal(_jax.random.fold_in(key, 0), shape, _jnp.float32)
    for i, name in enumerate(TWIN_WEIGHTS):
        w = inp[name].astype(_jnp.float32)
        if MOMENT_SCALE is None:
            s = _jnp.sqrt(_jnp.mean(_jnp.square(w)) + 1e-30)
        else:
            s = MOMENT_SCALE[name]
        km, kv = _jax.random.split(_jax.random.fold_in(key, i + 1))
        out[name] = w
        out["m_" + name] = s * _jax.random.normal(km, w.shape, _jnp.float32)
        out["v_" + name] = (s * s) * _jax.random.uniform(kv, w.shape, _jnp.float32, 0.5, 1.5)
    if N_MICROBATCH > 1:
        for name, axis in PER_EXAMPLE_BATCH_AXIS.items():
            out[name] = _to_microbatches(out[name], axis)
    return {'x': out['x'], 'mem': out['mem'], 'positions': out['positions'], 'ffn1_norm': out['ffn1_norm'], 'ffn1_w_gate_up': out['ffn1_w_gate_up'], 'ffn1_w_down': out['ffn1_w_down'], 'mix_norm': out['mix_norm'], 'sb_w_qkv': out['sb_w_qkv'], 'sb_w_o': out['sb_w_o'], 'dil_w_qkv': out['dil_w_qkv'], 'dil_w_o': out['dil_w_o'], 'swa_w_qkv': out['swa_w_qkv'], 'swa_b_qkv': out['swa_b_qkv'], 'swa_sinks': out['swa_sinks'], 'swa_w_o': out['swa_w_o'], 'swa_b_o': out['swa_b_o'], 'xattn_norm': out['xattn_norm'], 'mem_norm': out['mem_norm'], 'xattn_w_q': out['xattn_w_q'], 'xattn_w_kv': out['xattn_w_kv'], 'xattn_w_o': out['xattn_w_o'], 'ffn2_norm': out['ffn2_norm'], 'ffn2_w_gate_up': out['ffn2_w_gate_up'], 'ffn2_w_down': out['ffn2_w_down'], 'final_norm': out['final_norm'], 'loss_target': out['loss_target'], 'm_ffn1_norm': out['m_ffn1_norm'], 'm_ffn1_w_gate_up': out['m_ffn1_w_gate_up'], 'm_ffn1_w_down': out['m_ffn1_w_down'], 'm_mix_norm': out['m_mix_norm'], 'm_sb_w_qkv': out['m_sb_w_qkv'], 'm_sb_w_o': out['m_sb_w_o'], 'm_dil_w_qkv': out['m_dil_w_qkv'], 'm_dil_w_o': out['m_dil_w_o'], 'm_swa_w_qkv': out['m_swa_w_qkv'], 'm_swa_b_qkv': out['m_swa_b_qkv'], 'm_swa_sinks': out['m_swa_sinks'], 'm_swa_w_o': out['m_swa_w_o'], 'm_swa_b_o': out['m_swa_b_o'], 'm_xattn_norm': out['m_xattn_norm'], 'm_mem_norm': out['m_mem_norm'], 'm_xattn_w_q': out['m_xattn_w_q'], 'm_xattn_w_kv': out['m_xattn_w_kv'], 'm_xattn_w_o': out['m_xattn_w_o'], 'm_ffn2_norm': out['m_ffn2_norm'], 'm_ffn2_w_gate_up': out['m_ffn2_w_gate_up'], 'm_ffn2_w_down': out['m_ffn2_w_down'], 'm_final_norm': out['m_final_norm'], 'v_ffn1_norm': out['v_ffn1_norm'], 'v_ffn1_w_gate_up': out['v_ffn1_w_gate_up'], 'v_ffn1_w_down': out['v_ffn1_w_down'], 'v_mix_norm': out['v_mix_norm'], 'v_sb_w_qkv': out['v_sb_w_qkv'], 'v_sb_w_o': out['v_sb_w_o'], 'v_dil_w_qkv': out['v_dil_w_qkv'], 'v_dil_w_o': out['v_dil_w_o'], 'v_swa_w_qkv': out['v_swa_w_qkv'], 'v_swa_b_qkv': out['v_swa_b_qkv'], 'v_swa_sinks': out['v_swa_sinks'], 'v_swa_w_o': out['v_swa_w_o'], 'v_swa_b_o': out['v_swa_b_o'], 'v_xattn_norm': out['v_xattn_norm'], 'v_mem_norm': out['v_mem_norm'], 'v_xattn_w_q': out['v_xattn_w_q'], 'v_xattn_w_kv': out['v_xattn_w_kv'], 'v_xattn_w_o': out['v_xattn_w_o'], 'v_ffn2_norm': out['v_ffn2_norm'], 'v_ffn2_w_gate_up': out['v_ffn2_w_gate_up'], 'v_ffn2_w_down': out['v_ffn2_w_down'], 'v_final_norm': out['v_final_norm']}


def _loss(weights, diff, rest, loss_target):
    with _jax.named_scope("forward"):
        args = {**rest, TWIN_DIFF_INPUT: diff, **{k: w.astype(_WEIGHT_DTYPES[k]) for k, w in weights.items()}}
        y = _forward(args)
    with _jax.named_scope("loss_head"):
        err = _jnp.square(y.astype(_jnp.float32) - loss_target)
        return 0.5 * _jnp.sum(_jnp.mean(err, axis=-1)) if err.ndim else 0.5 * err


def _adamw(w, g, m, v):
    m = ADAM_B1 * m + (1.0 - ADAM_B1) * g
    v = ADAM_B2 * v + (1.0 - ADAM_B2) * _jnp.square(g)
    m_hat = m / (1.0 - ADAM_B1 ** ADAM_STEP)
    v_hat = v / (1.0 - ADAM_B2 ** ADAM_STEP)
    delta = -ADAM_LR * (m_hat / (_jnp.sqrt(v_hat) + ADAM_EPS) + ADAM_WD * w)
    return delta, m, v


def reference(x, mem, positions, ffn1_norm, ffn1_w_gate_up, ffn1_w_down, mix_norm, sb_w_qkv, sb_w_o, dil_w_qkv, dil_w_o, swa_w_qkv, swa_b_qkv, swa_sinks, swa_w_o, swa_b_o, xattn_norm, mem_norm, xattn_w_q, xattn_w_kv, xattn_w_o, ffn2_norm, ffn2_w_gate_up, ffn2_w_down, final_norm, loss_target, m_ffn1_norm, m_ffn1_w_gate_up, m_ffn1_w_down, m_mix_norm, m_sb_w_qkv, m_sb_w_o, m_dil_w_qkv, m_dil_w_o, m_swa_w_qkv, m_swa_b_qkv, m_swa_sinks, m_swa_w_o, m_swa_b_o, m_xattn_norm, m_mem_norm, m_xattn_w_q, m_xattn_w_kv, m_xattn_w_o, m_ffn2_norm, m_ffn2_w_gate_up, m_ffn2_w_down, m_final_norm, v_ffn1_norm, v_ffn1_w_gate_up, v_ffn1_w_down, v_mix_norm, v_sb_w_qkv, v_sb_w_o, v_dil_w_qkv, v_dil_w_o, v_swa_w_qkv, v_swa_b_qkv, v_swa_sinks, v_swa_w_o, v_swa_b_o, v_xattn_norm, v_mem_norm, v_xattn_w_q, v_xattn_w_kv, v_xattn_w_o, v_ffn2_norm, v_ffn2_w_gate_up, v_ffn2_w_down, v_final_norm):
    given = dict(x=x, mem=mem, positions=positions, ffn1_norm=ffn1_norm, ffn1_w_gate_up=ffn1_w_gate_up, ffn1_w_down=ffn1_w_down, mix_norm=mix_norm, sb_w_qkv=sb_w_qkv, sb_w_o=sb_w_o, dil_w_qkv=dil_w_qkv, dil_w_o=dil_w_o, swa_w_qkv=swa_w_qkv, swa_b_qkv=swa_b_qkv, swa_sinks=swa_sinks, swa_w_o=swa_w_o, swa_b_o=swa_b_o, xattn_norm=xattn_norm, mem_norm=mem_norm, xattn_w_q=xattn_w_q, xattn_w_kv=xattn_w_kv, xattn_w_o=xattn_w_o, ffn2_norm=ffn2_norm, ffn2_w_gate_up=ffn2_w_gate_up, ffn2_w_down=ffn2_w_down, final_norm=final_norm, loss_target=loss_target, m_ffn1_norm=m_ffn1_norm, m_ffn1_w_gate_up=m_ffn1_w_gate_up, m_ffn1_w_down=m_ffn1_w_down, m_mix_norm=m_mix_norm, m_sb_w_qkv=m_sb_w_qkv, m_sb_w_o=m_sb_w_o, m_dil_w_qkv=m_dil_w_qkv, m_dil_w_o=m_dil_w_o, m_swa_w_qkv=m_swa_w_qkv, m_swa_b_qkv=m_swa_b_qkv, m_swa_sinks=m_swa_sinks, m_swa_w_o=m_swa_w_o, m_swa_b_o=m_swa_b_o, m_xattn_norm=m_xattn_norm, m_mem_norm=m_mem_norm, m_xattn_w_q=m_xattn_w_q, m_xattn_w_kv=m_xattn_w_kv, m_xattn_w_o=m_xattn_w_o, m_ffn2_norm=m_ffn2_norm, m_ffn2_w_gate_up=m_ffn2_w_gate_up, m_ffn2_w_down=m_ffn2_w_down, m_final_norm=m_final_norm, v_ffn1_norm=v_ffn1_norm, v_ffn1_w_gate_up=v_ffn1_w_gate_up, v_ffn1_w_down=v_ffn1_w_down, v_mix_norm=v_mix_norm, v_sb_w_qkv=v_sb_w_qkv, v_sb_w_o=v_sb_w_o, v_dil_w_qkv=v_dil_w_qkv, v_dil_w_o=v_dil_w_o, v_swa_w_qkv=v_swa_w_qkv, v_swa_b_qkv=v_swa_b_qkv, v_swa_sinks=v_swa_sinks, v_swa_w_o=v_swa_w_o, v_swa_b_o=v_swa_b_o, v_xattn_norm=v_xattn_norm, v_mem_norm=v_mem_norm, v_xattn_w_q=v_xattn_w_q, v_xattn_w_kv=v_xattn_w_kv, v_xattn_w_o=v_xattn_w_o, v_ffn2_norm=v_ffn2_norm, v_ffn2_w_gate_up=v_ffn2_w_gate_up, v_ffn2_w_down=v_ffn2_w_down, v_final_norm=v_final_norm)
    weights = {n: given[n] for n in TWIN_WEIGHTS}
    shared = {n: given[n] for n in SHARED_INPUTS}
    per_example = {n: given[n] for n in ['x', 'mem', 'positions']}
    grad_fn = _jax.value_and_grad(_loss, argnums=(0, 1))

    def one_microbatch(ex, loss_target):
        ex = dict(ex)
        diff = ex.pop(TWIN_DIFF_INPUT)
        return grad_fn(weights, diff, {**shared, **ex}, loss_target)

    if N_MICROBATCH == 1:
        loss, (grad_w, grad_x) = one_microbatch(per_example, given["loss_target"])
    else:
        def body(carry, xs):
            loss_sum, grad_sum = carry
            l_k, (gw_k, gx_k) = one_microbatch(xs[0], xs[1])
            with _jax.named_scope("update"):
                return (loss_sum + l_k, _jax.tree.map(_jnp.add, grad_sum, gw_k)), gx_k

        init = (_jnp.zeros((), _jnp.float32), _jax.tree.map(_jnp.zeros_like, weights))
        (loss, grad_w), grad_x = _jax.lax.scan(body, init, (per_example, given["loss_target"]))
    with _jax.named_scope("update"):
        delta_w, new_m, new_v = {}, {}, {}
        for n in TWIN_WEIGHTS:
            delta_w[n], new_m[n], new_v[n] = _adamw(weights[n], grad_w[n], given["m_" + n], given["v_" + n])
    return (loss, grad_x, *[grad_w[n] for n in TWIN_WEIGHTS], *[delta_w[n] for n in TWIN_WEIGHTS],
            *[new_m[n] for n in TWIN_WEIGHTS], *[new_v[n] for n in TWIN_WEIGHTS])
```

```python
import jax
import jax.numpy as jnp
from jax import lax
from jax.experimental import pallas as pl
from jax.experimental.pallas import tpu as pltpu

F32 = jnp.float32
BF16 = jnp.bfloat16

N_DEV = 8
MESH_AXES = ("x", "y", "c")
LANES = 128
BAND = 128
NORM_EPS = 1e-6
ROPE_THETA = 500000.0
DIL_PATTERNS = ((128, 1), (512, 4), (2048, 16))
SWA_HEAD_DIM = 64
SWA_GROUP = 8
SWA_WINDOW = 128
XA_HEADS = 4
DEPTH = 4
ADAM_LR, ADAM_B1, ADAM_B2, ADAM_EPS, ADAM_WD, ADAM_STEP = 0.001, 0.9, 0.999, 1e-08, 0.01, 10
VMEM_LIMIT = 56 * 1024 * 1024
NEG = -1e30

NT = (((1,), (1,)), ((), ()))
TN = (((0,), (0,)), ((), ()))


def _tile(n, prefs):
    for p in prefs:
        if n % p == 0:
            return p
    return n


def _cp(sem):
    return pltpu.CompilerParams(dimension_semantics=sem, vmem_limit_bytes=VMEM_LIMIT)


def _mm(a, b, *, ta=False, tb=False, out_dtype=F32, alpha=1.0, res=None, bias=None, name):
    kdim, m = a.shape if ta else a.shape[::-1]
    kdim2, n = b.shape[::-1] if tb else b.shape
    assert kdim == kdim2, (a.shape, b.shape, ta, tb)
    bm = _tile(m, (1024, 512, 256, 128))
    bn = _tile(n, (1024, 512, 256, 128))
    bk = _tile(kdim, (1024, 512, 256, 128))
    nk = kdim // bk
    dn = (((0 if ta else 1,), (1 if tb else 0,)), ((), ()))
    has_res, has_bias = res is not None, bias is not None

    def body(*refs):
        a_ref, b_ref = refs[0], refs[1]
        res_ref = refs[2] if has_res else None
        bias_ref = refs[2 + has_res] if has_bias else None
        o_ref, acc_ref = refs[-2], refs[-1]
        k = pl.program_id(2)

        @pl.when(k == 0)
        def _():
            acc_ref[...] = jnp.zeros_like(acc_ref)

        acc_ref[...] += lax.dot_general(a_ref[...].astype(BF16), b_ref[...].astype(BF16), dn,
                                        preferred_element_type=F32)

        @pl.when(k == nk - 1)
        def _():
            r = acc_ref[...]
            if alpha != 1.0:
                r = r * alpha
            if has_bias:
                r = r + bias_ref[...]
            if has_res:
                r = r + res_ref[...]
            o_ref[...] = r.astype(o_ref.dtype)

    a_spec = pl.BlockSpec((bk, bm), lambda i, j, k: (k, i)) if ta else pl.BlockSpec((bm, bk), lambda i, j, k: (i, k))
    b_spec = pl.BlockSpec((bn, bk), lambda i, j, k: (j, k)) if tb else pl.BlockSpec((bk, bn), lambda i, j, k: (k, j))
    ins, specs = [a, b], [a_spec, b_spec]
    if has_res:
        ins.append(res)
        specs.append(pl.BlockSpec((bm, bn), lambda i, j, k: (i, j)))
    if has_bias:
        ins.append(bias)
        specs.append(pl.BlockSpec((1, bn), lambda i, j, k: (0, j)))
    return pl.pallas_call(
        body, name=name, grid=(m // bm, n // bn, nk), in_specs=specs,
        out_specs=pl.BlockSpec((bm, bn), lambda i, j, k: (i, j)),
        out_shape=jax.ShapeDtypeStruct((m, n), out_dtype),
        scratch_shapes=[pltpu.VMEM((bm, bn), F32)],
        compiler_params=_cp(("parallel", "parallel", "arbitrary")),
    )(*ins)


def _rms_fwd(x, g, name="rms_fwd"):
    s, d = x.shape
    bm = _tile(s, (256, 128, 8))

    def body(x_ref, g_ref, o_ref):
        xv = x_ref[...]
        r = lax.rsqrt(jnp.mean(xv * xv, axis=-1, keepdims=True) + NORM_EPS)
        o_ref[...] = (xv * r * g_ref[...]).astype(o_ref.dtype)

    return pl.pallas_call(
        body, name=name, grid=(s // bm,),
        in_specs=[pl.BlockSpec((bm, d), lambda i: (i, 0)), pl.BlockSpec((1, d), lambda i: (0, 0))],
        out_specs=pl.BlockSpec((bm, d), lambda i: (i, 0)),
        out_shape=jax.ShapeDtypeStruct((s, d), BF16), compiler_params=_cp(("parallel",)),
    )(x, g)


def _rms_bwd(x, g, dn, dy=None, name="rms_bwd"):
    s, d = x.shape
    bm = _tile(s, (256, 128, 8))
    has_dy = dy is not None

    def body(*refs):
        x_ref, g_ref, dn_ref = refs[:3]
        dy_ref = refs[3] if has_dy else None
        dx_ref, dg_ref = refs[-2], refs[-1]

        @pl.when(pl.program_id(0) == 0)
        def _():
            dg_ref[...] = jnp.zeros_like(dg_ref)

        xv = x_ref[...]
        r = lax.rsqrt(jnp.mean(xv * xv, axis=-1, keepdims=True) + NORM_EPS)
        xh = xv * r
        dnv = dn_ref[...].astype(F32)
        dxh = dnv * g_ref[...]
        dx = r * (dxh - xh * jnp.mean(dxh * xh, axis=-1, keepdims=True))
        if has_dy:
            dx = dx + dy_ref[...]
        dx_ref[...] = dx
        dg_ref[...] += jnp.sum(dnv * xh, axis=0, keepdims=True)

    row = pl.BlockSpec((bm, d), lambda i: (i, 0))
    vec = pl.BlockSpec((1, d), lambda i: (0, 0))
    ins, specs = [x, g, dn], [row, vec, row]
    if has_dy:
        ins.append(dy)
        specs.append(row)
    return pl.pallas_call(
        body, name=name, grid=(s // bm,), in_specs=specs, out_specs=[row, vec],
        out_shape=[jax.ShapeDtypeStruct((s, d), F32), jax.ShapeDtypeStruct((1, d), F32)],
        compiler_params=_cp(("arbitrary",)),
    )(*ins)


def _loss_head(x, g, target):
    s, d = x.shape
    bm = _tile(s, (256, 128, 8))

    def body(x_ref, g_ref, t_ref, loss_ref, dx_ref, dg_ref):
        @pl.when(pl.program_id(0) == 0)
        def _():
            dg_ref[...] = jnp.zeros_like(dg_ref)
            loss_ref[...] = jnp.zeros_like(loss_ref)

        xv = x_ref[...]
        gv = g_ref[...]
        r = lax.rsqrt(jnp.mean(xv * xv, axis=-1, keepdims=True) + NORM_EPS)
        xh = xv * r
        err = xh * gv - t_ref[...]
        part = 0.5 * jnp.sum(jnp.mean(err * err, axis=-1, keepdims=True), axis=0, keepdims=True)
        loss_ref[...] += jnp.broadcast_to(part, loss_ref.shape)
        dyv = err * (1.0 / d)
        dxh = dyv * gv
        dx_ref[...] = r * (dxh - xh * jnp.mean(dxh * xh, axis=-1, keepdims=True))
        dg_ref[...] += jnp.sum(dyv * xh, axis=0, keepdims=True)

    row = pl.BlockSpec((bm, d), lambda i: (i, 0))
    vec = pl.BlockSpec((1, d), lambda i: (0, 0))
    return pl.pallas_call(
        body, name="loss_head", grid=(s // bm,), in_specs=[row, vec, row],
        out_specs=[pl.BlockSpec((1, LANES), lambda i: (0, 0)), row, vec],
        out_shape=[jax.ShapeDtypeStruct((1, LANES), F32), jax.ShapeDtypeStruct((s, d), F32),
                   jax.ShapeDtypeStruct((1, d), F32)],
        compiler_params=_cp(("arbitrary",)),
    )(x, g, target)


def _colsum(x, name="colsum"):
    s, n = x.shape
    bm = _tile(s, (256, 128, 8))

    def body(x_ref, o_ref):
        @pl.when(pl.program_id(0) == 0)
        def _():
            o_ref[...] = jnp.zeros_like(o_ref)

        o_ref[...] += jnp.sum(x_ref[...].astype(F32), axis=0, keepdims=True)

    return pl.pallas_call(
        body, name=name, grid=(s // bm,), in_specs=[pl.BlockSpec((bm, n), lambda i: (i, 0))],
        out_specs=pl.BlockSpec((1, n), lambda i: (0, 0)), out_shape=jax.ShapeDtypeStruct((1, n), F32),
        compiler_params=_cp(("arbitrary",)),
    )(x)


def _swiglu_fwd(gu):
    s, f2 = gu.shape
    f = f2 // 2
    bm, bf = _tile(s, (512, 256, 128, 8)), _tile(f, (512, 256, 128))
    nf = f // bf

    def body(g_ref, u_ref, o_ref):
        gv = g_ref[...].astype(F32)
        o_ref[...] = (gv / (1.0 + jnp.exp(-gv)) * u_ref[...].astype(F32)).astype(o_ref.dtype)

    return pl.pallas_call(
        body, name="swiglu_fwd", grid=(s // bm, nf),
        in_specs=[pl.BlockSpec((bm, bf), lambda i, j: (i, j)), pl.BlockSpec((bm, bf), lambda i, j: (i, j + nf))],
        out_specs=pl.BlockSpec((bm, bf), lambda i, j: (i, j)),
        out_shape=jax.ShapeDtypeStruct((s, f), BF16), compiler_params=_cp(("parallel", "parallel")),
    )(gu, gu)


def _swiglu_bwd(gu, da):
    s, f2 = gu.shape
    f = f2 // 2
    bm, bf = _tile(s, (512, 256, 128, 8)), _tile(f, (512, 256, 128))
    nf = f // bf

    def body(g_ref, u_ref, da_ref, o_ref):
        t = pl.program_id(2)
        gv = g_ref[...].astype(F32)
        dav = da_ref[...].astype(F32)
        sig = 1.0 / (1.0 + jnp.exp(-gv))

        @pl.when(t == 0)
        def _():
            o_ref[...] = (dav * u_ref[...].astype(F32) * (sig * (1.0 + gv * (1.0 - sig)))).astype(o_ref.dtype)

        @pl.when(t == 1)
        def _():
            o_ref[...] = (dav * gv * sig).astype(o_ref.dtype)

    return pl.pallas_call(
        body, name="swiglu_bwd", grid=(s // bm, nf, 2),
        in_specs=[pl.BlockSpec((bm, bf), lambda i, j, t: (i, j)), pl.BlockSpec((bm, bf), lambda i, j, t: (i, j + nf)),
                  pl.BlockSpec((bm, bf), lambda i, j, t: (i, j))],
        out_specs=pl.BlockSpec((bm, bf), lambda i, j, t: (i, j + t * nf)),
        out_shape=jax.ShapeDtypeStruct((s, f2), BF16), compiler_params=_cp(("parallel", "parallel", "arbitrary")),
    )(gu, gu, da)


def _rope_tables(positions, rot, sign=1.0):
    half = rot // 2
    inv_freq = jnp.power(F32(ROPE_THETA), -jnp.arange(half, dtype=F32) * 2.0 / rot)
    ang = positions.reshape(-1).astype(F32)[:, None] * inv_freq
    cos, sin = jnp.cos(ang), jnp.sin(ang) * sign
    s = ang.shape[0]
    c_tab = jnp.concatenate([cos, cos, jnp.ones((s, LANES - rot), F32)], axis=1)
    s_tab = jnp.concatenate([-sin, sin, jnp.zeros((s, LANES - rot), F32)], axis=1)
    return c_tab, s_tab


def _rope(x, c_tab, s_tab, nblk, half, name):
    s = x.shape[0]
    bm = _tile(s, (512, 256, 128, 8))

    def body(x_ref, c_ref, s_ref, o_ref):
        xv = x_ref[...].astype(F32)
        lane = lax.broadcasted_iota(jnp.int32, xv.shape, 1)
        sw = jnp.where(lane < half, pltpu.roll(xv, LANES - half, 1), pltpu.roll(xv, half, 1))
        o_ref[...] = (xv * c_ref[...] + sw * s_ref[...]).astype(o_ref.dtype)

    blk = pl.BlockSpec((bm, LANES), lambda i, j: (i, j))
    tab = pl.BlockSpec((bm, LANES), lambda i, j: (i, 0))
    return pl.pallas_call(
        body, name=name, grid=(s // bm, nblk), in_specs=[blk, tab, tab], out_specs=blk,
        out_shape=jax.ShapeDtypeStruct((s, nblk * LANES), BF16), compiler_params=_cp(("parallel", "parallel")),
    )(x, c_tab, s_tab)


def _softplus(z):
    return jnp.maximum(z, 0.0) + jnp.log(1.0 + jnp.exp(-jnp.abs(z)))


def _split_dot(x, t):
    hi = x.astype(BF16)
    lo = (x - hi.astype(F32)).astype(BF16)
    return jnp.dot(hi, t, preferred_element_type=F32) + jnp.dot(lo, t, preferred_element_type=F32)


def _sb_fwd(qkv, n_heads):
    s = qkv.shape[0]
    tq = _tile(s, (256, 128))
    scale = LANES ** -0.5

    def body(q_ref, k_ref, v_ref, o_ref, lt_ref):
        i = pl.program_id(1)
        q = q_ref[...]
        row = lax.broadcasted_iota(jnp.int32, (tq, tq), 0)
        col = lax.broadcasted_iota(jnp.int32, (tq, tq), 1)
        later = (row > col).astype(BF16)

        def step(t, carry):
            c, acc = carry
            j = i - t
            k0 = pl.multiple_of(j * tq, tq)
            kb = k_ref[pl.ds(k0, tq), :]
            vb = v_ref[pl.ds(k0, tq), :]
            z = lax.dot_general(q, kb, NT, preferred_element_type=F32) * scale
            causal = (col + j * tq) < (row + i * tq)
            sp = _softplus(z)
            lk = jnp.where(causal, -sp, 0.0)
            tail = _split_dot(lk, later) + c
            a = jnp.where(causal, jnp.exp(z - sp + tail), 0.0)
            acc = acc + jnp.dot(a.astype(BF16), vb, preferred_element_type=F32)
            return c + jnp.sum(lk, axis=1, keepdims=True), acc

        c, acc = lax.fori_loop(0, i + 1, step, (jnp.zeros((tq, 1), F32), jnp.zeros((tq, LANES), F32)))
        o_ref[...] = acc.astype(o_ref.dtype)
        lt_ref[...] = jnp.broadcast_to(c, (tq, LANES))

    blk = pl.BlockSpec((tq, LANES), lambda h, i: (i, h))
    return pl.pallas_call(
        body, name="sb_fwd", grid=(n_heads, s // tq),
        in_specs=[blk, pl.BlockSpec((s, LANES), lambda h, i: (0, n_heads + h)),
                  pl.BlockSpec((s, LANES), lambda h, i: (0, 2 * n_heads + h))],
        out_specs=[blk, blk],
        out_shape=[jax.ShapeDtypeStruct((s, n_heads * LANES), BF16), jax.ShapeDtypeStruct((s, n_heads * LANES), F32)],
        compiler_params=_cp(("parallel", "arbitrary")),
    )(qkv, qkv, qkv)


def _sb_bwd(qkv, do, ltot, n_heads):
    s = qkv.shape[0]
    tq = _tile(s, (256, 128))
    scale = LANES ** -0.5

    def body(q_ref, k_ref, v_ref, do_ref, lt_ref, dq_ref, dk_ref, dv_ref):
        i = pl.program_id(1)

        @pl.when(i == 0)
        def _():
            dk_ref[...] = jnp.zeros_like(dk_ref)
            dv_ref[...] = jnp.zeros_like(dv_ref)

        q = q_ref[...]
        dob = do_ref[...]
        lt = lt_ref[:, 0:1]
        row = lax.broadcasted_iota(jnp.int32, (tq, tq), 0)
        col = lax.broadcasted_iota(jnp.int32, (tq, tq), 1)
        upto = (row <= col).astype(BF16)
        before = (row < col).astype(BF16)

        def step(j, carry):
            cpre, ce, dq = carry
            k0 = pl.multiple_of(j * tq, tq)
            kb = k_ref[pl.ds(k0, tq), :]
            vb = v_ref[pl.ds(k0, tq), :]
            z = lax.dot_general(q, kb, NT, preferred_element_type=F32) * scale
            causal = (col + j * tq) < (row + i * tq)
            sp = _softplus(z)
            lk = jnp.where(causal, -sp, 0.0)
            tail = lt - cpre - _split_dot(lk, upto)
            logsig = z - sp
            a = jnp.where(causal, jnp.exp(logsig + tail), 0.0)
            da = lax.dot_general(dob, vb, NT, preferred_element_type=F32)
            e = a * da
            e_before = ce + _split_dot(e, before)
            sig = jnp.exp(logsig)
            dz = jnp.where(causal, e * (1.0 - sig) - sig * e_before, 0.0) * scale
            dzb = dz.astype(BF16)
            dq = dq + jnp.dot(dzb, kb, preferred_element_type=F32)
            dk_ref[pl.ds(k0, tq), :] += lax.dot_general(dzb, q, TN, preferred_element_type=F32)
            dv_ref[pl.ds(k0, tq), :] += lax.dot_general(a.astype(BF16), dob, TN, preferred_element_type=F32)
            return cpre + jnp.sum(lk, axis=1, keepdims=True), ce + jnp.sum(e, axis=1, keepdims=True), dq

        z1 = jnp.zeros((tq, 1), F32)
        _, _, dq = lax.fori_loop(0, i + 1, step, (z1, z1, jnp.zeros((tq, LANES), F32)))
        dq_ref[...] = dq.astype(dq_ref.dtype)

    blk = pl.BlockSpec((tq, LANES), lambda h, i: (i, h))
    full = pl.BlockSpec((s, LANES), lambda h, i: (0, h))
    w = n_heads * LANES
    return pl.pallas_call(
        body, name="sb_bwd", grid=(n_heads, s // tq),
        in_specs=[blk, pl.BlockSpec((s, LANES), lambda h, i: (0, n_heads + h)),
                  pl.BlockSpec((s, LANES), lambda h, i: (0, 2 * n_heads + h)), blk, blk],
        out_specs=[blk, full, full],
        out_shape=[jax.ShapeDtypeStruct((s, w), BF16), jax.ShapeDtypeStruct((s, w), F32),
                   jax.ShapeDtypeStruct((s, w), F32)],
        compiler_params=_cp(("parallel", "arbitrary")),
    )(qkv, qkv, qkv, do, ltot)


def _band_masks(b, max_dist):
    qi = lax.broadcasted_iota(jnp.int32, (BAND, BAND), 0)
    kj = lax.broadcasted_iota(jnp.int32, (BAND, BAND), 1)
    dist = qi - kj
    return ((BAND + dist) <= max_dist) & (b > 0), (dist >= 0) & (dist <= max_dist)


def _band_fwd(qa, ka, va, *, n_cls, n_heads, group, q_col, k_col, v_col, max_dist, scale, sinks, name):
    length = qa.shape[0]
    nb = length // BAND
    has_sink = sinks is not None

    def body(*refs):
        q_ref, kp_ref, kc_ref, vp_ref, vc_ref = refs[:5]
        o_ref, lse_ref = refs[-2], refs[-1]
        mask_p, mask_c = _band_masks(pl.program_id(2), max_dist)
        q = q_ref[...].astype(BF16)
        s_p = lax.dot_general(q, kp_ref[...].astype(BF16), NT, preferred_element_type=F32) * scale
        s_c = lax.dot_general(q, kc_ref[...].astype(BF16), NT, preferred_element_type=F32) * scale
        s_p = jnp.where(mask_p, s_p, NEG)
        s_c = jnp.where(mask_c, s_c, NEG)
        m = jnp.maximum(jnp.max(s_p, axis=1, keepdims=True), jnp.max(s_c, axis=1, keepdims=True))
        l = jnp.sum(jnp.exp(s_p - m), axis=1, keepdims=True) + jnp.sum(jnp.exp(s_c - m), axis=1, keepdims=True)
        lse = m + jnp.log(l)
        if has_sink:
            sk = refs[5][:, 0:1]
            lse = jnp.maximum(lse, sk) + jnp.log(1.0 + jnp.exp(-jnp.abs(lse - sk)))
        p_p = jnp.exp(s_p - lse).astype(BF16)
        p_c = jnp.exp(s_c - lse).astype(BF16)
        o_ref[...] = (jnp.dot(p_p, vp_ref[...].astype(BF16), preferred_element_type=F32)
                      + jnp.dot(p_c, vc_ref[...].astype(BF16), preferred_element_type=F32))
        lse_ref[...] = jnp.broadcast_to(lse, (BAND, LANES))

    def prev(b):
        return jnp.maximum(b - 1, 0)

    blk = (BAND, LANES)
    specs = [pl.BlockSpec(blk, lambda n, h, b: (b, q_col(n, h))),
             pl.BlockSpec(blk, lambda n, h, b: (prev(b), k_col(n, h // group))),
             pl.BlockSpec(blk, lambda n, h, b: (b, k_col(n, h // group))),
             pl.BlockSpec(blk, lambda n, h, b: (prev(b), v_col(n, h // group))),
             pl.BlockSpec(blk, lambda n, h, b: (b, v_col(n, h // group)))]
    ins = [qa, ka, ka, va, va]
    if has_sink:
        ins.append(sinks)
        specs.append(pl.BlockSpec((1, LANES), lambda n, h, b: (0, h)))
    out = pl.BlockSpec(blk, lambda n, h, b: (b, n * n_heads + h))
    w = n_cls * n_heads * LANES
    return pl.pallas_call(
        body, name=name, grid=(n_cls, n_heads, nb), in_specs=specs, out_specs=[out, out],
        out_shape=[jax.ShapeDtypeStruct((length, w), F32), jax.ShapeDtypeStruct((length, w), F32)],
        compiler_params=_cp(("parallel", "parallel", "parallel")),
    )(*ins)


def _band_bwd(qa, ka, va, o, do, lse, dlse, *, n_cls, n_heads, group, q_col, k_col, v_col, max_dist, scale, sinks,
              name):
    length = qa.shape[0]
    nb = length // BAND
    has_sink, has_dlse = sinks is not None, dlse is not None

    def body(*refs):
        q_ref, kp_ref, kc_ref, vp_ref, vc_ref, o_ref, do_ref, lse_ref = refs[:8]
        pos = 8
        dlse_ref = refs[pos] if has_dlse else None
        pos += has_dlse
        sink_ref = refs[pos] if has_sink else None
        pos += has_sink
        dq_ref, dkc_ref, dkp_ref, dvc_ref, dvp_ref = refs[pos:pos + 5]
        b = pl.program_id(2)
        mask_p, mask_c = _band_masks(b, max_dist)
        q = q_ref[...].astype(BF16)
        kp, kc = kp_ref[...].astype(BF16), kc_ref[...].astype(BF16)
        vp, vc = vp_ref[...].astype(BF16), vc_ref[...].astype(BF16)
        dov = do_ref[...].astype(F32)
        dob = dov.astype(BF16)
        lse_v = lse_ref[:, 0:1]
        s_p = lax.dot_general(q, kp, NT, preferred_element_type=F32) * scale
        s_c = lax.dot_general(q, kc, NT, preferred_element_type=F32) * scale
        p_p = jnp.where(mask_p, jnp.exp(jnp.where(mask_p, s_p, NEG) - lse_v), 0.0)
        p_c = jnp.where(mask_c, jnp.exp(jnp.where(mask_c, s_c, NEG) - lse_v), 0.0)
        delta = jnp.sum(dov * o_ref[...], axis=1, keepdims=True)
        shift = -delta
        if has_dlse:
            shift = shift + dlse_ref[:, 0:1]
        dp_p = lax.dot_general(dob, vp, NT, preferred_element_type=F32)
        dp_c = lax.dot_general(dob, vc, NT, preferred_element_type=F32)
        ds_p = (p_p * (dp_p + shift) * scale).astype(BF16)
        ds_c = (p_c * (dp_c + shift) * scale).astype(BF16)
        dq_ref[...] = jnp.dot(ds_p, kp, preferred_element_type=F32) + jnp.dot(ds_c, kc, preferred_element_type=F32)
        dkc_ref[...] = lax.dot_general(ds_c, q, TN, preferred_element_type=F32)
        dkp_ref[...] = lax.dot_general(ds_p, q, TN, preferred_element_type=F32)
        dvc_ref[...] = lax.dot_general(p_c.astype(BF16), dob, TN, preferred_element_type=F32)
        dvp_ref[...] = lax.dot_general(p_p.astype(BF16), dob, TN, preferred_element_type=F32)
        if has_sink:
            dsink_ref = refs[pos + 5]

            @pl.when(b == 0)
            def _():
                dsink_ref[...] = jnp.zeros_like(dsink_ref)

            p_sink = jnp.exp(sink_ref[:, 0:1] - lse_v)
            dsink_ref[...] += jnp.broadcast_to(jnp.sum(-p_sink * delta, axis=0, keepdims=True), (1, LANES))

    def prev(b):
        return jnp.maximum(b - 1, 0)

    blk = (BAND, LANES)
    per_q = pl.BlockSpec(blk, lambda n, h, b: (b, n * n_heads + h))
    specs = [pl.BlockSpec(blk, lambda n, h, b: (b, q_col(n, h))),
             pl.BlockSpec(blk, lambda n, h, b: (prev(b), k_col(n, h // group))),
             pl.BlockSpec(blk, lambda n, h, b: (b, k_col(n, h // group))),
             pl.BlockSpec(blk, lambda n, h, b: (prev(b), v_col(n, h // group))),
             pl.BlockSpec(blk, lambda n, h, b: (b, v_col(n, h // group))),
             per_q, per_q, per_q]
    ins = [qa, ka, ka, va, va, o, do, lse]
    if has_dlse:
        ins.append(dlse)
        specs.append(per_q)
    w = n_cls * n_heads * LANES
    out_specs = [per_q] * 5
    out_shape = [jax.ShapeDtypeStruct((length, w), F32)] * 5
    if has_sink:
        ins.append(sinks)
        specs.append(pl.BlockSpec((1, LANES), lambda n, h, b: (0, h)))
        out_specs = out_specs + [pl.BlockSpec((1, LANES), lambda n, h, b: (0, h))]
        out_shape = out_shape + [jax.ShapeDtypeStruct((1, n_heads * LANES), F32)]
    return pl.pallas_call(
        body, name=name, grid=(n_cls, n_heads, nb), in_specs=specs, out_specs=out_specs, out_shape=out_shape,
        compiler_params=_cp(("parallel", "parallel", "arbitrary")),
    )(*ins)


def _band_fold(cur, prv, *, n_cls, n_heads, group, name):
    length = cur.shape[0]
    nb = length // BAND
    n_kv = n_heads // group

    def body(c_ref, p_ref, o_ref):
        b, gq = pl.program_id(2), pl.program_id(3)

        @pl.when(gq == 0)
        def _():
            o_ref[...] = jnp.zeros_like(o_ref)

        o_ref[...] += c_ref[...] + jnp.where(b + 1 < nb, p_ref[...], 0.0)

    blk = (BAND, LANES)
    return pl.pallas_call(
        body, name=name, grid=(n_cls, n_kv, nb, group),
        in_specs=[pl.BlockSpec(blk, lambda n, h, b, gq: (b, n * n_heads + h * group + gq)),
                  pl.BlockSpec(blk, lambda n, h, b, gq: (jnp.minimum(b + 1, nb - 1), n * n_heads + h * group + gq))],
        out_specs=pl.BlockSpec(blk, lambda n, h, b, gq: (b, n * n_kv + h)),
        out_shape=jax.ShapeDtypeStruct((length, n_cls * n_kv * LANES), F32),
        compiler_params=_cp(("parallel", "parallel", "parallel", "arbitrary")),
    )(cur, prv)


def _dil_mix_fwd(os_, lses):
    s, w = os_[0].shape
    bm = _tile(s, (256, 128, 8))

    def body(o0, o1, o2, l0, l1, l2, out_ref):
        ls = [l0[...], l1[...], l2[...]]
        m = jnp.maximum(jnp.maximum(ls[0], ls[1]), ls[2])
        es = [jnp.exp(v - m) for v in ls]
        inv = 1.0 / (es[0] + es[1] + es[2])
        for gi, o_ref in enumerate((o0, o1, o2)):
            out_ref[:, gi * w:(gi + 1) * w] = (o_ref[...] * (es[gi] * inv)).astype(out_ref.dtype)

    blk = pl.BlockSpec((bm, w), lambda i: (i, 0))
    return pl.pallas_call(
        body, name="dil_mix_fwd", grid=(s // bm,), in_specs=[blk] * 6,
        out_specs=pl.BlockSpec((bm, 3 * w), lambda i: (i, 0)),
        out_shape=jax.ShapeDtypeStruct((s, 3 * w), BF16), compiler_params=_cp(("parallel",)),
    )(*os_, *lses)


def _dil_mix_bwd(os_, lses, dmixed):
    s, w = os_[0].shape
    bm = _tile(s, (256, 128, 8))
    hg = w // LANES

    def body(o0, o1, o2, l0, l1, l2, dm_ref, do0, do1, do2, dl0, dl1, dl2):
        ls = [l0[...], l1[...], l2[...]]
        m = jnp.maximum(jnp.maximum(ls[0], ls[1]), ls[2])
        es = [jnp.exp(v - m) for v in ls]
        inv = 1.0 / (es[0] + es[1] + es[2])
        alphas = [e * inv for e in es]
        dalphas = []
        for gi, (o_ref, do_ref) in enumerate(((o0, do0), (o1, do1), (o2, do2))):
            dm = dm_ref[:, gi * w:(gi + 1) * w].astype(F32)
            do_ref[...] = (dm * alphas[gi]).astype(do_ref.dtype)
            prod = dm * o_ref[...]
            parts = [jnp.broadcast_to(jnp.sum(prod[:, j * LANES:(j + 1) * LANES], axis=1, keepdims=True), (bm, LANES))
                     for j in range(hg)]
            dalphas.append(jnp.concatenate(parts, axis=1) if hg > 1 else parts[0])
        mean = alphas[0] * dalphas[0] + alphas[1] * dalphas[1] + alphas[2] * dalphas[2]
        for gi, dl_ref in enumerate((dl0, dl1, dl2)):
            dl_ref[...] = alphas[gi] * (dalphas[gi] - mean)

    blk = pl.BlockSpec((bm, w), lambda i: (i, 0))
    return pl.pallas_call(
        body, name="dil_mix_bwd", grid=(s // bm,), in_specs=[blk] * 6 + [pl.BlockSpec((bm, 3 * w), lambda i: (i, 0))],
        out_specs=[blk] * 6,
        out_shape=[jax.ShapeDtypeStruct((s, w), BF16)] * 3 + [jax.ShapeDtypeStruct((s, w), F32)] * 3,
        compiler_params=_cp(("parallel",)),
    )(*os_, *lses, dmixed)


def _xattn_fwd(q, kv):
    s, w = q.shape
    mlen = kv.shape[0]
    tq = _tile(s, (512, 256, 128))
    scale = LANES ** -0.5

    def body(q_ref, k_ref, v_ref, o_ref):
        for h in range(XA_HEADS):
            sl = slice(h * LANES, (h + 1) * LANES)
            sc = lax.dot_general(q_ref[:, sl], k_ref[:, sl], NT, preferred_element_type=F32) * scale
            m = jnp.max(sc, axis=1, keepdims=True)
            e = jnp.exp(sc - m)
            p = e / jnp.sum(e, axis=1, keepdims=True)
            o_ref[:, sl] = jnp.dot(p.astype(BF16), v_ref[:, sl], preferred_element_type=F32).astype(o_ref.dtype)

    return pl.pallas_call(
        body, name="xattn_fwd", grid=(s // tq,),
        in_specs=[pl.BlockSpec((tq, w), lambda i: (i, 0)), pl.BlockSpec((mlen, w), lambda i: (0, 0)),
                  pl.BlockSpec((mlen, w), lambda i: (0, 1))],
        out_specs=pl.BlockSpec((tq, w), lambda i: (i, 0)),
        out_shape=jax.ShapeDtypeStruct((s, w), BF16), compiler_params=_cp(("parallel",)),
    )(q, kv, kv)


def _xattn_bwd(q, kv, do):
    s, w = q.shape
    mlen = kv.shape[0]
    tq = _tile(s, (512, 256, 128))
    scale = LANES ** -0.5

    def body(q_ref, k_ref, v_ref, do_ref, dq_ref, dk_ref, dv_ref):
        @pl.when(pl.program_id(0) == 0)
        def _():
            dk_ref[...] = jnp.zeros_like(dk_ref)
            dv_ref[...] = jnp.zeros_like(dv_ref)

        for h in range(XA_HEADS):
            sl = slice(h * LANES, (h + 1) * LANES)
            qh, kh, vh, doh = q_ref[:, sl], k_ref[:, sl], v_ref[:, sl], do_ref[:, sl]
            sc = lax.dot_general(qh, kh, NT, preferred_element_type=F32) * scale
            m = jnp.max(sc, axis=1, keepdims=True)
            e = jnp.exp(sc - m)
            p = e / jnp.sum(e, axis=1, keepdims=True)
            dp = lax.dot_general(doh, vh, NT, preferred_element_type=F32)
            ds = (p * (dp - jnp.sum(p * dp, axis=1, keepdims=True)) * scale).astype(BF16)
            dq_ref[:, sl] = jnp.dot(ds, kh, preferred_element_type=F32).astype(dq_ref.dtype)
            dk_ref[:, sl] += lax.dot_general(ds, qh, TN, preferred_element_type=F32)
            dv_ref[:, sl] += lax.dot_general(p.astype(BF16), doh, TN, preferred_element_type=F32)

    row = pl.BlockSpec((tq, w), lambda i: (i, 0))
    acc = pl.BlockSpec((mlen, w), lambda i: (0, 0))
    return pl.pallas_call(
        body, name="xattn_bwd", grid=(s // tq,),
        in_specs=[row, acc, pl.BlockSpec((mlen, w), lambda i: (0, 1)), row],
        out_specs=[row, acc, acc],
        out_shape=[jax.ShapeDtypeStruct((s, w), BF16), jax.ShapeDtypeStruct((mlen, w), F32),
                   jax.ShapeDtypeStruct((mlen, w), F32)],
        compiler_params=_cp(("arbitrary",)),
    )(q, kv, kv, do)


def _adamw(parts, w, m, v, name):
    r, c = w.shape
    br = _tile(r, tuple(p for p in (256, 128, 64, 32, 16, 8) if p * c * 4 <= 1024 * 1024))

    def body(p_ref, w_ref, m_ref, v_ref, g_ref, d_ref, nm_ref, nv_ref):
        g = p_ref[0].astype(F32)
        for t in range(1, N_DEV):
            g = g + p_ref[t].astype(F32)
        nm = ADAM_B1 * m_ref[...] + (1.0 - ADAM_B1) * g
        nv = ADAM_B2 * v_ref[...] + (1.0 - ADAM_B2) * (g * g)
        m_hat = nm / (1.0 - ADAM_B1 ** ADAM_STEP)
        v_hat = nv / (1.0 - ADAM_B2 ** ADAM_STEP)
        g_ref[...] = g
        d_ref[...] = -ADAM_LR * (m_hat / (jnp.sqrt(v_hat) + ADAM_EPS) + ADAM_WD * w_ref[...])
        nm_ref[...] = nm
        nv_ref[...] = nv

    blk = pl.BlockSpec((br, c), lambda i: (i, 0))
    return pl.pallas_call(
        body, name=name, grid=(r // br,),
        in_specs=[pl.BlockSpec((N_DEV, br, c), lambda i: (0, i, 0)), blk, blk, blk], out_specs=[blk] * 4,
        out_shape=[jax.ShapeDtypeStruct((r, c), F32)] * 4, compiler_params=_cp(("parallel",)),
    )(parts, w, m, v)


MESH_ID = pl.DeviceIdType.MESH
ANY = pl.BlockSpec(memory_space=pl.ANY)


def _my_place():
    return lax.axis_index("x"), lax.axis_index("y"), lax.axis_index("c")


def _all_gather(xs, name):
    nt = len(xs)

    def body(*refs):
        x_refs, out_refs = refs[:nt], refs[nt:2 * nt]
        send_sems, recv_sems, local_sems = refs[2 * nt:]
        x, y, c = _my_place()
        me, sibling = (x, y, c), (x, y, 1 - c)
        chips = [(1 - x, y), (x, 1 - y), (1 - x, 1 - y)]

        def slot(t, p):
            return out_refs[t].at[4 * p[0] + 2 * p[1] + p[2]]

        def copy(t, k, block, to, src=None):
            return pltpu.make_async_remote_copy(
                src_ref=slot(t, block) if src is None else src, dst_ref=slot(t, block),
                send_sem=send_sems.at[7 * t + k], recv_sem=recv_sems.at[7 * t + k], device_id=to,
                device_id_type=MESH_ID)

        mine, first, passed = [], [], []
        for t in range(nt):
            cp = pltpu.make_async_copy(x_refs[t], slot(t, me), local_sems.at[t])
            cp.start()
            mine.append(cp)
            group = [copy(t, 0, me, sibling, src=x_refs[t])]
            group += [copy(t, 1 + j, me, (*chip, c), src=x_refs[t]) for j, chip in enumerate(chips)]
            for cp in group:
                cp.start()
            first += group
        for t in range(nt):
            for j, chip in enumerate(chips):
                copy(t, 1 + j, (*chip, c), me).wait_recv()
                fw = copy(t, 4 + j, (*chip, c), sibling)
                fw.start()
                passed.append(fw)
        for t in range(nt):
            copy(t, 0, sibling, me).wait_recv()
            for j, chip in enumerate(chips):
                copy(t, 4 + j, (*chip, 1 - c), me).wait_recv()
        for cp in first + passed:
            cp.wait_send()
        for cp in mine:
            cp.wait()

    return pl.pallas_call(
        body, name=name, in_specs=[ANY] * nt, out_specs=[ANY] * nt,
        out_shape=[jax.ShapeDtypeStruct((N_DEV,) + tuple(v.shape), v.dtype) for v in xs],
        scratch_shapes=[pltpu.SemaphoreType.DMA((7 * nt,)), pltpu.SemaphoreType.DMA((7 * nt,)),
                        pltpu.SemaphoreType.DMA((nt,))],
    )(*xs)


def _exchange(gs, name):
    nt = len(gs)

    def body(*refs):
        g_refs, out_refs = refs[:nt], refs[nt:2 * nt]
        send_sems, recv_sems, local_sems = refs[2 * nt:]
        x, y, c = _my_place()
        my_slot = 4 * x + 2 * y + c
        mine, sent = [], []
        for t in range(nt):
            cp = pltpu.make_async_copy(g_refs[t].at[my_slot], out_refs[t].at[my_slot], local_sems.at[t])
            cp.start()
            mine.append(cp)
            for rel in range(1, N_DEV):
                px, py, pc = x ^ ((rel >> 2) & 1), y ^ ((rel >> 1) & 1), c ^ (rel & 1)
                cp = pltpu.make_async_remote_copy(
                    src_ref=g_refs[t].at[4 * px + 2 * py + pc], dst_ref=out_refs[t].at[my_slot],
                    send_sem=send_sems.at[7 * t + rel - 1], recv_sem=recv_sems.at[7 * t + rel - 1],
                    device_id=(px, py, pc), device_id_type=MESH_ID)
                cp.start()
                sent.append(cp)
        for cp in sent:
            cp.wait_recv()
        for cp in sent:
            cp.wait_send()
        for cp in mine:
            cp.wait()

    return pl.pallas_call(
        body, name=name, in_specs=[ANY] * nt, out_specs=[ANY] * nt,
        out_shape=[jax.ShapeDtypeStruct(tuple(v.shape), v.dtype) for v in gs],
        scratch_shapes=[pltpu.SemaphoreType.DMA((7 * nt,)), pltpu.SemaphoreType.DMA((7 * nt,)),
                        pltpu.SemaphoreType.DMA((nt,))],
    )(*gs)


def _ffn_fwd(x, g, w_gu, w_d, tag):
    n = _rms_fwd(x, g)
    gu = _mm(n, w_gu, out_dtype=BF16, name=f"{tag}_gu")
    a = _swiglu_fwd(gu)
    return _mm(a, w_d, alpha=0.5, res=x, name=f"{tag}_down"), (n, gu, a)


def _ffn_bwd(dy, x, g, w_gu, w_d, saved, tag):
    n, gu, a = saved
    da = _mm(dy, w_d, tb=True, alpha=0.5, out_dtype=BF16, name=f"{tag}_da")
    d_wd = _mm(a, dy, ta=True, alpha=0.5, out_dtype=BF16, name=f"{tag}_dwd")
    dgu = _swiglu_bwd(gu, da)
    dn = _mm(dgu, w_gu, tb=True, name=f"{tag}_dn")
    d_wgu = _mm(n, dgu, ta=True, out_dtype=BF16, name=f"{tag}_dwgu")
    dx, dg = _rms_bwd(x, g, dn, dy)
    return dx, dg, d_wgu, d_wd


def _sb_mixer_fwd(h, w_qkv, w_o, x):
    n_heads = w_o.shape[0] // LANES
    qkv = _mm(h, w_qkv, out_dtype=BF16, name="sb_qkv")
    o, ltot = _sb_fwd(qkv, n_heads)
    return _mm(o, w_o, res=x, name="sb_out"), (qkv, o, ltot)


def _sb_mixer_bwd(dy, h, w_qkv, w_o, saved):
    qkv, o, ltot = saved
    n_heads = w_o.shape[0] // LANES
    do = _mm(dy, w_o, tb=True, out_dtype=BF16, name="sb_do")
    d_wo = _mm(o, dy, ta=True, out_dtype=BF16, name="sb_dwo")
    dq, dk, dv = _sb_bwd(qkv, do, ltot, n_heads)
    dqkv = jnp.concatenate([dq, dk.astype(BF16), dv.astype(BF16)], axis=1)
    dh = _mm(dqkv, w_qkv, tb=True, name="sb_dh")
    d_wqkv = _mm(h, dqkv, ta=True, out_dtype=BF16, name="sb_dwqkv")
    return dh, d_wqkv, d_wo


def _dil_cols(n_all, hg, gi):
    return (lambda n, h: n * 2 * n_all + gi * hg + h, lambda n, h: n * 2 * n_all + n_all + gi * hg + h,
            lambda n, h: n * 3 * n_all + 2 * n_all + gi * hg + h)


def _dil_mixer_fwd(h, w_qkv, w_o, x, tabs):
    s = h.shape[0]
    n_all = w_o.shape[0] // LANES
    hg = n_all // len(DIL_PATTERNS)
    qkv = _mm(h, w_qkv, name="dil_qkv")
    qk = _rope(qkv, tabs[0], tabs[1], 2 * n_all, 16, "dil_rope")
    os_, lses = [], []
    for gi, (window, dil) in enumerate(DIL_PATTERNS):
        q_col, k_col, v_col = _dil_cols(n_all, hg, gi)
        o, lse = _band_fwd(qk.reshape(s // dil, -1), qk.reshape(s // dil, -1), qkv.reshape(s // dil, -1),
                           n_cls=dil, n_heads=hg, group=1, q_col=q_col, k_col=k_col, v_col=v_col,
                           max_dist=window // dil, scale=LANES ** -0.5, sinks=None, name=f"dil_fwd{gi}")
        os_.append(o.reshape(s, hg * LANES))
        lses.append(lse.reshape(s, hg * LANES))
    mixed = _dil_mix_fwd(os_, lses)
    return _mm(mixed, w_o, res=x, name="dil_out"), (qkv, qk, os_, lses, mixed)


def _dil_mixer_bwd(dy, h, w_qkv, w_o, saved, tabs_bwd):
    qkv, qk, os_, lses, mixed = saved
    s = h.shape[0]
    n_all = w_o.shape[0] // LANES
    hg = n_all // len(DIL_PATTERNS)
    dmixed = _mm(dy, w_o, tb=True, out_dtype=BF16, name="dil_dmix")
    d_wo = _mm(mixed, dy, ta=True, out_dtype=BF16, name="dil_dwo")
    mix_out = _dil_mix_bwd(os_, lses, dmixed)
    dos, dlses = mix_out[:3], mix_out[3:]
    dqs, dks, dvs = [], [], []
    for gi, (window, dil) in enumerate(DIL_PATTERNS):
        q_col, k_col, v_col = _dil_cols(n_all, hg, gi)
        length = s // dil
        dq, dkc, dkp, dvc, dvp = _band_bwd(
            qk.reshape(length, -1), qk.reshape(length, -1), qkv.reshape(length, -1), os_[gi].reshape(length, -1),
            dos[gi].reshape(length, -1), lses[gi].reshape(length, -1), dlses[gi].reshape(length, -1),
            n_cls=dil, n_heads=hg, group=1, q_col=q_col, k_col=k_col, v_col=v_col, max_dist=window // dil,
            scale=LANES ** -0.5, sinks=None, name=f"dil_bwd{gi}")
        dqs.append(dq.reshape(s, -1))
        dks.append(_band_fold(dkc, dkp, n_cls=dil, n_heads=hg, group=1, name=f"dil_foldk{gi}").reshape(s, -1))
        dvs.append(_band_fold(dvc, dvp, n_cls=dil, n_heads=hg, group=1, name=f"dil_foldv{gi}").reshape(s, -1))
    dqk_rot = jnp.concatenate(dqs + dks, axis=1)
    dqk = _rope(dqk_rot, tabs_bwd[0], tabs_bwd[1], 2 * n_all, 16, "dil_rope_bwd")
    dqkv = jnp.concatenate([dqk] + [t.astype(BF16) for t in dvs], axis=1)
    dh = _mm(dqkv, w_qkv, tb=True, name="dil_dh")
    d_wqkv = _mm(h, dqkv, ta=True, out_dtype=BF16, name="dil_dwqkv")
    return dh, d_wqkv, d_wo


def _pad_heads(w, axis):
    shape = list(w.shape)
    n = shape[axis] // SWA_HEAD_DIM
    w = w.reshape(shape[:axis] + [n, SWA_HEAD_DIM] + shape[axis + 1:])
    pad = [(0, 0)] * w.ndim
    pad[axis + 1] = (0, LANES - SWA_HEAD_DIM)
    shape[axis] = n * LANES
    return jnp.pad(w, pad).reshape(shape)


def _unpad_heads(w, axis):
    shape = list(w.shape)
    n = shape[axis] // LANES
    w = w.reshape(shape[:axis] + [n, LANES] + shape[axis + 1:])
    w = lax.slice_in_dim(w, 0, SWA_HEAD_DIM, axis=axis + 1)
    shape[axis] = n * SWA_HEAD_DIM
    return w.reshape(shape)


def _swa_mixer_fwd(h, w_qkv_p, b_qkv_p, sinks_b, w_o_p, b_o, x, tabs):
    nq = w_o_p.shape[0] // LANES
    nkv = nq // SWA_GROUP
    qkv = _mm(h, w_qkv_p, bias=b_qkv_p, name="swa_qkv")
    qk = _rope(qkv, tabs[0], tabs[1], nq + nkv, 8, "swa_rope")
    o, lse = _band_fwd(qk, qk, qkv, n_cls=1, n_heads=nq, group=SWA_GROUP, q_col=lambda n, hh: hh,
                       k_col=lambda n, hk: nq + hk, v_col=lambda n, hk: nq + nkv + hk,
                       max_dist=SWA_WINDOW - 1, scale=SWA_HEAD_DIM ** -0.5, sinks=sinks_b, name="swa_fwd")
    return _mm(o, w_o_p, res=x, bias=b_o, name="swa_out"), (qkv, qk, o, lse)


def _swa_mixer_bwd(dy, h, w_qkv_p, sinks_b, w_o_p, saved, tabs_bwd):
    qkv, qk, o, lse = saved
    nq = w_o_p.shape[0] // LANES
    nkv = nq // SWA_GROUP
    do = _mm(dy, w_o_p, tb=True, name="swa_do")
    d_wo_p = _mm(o, dy, ta=True, out_dtype=BF16, name="swa_dwo")
    d_bo = _colsum(dy, "swa_dbo")
    dq, dkc, dkp, dvc, dvp, dsink = _band_bwd(
        qk, qk, qkv, o, do, lse, None, n_cls=1, n_heads=nq, group=SWA_GROUP, q_col=lambda n, hh: hh,
        k_col=lambda n, hk: nq + hk, v_col=lambda n, hk: nq + nkv + hk, max_dist=SWA_WINDOW - 1,
        scale=SWA_HEAD_DIM ** -0.5, sinks=sinks_b, name="swa_bwd")
    dk = _band_fold(dkc, dkp, n_cls=1, n_heads=nq, group=SWA_GROUP, name="swa_foldk")
    dv = _band_fold(dvc, dvp, n_cls=1, n_heads=nq, group=SWA_GROUP, name="swa_foldv")
    dqk = _rope(jnp.concatenate([dq, dk], axis=1), tabs_bwd[0], tabs_bwd[1], nq + nkv, 8, "swa_rope_bwd")
    dqkv = jnp.concatenate([dqk, dv.astype(BF16)], axis=1)
    d_bqkv_p = _colsum(dqkv, "swa_dbqkv")
    dh = _mm(dqkv, w_qkv_p, tb=True, name="swa_dh")
    d_wqkv_p = _mm(h, dqkv, ta=True, out_dtype=BF16, name="swa_dwqkv")
    return dh, d_wqkv_p, d_bqkv_p, dsink, d_wo_p, d_bo


def _xattn_layer_fwd(x, mem, g_x, g_m, w_q, w_kv, w_o):
    hq = _rms_fwd(x, g_x, "rms_fwd")
    hm = _rms_fwd(mem, g_m, "rms_mem_fwd")
    q = _mm(hq, w_q, out_dtype=BF16, name="xa_q")
    kv = _mm(hm, w_kv, out_dtype=BF16, name="xa_kv")
    o = _xattn_fwd(q, kv)
    return _mm(o, w_o, res=x, name="xa_out"), (hq, hm, q, kv, o)


def _xattn_layer_bwd(dy, x, mem, g_x, g_m, w_q, w_kv, w_o, saved):
    hq, hm, q, kv, o = saved
    do = _mm(dy, w_o, tb=True, out_dtype=BF16, name="xa_do")
    d_wo = _mm(o, dy, ta=True, out_dtype=BF16, name="xa_dwo")
    dq, dk, dv = _xattn_bwd(q, kv, do)
    dkv = jnp.concatenate([dk, dv], axis=1).astype(BF16)
    dhq = _mm(dq, w_q, tb=True, name="xa_dhq")
    d_wq = _mm(hq, dq, ta=True, out_dtype=BF16, name="xa_dwq")
    dhm = _mm(dkv, w_kv, tb=True, name="xa_dhm")
    d_wkv = _mm(hm, dkv, ta=True, out_dtype=BF16, name="xa_dwkv")
    dx, dg_x = _rms_bwd(x, g_x, dhq, dy)
    _, dg_m = _rms_bwd(mem, g_m, dhm, None, "rms_mem_bwd")
    return dx, dg_x, dg_m, d_wq, d_wkv, d_wo


def _to_full(gathered, axis):
    t = jnp.moveaxis(gathered, 0, axis)
    shape = list(t.shape)
    return t.reshape(shape[:axis] + [shape[axis] * shape[axis + 1]] + shape[axis + 2:])


def _to_blocks(full, axis):
    shape = list(full.shape)
    t = full.reshape(shape[:axis] + [N_DEV, shape[axis] // N_DEV] + shape[axis + 1:])
    return jnp.moveaxis(t, axis, 0)


SHARD_AXIS = {
    "ffn1_w_gate_up": 2, "ffn1_w_down": 1, "sb_w_qkv": 2, "sb_w_o": 1, "dil_w_qkv": 2, "dil_w_o": 2,
    "swa_w_qkv": 2, "swa_b_qkv": 1, "swa_w_o": 1, "swa_b_o": 1, "xattn_w_q": 1, "xattn_w_kv": 1, "xattn_w_o": 2,
    "ffn2_w_gate_up": 2, "ffn2_w_down": 1,
}
SMALL = ("ffn1_norm", "mix_norm", "xattn_norm", "mem_norm", "ffn2_norm", "final_norm", "swa_sinks")
WEIGHTS = ("ffn1_norm", "ffn1_w_gate_up", "ffn1_w_down", "mix_norm", "sb_w_qkv", "sb_w_o", "dil_w_qkv", "dil_w_o",
           "swa_w_qkv", "swa_b_qkv", "swa_sinks", "swa_w_o", "swa_b_o", "xattn_norm", "mem_norm", "xattn_w_q",
           "xattn_w_kv", "xattn_w_o", "ffn2_norm", "ffn2_w_gate_up", "ffn2_w_down", "final_norm")


def _flat2(a):
    return a.reshape(-1, a.shape[-1])


def _pack_small(vals, d):
    rows = [vals[n].reshape(-1, d) for n in SMALL[:5]] + [vals["final_norm"].reshape(1, d)]
    sk = vals["swa_sinks"].reshape(1, -1)
    rows.append(jnp.pad(sk, ((0, 0), (0, d - sk.shape[1]))))
    rows.append(jnp.zeros((2, d), F32))
    return jnp.concatenate(rows, axis=0)


def _unpack_small(packed, like):
    out, r = {}, 0
    for n in SMALL[:5]:
        k = like[n].shape[0]
        out[n] = packed[r:r + k]
        r += k
    out["final_norm"] = packed[r]
    out["swa_sinks"] = packed[r + 1:r + 2, :like["swa_sinks"].shape[1]]
    return out


def _local_step(x0, mem0, positions, target, full, norm):
    d = x0.shape[1]
    names = list(SHARD_AXIS)
    swa_w_qkv_p = _pad_heads(full["swa_w_qkv"][0], 1)
    swa_b_qkv_p = _pad_heads(full["swa_b_qkv"], 1)
    swa_w_o_p = _pad_heads(full["swa_w_o"][0], 0)
    swa_b_o = full["swa_b_o"]
    sinks_b = jnp.repeat(norm["swa_sinks"], LANES, axis=1)
    tabs_dil, tabs_dil_bwd = _rope_tables(positions, 32), _rope_tables(positions, 32, -1.0)
    tabs_swa, tabs_swa_bwd = _rope_tables(positions, 16), _rope_tables(positions, 16, -1.0)

    def vec(name, i):
        return norm[name][i:i + 1]

    saved = []
    xc = x0
    for i in range(DEPTH):
        kind, j = i % 3, i // 3
        rec = {"x0": xc}
        xc, rec["ffn1"] = _ffn_fwd(xc, vec("ffn1_norm", i), full["ffn1_w_gate_up"][i], full["ffn1_w_down"][i], "ffn")
        rec["x1"] = xc
        h = _rms_fwd(xc, vec("mix_norm", i))
        rec["h"] = h
        if kind == 0:
            xc, rec["mix"] = _sb_mixer_fwd(h, full["sb_w_qkv"][j], full["sb_w_o"][j], xc)
        elif kind == 1:
            xc, rec["mix"] = _dil_mixer_fwd(h, full["dil_w_qkv"][j], full["dil_w_o"][j], xc, tabs_dil)
        else:
            xc, rec["mix"] = _swa_mixer_fwd(h, swa_w_qkv_p, swa_b_qkv_p, sinks_b, swa_w_o_p, swa_b_o, xc, tabs_swa)
        rec["x2"] = xc
        xc, rec["xa"] = _xattn_layer_fwd(xc, mem0, vec("xattn_norm", i), vec("mem_norm", i), full["xattn_w_q"][i],
                                         full["xattn_w_kv"][i], full["xattn_w_o"][i])
        rec["x3"] = xc
        xc, rec["ffn2"] = _ffn_fwd(xc, vec("ffn2_norm", i), full["ffn2_w_gate_up"][i], full["ffn2_w_down"][i], "ffn")
        saved.append(rec)

    loss_part, dx, dg_final = _loss_head(xc, norm["final_norm"].reshape(1, d), target)

    gfull = {n: [None] * full[n].shape[0] for n in names}
    gsmall = {n: [None] * DEPTH for n in SMALL[:5]}
    gsmall["final_norm"] = dg_final
    gsmall["swa_sinks"] = jnp.zeros_like(norm["swa_sinks"])
    for i in reversed(range(DEPTH)):
        kind, j = i % 3, i // 3
        rec = saved[i]
        dx, gsmall["ffn2_norm"][i], gfull["ffn2_w_gate_up"][i], gfull["ffn2_w_down"][i] = _ffn_bwd(
            dx, rec["x3"], vec("ffn2_norm", i), full["ffn2_w_gate_up"][i], full["ffn2_w_down"][i], rec["ffn2"], "ffn")
        (dx, gsmall["xattn_norm"][i], gsmall["mem_norm"][i], gfull["xattn_w_q"][i], gfull["xattn_w_kv"][i],
         gfull["xattn_w_o"][i]) = _xattn_layer_bwd(dx, rec["x2"], mem0, vec("xattn_norm", i), vec("mem_norm", i),
                                                   full["xattn_w_q"][i], full["xattn_w_kv"][i], full["xattn_w_o"][i],
                                                   rec["xa"])
        if kind == 0:
            dh, gfull["sb_w_qkv"][j], gfull["sb_w_o"][j] = _sb_mixer_bwd(
                dx, rec["h"], full["sb_w_qkv"][j], full["sb_w_o"][j], rec["mix"])
        elif kind == 1:
            dh, gfull["dil_w_qkv"][j], gfull["dil_w_o"][j] = _dil_mixer_bwd(
                dx, rec["h"], full["dil_w_qkv"][j], full["dil_w_o"][j], rec["mix"], tabs_dil_bwd)
        else:
            dh, d_wqkv_p, d_bqkv_p, dsink, d_wo_p, d_bo = _swa_mixer_bwd(
                dx, rec["h"], swa_w_qkv_p, sinks_b, swa_w_o_p, rec["mix"], tabs_swa_bwd)
            gfull["swa_w_qkv"][j] = _unpad_heads(d_wqkv_p, 1)
            gfull["swa_b_qkv"][j] = _unpad_heads(d_bqkv_p, 1)[0]
            gfull["swa_w_o"][j] = _unpad_heads(d_wo_p, 0)
            gfull["swa_b_o"][j] = d_bo[0]
            gsmall["swa_sinks"] = dsink.reshape(1, -1, LANES)[:, :, 0]
        dx, gsmall["mix_norm"][i] = _rms_bwd(rec["x1"], vec("mix_norm", i), dh, dx)
        dx, gsmall["ffn1_norm"][i], gfull["ffn1_w_gate_up"][i], gfull["ffn1_w_down"][i] = _ffn_bwd(
            dx, rec["x0"], vec("ffn1_norm", i), full["ffn1_w_gate_up"][i], full["ffn1_w_down"][i], rec["ffn1"], "ffn")
    for n in SMALL[:5]:
        gsmall[n] = jnp.concatenate(gsmall[n], axis=0)
    return loss_part[0, 0], dx, gfull, gsmall


def _train_step(x, mem, positions, loss_target, w, m, v):
    d = x.shape[2]
    names = list(SHARD_AXIS)
    norm = {n: w[n] for n in SMALL}

    local = [_flat2(w[n].astype(BF16) if w[n].ndim == 3 else w[n]) for n in names]
    gathered = _all_gather(local, "gather_weights")
    full = {n: _to_full(g.reshape((N_DEV,) + w[n].shape), SHARD_AXIS[n]) for n, g in zip(names, gathered)}

    loss_part, dx, gfull, gsmall = _local_step(x[0], mem[0], positions, loss_target[0], full, norm)
    loss = lax.psum(loss_part, MESH_AXES)
    grad_x = dx[None]

    blocks = [_to_blocks(jnp.stack(gfull[n], axis=0), SHARD_AXIS[n]) for n in names]
    blocks = [b.reshape(N_DEV, -1, b.shape[-1]) for b in blocks]
    received = _exchange(blocks, "exchange_grads")
    grad, delta, new_m, new_v = {}, {}, {}, {}
    for n, parts in zip(names, received):
        outs = _adamw(parts, _flat2(w[n]), _flat2(m[n]), _flat2(v[n]), "adamw")
        grad[n], delta[n], new_m[n], new_v[n] = (o.reshape(w[n].shape) for o in outs)

    small_parts = _all_gather([_pack_small(gsmall, d)], "gather_small_grads")[0]
    outs = _adamw(small_parts, _pack_small(norm, d), _pack_small({n: m[n] for n in SMALL}, d),
                  _pack_small({n: v[n] for n in SMALL}, d), "adamw_small")
    for res, o in zip((grad, delta, new_m, new_v), outs):
        res.update(_unpack_small(o, norm))
    return loss, grad_x, grad, delta, new_m, new_v


def kernel(x, mem, positions, ffn1_norm, ffn1_w_gate_up, ffn1_w_down, mix_norm, sb_w_qkv, sb_w_o, dil_w_qkv, dil_w_o, swa_w_qkv, swa_b_qkv, swa_sinks, swa_w_o, swa_b_o, xattn_norm, mem_norm, xattn_w_q, xattn_w_kv, xattn_w_o, ffn2_norm, ffn2_w_gate_up, ffn2_w_down, final_norm, loss_target, m_ffn1_norm, m_ffn1_w_gate_up, m_ffn1_w_down, m_mix_norm, m_sb_w_qkv, m_sb_w_o, m_dil_w_qkv, m_dil_w_o, m_swa_w_qkv, m_swa_b_qkv, m_swa_sinks, m_swa_w_o, m_swa_b_o, m_xattn_norm, m_mem_norm, m_xattn_w_q, m_xattn_w_kv, m_xattn_w_o, m_ffn2_norm, m_ffn2_w_gate_up, m_ffn2_w_down, m_final_norm, v_ffn1_norm, v_ffn1_w_gate_up, v_ffn1_w_down, v_mix_norm, v_sb_w_qkv, v_sb_w_o, v_dil_w_qkv, v_dil_w_o, v_swa_w_qkv, v_swa_b_qkv, v_swa_sinks, v_swa_w_o, v_swa_b_o, v_xattn_norm, v_mem_norm, v_xattn_w_q, v_xattn_w_kv, v_xattn_w_o, v_ffn2_norm, v_ffn2_w_gate_up, v_ffn2_w_down, v_final_norm):
    args = dict(locals())
    w = {n: args[n] for n in WEIGHTS}
    m = {n: args["m_" + n] for n in WEIGHTS}
    v = {n: args["v_" + n] for n in WEIGHTS}
    loss, grad_x, grad, delta, new_m, new_v = _train_step(x, mem, positions, loss_target, w, m, v)
    return (loss, grad_x, *[grad[n] for n in WEIGHTS], *[delta[n] for n in WEIGHTS],
            *[new_m[n] for n in WEIGHTS], *[new_v[n] for n in WEIGHTS])
```

```python
import jax
import jax.numpy as jnp
from jax import lax
from jax.experimental import pallas as pl
from jax.experimental.pallas import tpu as pltpu

F32 = jnp.float32
BF16 = jnp.bfloat16

N_DEV = 8
MESH_AXES = ("x", "y", "c")
LANES = 128
BAND = 128
NORM_EPS = 1e-6
ROPE_THETA = 500000.0
DIL_PATTERNS = ((128, 1), (512, 4), (2048, 16))
SWA_HEAD_DIM = 64
SWA_GROUP = 8
SWA_WINDOW = 128
XA_HEADS = 4
DEPTH = 4
ADAM_LR, ADAM_B1, ADAM_B2, ADAM_EPS, ADAM_WD, ADAM_STEP = 0.001, 0.9, 0.999, 1e-08, 0.01, 10
VMEM_LIMIT = 56 * 1024 * 1024
NEG = -1e30

NT = (((1,), (1,)), ((), ()))
TN = (((0,), (0,)), ((), ()))


def _tile(n, prefs):
    for p in prefs:
        if n % p == 0:
            return p
    return n


def _cp(sem):
    return pltpu.CompilerParams(dimension_semantics=sem, vmem_limit_bytes=VMEM_LIMIT)


def _mm(a, b, *, ta=False, tb=False, out_dtype=F32, alpha=1.0, res=None, bias=None, name, ex=None):
    kdim, m = a.shape if ta else a.shape[::-1]
    kdim2, n = b.shape[::-1] if tb else b.shape
    assert kdim == kdim2, (a.shape, b.shape, ta, tb)
    bm = _tile(m, (1024, 512, 256, 128))
    bn = _tile(n, (1024, 512, 256, 128))
    bk = _tile(kdim, (1024, 512, 256, 128))
    nk = kdim // bk
    grid = (m // bm, n // bn, nk)
    dn = (((0 if ta else 1,), (1 if tb else 0,)), ((), ()))
    has_res, has_bias = res is not None, bias is not None
    jobs = ex.take(2.0 * m * n * kdim * EXCHANGE_BYTES_PER_FLOP) if ex is not None else []
    nj = len(jobs)
    n_in = 2 + has_res + has_bias

    def body(*refs):
        a_ref, b_ref = refs[0], refs[1]
        res_ref = refs[2] if has_res else None
        bias_ref = refs[2 + has_res] if has_bias else None
        o_ref, acc_ref = refs[n_in + nj], refs[n_in + 2 * nj + 1]
        k = pl.program_id(2)
        if nj:
            job_refs = (refs[n_in:n_in + nj], refs[n_in + nj + 1:n_in + 2 * nj + 1], refs[n_in + 2 * nj + 2:])
            ids = [pl.program_id(t) for t in range(3)]

            @pl.when((ids[0] == 0) & (ids[1] == 0) & (ids[2] == 0))
            def _():
                _pieces_start(jobs, *job_refs)

            @pl.when((ids[0] == grid[0] - 1) & (ids[1] == grid[1] - 1) & (ids[2] == grid[2] - 1))
            def _():
                _pieces_wait(jobs, *job_refs)

        @pl.when(k == 0)
        def _():
            acc_ref[...] = jnp.zeros_like(acc_ref)

        acc_ref[...] += lax.dot_general(a_ref[...].astype(BF16), b_ref[...].astype(BF16), dn,
                                        preferred_element_type=F32)

        @pl.when(k == nk - 1)
        def _():
            r = acc_ref[...]
            if alpha != 1.0:
                r = r * alpha
            if has_bias:
                r = r + bias_ref[...]
            if has_res:
                r = r + res_ref[...]
            o_ref[...] = r.astype(o_ref.dtype)

    a_spec = pl.BlockSpec((bk, bm), lambda i, j, k: (k, i)) if ta else pl.BlockSpec((bm, bk), lambda i, j, k: (i, k))
    b_spec = pl.BlockSpec((bn, bk), lambda i, j, k: (j, k)) if tb else pl.BlockSpec((bk, bn), lambda i, j, k: (k, j))
    ins, specs = [a, b], [a_spec, b_spec]
    if has_res:
        ins.append(res)
        specs.append(pl.BlockSpec((bm, bn), lambda i, j, k: (i, j)))
    if has_bias:
        ins.append(bias)
        specs.append(pl.BlockSpec((1, bn), lambda i, j, k: (0, j)))
    out_spec = pl.BlockSpec((bm, bn), lambda i, j, k: (i, j))
    out_shape = jax.ShapeDtypeStruct((m, n), out_dtype)
    scratch = [pltpu.VMEM((bm, bn), F32)]
    if not nj:
        return pl.pallas_call(
            body, name=name, grid=grid, in_specs=specs, out_specs=out_spec, out_shape=out_shape,
            scratch_shapes=scratch, compiler_params=_cp(("parallel", "parallel", "arbitrary")),
        )(*ins)
    outs = pl.pallas_call(
        body, name=name + "_carry", grid=grid, in_specs=specs + [ANY] * nj, out_specs=[out_spec] + [ANY] * nj,
        out_shape=[out_shape] + [j.recv_shape for j in jobs], scratch_shapes=scratch + _piece_sems(nj),
        compiler_params=_cp(("arbitrary", "arbitrary", "arbitrary")),
    )(*ins, *[j.g for j in jobs])
    ex.landed(jobs, outs[1:])
    return outs[0]


def _rms_fwd(x, g, name="rms_fwd"):
    s, d = x.shape
    bm = _tile(s, (256, 128, 8))

    def body(x_ref, g_ref, o_ref):
        xv = x_ref[...]
        r = lax.rsqrt(jnp.mean(xv * xv, axis=-1, keepdims=True) + NORM_EPS)
        o_ref[...] = (xv * r * g_ref[...]).astype(o_ref.dtype)

    return pl.pallas_call(
        body, name=name, grid=(s // bm,),
        in_specs=[pl.BlockSpec((bm, d), lambda i: (i, 0)), pl.BlockSpec((1, d), lambda i: (0, 0))],
        out_specs=pl.BlockSpec((bm, d), lambda i: (i, 0)),
        out_shape=jax.ShapeDtypeStruct((s, d), BF16), compiler_params=_cp(("parallel",)),
    )(x, g)


def _rms_bwd(x, g, dn, dy=None, name="rms_bwd"):
    s, d = x.shape
    bm = _tile(s, (256, 128, 8))
    has_dy = dy is not None

    def body(*refs):
        x_ref, g_ref, dn_ref = refs[:3]
        dy_ref = refs[3] if has_dy else None
        dx_ref, dg_ref = refs[-2], refs[-1]

        @pl.when(pl.program_id(0) == 0)
        def _():
            dg_ref[...] = jnp.zeros_like(dg_ref)

        xv = x_ref[...]
        r = lax.rsqrt(jnp.mean(xv * xv, axis=-1, keepdims=True) + NORM_EPS)
        xh = xv * r
        dnv = dn_ref[...].astype(F32)
        dxh = dnv * g_ref[...]
        dx = r * (dxh - xh * jnp.mean(dxh * xh, axis=-1, keepdims=True))
        if has_dy:
            dx = dx + dy_ref[...]
        dx_ref[...] = dx
        dg_ref[...] += jnp.sum(dnv * xh, axis=0, keepdims=True)

    row = pl.BlockSpec((bm, d), lambda i: (i, 0))
    vec = pl.BlockSpec((1, d), lambda i: (0, 0))
    ins, specs = [x, g, dn], [row, vec, row]
    if has_dy:
        ins.append(dy)
        specs.append(row)
    return pl.pallas_call(
        body, name=name, grid=(s // bm,), in_specs=specs, out_specs=[row, vec],
        out_shape=[jax.ShapeDtypeStruct((s, d), F32), jax.ShapeDtypeStruct((1, d), F32)],
        compiler_params=_cp(("arbitrary",)),
    )(*ins)


def _loss_head(x, g, target):
    s, d = x.shape
    bm = _tile(s, (256, 128, 8))

    def body(x_ref, g_ref, t_ref, loss_ref, dx_ref, dg_ref):
        @pl.when(pl.program_id(0) == 0)
        def _():
            dg_ref[...] = jnp.zeros_like(dg_ref)
            loss_ref[...] = jnp.zeros_like(loss_ref)

        xv = x_ref[...]
        gv = g_ref[...]
        r = lax.rsqrt(jnp.mean(xv * xv, axis=-1, keepdims=True) + NORM_EPS)
        xh = xv * r
        err = xh * gv - t_ref[...]
        part = 0.5 * jnp.sum(jnp.mean(err * err, axis=-1, keepdims=True), axis=0, keepdims=True)
        loss_ref[...] += jnp.broadcast_to(part, loss_ref.shape)
        dyv = err * (1.0 / d)
        dxh = dyv * gv
        dx_ref[...] = r * (dxh - xh * jnp.mean(dxh * xh, axis=-1, keepdims=True))
        dg_ref[...] += jnp.sum(dyv * xh, axis=0, keepdims=True)

    row = pl.BlockSpec((bm, d), lambda i: (i, 0))
    vec = pl.BlockSpec((1, d), lambda i: (0, 0))
    return pl.pallas_call(
        body, name="loss_head", grid=(s // bm,), in_specs=[row, vec, row],
        out_specs=[pl.BlockSpec((1, LANES), lambda i: (0, 0)), row, vec],
        out_shape=[jax.ShapeDtypeStruct((1, LANES), F32), jax.ShapeDtypeStruct((s, d), F32),
                   jax.ShapeDtypeStruct((1, d), F32)],
        compiler_params=_cp(("arbitrary",)),
    )(x, g, target)


def _colsum(x, name="colsum"):
    s, n = x.shape
    bm = _tile(s, (256, 128, 8))

    def body(x_ref, o_ref):
        @pl.when(pl.program_id(0) == 0)
        def _():
            o_ref[...] = jnp.zeros_like(o_ref)

        o_ref[...] += jnp.sum(x_ref[...].astype(F32), axis=0, keepdims=True)

    return pl.pallas_call(
        body, name=name, grid=(s // bm,), in_specs=[pl.BlockSpec((bm, n), lambda i: (i, 0))],
        out_specs=pl.BlockSpec((1, n), lambda i: (0, 0)), out_shape=jax.ShapeDtypeStruct((1, n), F32),
        compiler_params=_cp(("arbitrary",)),
    )(x)


def _swiglu_fwd(gu):
    s, f2 = gu.shape
    f = f2 // 2
    bm, bf = _tile(s, (512, 256, 128, 8)), _tile(f, (512, 256, 128))
    nf = f // bf

    def body(g_ref, u_ref, o_ref):
        gv = g_ref[...].astype(F32)
        o_ref[...] = (gv / (1.0 + jnp.exp(-gv)) * u_ref[...].astype(F32)).astype(o_ref.dtype)

    return pl.pallas_call(
        body, name="swiglu_fwd", grid=(s // bm, nf),
        in_specs=[pl.BlockSpec((bm, bf), lambda i, j: (i, j)), pl.BlockSpec((bm, bf), lambda i, j: (i, j + nf))],
        out_specs=pl.BlockSpec((bm, bf), lambda i, j: (i, j)),
        out_shape=jax.ShapeDtypeStruct((s, f), BF16), compiler_params=_cp(("parallel", "parallel")),
    )(gu, gu)


def _swiglu_bwd(gu, da):
    s, f2 = gu.shape
    f = f2 // 2
    bm, bf = _tile(s, (512, 256, 128, 8)), _tile(f, (512, 256, 128))
    nf = f // bf

    def body(g_ref, u_ref, da_ref, o_ref):
        t = pl.program_id(2)
        gv = g_ref[...].astype(F32)
        dav = da_ref[...].astype(F32)
        sig = 1.0 / (1.0 + jnp.exp(-gv))

        @pl.when(t == 0)
        def _():
            o_ref[...] = (dav * u_ref[...].astype(F32) * (sig * (1.0 + gv * (1.0 - sig)))).astype(o_ref.dtype)

        @pl.when(t == 1)
        def _():
            o_ref[...] = (dav * gv * sig).astype(o_ref.dtype)

    return pl.pallas_call(
        body, name="swiglu_bwd", grid=(s // bm, nf, 2),
        in_specs=[pl.BlockSpec((bm, bf), lambda i, j, t: (i, j)), pl.BlockSpec((bm, bf), lambda i, j, t: (i, j + nf)),
                  pl.BlockSpec((bm, bf), lambda i, j, t: (i, j))],
        out_specs=pl.BlockSpec((bm, bf), lambda i, j, t: (i, j + t * nf)),
        out_shape=jax.ShapeDtypeStruct((s, f2), BF16), compiler_params=_cp(("parallel", "parallel", "arbitrary")),
    )(gu, gu, da)


def _rope_tables(positions, rot, sign=1.0):
    half = rot // 2
    inv_freq = jnp.power(F32(ROPE_THETA), -jnp.arange(half, dtype=F32) * 2.0 / rot)
    ang = positions.reshape(-1).astype(F32)[:, None] * inv_freq
    cos, sin = jnp.cos(ang), jnp.sin(ang) * sign
    s = ang.shape[0]
    c_tab = jnp.concatenate([cos, cos, jnp.ones((s, LANES - rot), F32)], axis=1)
    s_tab = jnp.concatenate([-sin, sin, jnp.zeros((s, LANES - rot), F32)], axis=1)
    return c_tab, s_tab


def _rope(x, c_tab, s_tab, nblk, half, name):
    s = x.shape[0]
    bm = _tile(s, (512, 256, 128, 8))

    def body(x_ref, c_ref, s_ref, o_ref):
        xv = x_ref[...].astype(F32)
        lane = lax.broadcasted_iota(jnp.int32, xv.shape, 1)
        sw = jnp.where(lane < half, pltpu.roll(xv, LANES - half, 1), pltpu.roll(xv, half, 1))
        o_ref[...] = (xv * c_ref[...] + sw * s_ref[...]).astype(o_ref.dtype)

    blk = pl.BlockSpec((bm, LANES), lambda i, j: (i, j))
    tab = pl.BlockSpec((bm, LANES), lambda i, j: (i, 0))
    return pl.pallas_call(
        body, name=name, grid=(s // bm, nblk), in_specs=[blk, tab, tab], out_specs=blk,
        out_shape=jax.ShapeDtypeStruct((s, nblk * LANES), BF16), compiler_params=_cp(("parallel", "parallel")),
    )(x, c_tab, s_tab)


def _split_dot(x, t):
    hi = x.astype(BF16)
    lo = (x - hi.astype(F32)).astype(BF16)
    return jnp.dot(hi, t, preferred_element_type=F32) + jnp.dot(lo, t, preferred_element_type=F32)


def _sb_terms(q, kb, scale):
    z = lax.dot_general(q, kb, NT, preferred_element_type=F32) * scale
    u = jnp.log(1.0 + jnp.exp(-jnp.abs(z)))
    return jnp.minimum(-z, 0.0) - u, jnp.minimum(z, 0.0) - u


def _sb_heads_per_step(n_heads):
    return 2 if n_heads % 2 == 0 else 1


def _sb_fwd(qkv, n_heads):
    s = qkv.shape[0]
    tq = _tile(s, (256, 128))
    hp = _sb_heads_per_step(n_heads)
    w = hp * LANES
    scale = LANES ** -0.5

    def body(q_ref, k_ref, v_ref, o_ref, lt_ref):
        i = pl.program_id(1)
        row = lax.broadcasted_iota(jnp.int32, (tq, tq), 0)
        col = lax.broadcasted_iota(jnp.int32, (tq, tq), 1)
        below = col < row
        later = (row > col).astype(BF16)
        qs = [q_ref[:, h * LANES:(h + 1) * LANES] for h in range(hp)]

        def block(h, k0, c, acc, diag):
            sl = slice(h * LANES, (h + 1) * LANES)
            lk, logsig = _sb_terms(qs[h], k_ref[pl.ds(k0, tq), sl], scale)
            if diag:
                lk = jnp.where(below, lk, 0.0)
            a = jnp.exp(logsig + _split_dot(lk, later) + c)
            if diag:
                a = jnp.where(below, a, 0.0)
            acc = acc + jnp.dot(a.astype(BF16), v_ref[pl.ds(k0, tq), sl], preferred_element_type=F32)
            return c + jnp.sum(lk, axis=1, keepdims=True), acc

        d0 = pl.multiple_of(i * tq, tq)
        carry = []
        for h in range(hp):
            carry += list(block(h, d0, jnp.zeros((tq, 1), F32), jnp.zeros((tq, LANES), F32), True))

        def step(t, carry):
            k0 = pl.multiple_of((i - 1 - t) * tq, tq)
            out = []
            for h in range(hp):
                out += list(block(h, k0, carry[2 * h], carry[2 * h + 1], False))
            return tuple(out)

        carry = lax.fori_loop(0, i, step, tuple(carry))
        for h in range(hp):
            sl = slice(h * LANES, (h + 1) * LANES)
            o_ref[:, sl] = carry[2 * h + 1].astype(o_ref.dtype)
            lt_ref[:, sl] = jnp.broadcast_to(carry[2 * h], (tq, LANES))

    ng = n_heads // hp
    blk = pl.BlockSpec((tq, w), lambda g, i: (i, g))
    return pl.pallas_call(
        body, name="sb_fwd", grid=(ng, s // tq),
        in_specs=[blk, pl.BlockSpec((s, w), lambda g, i: (0, ng + g)), pl.BlockSpec((s, w), lambda g, i: (0, 2 * ng + g))],
        out_specs=[blk, blk],
        out_shape=[jax.ShapeDtypeStruct((s, n_heads * LANES), BF16), jax.ShapeDtypeStruct((s, n_heads * LANES), F32)],
        compiler_params=_cp(("parallel", "arbitrary")),
    )(qkv, qkv, qkv)


def _sb_bwd(qkv, do, ltot, n_heads, ex=None):
    s = qkv.shape[0]
    tq = _tile(s, (256, 128))
    hp = _sb_heads_per_step(n_heads)
    w = hp * LANES
    ng = n_heads // hp
    scale = LANES ** -0.5
    jobs = ex.take(4 * 9 * n_heads * s * s * LANES * EXCHANGE_BYTES_PER_FLOP) if ex is not None else []
    nj = len(jobs)

    def body(*refs):
        q_ref, k_ref, v_ref, do_ref, lt_ref = refs[:5]
        dq_ref, dk_ref, dv_ref = refs[5 + nj:8 + nj]
        i = pl.program_id(1)
        if nj:
            job_refs = (refs[5:5 + nj], refs[8 + nj:8 + 2 * nj], refs[8 + 2 * nj:])
            gi = pl.program_id(0)
            pl.when((gi == 0) & (i == 0))(lambda: _pieces_start(jobs, *job_refs))
            pl.when((gi == ng - 1) & (i == s // tq - 1))(lambda: _pieces_wait(jobs, *job_refs))

        @pl.when(i == 0)
        def _():
            dk_ref[...] = jnp.zeros_like(dk_ref)
            dv_ref[...] = jnp.zeros_like(dv_ref)

        row = lax.broadcasted_iota(jnp.int32, (tq, tq), 0)
        col = lax.broadcasted_iota(jnp.int32, (tq, tq), 1)
        below = col < row
        upto = (row <= col).astype(BF16)
        before = (row < col).astype(BF16)
        qs = [q_ref[:, h * LANES:(h + 1) * LANES] for h in range(hp)]
        dos = [do_ref[:, h * LANES:(h + 1) * LANES] for h in range(hp)]
        lts = [lt_ref[:, h * LANES:h * LANES + 1] for h in range(hp)]

        def block(h, k0, cpre, ce, dq, diag):
            sl = slice(h * LANES, (h + 1) * LANES)
            kb = k_ref[pl.ds(k0, tq), sl]
            vb = v_ref[pl.ds(k0, tq), sl]
            lk, logsig = _sb_terms(qs[h], kb, scale)
            if diag:
                lk = jnp.where(below, lk, 0.0)
            a = jnp.exp(logsig + (lts[h] - cpre) - _split_dot(lk, upto))
            if diag:
                a = jnp.where(below, a, 0.0)
            e = a * lax.dot_general(dos[h], vb, NT, preferred_element_type=F32)
            e_before = ce + jnp.dot(e.astype(BF16), before, preferred_element_type=F32)
            sig = jnp.exp(logsig)
            dz = (e - sig * (e + e_before)) * scale
            if diag:
                dz = jnp.where(below, dz, 0.0)
            dzb = dz.astype(BF16)
            dq = dq + jnp.dot(dzb, kb, preferred_element_type=F32)
            dk_ref[pl.ds(k0, tq), sl] += lax.dot_general(dzb, qs[h], TN, preferred_element_type=F32)
            dv_ref[pl.ds(k0, tq), sl] += lax.dot_general(a.astype(BF16), dos[h], TN, preferred_element_type=F32)
            return cpre + jnp.sum(lk, axis=1, keepdims=True), ce + jnp.sum(e, axis=1, keepdims=True), dq

        def step(j, carry):
            k0 = pl.multiple_of(j * tq, tq)
            out = []
            for h in range(hp):
                out += list(block(h, k0, *carry[3 * h:3 * h + 3], False))
            return tuple(out)

        z1 = jnp.zeros((tq, 1), F32)
        carry = lax.fori_loop(0, i, step, (z1, z1, jnp.zeros((tq, LANES), F32)) * hp)
        d0 = pl.multiple_of(i * tq, tq)
        for h in range(hp):
            _, _, dq = block(h, d0, *carry[3 * h:3 * h + 3], True)
            dq_ref[:, h * LANES:(h + 1) * LANES] = dq.astype(dq_ref.dtype)

    blk = pl.BlockSpec((tq, w), lambda g, i: (i, g))
    full = pl.BlockSpec((s, w), lambda g, i: (0, g))
    wt = n_heads * LANES
    outs = pl.pallas_call(
        body, name="sb_bwd_carry" if nj else "sb_bwd", grid=(ng, s // tq),
        in_specs=[blk, pl.BlockSpec((s, w), lambda g, i: (0, ng + g)), pl.BlockSpec((s, w), lambda g, i: (0, 2 * ng + g)),
                  blk, blk] + [ANY] * nj,
        out_specs=[blk, full, full] + [ANY] * nj,
        out_shape=[jax.ShapeDtypeStruct((s, wt), BF16), jax.ShapeDtypeStruct((s, wt), F32),
                   jax.ShapeDtypeStruct((s, wt), F32)] + [j.recv_shape for j in jobs],
        scratch_shapes=_piece_sems(nj) if nj else [],
        compiler_params=_cp(("arbitrary", "arbitrary")),
    )(qkv, qkv, qkv, do, ltot, *[j.g for j in jobs])
    if nj:
        ex.landed(jobs, outs[3:])
    return outs[:3]


def _band_masks(b, max_dist):
    qi = lax.broadcasted_iota(jnp.int32, (BAND, BAND), 0)
    kj = lax.broadcasted_iota(jnp.int32, (BAND, BAND), 1)
    dist = qi - kj
    return ((BAND + dist) <= max_dist) & (b > 0), (dist >= 0) & (dist <= max_dist)


def _band_fwd(qa, ka, va, *, n_cls, n_steps, hpb, group, q_blk, k_blk, v_blk, max_dist, scale, sinks, name):
    length = qa.shape[0]
    nb = length // BAND
    has_sink = sinks is not None
    qw, kw = hpb * LANES, (hpb // group) * LANES

    def body(*refs):
        q_ref, kp_ref, kc_ref, vp_ref, vc_ref = refs[:5]
        o_ref, lse_ref = refs[-2], refs[-1]
        mask_p, mask_c = _band_masks(pl.program_id(2), max_dist)
        for hh in range(hpb):
            qs = slice(hh * LANES, (hh + 1) * LANES)
            ks = slice((hh // group) * LANES, (hh // group + 1) * LANES)
            q = q_ref[:, qs].astype(BF16)
            s_p = lax.dot_general(q, kp_ref[:, ks].astype(BF16), NT, preferred_element_type=F32) * scale
            s_c = lax.dot_general(q, kc_ref[:, ks].astype(BF16), NT, preferred_element_type=F32) * scale
            s_p = jnp.where(mask_p, s_p, NEG)
            s_c = jnp.where(mask_c, s_c, NEG)
            m = jnp.maximum(jnp.max(s_p, axis=1, keepdims=True), jnp.max(s_c, axis=1, keepdims=True))
            l = jnp.sum(jnp.exp(s_p - m), axis=1, keepdims=True) + jnp.sum(jnp.exp(s_c - m), axis=1, keepdims=True)
            lse = m + jnp.log(l)
            if has_sink:
                sk = refs[5][:, hh * LANES:hh * LANES + 1]
                lse = jnp.maximum(lse, sk) + jnp.log(1.0 + jnp.exp(-jnp.abs(lse - sk)))
            p_p = jnp.exp(s_p - lse).astype(BF16)
            p_c = jnp.exp(s_c - lse).astype(BF16)
            o_ref[:, qs] = (jnp.dot(p_p, vp_ref[:, ks].astype(BF16), preferred_element_type=F32)
                            + jnp.dot(p_c, vc_ref[:, ks].astype(BF16), preferred_element_type=F32))
            lse_ref[:, qs] = jnp.broadcast_to(lse, (BAND, LANES))

    def prev(b):
        return jnp.maximum(b - 1, 0)

    specs = [pl.BlockSpec((BAND, qw), lambda n, st, b: (b, q_blk(n, st))),
             pl.BlockSpec((BAND, kw), lambda n, st, b: (prev(b), k_blk(n, st))),
             pl.BlockSpec((BAND, kw), lambda n, st, b: (b, k_blk(n, st))),
             pl.BlockSpec((BAND, kw), lambda n, st, b: (prev(b), v_blk(n, st))),
             pl.BlockSpec((BAND, kw), lambda n, st, b: (b, v_blk(n, st)))]
    ins = [qa, ka, ka, va, va]
    if has_sink:
        ins.append(sinks)
        specs.append(pl.BlockSpec((1, qw), lambda n, st, b: (0, st)))
    out = pl.BlockSpec((BAND, qw), lambda n, st, b: (b, n * n_steps + st))
    w = n_cls * n_steps * qw
    return pl.pallas_call(
        body, name=name, grid=(n_cls, n_steps, nb), in_specs=specs, out_specs=[out, out],
        out_shape=[jax.ShapeDtypeStruct((length, w), F32), jax.ShapeDtypeStruct((length, w), F32)],
        compiler_params=_cp(("parallel", "parallel", "parallel")),
    )(*ins)


def _band_bwd(qa, ka, va, o, do, lse, dlse, *, n_cls, n_steps, hpb, group, q_blk, k_blk, v_blk, max_dist, scale, sinks,
              name):
    length = qa.shape[0]
    nb = length // BAND
    has_sink, has_dlse = sinks is not None, dlse is not None
    qw, kw = hpb * LANES, (hpb // group) * LANES

    def body(*refs):
        q_ref, kp_ref, kc_ref, vp_ref, vc_ref, o_ref, do_ref, lse_ref = refs[:8]
        pos = 8
        dlse_ref = refs[pos] if has_dlse else None
        pos += has_dlse
        sink_ref = refs[pos] if has_sink else None
        pos += has_sink
        dq_ref, dkc_ref, dkp_ref, dvc_ref, dvp_ref = refs[pos:pos + 5]
        b = pl.program_id(2)
        mask_p, mask_c = _band_masks(b, max_dist)
        if has_sink:
            dsink_ref = refs[pos + 5]

            @pl.when(b == 0)
            def _():
                dsink_ref[...] = jnp.zeros_like(dsink_ref)

        for hh in range(hpb):
            qs = slice(hh * LANES, (hh + 1) * LANES)
            ks = slice((hh // group) * LANES, (hh // group + 1) * LANES)
            q = q_ref[:, qs].astype(BF16)
            kp, kc = kp_ref[:, ks].astype(BF16), kc_ref[:, ks].astype(BF16)
            vp, vc = vp_ref[:, ks].astype(BF16), vc_ref[:, ks].astype(BF16)
            dov = do_ref[:, qs].astype(F32)
            dob = dov.astype(BF16)
            lse_v = lse_ref[:, hh * LANES:hh * LANES + 1]
            s_p = lax.dot_general(q, kp, NT, preferred_element_type=F32) * scale
            s_c = lax.dot_general(q, kc, NT, preferred_element_type=F32) * scale
            p_p = jnp.where(mask_p, jnp.exp(jnp.where(mask_p, s_p, NEG) - lse_v), 0.0)
            p_c = jnp.where(mask_c, jnp.exp(jnp.where(mask_c, s_c, NEG) - lse_v), 0.0)
            delta = jnp.sum(dov * o_ref[:, qs], axis=1, keepdims=True)
            shift = -delta
            if has_dlse:
                shift = shift + dlse_ref[:, hh * LANES:hh * LANES + 1]
            dp_p = lax.dot_general(dob, vp, NT, preferred_element_type=F32)
            dp_c = lax.dot_general(dob, vc, NT, preferred_element_type=F32)
            ds_p = (p_p * (dp_p + shift) * scale).astype(BF16)
            ds_c = (p_c * (dp_c + shift) * scale).astype(BF16)
            dq_ref[:, qs] = (jnp.dot(ds_p, kp, preferred_element_type=F32)
                             + jnp.dot(ds_c, kc, preferred_element_type=F32))
            parts = (lax.dot_general(ds_c, q, TN, preferred_element_type=F32),
                     lax.dot_general(ds_p, q, TN, preferred_element_type=F32),
                     lax.dot_general(p_c.astype(BF16), dob, TN, preferred_element_type=F32),
                     lax.dot_general(p_p.astype(BF16), dob, TN, preferred_element_type=F32))
            for ref, part in zip((dkc_ref, dkp_ref, dvc_ref, dvp_ref), parts):
                if hh % group == 0:
                    ref[:, ks] = part
                else:
                    ref[:, ks] += part
            if has_sink:
                p_sink = jnp.exp(sink_ref[:, hh * LANES:hh * LANES + 1] - lse_v)
                dsink_ref[:, qs] += jnp.broadcast_to(jnp.sum(-p_sink * delta, axis=0, keepdims=True), (1, LANES))

    def prev(b):
        return jnp.maximum(b - 1, 0)

    per_q = pl.BlockSpec((BAND, qw), lambda n, st, b: (b, n * n_steps + st))
    per_k = pl.BlockSpec((BAND, kw), lambda n, st, b: (b, n * n_steps + st))
    specs = [pl.BlockSpec((BAND, qw), lambda n, st, b: (b, q_blk(n, st))),
             pl.BlockSpec((BAND, kw), lambda n, st, b: (prev(b), k_blk(n, st))),
             pl.BlockSpec((BAND, kw), lambda n, st, b: (b, k_blk(n, st))),
             pl.BlockSpec((BAND, kw), lambda n, st, b: (prev(b), v_blk(n, st))),
             pl.BlockSpec((BAND, kw), lambda n, st, b: (b, v_blk(n, st))),
             per_q, per_q, per_q]
    ins = [qa, ka, ka, va, va, o, do, lse]
    if has_dlse:
        ins.append(dlse)
        specs.append(per_q)
    out_specs = [per_q] + [per_k] * 4
    out_shape = ([jax.ShapeDtypeStruct((length, n_cls * n_steps * qw), F32)]
                 + [jax.ShapeDtypeStruct((length, n_cls * n_steps * kw), F32)] * 4)
    if has_sink:
        ins.append(sinks)
        specs.append(pl.BlockSpec((1, qw), lambda n, st, b: (0, st)))
        out_specs = out_specs + [pl.BlockSpec((1, qw), lambda n, st, b: (0, st))]
        out_shape = out_shape + [jax.ShapeDtypeStruct((1, n_steps * qw), F32)]
    return pl.pallas_call(
        body, name=name, grid=(n_cls, n_steps, nb), in_specs=specs, out_specs=out_specs, out_shape=out_shape,
        compiler_params=_cp(("parallel", "parallel", "arbitrary")),
    )(*ins)


def _band_fold(cur, prv, *, n_cls, n_heads, group, name):
    length = cur.shape[0]
    nb = length // BAND
    n_kv = n_heads // group

    def body(c_ref, p_ref, o_ref):
        b, gq = pl.program_id(2), pl.program_id(3)

        @pl.when(gq == 0)
        def _():
            o_ref[...] = jnp.zeros_like(o_ref)

        o_ref[...] += c_ref[...] + jnp.where(b + 1 < nb, p_ref[...], 0.0)

    blk = (BAND, LANES)
    return pl.pallas_call(
        body, name=name, grid=(n_cls, n_kv, nb, group),
        in_specs=[pl.BlockSpec(blk, lambda n, h, b, gq: (b, n * n_heads + h * group + gq)),
                  pl.BlockSpec(blk, lambda n, h, b, gq: (jnp.minimum(b + 1, nb - 1), n * n_heads + h * group + gq))],
        out_specs=pl.BlockSpec(blk, lambda n, h, b, gq: (b, n * n_kv + h)),
        out_shape=jax.ShapeDtypeStruct((length, n_cls * n_kv * LANES), F32),
        compiler_params=_cp(("parallel", "parallel", "parallel", "arbitrary")),
    )(cur, prv)


def _dil_mix_fwd(os_, lses):
    s, w = os_[0].shape
    bm = _tile(s, (256, 128, 8))

    def body(o0, o1, o2, l0, l1, l2, out_ref):
        ls = [l0[...], l1[...], l2[...]]
        m = jnp.maximum(jnp.maximum(ls[0], ls[1]), ls[2])
        es = [jnp.exp(v - m) for v in ls]
        inv = 1.0 / (es[0] + es[1] + es[2])
        for gi, o_ref in enumerate((o0, o1, o2)):
            out_ref[:, gi * w:(gi + 1) * w] = (o_ref[...] * (es[gi] * inv)).astype(out_ref.dtype)

    blk = pl.BlockSpec((bm, w), lambda i: (i, 0))
    return pl.pallas_call(
        body, name="dil_mix_fwd", grid=(s // bm,), in_specs=[blk] * 6,
        out_specs=pl.BlockSpec((bm, 3 * w), lambda i: (i, 0)),
        out_shape=jax.ShapeDtypeStruct((s, 3 * w), BF16), compiler_params=_cp(("parallel",)),
    )(*os_, *lses)


def _dil_mix_bwd(os_, lses, dmixed):
    s, w = os_[0].shape
    bm = _tile(s, (256, 128, 8))
    hg = w // LANES

    def body(o0, o1, o2, l0, l1, l2, dm_ref, do0, do1, do2, dl0, dl1, dl2):
        ls = [l0[...], l1[...], l2[...]]
        m = jnp.maximum(jnp.maximum(ls[0], ls[1]), ls[2])
        es = [jnp.exp(v - m) for v in ls]
        inv = 1.0 / (es[0] + es[1] + es[2])
        alphas = [e * inv for e in es]
        dalphas = []
        for gi, (o_ref, do_ref) in enumerate(((o0, do0), (o1, do1), (o2, do2))):
            dm = dm_ref[:, gi * w:(gi + 1) * w].astype(F32)
            do_ref[...] = (dm * alphas[gi]).astype(do_ref.dtype)
            prod = dm * o_ref[...]
            parts = [jnp.broadcast_to(jnp.sum(prod[:, j * LANES:(j + 1) * LANES], axis=1, keepdims=True), (bm, LANES))
                     for j in range(hg)]
            dalphas.append(jnp.concatenate(parts, axis=1) if hg > 1 else parts[0])
        mean = alphas[0] * dalphas[0] + alphas[1] * dalphas[1] + alphas[2] * dalphas[2]
        for gi, dl_ref in enumerate((dl0, dl1, dl2)):
            dl_ref[...] = alphas[gi] * (dalphas[gi] - mean)

    blk = pl.BlockSpec((bm, w), lambda i: (i, 0))
    return pl.pallas_call(
        body, name="dil_mix_bwd", grid=(s // bm,), in_specs=[blk] * 6 + [pl.BlockSpec((bm, 3 * w), lambda i: (i, 0))],
        out_specs=[blk] * 6,
        out_shape=[jax.ShapeDtypeStruct((s, w), BF16)] * 3 + [jax.ShapeDtypeStruct((s, w), F32)] * 3,
        compiler_params=_cp(("parallel",)),
    )(*os_, *lses, dmixed)


def _xattn_fwd(q, kv):
    s, w = q.shape
    mlen = kv.shape[0]
    tq = _tile(s, (512, 256, 128))
    scale = LANES ** -0.5

    def body(q_ref, k_ref, v_ref, o_ref):
        for h in range(XA_HEADS):
            sl = slice(h * LANES, (h + 1) * LANES)
            sc = lax.dot_general(q_ref[:, sl], k_ref[:, sl], NT, preferred_element_type=F32) * scale
            m = jnp.max(sc, axis=1, keepdims=True)
            e = jnp.exp(sc - m)
            p = e / jnp.sum(e, axis=1, keepdims=True)
            o_ref[:, sl] = jnp.dot(p.astype(BF16), v_ref[:, sl], preferred_element_type=F32).astype(o_ref.dtype)

    return pl.pallas_call(
        body, name="xattn_fwd", grid=(s // tq,),
        in_specs=[pl.BlockSpec((tq, w), lambda i: (i, 0)), pl.BlockSpec((mlen, w), lambda i: (0, 0)),
                  pl.BlockSpec((mlen, w), lambda i: (0, 1))],
        out_specs=pl.BlockSpec((tq, w), lambda i: (i, 0)),
        out_shape=jax.ShapeDtypeStruct((s, w), BF16), compiler_params=_cp(("parallel",)),
    )(q, kv, kv)


def _xattn_bwd(q, kv, do):
    s, w = q.shape
    mlen = kv.shape[0]
    tq = _tile(s, (512, 256, 128))
    scale = LANES ** -0.5

    def body(q_ref, k_ref, v_ref, do_ref, dq_ref, dk_ref, dv_ref):
        @pl.when(pl.program_id(0) == 0)
        def _():
            dk_ref[...] = jnp.zeros_like(dk_ref)
            dv_ref[...] = jnp.zeros_like(dv_ref)

        for h in range(XA_HEADS):
            sl = slice(h * LANES, (h + 1) * LANES)
            qh, kh, vh, doh = q_ref[:, sl], k_ref[:, sl], v_ref[:, sl], do_ref[:, sl]
            sc = lax.dot_general(qh, kh, NT, preferred_element_type=F32) * scale
            m = jnp.max(sc, axis=1, keepdims=True)
            e = jnp.exp(sc - m)
            p = e / jnp.sum(e, axis=1, keepdims=True)
            dp = lax.dot_general(doh, vh, NT, preferred_element_type=F32)
            ds = (p * (dp - jnp.sum(p * dp, axis=1, keepdims=True)) * scale).astype(BF16)
            dq_ref[:, sl] = jnp.dot(ds, kh, preferred_element_type=F32).astype(dq_ref.dtype)
            dk_ref[:, sl] += lax.dot_general(ds, qh, TN, preferred_element_type=F32)
            dv_ref[:, sl] += lax.dot_general(p.astype(BF16), doh, TN, preferred_element_type=F32)

    row = pl.BlockSpec((tq, w), lambda i: (i, 0))
    acc = pl.BlockSpec((mlen, w), lambda i: (0, 0))
    return pl.pallas_call(
        body, name="xattn_bwd", grid=(s // tq,),
        in_specs=[row, acc, pl.BlockSpec((mlen, w), lambda i: (0, 1)), row],
        out_specs=[row, acc, acc],
        out_shape=[jax.ShapeDtypeStruct((s, w), BF16), jax.ShapeDtypeStruct((mlen, w), F32),
                   jax.ShapeDtypeStruct((mlen, w), F32)],
        compiler_params=_cp(("arbitrary",)),
    )(q, kv, kv, do)


def _adamw(parts, w, m, v, name):
    r, c = w.shape
    npc = len(parts)
    rp = r // npc
    br = _tile(rp, tuple(p for p in (256, 128, 64, 32, 16, 8) if p * c * 4 <= 1024 * 1024))
    steps = rp // br

    def body(*refs):
        w_ref, m_ref, v_ref, g_ref, d_ref, nm_ref, nv_ref = refs[npc:]

        def update(p_ref):
            g = p_ref[0].astype(F32)
            for t in range(1, N_DEV):
                g = g + p_ref[t].astype(F32)
            nm = ADAM_B1 * m_ref[...] + (1.0 - ADAM_B1) * g
            nv = ADAM_B2 * v_ref[...] + (1.0 - ADAM_B2) * (g * g)
            m_hat = nm / (1.0 - ADAM_B1 ** ADAM_STEP)
            v_hat = nv / (1.0 - ADAM_B2 ** ADAM_STEP)
            g_ref[...] = g
            d_ref[...] = -ADAM_LR * (m_hat / (jnp.sqrt(v_hat) + ADAM_EPS) + ADAM_WD * w_ref[...])
            nm_ref[...] = nm
            nv_ref[...] = nv

        for k in range(npc):
            pl.when(pl.program_id(0) == k)(lambda k=k: update(refs[k]))

    blk = pl.BlockSpec((br, c), lambda p, i: (p * steps + i, 0))
    part_specs = [pl.BlockSpec((N_DEV, br, c), lambda p, i, k=k: (0, jnp.where(p == k, i, 0), 0)) for k in range(npc)]
    return pl.pallas_call(
        body, name=name, grid=(npc, steps), in_specs=part_specs + [blk, blk, blk], out_specs=[blk] * 4,
        out_shape=[jax.ShapeDtypeStruct((r, c), F32)] * 4, compiler_params=_cp(("parallel", "parallel")),
    )(*parts, w, m, v)


MESH_ID = pl.DeviceIdType.MESH
ANY = pl.BlockSpec(memory_space=pl.ANY)


def _my_place():
    return lax.axis_index("x"), lax.axis_index("y"), lax.axis_index("c")


def _all_gather(xs, name):
    nt = len(xs)

    def body(*refs):
        x_refs, out_refs = refs[:nt], refs[nt:2 * nt]
        send_sems, recv_sems, local_sems = refs[2 * nt:]
        x, y, c = _my_place()
        me, sibling = (x, y, c), (x, y, 1 - c)
        chips = [(1 - x, y), (x, 1 - y), (1 - x, 1 - y)]

        def slot(t, p):
            return out_refs[t].at[4 * p[0] + 2 * p[1] + p[2]]

        def copy(t, k, block, to, src=None):
            return pltpu.make_async_remote_copy(
                src_ref=slot(t, block) if src is None else src, dst_ref=slot(t, block),
                send_sem=send_sems.at[7 * t + k], recv_sem=recv_sems.at[7 * t + k], device_id=to,
                device_id_type=MESH_ID)

        mine, first, passed = [], [], []
        for t in range(nt):
            cp = pltpu.make_async_copy(x_refs[t], slot(t, me), local_sems.at[t])
            cp.start()
            mine.append(cp)
            group = [copy(t, 0, me, sibling, src=x_refs[t])]
            group += [copy(t, 1 + j, me, (*chip, c), src=x_refs[t]) for j, chip in enumerate(chips)]
            for cp in group:
                cp.start()
            first += group
        for t in range(nt):
            for j, chip in enumerate(chips):
                copy(t, 1 + j, (*chip, c), me).wait_recv()
                fw = copy(t, 4 + j, (*chip, c), sibling)
                fw.start()
                passed.append(fw)
        for t in range(nt):
            copy(t, 0, sibling, me).wait_recv()
            for j, chip in enumerate(chips):
                copy(t, 4 + j, (*chip, 1 - c), me).wait_recv()
        for cp in first + passed:
            cp.wait_send()
        for cp in mine:
            cp.wait()

    return pl.pallas_call(
        body, name=name, in_specs=[ANY] * nt, out_specs=[ANY] * nt,
        out_shape=[jax.ShapeDtypeStruct((N_DEV,) + tuple(v.shape), v.dtype) for v in xs],
        scratch_shapes=[pltpu.SemaphoreType.DMA((7 * nt,)), pltpu.SemaphoreType.DMA((7 * nt,)),
                        pltpu.SemaphoreType.DMA((nt,))],
    )(*xs)


def _exchange(gs, name):
    nt = len(gs)

    def body(*refs):
        g_refs, out_refs = refs[:nt], refs[nt:2 * nt]
        send_sems, recv_sems, local_sems = refs[2 * nt:]
        x, y, c = _my_place()
        my_slot = 4 * x + 2 * y + c
        mine, sent = [], []
        for t in range(nt):
            cp = pltpu.make_async_copy(g_refs[t].at[my_slot], out_refs[t].at[my_slot], local_sems.at[t])
            cp.start()
            mine.append(cp)
            for rel in range(1, N_DEV):
                px, py, pc = x ^ ((rel >> 2) & 1), y ^ ((rel >> 1) & 1), c ^ (rel & 1)
                cp = pltpu.make_async_remote_copy(
                    src_ref=g_refs[t].at[4 * px + 2 * py + pc], dst_ref=out_refs[t].at[my_slot],
                    send_sem=send_sems.at[7 * t + rel - 1], recv_sem=recv_sems.at[7 * t + rel - 1],
                    device_id=(px, py, pc), device_id_type=MESH_ID)
                cp.start()
                sent.append(cp)
        for cp in sent:
            cp.wait_recv()
        for cp in sent:
            cp.wait_send()
        for cp in mine:
            cp.wait()

    return pl.pallas_call(
        body, name=name, in_specs=[ANY] * nt, out_specs=[ANY] * nt,
        out_shape=[jax.ShapeDtypeStruct(tuple(v.shape), v.dtype) for v in gs],
        scratch_shapes=[pltpu.SemaphoreType.DMA((7 * nt,)), pltpu.SemaphoreType.DMA((7 * nt,)),
                        pltpu.SemaphoreType.DMA((nt,))],
    )(*gs)


EXCHANGE_BYTES_PER_FLOP = 1.3e-4
PIECE_BYTES = 24 * 1024 * 1024
CARRIER_MIN_BYTES = 6 * 1024 * 1024
ROW_ALIGN = 16


class _Piece:
    def __init__(self, key, g, axis, lo, hi):
        self.key, self.g, self.axis, self.lo, self.hi = key, g, axis, lo, hi
        cols = g.shape[1] if axis == 0 else g.shape[1] // N_DEV
        self.recv_shape = jax.ShapeDtypeStruct((N_DEV, hi - lo, cols), g.dtype)
        self.nbytes = N_DEV * (hi - lo) * cols * g.dtype.itemsize


def _piece_sems(nj):
    return [pltpu.SemaphoreType.DMA((7 * nj,)), pltpu.SemaphoreType.DMA((7 * nj,)), pltpu.SemaphoreType.DMA((nj,))]


def _piece_copies(jobs, g_refs, recv_refs, sems):
    send_sems, recv_sems, local_sems = sems
    x, y, c = _my_place()
    me = 4 * x + 2 * y + c
    local, remote = [], []
    for t, (job, g, r) in enumerate(zip(jobs, g_refs, recv_refs)):
        rows = job.hi - job.lo

        def block(slot, job=job, g=g, r=r, rows=rows):
            if job.axis == 0:
                start = pl.multiple_of(slot * (g.shape[0] // N_DEV) + job.lo, ROW_ALIGN)
                return g.at[pl.ds(start, rows), :]
            cols = r.shape[2]
            return g.at[pl.ds(job.lo, rows), pl.ds(pl.multiple_of(slot * cols, LANES), cols)]

        local.append(pltpu.make_async_copy(block(me), r.at[me], local_sems.at[t]))
        for rel in range(1, N_DEV):
            px, py, pc = x ^ ((rel >> 2) & 1), y ^ ((rel >> 1) & 1), c ^ (rel & 1)
            remote.append(pltpu.make_async_remote_copy(
                src_ref=block(4 * px + 2 * py + pc), dst_ref=r.at[me], send_sem=send_sems.at[7 * t + rel - 1],
                recv_sem=recv_sems.at[7 * t + rel - 1], device_id=(px, py, pc), device_id_type=MESH_ID))
    return local, remote


def _pieces_start(jobs, g_refs, recv_refs, sems):
    local, remote = _piece_copies(jobs, g_refs, recv_refs, sems)
    for cp in local + remote:
        cp.start()


def _pieces_wait(jobs, g_refs, recv_refs, sems):
    local, remote = _piece_copies(jobs, g_refs, recv_refs, sems)
    for cp in remote:
        cp.wait_recv()
    for cp in remote:
        cp.wait_send()
    for cp in local:
        cp.wait()


def _exchange_pieces(jobs, name):
    nj = len(jobs)

    def body(*refs):
        job_refs = (refs[:nj], refs[nj:2 * nj], refs[2 * nj:])
        _pieces_start(jobs, *job_refs)
        _pieces_wait(jobs, *job_refs)

    return pl.pallas_call(
        body, name=name, in_specs=[ANY] * nj, out_specs=[ANY] * nj, out_shape=[j.recv_shape for j in jobs],
        scratch_shapes=_piece_sems(nj),
    )(*[j.g for j in jobs])


class _GradExchange:
    def __init__(self):
        self.queue, self.recv = [], {}

    def put(self, name, layer, g):
        axis = SHARD_AXIS[name] - 1
        rows = g.shape[0] // N_DEV if axis == 0 else g.shape[0]
        if g.dtype != BF16 or rows % ROW_ALIGN or (axis == 1 and (g.shape[1] // N_DEV) % LANES):
            return False
        n_split = max(1, round(g.size * g.dtype.itemsize / PIECE_BYTES))
        while rows % (n_split * ROW_ALIGN):
            n_split -= 1
        for k in range(n_split):
            self.queue.append(_Piece((name, layer, k), g, axis, k * rows // n_split, (k + 1) * rows // n_split))
        return True

    def take(self, capacity):
        jobs, used = [], 0
        if capacity >= CARRIER_MIN_BYTES and self.queue and self.queue[0].nbytes <= 2.2 * capacity:
            jobs.append(self.queue.pop(0))
            used = jobs[0].nbytes
            while self.queue and used + self.queue[0].nbytes <= capacity:
                used += self.queue[0].nbytes
                jobs.append(self.queue.pop(0))
        return jobs

    def landed(self, jobs, recvs):
        for j, r in zip(jobs, recvs):
            self.recv[j.key] = r

    def flush(self):
        if self.queue:
            jobs, self.queue = self.queue, []
            self.landed(jobs, _exchange_pieces(jobs, "exchange_rest"))

    def pieces_of(self, name):
        return [self.recv[k] for k in sorted(k for k in self.recv if k[0] == name)]


def _ffn_fwd(x, g, w_gu, w_d, tag):
    n = _rms_fwd(x, g)
    gu = _mm(n, w_gu, out_dtype=BF16, name=f"{tag}_gu")
    a = _swiglu_fwd(gu)
    return _mm(a, w_d, alpha=0.5, res=x, name=f"{tag}_down"), (n, gu, a)


def _put(ex, name, layer, g):
    if ex is not None:
        ex.put(name, layer, g)


def _ffn_bwd(dy, x, g, w_gu, w_d, saved, tag, ex=None, which="ffn1", layer=0):
    n, gu, a = saved
    da = _mm(dy, w_d, tb=True, alpha=0.5, out_dtype=BF16, name=f"{tag}_da", ex=ex)
    d_wd = _mm(a, dy, ta=True, alpha=0.5, out_dtype=BF16, name=f"{tag}_dwd", ex=ex)
    _put(ex, f"{which}_w_down", layer, d_wd)
    dgu = _swiglu_bwd(gu, da)
    dn = _mm(dgu, w_gu, tb=True, name=f"{tag}_dn", ex=ex)
    d_wgu = _mm(n, dgu, ta=True, out_dtype=BF16, name=f"{tag}_dwgu", ex=ex)
    _put(ex, f"{which}_w_gate_up", layer, d_wgu)
    dx, dg = _rms_bwd(x, g, dn, dy)
    return dx, dg, d_wgu, d_wd


def _sb_mixer_fwd(h, w_qkv, w_o, x):
    n_heads = w_o.shape[0] // LANES
    qkv = _mm(h, w_qkv, out_dtype=BF16, name="sb_qkv")
    o, ltot = _sb_fwd(qkv, n_heads)
    return _mm(o, w_o, res=x, name="sb_out"), (qkv, o, ltot)


def _sb_mixer_bwd(dy, h, w_qkv, w_o, saved, ex=None, layer=0):
    qkv, o, ltot = saved
    n_heads = w_o.shape[0] // LANES
    do = _mm(dy, w_o, tb=True, out_dtype=BF16, name="sb_do", ex=ex)
    d_wo = _mm(o, dy, ta=True, out_dtype=BF16, name="sb_dwo", ex=ex)
    _put(ex, "sb_w_o", layer, d_wo)
    dq, dk, dv = _sb_bwd(qkv, do, ltot, n_heads, ex)
    dqkv = jnp.concatenate([dq, dk.astype(BF16), dv.astype(BF16)], axis=1)
    dh = _mm(dqkv, w_qkv, tb=True, name="sb_dh", ex=ex)
    d_wqkv = _mm(h, dqkv, ta=True, out_dtype=BF16, name="sb_dwqkv", ex=ex)
    _put(ex, "sb_w_qkv", layer, d_wqkv)
    return dh, d_wqkv, d_wo


def _dil_cols(gi):
    ng = len(DIL_PATTERNS)
    return dict(q_blk=lambda n, st: n * 2 * ng + gi, k_blk=lambda n, st: n * 2 * ng + ng + gi,
                v_blk=lambda n, st: n * 3 * ng + 2 * ng + gi)


def _dil_mixer_fwd(h, w_qkv, w_o, x, tabs):
    s = h.shape[0]
    n_all = w_o.shape[0] // LANES
    hg = n_all // len(DIL_PATTERNS)
    qkv = _mm(h, w_qkv, name="dil_qkv")
    qk = _rope(qkv, tabs[0], tabs[1], 2 * n_all, 16, "dil_rope")
    os_, lses = [], []
    for gi, (window, dil) in enumerate(DIL_PATTERNS):
        o, lse = _band_fwd(qk.reshape(s // dil, -1), qk.reshape(s // dil, -1), qkv.reshape(s // dil, -1),
                           n_cls=dil, n_steps=1, hpb=hg, group=1, **_dil_cols(gi),
                           max_dist=window // dil, scale=LANES ** -0.5, sinks=None, name=f"dil_fwd{gi}")
        os_.append(o.reshape(s, hg * LANES))
        lses.append(lse.reshape(s, hg * LANES))
    mixed = _dil_mix_fwd(os_, lses)
    return _mm(mixed, w_o, res=x, name="dil_out"), (qkv, qk, os_, lses, mixed)


def _dil_mixer_bwd(dy, h, w_qkv, w_o, saved, tabs_bwd, ex=None):
    qkv, qk, os_, lses, mixed = saved
    s = h.shape[0]
    n_all = w_o.shape[0] // LANES
    hg = n_all // len(DIL_PATTERNS)
    dmixed = _mm(dy, w_o, tb=True, out_dtype=BF16, name="dil_dmix", ex=ex)
    d_wo = _mm(mixed, dy, ta=True, out_dtype=BF16, name="dil_dwo", ex=ex)
    _put(ex, "dil_w_o", 0, d_wo)
    mix_out = _dil_mix_bwd(os_, lses, dmixed)
    dos, dlses = mix_out[:3], mix_out[3:]
    dqs, dks, dvs = [], [], []
    for gi, (window, dil) in enumerate(DIL_PATTERNS):
        length = s // dil
        dq, dkc, dkp, dvc, dvp = _band_bwd(
            qk.reshape(length, -1), qk.reshape(length, -1), qkv.reshape(length, -1), os_[gi].reshape(length, -1),
            dos[gi].reshape(length, -1), lses[gi].reshape(length, -1), dlses[gi].reshape(length, -1),
            n_cls=dil, n_steps=1, hpb=hg, group=1, **_dil_cols(gi), max_dist=window // dil,
            scale=LANES ** -0.5, sinks=None, name=f"dil_bwd{gi}")
        dqs.append(dq.reshape(s, -1))
        dks.append(_band_fold(dkc, dkp, n_cls=dil, n_heads=hg, group=1, name=f"dil_foldk{gi}").reshape(s, -1))
        dvs.append(_band_fold(dvc, dvp, n_cls=dil, n_heads=hg, group=1, name=f"dil_foldv{gi}").reshape(s, -1))
    dqk_rot = jnp.concatenate(dqs + dks, axis=1)
    dqk = _rope(dqk_rot, tabs_bwd[0], tabs_bwd[1], 2 * n_all, 16, "dil_rope_bwd")
    dqkv = jnp.concatenate([dqk] + [t.astype(BF16) for t in dvs], axis=1)
    dh = _mm(dqkv, w_qkv, tb=True, name="dil_dh", ex=ex)
    d_wqkv = _mm(h, dqkv, ta=True, out_dtype=BF16, name="dil_dwqkv", ex=ex)
    return dh, d_wqkv, d_wo


def _pad_heads(w, axis):
    shape = list(w.shape)
    n = shape[axis] // SWA_HEAD_DIM
    w = w.reshape(shape[:axis] + [n, SWA_HEAD_DIM] + shape[axis + 1:])
    pad = [(0, 0)] * w.ndim
    pad[axis + 1] = (0, LANES - SWA_HEAD_DIM)
    shape[axis] = n * LANES
    return jnp.pad(w, pad).reshape(shape)


def _unpad_heads(w, axis):
    shape = list(w.shape)
    n = shape[axis] // LANES
    w = w.reshape(shape[:axis] + [n, LANES] + shape[axis + 1:])
    w = lax.slice_in_dim(w, 0, SWA_HEAD_DIM, axis=axis + 1)
    shape[axis] = n * SWA_HEAD_DIM
    return w.reshape(shape)


def _swa_mixer_fwd(h, w_qkv_p, b_qkv_p, sinks_b, w_o_p, b_o, x, tabs):
    nq = w_o_p.shape[0] // LANES
    nkv = nq // SWA_GROUP
    qkv = _mm(h, w_qkv_p, bias=b_qkv_p, name="swa_qkv")
    qk = _rope(qkv, tabs[0], tabs[1], nq + nkv, 8, "swa_rope")
    o, lse = _band_fwd(qk, qk, qkv, n_cls=1, n_steps=nkv, hpb=SWA_GROUP, group=SWA_GROUP, q_blk=lambda n, st: st,
                       k_blk=lambda n, st: nq + st, v_blk=lambda n, st: nq + nkv + st,
                       max_dist=SWA_WINDOW - 1, scale=SWA_HEAD_DIM ** -0.5, sinks=sinks_b, name="swa_fwd")
    return _mm(o, w_o_p, res=x, bias=b_o, name="swa_out"), (qkv, qk, o, lse)


def _swa_mixer_bwd(dy, h, w_qkv_p, sinks_b, w_o_p, saved, tabs_bwd, ex=None):
    qkv, qk, o, lse = saved
    nq = w_o_p.shape[0] // LANES
    nkv = nq // SWA_GROUP
    do = _mm(dy, w_o_p, tb=True, name="swa_do", ex=ex)
    d_wo_p = _mm(o, dy, ta=True, out_dtype=BF16, name="swa_dwo", ex=ex)
    d_bo = _colsum(dy, "swa_dbo")
    dq, dkc, dkp, dvc, dvp, dsink = _band_bwd(
        qk, qk, qkv, o, do, lse, None, n_cls=1, n_steps=nkv, hpb=SWA_GROUP, group=SWA_GROUP, q_blk=lambda n, st: st,
        k_blk=lambda n, st: nq + st, v_blk=lambda n, st: nq + nkv + st, max_dist=SWA_WINDOW - 1,
        scale=SWA_HEAD_DIM ** -0.5, sinks=sinks_b, name="swa_bwd")
    dk = _band_fold(dkc, dkp, n_cls=1, n_heads=nkv, group=1, name="swa_foldk")
    dv = _band_fold(dvc, dvp, n_cls=1, n_heads=nkv, group=1, name="swa_foldv")
    dqk = _rope(jnp.concatenate([dq, dk], axis=1), tabs_bwd[0], tabs_bwd[1], nq + nkv, 8, "swa_rope_bwd")
    dqkv = jnp.concatenate([dqk, dv.astype(BF16)], axis=1)
    d_bqkv_p = _colsum(dqkv, "swa_dbqkv")
    dh = _mm(dqkv, w_qkv_p, tb=True, name="swa_dh", ex=ex)
    d_wqkv_p = _mm(h, dqkv, ta=True, out_dtype=BF16, name="swa_dwqkv", ex=ex)
    return dh, d_wqkv_p, d_bqkv_p, dsink, d_wo_p, d_bo


def _xattn_layer_fwd(x, mem, g_x, g_m, w_q, w_kv, w_o):
    hq = _rms_fwd(x, g_x, "rms_fwd")
    hm = _rms_fwd(mem, g_m, "rms_mem_fwd")
    q = _mm(hq, w_q, out_dtype=BF16, name="xa_q")
    kv = _mm(hm, w_kv, out_dtype=BF16, name="xa_kv")
    o = _xattn_fwd(q, kv)
    return _mm(o, w_o, res=x, name="xa_out"), (hq, hm, q, kv, o)


def _xattn_layer_bwd(dy, x, mem, g_x, g_m, w_q, w_kv, w_o, saved):
    hq, hm, q, kv, o = saved
    do = _mm(dy, w_o, tb=True, out_dtype=BF16, name="xa_do")
    d_wo = _mm(o, dy, ta=True, out_dtype=BF16, name="xa_dwo")
    dq, dk, dv = _xattn_bwd(q, kv, do)
    dkv = jnp.concatenate([dk, dv], axis=1).astype(BF16)
    dhq = _mm(dq, w_q, tb=True, name="xa_dhq")
    d_wq = _mm(hq, dq, ta=True, out_dtype=BF16, name="xa_dwq")
    dhm = _mm(dkv, w_kv, tb=True, name="xa_dhm")
    d_wkv = _mm(hm, dkv, ta=True, out_dtype=BF16, name="xa_dwkv")
    dx, dg_x = _rms_bwd(x, g_x, dhq, dy)
    _, dg_m = _rms_bwd(mem, g_m, dhm, None, "rms_mem_bwd")
    return dx, dg_x, dg_m, d_wq, d_wkv, d_wo


def _to_full(gathered, axis):
    t = jnp.moveaxis(gathered, 0, axis)
    shape = list(t.shape)
    return t.reshape(shape[:axis] + [shape[axis] * shape[axis + 1]] + shape[axis + 2:])


def _to_blocks(full, axis):
    shape = list(full.shape)
    t = full.reshape(shape[:axis] + [N_DEV, shape[axis] // N_DEV] + shape[axis + 1:])
    return jnp.moveaxis(t, axis, 0)


SHARD_AXIS = {
    "ffn1_w_gate_up": 2, "ffn1_w_down": 1, "sb_w_qkv": 2, "sb_w_o": 1, "dil_w_qkv": 2, "dil_w_o": 2,
    "swa_w_qkv": 2, "swa_b_qkv": 1, "swa_w_o": 1, "swa_b_o": 1, "xattn_w_q": 1, "xattn_w_kv": 1, "xattn_w_o": 2,
    "ffn2_w_gate_up": 2, "ffn2_w_down": 1,
}
SMALL = ("ffn1_norm", "mix_norm", "xattn_norm", "mem_norm", "ffn2_norm", "final_norm", "swa_sinks")
WEIGHTS = ("ffn1_norm", "ffn1_w_gate_up", "ffn1_w_down", "mix_norm", "sb_w_qkv", "sb_w_o", "dil_w_qkv", "dil_w_o",
           "swa_w_qkv", "swa_b_qkv", "swa_sinks", "swa_w_o", "swa_b_o", "xattn_norm", "mem_norm", "xattn_w_q",
           "xattn_w_kv", "xattn_w_o", "ffn2_norm", "ffn2_w_gate_up", "ffn2_w_down", "final_norm")


def _flat2(a):
    return a.reshape(-1, a.shape[-1])


def _pack_small(vals, d):
    rows = [vals[n].reshape(-1, d) for n in SMALL[:5]] + [vals["final_norm"].reshape(1, d)]
    sk = vals["swa_sinks"].reshape(1, -1)
    rows.append(jnp.pad(sk, ((0, 0), (0, d - sk.shape[1]))))
    rows.append(jnp.zeros((2, d), F32))
    return jnp.concatenate(rows, axis=0)


def _unpack_small(packed, like):
    out, r = {}, 0
    for n in SMALL[:5]:
        k = like[n].shape[0]
        out[n] = packed[r:r + k]
        r += k
    out["final_norm"] = packed[r]
    out["swa_sinks"] = packed[r + 1:r + 2, :like["swa_sinks"].shape[1]]
    return out


def _local_step(x0, mem0, positions, target, full, norm, ex=None):
    d = x0.shape[1]
    names = list(SHARD_AXIS)
    swa_w_qkv_p = _pad_heads(full["swa_w_qkv"][0], 1)
    swa_b_qkv_p = _pad_heads(full["swa_b_qkv"], 1)
    swa_w_o_p = _pad_heads(full["swa_w_o"][0], 0)
    swa_b_o = full["swa_b_o"]
    sinks_b = jnp.repeat(norm["swa_sinks"], LANES, axis=1)
    tabs_dil, tabs_dil_bwd = _rope_tables(positions, 32), _rope_tables(positions, 32, -1.0)
    tabs_swa, tabs_swa_bwd = _rope_tables(positions, 16), _rope_tables(positions, 16, -1.0)

    def vec(name, i):
        return norm[name][i:i + 1]

    saved = []
    xc = x0
    for i in range(DEPTH):
        kind, j = i % 3, i // 3
        rec = {"x0": xc}
        xc, rec["ffn1"] = _ffn_fwd(xc, vec("ffn1_norm", i), full["ffn1_w_gate_up"][i], full["ffn1_w_down"][i], "ffn")
        rec["x1"] = xc
        h = _rms_fwd(xc, vec("mix_norm", i))
        rec["h"] = h
        if kind == 0:
            xc, rec["mix"] = _sb_mixer_fwd(h, full["sb_w_qkv"][j], full["sb_w_o"][j], xc)
        elif kind == 1:
            xc, rec["mix"] = _dil_mixer_fwd(h, full["dil_w_qkv"][j], full["dil_w_o"][j], xc, tabs_dil)
        else:
            xc, rec["mix"] = _swa_mixer_fwd(h, swa_w_qkv_p, swa_b_qkv_p, sinks_b, swa_w_o_p, swa_b_o, xc, tabs_swa)
        rec["x2"] = xc
        xc, rec["xa"] = _xattn_layer_fwd(xc, mem0, vec("xattn_norm", i), vec("mem_norm", i), full["xattn_w_q"][i],
                                         full["xattn_w_kv"][i], full["xattn_w_o"][i])
        rec["x3"] = xc
        xc, rec["ffn2"] = _ffn_fwd(xc, vec("ffn2_norm", i), full["ffn2_w_gate_up"][i], full["ffn2_w_down"][i], "ffn")
        saved.append(rec)

    loss_part, dx, dg_final = _loss_head(xc, norm["final_norm"].reshape(1, d), target)

    gfull = {n: [None] * full[n].shape[0] for n in names}
    gsmall = {n: [None] * DEPTH for n in SMALL[:5]}
    gsmall["final_norm"] = dg_final
    gsmall["swa_sinks"] = jnp.zeros_like(norm["swa_sinks"])
    for i in reversed(range(DEPTH)):
        kind, j = i % 3, i // 3
        rec = saved[i]
        dx, gsmall["ffn2_norm"][i], gfull["ffn2_w_gate_up"][i], gfull["ffn2_w_down"][i] = _ffn_bwd(
            dx, rec["x3"], vec("ffn2_norm", i), full["ffn2_w_gate_up"][i], full["ffn2_w_down"][i], rec["ffn2"], "ffn",
            ex, "ffn2", i)
        (dx, gsmall["xattn_norm"][i], gsmall["mem_norm"][i], gfull["xattn_w_q"][i], gfull["xattn_w_kv"][i],
         gfull["xattn_w_o"][i]) = _xattn_layer_bwd(dx, rec["x2"], mem0, vec("xattn_norm", i), vec("mem_norm", i),
                                                   full["xattn_w_q"][i], full["xattn_w_kv"][i], full["xattn_w_o"][i],
                                                   rec["xa"])
        for n in ("xattn_w_q", "xattn_w_kv", "xattn_w_o"):
            _put(ex, n, i, gfull[n][i])
        if kind == 0:
            dh, gfull["sb_w_qkv"][j], gfull["sb_w_o"][j] = _sb_mixer_bwd(
                dx, rec["h"], full["sb_w_qkv"][j], full["sb_w_o"][j], rec["mix"], ex, j)
        elif kind == 1:
            dh, gfull["dil_w_qkv"][j], gfull["dil_w_o"][j] = _dil_mixer_bwd(
                dx, rec["h"], full["dil_w_qkv"][j], full["dil_w_o"][j], rec["mix"], tabs_dil_bwd, ex)
        else:
            dh, d_wqkv_p, d_bqkv_p, dsink, d_wo_p, d_bo = _swa_mixer_bwd(
                dx, rec["h"], swa_w_qkv_p, sinks_b, swa_w_o_p, rec["mix"], tabs_swa_bwd, ex)
            gfull["swa_w_qkv"][j] = _unpad_heads(d_wqkv_p, 1)
            gfull["swa_b_qkv"][j] = _unpad_heads(d_bqkv_p, 1)[0]
            gfull["swa_w_o"][j] = _unpad_heads(d_wo_p, 0)
            gfull["swa_b_o"][j] = d_bo[0]
            gsmall["swa_sinks"] = dsink.reshape(1, -1, LANES)[:, :, 0]
            _put(ex, "swa_w_o", j, gfull["swa_w_o"][j])
        dx, gsmall["mix_norm"][i] = _rms_bwd(rec["x1"], vec("mix_norm", i), dh, dx)
        dx, gsmall["ffn1_norm"][i], gfull["ffn1_w_gate_up"][i], gfull["ffn1_w_down"][i] = _ffn_bwd(
            dx, rec["x0"], vec("ffn1_norm", i), full["ffn1_w_gate_up"][i], full["ffn1_w_down"][i], rec["ffn1"], "ffn",
            ex, "ffn1", i)
    for n in SMALL[:5]:
        gsmall[n] = jnp.concatenate(gsmall[n], axis=0)
    return loss_part[0, 0], dx, gfull, gsmall


def _train_step(x, mem, positions, loss_target, w, m, v):
    d = x.shape[2]
    names = list(SHARD_AXIS)
    norm = {n: w[n] for n in SMALL}

    local = [_flat2(w[n].astype(BF16) if w[n].ndim == 3 else w[n]) for n in names]
    gathered = _all_gather(local, "gather_weights")
    full = {n: _to_full(g.reshape((N_DEV,) + w[n].shape), SHARD_AXIS[n]) for n, g in zip(names, gathered)}

    ex = _GradExchange()
    loss_part, dx, gfull, gsmall = _local_step(x[0], mem[0], positions, loss_target[0], full, norm, ex)
    loss = lax.psum(loss_part, MESH_AXES)
    grad_x = dx[None]

    ex.flush()
    parts = {n: ex.pieces_of(n) for n in names}
    rest = [n for n in names if not parts[n]]
    blocks = [_to_blocks(jnp.stack(gfull[n], axis=0), SHARD_AXIS[n]) for n in rest]
    blocks = [b.reshape(N_DEV, -1, b.shape[-1]) for b in blocks]
    for n, received in zip(rest, _exchange(blocks, "exchange_grads")):
        parts[n] = [received]
    grad, delta, new_m, new_v = {}, {}, {}, {}
    for n in names:
        outs = _adamw(parts[n], _flat2(w[n]), _flat2(m[n]), _flat2(v[n]), "adamw")
        grad[n], delta[n], new_m[n], new_v[n] = (o.reshape(w[n].shape) for o in outs)

    small_parts = _all_gather([_pack_small(gsmall, d)], "gather_small_grads")[0]
    outs = _adamw([small_parts], _pack_small(norm, d), _pack_small({n: m[n] for n in SMALL}, d),
                  _pack_small({n: v[n] for n in SMALL}, d), "adamw_small")
    for res, o in zip((grad, delta, new_m, new_v), outs):
        res.update(_unpack_small(o, norm))
    return loss, grad_x, grad, delta, new_m, new_v


def kernel(x, mem, positions, ffn1_norm, ffn1_w_gate_up, ffn1_w_down, mix_norm, sb_w_qkv, sb_w_o, dil_w_qkv, dil_w_o, swa_w_qkv, swa_b_qkv, swa_sinks, swa_w_o, swa_b_o, xattn_norm, mem_norm, xattn_w_q, xattn_w_kv, xattn_w_o, ffn2_norm, ffn2_w_gate_up, ffn2_w_down, final_norm, loss_target, m_ffn1_norm, m_ffn1_w_gate_up, m_ffn1_w_down, m_mix_norm, m_sb_w_qkv, m_sb_w_o, m_dil_w_qkv, m_dil_w_o, m_swa_w_qkv, m_swa_b_qkv, m_swa_sinks, m_swa_w_o, m_swa_b_o, m_xattn_norm, m_mem_norm, m_xattn_w_q, m_xattn_w_kv, m_xattn_w_o, m_ffn2_norm, m_ffn2_w_gate_up, m_ffn2_w_down, m_final_norm, v_ffn1_norm, v_ffn1_w_gate_up, v_ffn1_w_down, v_mix_norm, v_sb_w_qkv, v_sb_w_o, v_dil_w_qkv, v_dil_w_o, v_swa_w_qkv, v_swa_b_qkv, v_swa_sinks, v_swa_w_o, v_swa_b_o, v_xattn_norm, v_mem_norm, v_xattn_w_q, v_xattn_w_kv, v_xattn_w_o, v_ffn2_norm, v_ffn2_w_gate_up, v_ffn2_w_down, v_final_norm):
    args = dict(locals())
    w = {n: args[n] for n in WEIGHTS}
    m = {n: args["m_" + n] for n in WEIGHTS}
    v = {n: args["v_" + n] for n in WEIGHTS}
    loss, grad_x, grad, delta, new_m, new_v = _train_step(x, mem, positions, loss_target, w, m, v)
    return (loss, grad_x, *[grad[n] for n in WEIGHTS], *[delta[n] for n in WEIGHTS],
            *[new_m[n] for n in WEIGHTS], *[new_v[n] for n in WEIGHTS])
```

```python
import jax
import jax.numpy as jnp
from jax import lax
from jax.experimental import pallas as pl
from jax.experimental.pallas import tpu as pltpu

F32 = jnp.float32
BF16 = jnp.bfloat16

N_DEV = 8
MESH_AXES = ("x", "y", "c")
LANES = 128
BAND = 128
NORM_EPS = 1e-6
ROPE_THETA = 500000.0
DIL_PATTERNS = ((128, 1), (512, 4), (2048, 16))
SWA_HEAD_DIM = 64
SWA_GROUP = 8
SWA_WINDOW = 128
XA_HEADS = 4
DEPTH = 4
ADAM_LR, ADAM_B1, ADAM_B2, ADAM_EPS, ADAM_WD, ADAM_STEP = 0.001, 0.9, 0.999, 1e-08, 0.01, 10
VMEM_LIMIT = 56 * 1024 * 1024
NEG = -1e30

NT = (((1,), (1,)), ((), ()))
TN = (((0,), (0,)), ((), ()))


def _tile(n, prefs):
    for p in prefs:
        if n % p == 0:
            return p
    return n


def _cp(sem):
    return pltpu.CompilerParams(dimension_semantics=sem, vmem_limit_bytes=VMEM_LIMIT)


def _mm(a, b, *, ta=False, tb=False, out_dtype=F32, alpha=1.0, res=None, bias=None, name, ex=None):
    kdim, m = a.shape if ta else a.shape[::-1]
    kdim2, n = b.shape[::-1] if tb else b.shape
    assert kdim == kdim2, (a.shape, b.shape, ta, tb)
    bm = _tile(m, (1024, 512, 256, 128))
    bn = _tile(n, (1024, 512, 256, 128))
    bk = _tile(kdim, (1024, 512, 256, 128))
    nk = kdim // bk
    grid = (m // bm, n // bn, nk)
    dn = (((0 if ta else 1,), (1 if tb else 0,)), ((), ()))
    has_res, has_bias = res is not None, bias is not None
    jobs = ex.take(2.0 * m * n * kdim * EXCHANGE_BYTES_PER_FLOP) if ex is not None else []
    nj = len(jobs)
    n_in = 2 + has_res + has_bias

    def body(*refs):
        a_ref, b_ref = refs[0], refs[1]
        res_ref = refs[2] if has_res else None
        bias_ref = refs[2 + has_res] if has_bias else None
        o_ref, acc_ref = refs[n_in + nj], refs[n_in + 2 * nj + 1]
        k = pl.program_id(2)
        if nj:
            job_refs = (refs[n_in:n_in + nj], refs[n_in + nj + 1:n_in + 2 * nj + 1], refs[n_in + 2 * nj + 2:])
            ids = [pl.program_id(t) for t in range(3)]

            @pl.when((ids[0] == 0) & (ids[1] == 0) & (ids[2] == 0))
            def _():
                ex.start(jobs, *job_refs)

            @pl.when((ids[0] == grid[0] - 1) & (ids[1] == grid[1] - 1) & (ids[2] == grid[2] - 1))
            def _():
                ex.wait(jobs, *job_refs)

        @pl.when(k == 0)
        def _():
            acc_ref[...] = jnp.zeros_like(acc_ref)

        acc_ref[...] += lax.dot_general(a_ref[...].astype(BF16), b_ref[...].astype(BF16), dn,
                                        preferred_element_type=F32)

        @pl.when(k == nk - 1)
        def _():
            r = acc_ref[...]
            if alpha != 1.0:
                r = r * alpha
            if has_bias:
                r = r + bias_ref[...]
            if has_res:
                r = r + res_ref[...]
            o_ref[...] = r.astype(o_ref.dtype)

    a_spec = pl.BlockSpec((bk, bm), lambda i, j, k: (k, i)) if ta else pl.BlockSpec((bm, bk), lambda i, j, k: (i, k))
    b_spec = pl.BlockSpec((bn, bk), lambda i, j, k: (j, k)) if tb else pl.BlockSpec((bk, bn), lambda i, j, k: (k, j))
    ins, specs = [a, b], [a_spec, b_spec]
    if has_res:
        ins.append(res)
        specs.append(pl.BlockSpec((bm, bn), lambda i, j, k: (i, j)))
    if has_bias:
        ins.append(bias)
        specs.append(pl.BlockSpec((1, bn), lambda i, j, k: (0, j)))
    out_spec = pl.BlockSpec((bm, bn), lambda i, j, k: (i, j))
    out_shape = jax.ShapeDtypeStruct((m, n), out_dtype)
    scratch = [pltpu.VMEM((bm, bn), F32)]
    if not nj:
        return pl.pallas_call(
            body, name=name, grid=grid, in_specs=specs, out_specs=out_spec, out_shape=out_shape,
            scratch_shapes=scratch, compiler_params=_cp(("parallel", "parallel", "arbitrary")),
        )(*ins)
    outs = pl.pallas_call(
        body, name=name + "_carry", grid=grid, in_specs=specs + [ANY] * nj, out_specs=[out_spec] + [ANY] * nj,
        out_shape=[out_shape] + [j.recv_shape for j in jobs], scratch_shapes=scratch + _piece_sems(nj),
        compiler_params=_cp(("arbitrary", "arbitrary", "arbitrary")),
    )(*ins, *[j.g for j in jobs])
    ex.landed(jobs, outs[1:])
    return outs[0]


def _rms_fwd(x, g, name="rms_fwd"):
    s, d = x.shape
    bm = _tile(s, (256, 128, 8))

    def body(x_ref, g_ref, o_ref):
        xv = x_ref[...]
        r = lax.rsqrt(jnp.mean(xv * xv, axis=-1, keepdims=True) + NORM_EPS)
        o_ref[...] = (xv * r * g_ref[...]).astype(o_ref.dtype)

    return pl.pallas_call(
        body, name=name, grid=(s // bm,),
        in_specs=[pl.BlockSpec((bm, d), lambda i: (i, 0)), pl.BlockSpec((1, d), lambda i: (0, 0))],
        out_specs=pl.BlockSpec((bm, d), lambda i: (i, 0)),
        out_shape=jax.ShapeDtypeStruct((s, d), BF16), compiler_params=_cp(("parallel",)),
    )(x, g)


def _rms_bwd(x, g, dn, dy=None, name="rms_bwd"):
    s, d = x.shape
    bm = _tile(s, (256, 128, 8))
    has_dy = dy is not None

    def body(*refs):
        x_ref, g_ref, dn_ref = refs[:3]
        dy_ref = refs[3] if has_dy else None
        dx_ref, dg_ref = refs[-2], refs[-1]

        @pl.when(pl.program_id(0) == 0)
        def _():
            dg_ref[...] = jnp.zeros_like(dg_ref)

        xv = x_ref[...]
        r = lax.rsqrt(jnp.mean(xv * xv, axis=-1, keepdims=True) + NORM_EPS)
        xh = xv * r
        dnv = dn_ref[...].astype(F32)
        dxh = dnv * g_ref[...]
        dx = r * (dxh - xh * jnp.mean(dxh * xh, axis=-1, keepdims=True))
        if has_dy:
            dx = dx + dy_ref[...]
        dx_ref[...] = dx
        dg_ref[...] += jnp.sum(dnv * xh, axis=0, keepdims=True)

    row = pl.BlockSpec((bm, d), lambda i: (i, 0))
    vec = pl.BlockSpec((1, d), lambda i: (0, 0))
    ins, specs = [x, g, dn], [row, vec, row]
    if has_dy:
        ins.append(dy)
        specs.append(row)
    return pl.pallas_call(
        body, name=name, grid=(s // bm,), in_specs=specs, out_specs=[row, vec],
        out_shape=[jax.ShapeDtypeStruct((s, d), F32), jax.ShapeDtypeStruct((1, d), F32)],
        compiler_params=_cp(("arbitrary",)),
    )(*ins)


def _loss_head(x, g, target):
    s, d = x.shape
    bm = _tile(s, (256, 128, 8))

    def body(x_ref, g_ref, t_ref, loss_ref, dx_ref, dg_ref):
        @pl.when(pl.program_id(0) == 0)
        def _():
            dg_ref[...] = jnp.zeros_like(dg_ref)
            loss_ref[...] = jnp.zeros_like(loss_ref)

        xv = x_ref[...]
        gv = g_ref[...]
        r = lax.rsqrt(jnp.mean(xv * xv, axis=-1, keepdims=True) + NORM_EPS)
        xh = xv * r
        err = xh * gv - t_ref[...]
        part = 0.5 * jnp.sum(jnp.mean(err * err, axis=-1, keepdims=True), axis=0, keepdims=True)
        loss_ref[...] += jnp.broadcast_to(part, loss_ref.shape)
        dyv = err * (1.0 / d)
        dxh = dyv * gv
        dx_ref[...] = r * (dxh - xh * jnp.mean(dxh * xh, axis=-1, keepdims=True))
        dg_ref[...] += jnp.sum(dyv * xh, axis=0, keepdims=True)

    row = pl.BlockSpec((bm, d), lambda i: (i, 0))
    vec = pl.BlockSpec((1, d), lambda i: (0, 0))
    return pl.pallas_call(
        body, name="loss_head", grid=(s // bm,), in_specs=[row, vec, row],
        out_specs=[pl.BlockSpec((1, LANES), lambda i: (0, 0)), row, vec],
        out_shape=[jax.ShapeDtypeStruct((1, LANES), F32), jax.ShapeDtypeStruct((s, d), F32),
                   jax.ShapeDtypeStruct((1, d), F32)],
        compiler_params=_cp(("arbitrary",)),
    )(x, g, target)


def _colsum(x, name="colsum"):
    s, n = x.shape
    bm = _tile(s, (256, 128, 8))

    def body(x_ref, o_ref):
        @pl.when(pl.program_id(0) == 0)
        def _():
            o_ref[...] = jnp.zeros_like(o_ref)

        o_ref[...] += jnp.sum(x_ref[...].astype(F32), axis=0, keepdims=True)

    return pl.pallas_call(
        body, name=name, grid=(s // bm,), in_specs=[pl.BlockSpec((bm, n), lambda i: (i, 0))],
        out_specs=pl.BlockSpec((1, n), lambda i: (0, 0)), out_shape=jax.ShapeDtypeStruct((1, n), F32),
        compiler_params=_cp(("arbitrary",)),
    )(x)


def _swiglu_fwd(gu):
    s, f2 = gu.shape
    f = f2 // 2
    bm, bf = _tile(s, (512, 256, 128, 8)), _tile(f, (512, 256, 128))
    nf = f // bf

    def body(g_ref, u_ref, o_ref):
        gv = g_ref[...].astype(F32)
        o_ref[...] = (gv / (1.0 + jnp.exp(-gv)) * u_ref[...].astype(F32)).astype(o_ref.dtype)

    return pl.pallas_call(
        body, name="swiglu_fwd", grid=(s // bm, nf),
        in_specs=[pl.BlockSpec((bm, bf), lambda i, j: (i, j)), pl.BlockSpec((bm, bf), lambda i, j: (i, j + nf))],
        out_specs=pl.BlockSpec((bm, bf), lambda i, j: (i, j)),
        out_shape=jax.ShapeDtypeStruct((s, f), BF16), compiler_params=_cp(("parallel", "parallel")),
    )(gu, gu)


def _swiglu_bwd(gu, da):
    s, f2 = gu.shape
    f = f2 // 2
    bm, bf = _tile(s, (512, 256, 128, 8)), _tile(f, (512, 256, 128))
    nf = f // bf

    def body(g_ref, u_ref, da_ref, o_ref):
        t = pl.program_id(2)
        gv = g_ref[...].astype(F32)
        dav = da_ref[...].astype(F32)
        sig = 1.0 / (1.0 + jnp.exp(-gv))

        @pl.when(t == 0)
        def _():
            o_ref[...] = (dav * u_ref[...].astype(F32) * (sig * (1.0 + gv * (1.0 - sig)))).astype(o_ref.dtype)

        @pl.when(t == 1)
        def _():
            o_ref[...] = (dav * gv * sig).astype(o_ref.dtype)

    return pl.pallas_call(
        body, name="swiglu_bwd", grid=(s // bm, nf, 2),
        in_specs=[pl.BlockSpec((bm, bf), lambda i, j, t: (i, j)), pl.BlockSpec((bm, bf), lambda i, j, t: (i, j + nf)),
                  pl.BlockSpec((bm, bf), lambda i, j, t: (i, j))],
        out_specs=pl.BlockSpec((bm, bf), lambda i, j, t: (i, j + t * nf)),
        out_shape=jax.ShapeDtypeStruct((s, f2), BF16), compiler_params=_cp(("parallel", "parallel", "arbitrary")),
    )(gu, gu, da)


def _rope_tables(positions, rot, sign=1.0):
    half = rot // 2
    inv_freq = jnp.power(F32(ROPE_THETA), -jnp.arange(half, dtype=F32) * 2.0 / rot)
    ang = positions.reshape(-1).astype(F32)[:, None] * inv_freq
    cos, sin = jnp.cos(ang), jnp.sin(ang) * sign
    s = ang.shape[0]
    c_tab = jnp.concatenate([cos, cos, jnp.ones((s, LANES - rot), F32)], axis=1)
    s_tab = jnp.concatenate([-sin, sin, jnp.zeros((s, LANES - rot), F32)], axis=1)
    return c_tab, s_tab


def _rope(x, c_tab, s_tab, nblk, half, name):
    s = x.shape[0]
    bm = _tile(s, (512, 256, 128, 8))

    def body(x_ref, c_ref, s_ref, o_ref):
        xv = x_ref[...].astype(F32)
        lane = lax.broadcasted_iota(jnp.int32, xv.shape, 1)
        sw = jnp.where(lane < half, pltpu.roll(xv, LANES - half, 1), pltpu.roll(xv, half, 1))
        o_ref[...] = (xv * c_ref[...] + sw * s_ref[...]).astype(o_ref.dtype)

    blk = pl.BlockSpec((bm, LANES), lambda i, j: (i, j))
    tab = pl.BlockSpec((bm, LANES), lambda i, j: (i, 0))
    return pl.pallas_call(
        body, name=name, grid=(s // bm, nblk), in_specs=[blk, tab, tab], out_specs=blk,
        out_shape=jax.ShapeDtypeStruct((s, nblk * LANES), BF16), compiler_params=_cp(("parallel", "parallel")),
    )(x, c_tab, s_tab)


def _split_dot(x, t):
    hi = x.astype(BF16)
    lo = (x - hi.astype(F32)).astype(BF16)
    return jnp.dot(hi, t, preferred_element_type=F32) + jnp.dot(lo, t, preferred_element_type=F32)


def _sb_terms(q, kb, scale):
    z = lax.dot_general(q, kb, NT, preferred_element_type=F32) * scale
    u = jnp.log(1.0 + jnp.exp(-jnp.abs(z)))
    return jnp.minimum(-z, 0.0) - u, jnp.minimum(z, 0.0) - u


def _sb_heads_per_step(n_heads):
    return 2 if n_heads % 2 == 0 else 1


def _sb_fwd(qkv, n_heads, ex=None):
    s = qkv.shape[0]
    tq = _tile(s, (256, 128))
    hp = _sb_heads_per_step(n_heads)
    w = hp * LANES
    ng = n_heads // hp
    scale = LANES ** -0.5
    jobs = ex.take(4 * 4 * n_heads * s * s * LANES * EXCHANGE_BYTES_PER_FLOP) if ex is not None else []
    nj = len(jobs)

    def body(*refs):
        q_ref, k_ref, v_ref = refs[:3]
        o_ref, lt_ref = refs[3 + nj:5 + nj]
        i = pl.program_id(1)
        if nj:
            job_refs = (refs[3:3 + nj], refs[5 + nj:5 + 2 * nj], refs[5 + 2 * nj:])
            gi = pl.program_id(0)
            pl.when((gi == 0) & (i == 0))(lambda: ex.start(jobs, *job_refs))
            pl.when((gi == ng - 1) & (i == s // tq - 1))(lambda: ex.wait(jobs, *job_refs))
        row = lax.broadcasted_iota(jnp.int32, (tq, tq), 0)
        col = lax.broadcasted_iota(jnp.int32, (tq, tq), 1)
        below = col < row
        later = (row > col).astype(BF16)
        qs = [q_ref[:, h * LANES:(h + 1) * LANES] for h in range(hp)]

        def block(h, k0, c, acc, diag):
            sl = slice(h * LANES, (h + 1) * LANES)
            lk, logsig = _sb_terms(qs[h], k_ref[pl.ds(k0, tq), sl], scale)
            if diag:
                lk = jnp.where(below, lk, 0.0)
            a = jnp.exp(logsig + _split_dot(lk, later) + c)
            if diag:
                a = jnp.where(below, a, 0.0)
            acc = acc + jnp.dot(a.astype(BF16), v_ref[pl.ds(k0, tq), sl], preferred_element_type=F32)
            return c + jnp.sum(lk, axis=1, keepdims=True), acc

        d0 = pl.multiple_of(i * tq, tq)
        carry = []
        for h in range(hp):
            carry += list(block(h, d0, jnp.zeros((tq, 1), F32), jnp.zeros((tq, LANES), F32), True))

        def step(t, carry):
            k0 = pl.multiple_of((i - 1 - t) * tq, tq)
            out = []
            for h in range(hp):
                out += list(block(h, k0, carry[2 * h], carry[2 * h + 1], False))
            return tuple(out)

        carry = lax.fori_loop(0, i, step, tuple(carry))
        for h in range(hp):
            sl = slice(h * LANES, (h + 1) * LANES)
            o_ref[:, sl] = carry[2 * h + 1].astype(o_ref.dtype)
            lt_ref[:, sl] = jnp.broadcast_to(carry[2 * h], (tq, LANES))

    blk = pl.BlockSpec((tq, w), lambda g, i: (i, g))
    outs = pl.pallas_call(
        body, name="sb_fwd_carry" if nj else "sb_fwd", grid=(ng, s // tq),
        in_specs=[blk, pl.BlockSpec((s, w), lambda g, i: (0, ng + g)), pl.BlockSpec((s, w), lambda g, i: (0, 2 * ng + g))]
        + [ANY] * nj,
        out_specs=[blk, blk] + [ANY] * nj,
        out_shape=[jax.ShapeDtypeStruct((s, n_heads * LANES), BF16), jax.ShapeDtypeStruct((s, n_heads * LANES), F32)]
        + [j.recv_shape for j in jobs],
        scratch_shapes=_piece_sems(nj) if nj else [],
        compiler_params=_cp(("arbitrary", "arbitrary")),
    )(qkv, qkv, qkv, *[j.g for j in jobs])
    if nj:
        ex.landed(jobs, outs[2:])
    return outs[:2]


def _sb_bwd(qkv, do, ltot, n_heads, ex=None):
    s = qkv.shape[0]
    tq = _tile(s, (256, 128))
    hp = _sb_heads_per_step(n_heads)
    w = hp * LANES
    ng = n_heads // hp
    scale = LANES ** -0.5
    jobs = ex.take(4 * 9 * n_heads * s * s * LANES * EXCHANGE_BYTES_PER_FLOP) if ex is not None else []
    nj = len(jobs)

    def body(*refs):
        q_ref, k_ref, v_ref, do_ref, lt_ref = refs[:5]
        dq_ref, dk_ref, dv_ref = refs[5 + nj:8 + nj]
        i = pl.program_id(1)
        if nj:
            job_refs = (refs[5:5 + nj], refs[8 + nj:8 + 2 * nj], refs[8 + 2 * nj:])
            gi = pl.program_id(0)
            pl.when((gi == 0) & (i == 0))(lambda: ex.start(jobs, *job_refs))
            pl.when((gi == ng - 1) & (i == s // tq - 1))(lambda: ex.wait(jobs, *job_refs))

        @pl.when(i == 0)
        def _():
            dk_ref[...] = jnp.zeros_like(dk_ref)
            dv_ref[...] = jnp.zeros_like(dv_ref)

        row = lax.broadcasted_iota(jnp.int32, (tq, tq), 0)
        col = lax.broadcasted_iota(jnp.int32, (tq, tq), 1)
        below = col < row
        upto = (row <= col).astype(BF16)
        before = (row < col).astype(BF16)
        qs = [q_ref[:, h * LANES:(h + 1) * LANES] for h in range(hp)]
        dos = [do_ref[:, h * LANES:(h + 1) * LANES] for h in range(hp)]
        lts = [lt_ref[:, h * LANES:h * LANES + 1] for h in range(hp)]

        def block(h, k0, cpre, ce, dq, diag):
            sl = slice(h * LANES, (h + 1) * LANES)
            kb = k_ref[pl.ds(k0, tq), sl]
            vb = v_ref[pl.ds(k0, tq), sl]
            lk, logsig = _sb_terms(qs[h], kb, scale)
            if diag:
                lk = jnp.where(below, lk, 0.0)
            a = jnp.exp(logsig + (lts[h] - cpre) - _split_dot(lk, upto))
            if diag:
                a = jnp.where(below, a, 0.0)
            e = a * lax.dot_general(dos[h], vb, NT, preferred_element_type=F32)
            e_before = ce + jnp.dot(e.astype(BF16), before, preferred_element_type=F32)
            sig = jnp.exp(logsig)
            dz = (e - sig * (e + e_before)) * scale
            if diag:
                dz = jnp.where(below, dz, 0.0)
            dzb = dz.astype(BF16)
            dq = dq + jnp.dot(dzb, kb, preferred_element_type=F32)
            dk_ref[pl.ds(k0, tq), sl] += lax.dot_general(dzb, qs[h], TN, preferred_element_type=F32)
            dv_ref[pl.ds(k0, tq), sl] += lax.dot_general(a.astype(BF16), dos[h], TN, preferred_element_type=F32)
            return cpre + jnp.sum(lk, axis=1, keepdims=True), ce + jnp.sum(e, axis=1, keepdims=True), dq

        def step(j, carry):
            k0 = pl.multiple_of(j * tq, tq)
            out = []
            for h in range(hp):
                out += list(block(h, k0, *carry[3 * h:3 * h + 3], False))
            return tuple(out)

        z1 = jnp.zeros((tq, 1), F32)
        carry = lax.fori_loop(0, i, step, (z1, z1, jnp.zeros((tq, LANES), F32)) * hp)
        d0 = pl.multiple_of(i * tq, tq)
        for h in range(hp):
            _, _, dq = block(h, d0, *carry[3 * h:3 * h + 3], True)
            dq_ref[:, h * LANES:(h + 1) * LANES] = dq.astype(dq_ref.dtype)

    blk = pl.BlockSpec((tq, w), lambda g, i: (i, g))
    full = pl.BlockSpec((s, w), lambda g, i: (0, g))
    wt = n_heads * LANES
    outs = pl.pallas_call(
        body, name="sb_bwd_carry" if nj else "sb_bwd", grid=(ng, s // tq),
        in_specs=[blk, pl.BlockSpec((s, w), lambda g, i: (0, ng + g)), pl.BlockSpec((s, w), lambda g, i: (0, 2 * ng + g)),
                  blk, blk] + [ANY] * nj,
        out_specs=[blk, full, full] + [ANY] * nj,
        out_shape=[jax.ShapeDtypeStruct((s, wt), BF16), jax.ShapeDtypeStruct((s, wt), F32),
                   jax.ShapeDtypeStruct((s, wt), F32)] + [j.recv_shape for j in jobs],
        scratch_shapes=_piece_sems(nj) if nj else [],
        compiler_params=_cp(("arbitrary", "arbitrary")),
    )(qkv, qkv, qkv, do, ltot, *[j.g for j in jobs])
    if nj:
        ex.landed(jobs, outs[3:])
    return outs[:3]


def _band_masks(b, max_dist):
    qi = lax.broadcasted_iota(jnp.int32, (BAND, BAND), 0)
    kj = lax.broadcasted_iota(jnp.int32, (BAND, BAND), 1)
    dist = qi - kj
    return ((BAND + dist) <= max_dist) & (b > 0), (dist >= 0) & (dist <= max_dist)


def _band_fwd(qa, ka, va, *, n_cls, n_steps, hpb, group, q_blk, k_blk, v_blk, max_dist, scale, sinks, name):
    length = qa.shape[0]
    nb = length // BAND
    has_sink = sinks is not None
    qw, kw = hpb * LANES, (hpb // group) * LANES

    def body(*refs):
        q_ref, kp_ref, kc_ref, vp_ref, vc_ref = refs[:5]
        o_ref, lse_ref = refs[-2], refs[-1]
        mask_p, mask_c = _band_masks(pl.program_id(2), max_dist)
        for hh in range(hpb):
            qs = slice(hh * LANES, (hh + 1) * LANES)
            ks = slice((hh // group) * LANES, (hh // group + 1) * LANES)
            q = q_ref[:, qs].astype(BF16)
            s_p = lax.dot_general(q, kp_ref[:, ks].astype(BF16), NT, preferred_element_type=F32) * scale
            s_c = lax.dot_general(q, kc_ref[:, ks].astype(BF16), NT, preferred_element_type=F32) * scale
            s_p = jnp.where(mask_p, s_p, NEG)
            s_c = jnp.where(mask_c, s_c, NEG)
            m = jnp.maximum(jnp.max(s_p, axis=1, keepdims=True), jnp.max(s_c, axis=1, keepdims=True))
            l = jnp.sum(jnp.exp(s_p - m), axis=1, keepdims=True) + jnp.sum(jnp.exp(s_c - m), axis=1, keepdims=True)
            lse = m + jnp.log(l)
            if has_sink:
                sk = refs[5][:, hh * LANES:hh * LANES + 1]
                lse = jnp.maximum(lse, sk) + jnp.log(1.0 + jnp.exp(-jnp.abs(lse - sk)))
            p_p = jnp.exp(s_p - lse).astype(BF16)
            p_c = jnp.exp(s_c - lse).astype(BF16)
            o_ref[:, qs] = (jnp.dot(p_p, vp_ref[:, ks].astype(BF16), preferred_element_type=F32)
                            + jnp.dot(p_c, vc_ref[:, ks].astype(BF16), preferred_element_type=F32))
            lse_ref[:, qs] = jnp.broadcast_to(lse, (BAND, LANES))

    def prev(b):
        return jnp.maximum(b - 1, 0)

    specs = [pl.BlockSpec((BAND, qw), lambda n, st, b: (b, q_blk(n, st))),
             pl.BlockSpec((BAND, kw), lambda n, st, b: (prev(b), k_blk(n, st))),
             pl.BlockSpec((BAND, kw), lambda n, st, b: (b, k_blk(n, st))),
             pl.BlockSpec((BAND, kw), lambda n, st, b: (prev(b), v_blk(n, st))),
             pl.BlockSpec((BAND, kw), lambda n, st, b: (b, v_blk(n, st)))]
    ins = [qa, ka, ka, va, va]
    if has_sink:
        ins.append(sinks)
        specs.append(pl.BlockSpec((1, qw), lambda n, st, b: (0, st)))
    out = pl.BlockSpec((BAND, qw), lambda n, st, b: (b, n * n_steps + st))
    w = n_cls * n_steps * qw
    return pl.pallas_call(
        body, name=name, grid=(n_cls, n_steps, nb), in_specs=specs, out_specs=[out, out],
        out_shape=[jax.ShapeDtypeStruct((length, w), F32), jax.ShapeDtypeStruct((length, w), F32)],
        compiler_params=_cp(("parallel", "parallel", "parallel")),
    )(*ins)


def _band_bwd(qa, ka, va, o, do, lse, dlse, *, n_cls, n_steps, hpb, group, q_blk, k_blk, v_blk, max_dist, scale, sinks,
              name):
    length = qa.shape[0]
    nb = length // BAND
    has_sink, has_dlse = sinks is not None, dlse is not None
    qw, kw = hpb * LANES, (hpb // group) * LANES

    def body(*refs):
        q_ref, kp_ref, kc_ref, vp_ref, vc_ref, o_ref, do_ref, lse_ref = refs[:8]
        pos = 8
        dlse_ref = refs[pos] if has_dlse else None
        pos += has_dlse
        sink_ref = refs[pos] if has_sink else None
        pos += has_sink
        dq_ref, dkc_ref, dkp_ref, dvc_ref, dvp_ref = refs[pos:pos + 5]
        b = pl.program_id(2)
        mask_p, mask_c = _band_masks(b, max_dist)
        if has_sink:
            dsink_ref = refs[pos + 5]

            @pl.when(b == 0)
            def _():
                dsink_ref[...] = jnp.zeros_like(dsink_ref)

        for hh in range(hpb):
            qs = slice(hh * LANES, (hh + 1) * LANES)
            ks = slice((hh // group) * LANES, (hh // group + 1) * LANES)
            q = q_ref[:, qs].astype(BF16)
            kp, kc = kp_ref[:, ks].astype(BF16), kc_ref[:, ks].astype(BF16)
            vp, vc = vp_ref[:, ks].astype(BF16), vc_ref[:, ks].astype(BF16)
            dov = do_ref[:, qs].astype(F32)
            dob = dov.astype(BF16)
            lse_v = lse_ref[:, hh * LANES:hh * LANES + 1]
            s_p = lax.dot_general(q, kp, NT, preferred_element_type=F32) * scale
            s_c = lax.dot_general(q, kc, NT, preferred_element_type=F32) * scale
            p_p = jnp.where(mask_p, jnp.exp(jnp.where(mask_p, s_p, NEG) - lse_v), 0.0)
            p_c = jnp.where(mask_c, jnp.exp(jnp.where(mask_c, s_c, NEG) - lse_v), 0.0)
            delta = jnp.sum(dov * o_ref[:, qs], axis=1, keepdims=True)
            shift = -delta
            if has_dlse:
                shift = shift + dlse_ref[:, hh * LANES:hh * LANES + 1]
            dp_p = lax.dot_general(dob, vp, NT, preferred_element_type=F32)
            dp_c = lax.dot_general(dob, vc, NT, preferred_element_type=F32)
            ds_p = (p_p * (dp_p + shift) * scale).astype(BF16)
            ds_c = (p_c * (dp_c + shift) * scale).astype(BF16)
            dq_ref[:, qs] = (jnp.dot(ds_p, kp, preferred_element_type=F32)
                             + jnp.dot(ds_c, kc, preferred_element_type=F32))
            parts = (lax.dot_general(ds_c, q, TN, preferred_element_type=F32),
                     lax.dot_general(ds_p, q, TN, preferred_element_type=F32),
                     lax.dot_general(p_c.astype(BF16), dob, TN, preferred_element_type=F32),
                     lax.dot_general(p_p.astype(BF16), dob, TN, preferred_element_type=F32))
            for ref, part in zip((dkc_ref, dkp_ref, dvc_ref, dvp_ref), parts):
                if hh % group == 0:
                    ref[:, ks] = part
                else:
                    ref[:, ks] += part
            if has_sink:
                p_sink = jnp.exp(sink_ref[:, hh * LANES:hh * LANES + 1] - lse_v)
                dsink_ref[:, qs] += jnp.broadcast_to(jnp.sum(-p_sink * delta, axis=0, keepdims=True), (1, LANES))

    def prev(b):
        return jnp.maximum(b - 1, 0)

    per_q = pl.BlockSpec((BAND, qw), lambda n, st, b: (b, n * n_steps + st))
    per_k = pl.BlockSpec((BAND, kw), lambda n, st, b: (b, n * n_steps + st))
    specs = [pl.BlockSpec((BAND, qw), lambda n, st, b: (b, q_blk(n, st))),
             pl.BlockSpec((BAND, kw), lambda n, st, b: (prev(b), k_blk(n, st))),
             pl.BlockSpec((BAND, kw), lambda n, st, b: (b, k_blk(n, st))),
             pl.BlockSpec((BAND, kw), lambda n, st, b: (prev(b), v_blk(n, st))),
             pl.BlockSpec((BAND, kw), lambda n, st, b: (b, v_blk(n, st))),
             per_q, per_q, per_q]
    ins = [qa, ka, ka, va, va, o, do, lse]
    if has_dlse:
        ins.append(dlse)
        specs.append(per_q)
    out_specs = [per_q] + [per_k] * 4
    out_shape = ([jax.ShapeDtypeStruct((length, n_cls * n_steps * qw), F32)]
                 + [jax.ShapeDtypeStruct((length, n_cls * n_steps * kw), F32)] * 4)
    if has_sink:
        ins.append(sinks)
        specs.append(pl.BlockSpec((1, qw), lambda n, st, b: (0, st)))
        out_specs = out_specs + [pl.BlockSpec((1, qw), lambda n, st, b: (0, st))]
        out_shape = out_shape + [jax.ShapeDtypeStruct((1, n_steps * qw), F32)]
    return pl.pallas_call(
        body, name=name, grid=(n_cls, n_steps, nb), in_specs=specs, out_specs=out_specs, out_shape=out_shape,
        compiler_params=_cp(("parallel", "parallel", "arbitrary")),
    )(*ins)


def _band_fold(cur, prv, *, n_cls, n_heads, group, name):
    length = cur.shape[0]
    nb = length // BAND
    n_kv = n_heads // group

    def body(c_ref, p_ref, o_ref):
        b, gq = pl.program_id(2), pl.program_id(3)

        @pl.when(gq == 0)
        def _():
            o_ref[...] = jnp.zeros_like(o_ref)

        o_ref[...] += c_ref[...] + jnp.where(b + 1 < nb, p_ref[...], 0.0)

    blk = (BAND, LANES)
    return pl.pallas_call(
        body, name=name, grid=(n_cls, n_kv, nb, group),
        in_specs=[pl.BlockSpec(blk, lambda n, h, b, gq: (b, n * n_heads + h * group + gq)),
                  pl.BlockSpec(blk, lambda n, h, b, gq: (jnp.minimum(b + 1, nb - 1), n * n_heads + h * group + gq))],
        out_specs=pl.BlockSpec(blk, lambda n, h, b, gq: (b, n * n_kv + h)),
        out_shape=jax.ShapeDtypeStruct((length, n_cls * n_kv * LANES), F32),
        compiler_params=_cp(("parallel", "parallel", "parallel", "arbitrary")),
    )(cur, prv)


def _dil_mix_fwd(os_, lses):
    s, w = os_[0].shape
    bm = _tile(s, (256, 128, 8))

    def body(o0, o1, o2, l0, l1, l2, out_ref):
        ls = [l0[...], l1[...], l2[...]]
        m = jnp.maximum(jnp.maximum(ls[0], ls[1]), ls[2])
        es = [jnp.exp(v - m) for v in ls]
        inv = 1.0 / (es[0] + es[1] + es[2])
        for gi, o_ref in enumerate((o0, o1, o2)):
            out_ref[:, gi * w:(gi + 1) * w] = (o_ref[...] * (es[gi] * inv)).astype(out_ref.dtype)

    blk = pl.BlockSpec((bm, w), lambda i: (i, 0))
    return pl.pallas_call(
        body, name="dil_mix_fwd", grid=(s // bm,), in_specs=[blk] * 6,
        out_specs=pl.BlockSpec((bm, 3 * w), lambda i: (i, 0)),
        out_shape=jax.ShapeDtypeStruct((s, 3 * w), BF16), compiler_params=_cp(("parallel",)),
    )(*os_, *lses)


def _dil_mix_bwd(os_, lses, dmixed):
    s, w = os_[0].shape
    bm = _tile(s, (256, 128, 8))
    hg = w // LANES

    def body(o0, o1, o2, l0, l1, l2, dm_ref, do0, do1, do2, dl0, dl1, dl2):
        ls = [l0[...], l1[...], l2[...]]
        m = jnp.maximum(jnp.maximum(ls[0], ls[1]), ls[2])
        es = [jnp.exp(v - m) for v in ls]
        inv = 1.0 / (es[0] + es[1] + es[2])
        alphas = [e * inv for e in es]
        dalphas = []
        for gi, (o_ref, do_ref) in enumerate(((o0, do0), (o1, do1), (o2, do2))):
            dm = dm_ref[:, gi * w:(gi + 1) * w].astype(F32)
            do_ref[...] = (dm * alphas[gi]).astype(do_ref.dtype)
            prod = dm * o_ref[...]
            parts = [jnp.broadcast_to(jnp.sum(prod[:, j * LANES:(j + 1) * LANES], axis=1, keepdims=True), (bm, LANES))
                     for j in range(hg)]
            dalphas.append(jnp.concatenate(parts, axis=1) if hg > 1 else parts[0])
        mean = alphas[0] * dalphas[0] + alphas[1] * dalphas[1] + alphas[2] * dalphas[2]
        for gi, dl_ref in enumerate((dl0, dl1, dl2)):
            dl_ref[...] = alphas[gi] * (dalphas[gi] - mean)

    blk = pl.BlockSpec((bm, w), lambda i: (i, 0))
    return pl.pallas_call(
        body, name="dil_mix_bwd", grid=(s // bm,), in_specs=[blk] * 6 + [pl.BlockSpec((bm, 3 * w), lambda i: (i, 0))],
        out_specs=[blk] * 6,
        out_shape=[jax.ShapeDtypeStruct((s, w), BF16)] * 3 + [jax.ShapeDtypeStruct((s, w), F32)] * 3,
        compiler_params=_cp(("parallel",)),
    )(*os_, *lses, dmixed)


def _xattn_fwd(q, kv):
    s, w = q.shape
    mlen = kv.shape[0]
    tq = _tile(s, (512, 256, 128))
    scale = LANES ** -0.5

    def body(q_ref, k_ref, v_ref, o_ref):
        for h in range(XA_HEADS):
            sl = slice(h * LANES, (h + 1) * LANES)
            sc = lax.dot_general(q_ref[:, sl], k_ref[:, sl], NT, preferred_element_type=F32) * scale
            m = jnp.max(sc, axis=1, keepdims=True)
            e = jnp.exp(sc - m)
            p = e / jnp.sum(e, axis=1, keepdims=True)
            o_ref[:, sl] = jnp.dot(p.astype(BF16), v_ref[:, sl], preferred_element_type=F32).astype(o_ref.dtype)

    return pl.pallas_call(
        body, name="xattn_fwd", grid=(s // tq,),
        in_specs=[pl.BlockSpec((tq, w), lambda i: (i, 0)), pl.BlockSpec((mlen, w), lambda i: (0, 0)),
                  pl.BlockSpec((mlen, w), lambda i: (0, 1))],
        out_specs=pl.BlockSpec((tq, w), lambda i: (i, 0)),
        out_shape=jax.ShapeDtypeStruct((s, w), BF16), compiler_params=_cp(("parallel",)),
    )(q, kv, kv)


def _xattn_bwd(q, kv, do):
    s, w = q.shape
    mlen = kv.shape[0]
    tq = _tile(s, (512, 256, 128))
    scale = LANES ** -0.5

    def body(q_ref, k_ref, v_ref, do_ref, dq_ref, dk_ref, dv_ref):
        @pl.when(pl.program_id(0) == 0)
        def _():
            dk_ref[...] = jnp.zeros_like(dk_ref)
            dv_ref[...] = jnp.zeros_like(dv_ref)

        for h in range(XA_HEADS):
            sl = slice(h * LANES, (h + 1) * LANES)
            qh, kh, vh, doh = q_ref[:, sl], k_ref[:, sl], v_ref[:, sl], do_ref[:, sl]
            sc = lax.dot_general(qh, kh, NT, preferred_element_type=F32) * scale
            m = jnp.max(sc, axis=1, keepdims=True)
            e = jnp.exp(sc - m)
            p = e / jnp.sum(e, axis=1, keepdims=True)
            dp = lax.dot_general(doh, vh, NT, preferred_element_type=F32)
            ds = (p * (dp - jnp.sum(p * dp, axis=1, keepdims=True)) * scale).astype(BF16)
            dq_ref[:, sl] = jnp.dot(ds, kh, preferred_element_type=F32).astype(dq_ref.dtype)
            dk_ref[:, sl] += lax.dot_general(ds, qh, TN, preferred_element_type=F32)
            dv_ref[:, sl] += lax.dot_general(p.astype(BF16), doh, TN, preferred_element_type=F32)

    row = pl.BlockSpec((tq, w), lambda i: (i, 0))
    acc = pl.BlockSpec((mlen, w), lambda i: (0, 0))
    return pl.pallas_call(
        body, name="xattn_bwd", grid=(s // tq,),
        in_specs=[row, acc, pl.BlockSpec((mlen, w), lambda i: (0, 1)), row],
        out_specs=[row, acc, acc],
        out_shape=[jax.ShapeDtypeStruct((s, w), BF16), jax.ShapeDtypeStruct((mlen, w), F32),
                   jax.ShapeDtypeStruct((mlen, w), F32)],
        compiler_params=_cp(("arbitrary",)),
    )(q, kv, kv, do)


def _adamw(parts, w, m, v, name):
    r, c = w.shape
    npc = len(parts)
    rp = r // npc
    br = _tile(rp, tuple(p for p in (256, 128, 64, 32, 16, 8) if p * c * 4 <= 1024 * 1024))
    steps = rp // br

    def body(*refs):
        w_ref, m_ref, v_ref, g_ref, d_ref, nm_ref, nv_ref = refs[npc:]

        def update(p_ref):
            g = p_ref[0].astype(F32)
            for t in range(1, N_DEV):
                g = g + p_ref[t].astype(F32)
            nm = ADAM_B1 * m_ref[...] + (1.0 - ADAM_B1) * g
            nv = ADAM_B2 * v_ref[...] + (1.0 - ADAM_B2) * (g * g)
            m_hat = nm / (1.0 - ADAM_B1 ** ADAM_STEP)
            v_hat = nv / (1.0 - ADAM_B2 ** ADAM_STEP)
            g_ref[...] = g
            d_ref[...] = -ADAM_LR * (m_hat / (jnp.sqrt(v_hat) + ADAM_EPS) + ADAM_WD * w_ref[...])
            nm_ref[...] = nm
            nv_ref[...] = nv

        for k in range(npc):
            pl.when(pl.program_id(0) == k)(lambda k=k: update(refs[k]))

    blk = pl.BlockSpec((br, c), lambda p, i: (p * steps + i, 0))
    part_specs = [pl.BlockSpec((N_DEV, br, c), lambda p, i, k=k: (0, jnp.where(p == k, i, 0), 0)) for k in range(npc)]
    return pl.pallas_call(
        body, name=name, grid=(npc, steps), in_specs=part_specs + [blk, blk, blk], out_specs=[blk] * 4,
        out_shape=[jax.ShapeDtypeStruct((r, c), F32)] * 4, compiler_params=_cp(("parallel", "parallel")),
    )(*parts, w, m, v)


MESH_ID = pl.DeviceIdType.MESH
ANY = pl.BlockSpec(memory_space=pl.ANY)


def _my_place():
    return lax.axis_index("x"), lax.axis_index("y"), lax.axis_index("c")


def _all_gather(xs, name):
    nt = len(xs)

    def body(*refs):
        x_refs, out_refs = refs[:nt], refs[nt:2 * nt]
        send_sems, recv_sems, local_sems = refs[2 * nt:]
        x, y, c = _my_place()
        me, sibling = (x, y, c), (x, y, 1 - c)
        chips = [(1 - x, y), (x, 1 - y), (1 - x, 1 - y)]

        def slot(t, p):
            return out_refs[t].at[4 * p[0] + 2 * p[1] + p[2]]

        def copy(t, k, block, to, src=None):
            return pltpu.make_async_remote_copy(
                src_ref=slot(t, block) if src is None else src, dst_ref=slot(t, block),
                send_sem=send_sems.at[7 * t + k], recv_sem=recv_sems.at[7 * t + k], device_id=to,
                device_id_type=MESH_ID)

        mine, first, passed = [], [], []
        for t in range(nt):
            cp = pltpu.make_async_copy(x_refs[t], slot(t, me), local_sems.at[t])
            cp.start()
            mine.append(cp)
            group = [copy(t, 0, me, sibling, src=x_refs[t])]
            group += [copy(t, 1 + j, me, (*chip, c), src=x_refs[t]) for j, chip in enumerate(chips)]
            for cp in group:
                cp.start()
            first += group
        for t in range(nt):
            for j, chip in enumerate(chips):
                copy(t, 1 + j, (*chip, c), me).wait_recv()
                fw = copy(t, 4 + j, (*chip, c), sibling)
                fw.start()
                passed.append(fw)
        for t in range(nt):
            copy(t, 0, sibling, me).wait_recv()
            for j, chip in enumerate(chips):
                copy(t, 4 + j, (*chip, 1 - c), me).wait_recv()
        for cp in first + passed:
            cp.wait_send()
        for cp in mine:
            cp.wait()

    return pl.pallas_call(
        body, name=name, in_specs=[ANY] * nt, out_specs=[ANY] * nt,
        out_shape=[jax.ShapeDtypeStruct((N_DEV,) + tuple(v.shape), v.dtype) for v in xs],
        scratch_shapes=[pltpu.SemaphoreType.DMA((7 * nt,)), pltpu.SemaphoreType.DMA((7 * nt,)),
                        pltpu.SemaphoreType.DMA((nt,))],
    )(*xs)


def _exchange(gs, name):
    nt = len(gs)

    def body(*refs):
        g_refs, out_refs = refs[:nt], refs[nt:2 * nt]
        send_sems, recv_sems, local_sems = refs[2 * nt:]
        x, y, c = _my_place()
        my_slot = 4 * x + 2 * y + c
        mine, sent = [], []
        for t in range(nt):
            cp = pltpu.make_async_copy(g_refs[t].at[my_slot], out_refs[t].at[my_slot], local_sems.at[t])
            cp.start()
            mine.append(cp)
            for rel in range(1, N_DEV):
                px, py, pc = x ^ ((rel >> 2) & 1), y ^ ((rel >> 1) & 1), c ^ (rel & 1)
                cp = pltpu.make_async_remote_copy(
                    src_ref=g_refs[t].at[4 * px + 2 * py + pc], dst_ref=out_refs[t].at[my_slot],
                    send_sem=send_sems.at[7 * t + rel - 1], recv_sem=recv_sems.at[7 * t + rel - 1],
                    device_id=(px, py, pc), device_id_type=MESH_ID)
                cp.start()
                sent.append(cp)
        for cp in sent:
            cp.wait_recv()
        for cp in sent:
            cp.wait_send()
        for cp in mine:
            cp.wait()

    return pl.pallas_call(
        body, name=name, in_specs=[ANY] * nt, out_specs=[ANY] * nt,
        out_shape=[jax.ShapeDtypeStruct(tuple(v.shape), v.dtype) for v in gs],
        scratch_shapes=[pltpu.SemaphoreType.DMA((7 * nt,)), pltpu.SemaphoreType.DMA((7 * nt,)),
                        pltpu.SemaphoreType.DMA((nt,))],
    )(*gs)


EXCHANGE_BYTES_PER_FLOP = 1.3e-4
PIECE_BYTES = 24 * 1024 * 1024
CARRIER_MIN_BYTES = 6 * 1024 * 1024
ROW_ALIGN = 16


class _Piece:
    def __init__(self, key, g, axis, lo, hi):
        self.key, self.g, self.axis, self.lo, self.hi = key, g, axis, lo, hi
        cols = g.shape[1] if axis == 0 else g.shape[1] // N_DEV
        self.recv_shape = jax.ShapeDtypeStruct((N_DEV, hi - lo, cols), g.dtype)
        self.nbytes = N_DEV * (hi - lo) * cols * g.dtype.itemsize


def _piece_sems(nj):
    return [pltpu.SemaphoreType.DMA((7 * nj,)), pltpu.SemaphoreType.DMA((7 * nj,)), pltpu.SemaphoreType.DMA((nj,))]


def _piece_copies(jobs, g_refs, recv_refs, sems):
    send_sems, recv_sems, local_sems = sems
    x, y, c = _my_place()
    me = 4 * x + 2 * y + c
    local, remote = [], []
    for t, (job, g, r) in enumerate(zip(jobs, g_refs, recv_refs)):
        rows = job.hi - job.lo

        def block(slot, job=job, g=g, r=r, rows=rows):
            if job.axis == 0:
                start = pl.multiple_of(slot * (g.shape[0] // N_DEV) + job.lo, ROW_ALIGN)
                return g.at[pl.ds(start, rows), :]
            cols = r.shape[2]
            return g.at[pl.ds(job.lo, rows), pl.ds(pl.multiple_of(slot * cols, LANES), cols)]

        local.append(pltpu.make_async_copy(block(me), r.at[me], local_sems.at[t]))
        for rel in range(1, N_DEV):
            px, py, pc = x ^ ((rel >> 2) & 1), y ^ ((rel >> 1) & 1), c ^ (rel & 1)
            remote.append(pltpu.make_async_remote_copy(
                src_ref=block(4 * px + 2 * py + pc), dst_ref=r.at[me], send_sem=send_sems.at[7 * t + rel - 1],
                recv_sem=recv_sems.at[7 * t + rel - 1], device_id=(px, py, pc), device_id_type=MESH_ID))
    return local, remote


def _pieces_start(jobs, g_refs, recv_refs, sems):
    local, remote = _piece_copies(jobs, g_refs, recv_refs, sems)
    for cp in local + remote:
        cp.start()


def _pieces_wait(jobs, g_refs, recv_refs, sems):
    local, remote = _piece_copies(jobs, g_refs, recv_refs, sems)
    for cp in remote:
        cp.wait_recv()
    for cp in remote:
        cp.wait_send()
    for cp in local:
        cp.wait()


def _exchange_pieces(jobs, name):
    nj = len(jobs)

    def body(*refs):
        job_refs = (refs[:nj], refs[nj:2 * nj], refs[2 * nj:])
        _pieces_start(jobs, *job_refs)
        _pieces_wait(jobs, *job_refs)

    return pl.pallas_call(
        body, name=name, in_specs=[ANY] * nj, out_specs=[ANY] * nj, out_shape=[j.recv_shape for j in jobs],
        scratch_shapes=_piece_sems(nj),
    )(*[j.g for j in jobs])


class _GradExchange:
    def __init__(self):
        self.queue, self.recv = [], {}

    def put(self, name, layer, g):
        axis = SHARD_AXIS[name] - 1
        rows = g.shape[0] // N_DEV if axis == 0 else g.shape[0]
        if g.dtype != BF16 or rows % ROW_ALIGN or (axis == 1 and (g.shape[1] // N_DEV) % LANES):
            return False
        n_split = max(1, round(g.size * g.dtype.itemsize / PIECE_BYTES))
        while rows % (n_split * ROW_ALIGN):
            n_split -= 1
        for k in range(n_split):
            self.queue.append(_Piece((name, layer, k), g, axis, k * rows // n_split, (k + 1) * rows // n_split))
        return True

    def take(self, capacity):
        jobs, used = [], 0
        if capacity >= CARRIER_MIN_BYTES and self.queue and self.queue[0].nbytes <= 2.2 * capacity:
            jobs.append(self.queue.pop(0))
            used = jobs[0].nbytes
            while self.queue and used + self.queue[0].nbytes <= capacity:
                used += self.queue[0].nbytes
                jobs.append(self.queue.pop(0))
        return jobs

    def landed(self, jobs, recvs):
        for j, r in zip(jobs, recvs):
            self.recv[j.key] = r

    def flush(self):
        if self.queue:
            jobs, self.queue = self.queue, []
            self.landed(jobs, _exchange_pieces(jobs, "exchange_rest"))

    def pieces_of(self, name):
        return [self.recv[k] for k in sorted(k for k in self.recv if k[0] == name)]

    start = staticmethod(_pieces_start)
    wait = staticmethod(_pieces_wait)


GATHER_SPEEDUP = 2.0


class _WeightPiece:
    def __init__(self, key, local, axis):
        self.key, self.g, self.axis = key, local, axis
        r, c = local.shape
        self.recv_shape = jax.ShapeDtypeStruct((r * N_DEV, c) if axis == 0 else (r, c * N_DEV), local.dtype)
        self.nbytes = N_DEV * r * c * local.dtype.itemsize


def _gather_copies(jobs, x_refs, full_refs, sems):
    send_sems, recv_sems, local_sems = sems
    x, y, c = _my_place()
    me, sibling = (x, y, c), (x, y, 1 - c)
    chips = [(1 - x, y), (x, 1 - y), (1 - x, 1 - y)]
    plans = []
    for t, (job, xr, fr) in enumerate(zip(jobs, x_refs, full_refs)):
        def blk(p, job=job, xr=xr, fr=fr):
            slot = 4 * p[0] + 2 * p[1] + p[2]
            if job.axis == 0:
                return fr.at[pl.ds(pl.multiple_of(slot * xr.shape[0], ROW_ALIGN), xr.shape[0]), :]
            return fr.at[:, pl.ds(pl.multiple_of(slot * xr.shape[1], LANES), xr.shape[1])]

        def copy(k, block, to, src=None, t=t, blk=blk):
            return pltpu.make_async_remote_copy(
                src_ref=blk(block) if src is None else src, dst_ref=blk(block), send_sem=send_sems.at[7 * t + k],
                recv_sem=recv_sems.at[7 * t + k], device_id=to, device_id_type=MESH_ID)

        plans.append(dict(
            mine=pltpu.make_async_copy(xr, blk(me), local_sems.at[t]),
            first=[copy(0, me, sibling, src=xr)] + [copy(1 + j, me, (*chip, c), src=xr) for j, chip in enumerate(chips)],
            landed=[copy(1 + j, (*chip, c), me) for j, chip in enumerate(chips)],
            passed=[copy(4 + j, (*chip, c), sibling) for j, chip in enumerate(chips)],
            from_sibling=[copy(0, sibling, me)] + [copy(4 + j, (*chip, 1 - c), me) for j, chip in enumerate(chips)]))
    return plans


def _gather_start(jobs, x_refs, full_refs, sems):
    for plan in _gather_copies(jobs, x_refs, full_refs, sems):
        plan["mine"].start()
        for cp in plan["first"]:
            cp.start()


def _gather_wait(jobs, x_refs, full_refs, sems):
    plans = _gather_copies(jobs, x_refs, full_refs, sems)
    for plan in plans:
        for landed, passed in zip(plan["landed"], plan["passed"]):
            landed.wait_recv()
            passed.start()
    for plan in plans:
        for cp in plan["from_sibling"]:
            cp.wait_recv()
        for cp in plan["first"] + plan["passed"]:
            cp.wait_send()
        plan["mine"].wait()


def _gather_pieces(jobs, name):
    nj = len(jobs)

    def body(*refs):
        job_refs = (refs[:nj], refs[nj:2 * nj], refs[2 * nj:])
        _gather_start(jobs, *job_refs)
        _gather_wait(jobs, *job_refs)

    return pl.pallas_call(
        body, name=name, in_specs=[ANY] * nj, out_specs=[ANY] * nj, out_shape=[j.recv_shape for j in jobs],
        scratch_shapes=_piece_sems(nj),
    )(*[j.g for j in jobs])


class _WeightGather:
    def __init__(self):
        self.queue, self.full = [], {}

    def add(self, name, layer, local):
        self.queue.append(_WeightPiece((name, layer), local, SHARD_AXIS[name] - 1))

    def take(self, capacity):
        capacity *= GATHER_SPEEDUP
        jobs, used = [], 0
        if capacity >= CARRIER_MIN_BYTES and self.queue and self.queue[0].nbytes <= 2.2 * capacity:
            jobs.append(self.queue.pop(0))
            used = jobs[0].nbytes
            while self.queue and used + self.queue[0].nbytes <= capacity:
                used += self.queue[0].nbytes
                jobs.append(self.queue.pop(0))
        return jobs

    def landed(self, jobs, fulls):
        for j, f in zip(jobs, fulls):
            self.full[j.key] = f

    def get(self, name, layer):
        if (name, layer) not in self.full:
            at = [j.key for j in self.queue].index((name, layer))
            jobs, self.queue = self.queue[:at + 1], self.queue[at + 1:]
            self.landed(jobs, _gather_pieces(jobs, "gather_now"))
        return self.full[(name, layer)]

    start = staticmethod(_gather_start)
    wait = staticmethod(_gather_wait)


def _val(w):
    return w() if callable(w) else w


def _ffn_fwd(x, g, w_gu, w_d, tag, wg=None):
    n = _rms_fwd(x, g)
    gu = _mm(n, _val(w_gu), out_dtype=BF16, name=f"{tag}_gu", ex=wg)
    a = _swiglu_fwd(gu)
    return _mm(a, _val(w_d), alpha=0.5, res=x, name=f"{tag}_down", ex=wg), (n, gu, a)


def _put(ex, name, layer, g):
    if ex is not None:
        ex.put(name, layer, g)


def _ffn_bwd(dy, x, g, w_gu, w_d, saved, tag, ex=None, which="ffn1", layer=0):
    n, gu, a = saved
    da = _mm(dy, w_d, tb=True, alpha=0.5, out_dtype=BF16, name=f"{tag}_da", ex=ex)
    d_wd = _mm(a, dy, ta=True, alpha=0.5, out_dtype=BF16, name=f"{tag}_dwd", ex=ex)
    _put(ex, f"{which}_w_down", layer, d_wd)
    dgu = _swiglu_bwd(gu, da)
    dn = _mm(dgu, w_gu, tb=True, name=f"{tag}_dn", ex=ex)
    d_wgu = _mm(n, dgu, ta=True, out_dtype=BF16, name=f"{tag}_dwgu", ex=ex)
    _put(ex, f"{which}_w_gate_up", layer, d_wgu)
    dx, dg = _rms_bwd(x, g, dn, dy)
    return dx, dg, d_wgu, d_wd


def _sb_mixer_fwd(h, w_qkv, w_o, x, wg=None):
    qkv = _mm(h, _val(w_qkv), out_dtype=BF16, name="sb_qkv", ex=wg)
    o, ltot = _sb_fwd(qkv, qkv.shape[1] // (3 * LANES), wg)
    return _mm(o, _val(w_o), res=x, name="sb_out", ex=wg), (qkv, o, ltot)


def _sb_mixer_bwd(dy, h, w_qkv, w_o, saved, ex=None, layer=0):
    qkv, o, ltot = saved
    n_heads = w_o.shape[0] // LANES
    do = _mm(dy, w_o, tb=True, out_dtype=BF16, name="sb_do", ex=ex)
    d_wo = _mm(o, dy, ta=True, out_dtype=BF16, name="sb_dwo", ex=ex)
    _put(ex, "sb_w_o", layer, d_wo)
    dq, dk, dv = _sb_bwd(qkv, do, ltot, n_heads, ex)
    dqkv = jnp.concatenate([dq, dk.astype(BF16), dv.astype(BF16)], axis=1)
    dh = _mm(dqkv, w_qkv, tb=True, name="sb_dh", ex=ex)
    d_wqkv = _mm(h, dqkv, ta=True, out_dtype=BF16, name="sb_dwqkv", ex=ex)
    _put(ex, "sb_w_qkv", layer, d_wqkv)
    return dh, d_wqkv, d_wo


def _dil_cols(gi):
    ng = len(DIL_PATTERNS)
    return dict(q_blk=lambda n, st: n * 2 * ng + gi, k_blk=lambda n, st: n * 2 * ng + ng + gi,
                v_blk=lambda n, st: n * 3 * ng + 2 * ng + gi)


def _dil_mixer_fwd(h, w_qkv, w_o, x, tabs, wg=None):
    s = h.shape[0]
    qkv = _mm(h, _val(w_qkv), name="dil_qkv", ex=wg)
    n_all = qkv.shape[1] // (3 * LANES)
    hg = n_all // len(DIL_PATTERNS)
    qk = _rope(qkv, tabs[0], tabs[1], 2 * n_all, 16, "dil_rope")
    os_, lses = [], []
    for gi, (window, dil) in enumerate(DIL_PATTERNS):
        o, lse = _band_fwd(qk.reshape(s // dil, -1), qk.reshape(s // dil, -1), qkv.reshape(s // dil, -1),
                           n_cls=dil, n_steps=1, hpb=hg, group=1, **_dil_cols(gi),
                           max_dist=window // dil, scale=LANES ** -0.5, sinks=None, name=f"dil_fwd{gi}")
        os_.append(o.reshape(s, hg * LANES))
        lses.append(lse.reshape(s, hg * LANES))
    mixed = _dil_mix_fwd(os_, lses)
    return _mm(mixed, _val(w_o), res=x, name="dil_out", ex=wg), (qkv, qk, os_, lses, mixed)


def _dil_mixer_bwd(dy, h, w_qkv, w_o, saved, tabs_bwd, ex=None):
    qkv, qk, os_, lses, mixed = saved
    s = h.shape[0]
    n_all = w_o.shape[0] // LANES
    hg = n_all // len(DIL_PATTERNS)
    dmixed = _mm(dy, w_o, tb=True, out_dtype=BF16, name="dil_dmix", ex=ex)
    d_wo = _mm(mixed, dy, ta=True, out_dtype=BF16, name="dil_dwo", ex=ex)
    _put(ex, "dil_w_o", 0, d_wo)
    mix_out = _dil_mix_bwd(os_, lses, dmixed)
    dos, dlses = mix_out[:3], mix_out[3:]
    dqs, dks, dvs = [], [], []
    for gi, (window, dil) in enumerate(DIL_PATTERNS):
        length = s // dil
        dq, dkc, dkp, dvc, dvp = _band_bwd(
            qk.reshape(length, -1), qk.reshape(length, -1), qkv.reshape(length, -1), os_[gi].reshape(length, -1),
            dos[gi].reshape(length, -1), lses[gi].reshape(length, -1), dlses[gi].reshape(length, -1),
            n_cls=dil, n_steps=1, hpb=hg, group=1, **_dil_cols(gi), max_dist=window // dil,
            scale=LANES ** -0.5, sinks=None, name=f"dil_bwd{gi}")
        dqs.append(dq.reshape(s, -1))
        dks.append(_band_fold(dkc, dkp, n_cls=dil, n_heads=hg, group=1, name=f"dil_foldk{gi}").reshape(s, -1))
        dvs.append(_band_fold(dvc, dvp, n_cls=dil, n_heads=hg, group=1, name=f"dil_foldv{gi}").reshape(s, -1))
    dqk_rot = jnp.concatenate(dqs + dks, axis=1)
    dqk = _rope(dqk_rot, tabs_bwd[0], tabs_bwd[1], 2 * n_all, 16, "dil_rope_bwd")
    dqkv = jnp.concatenate([dqk] + [t.astype(BF16) for t in dvs], axis=1)
    dh = _mm(dqkv, w_qkv, tb=True, name="dil_dh", ex=ex)
    d_wqkv = _mm(h, dqkv, ta=True, out_dtype=BF16, name="dil_dwqkv", ex=ex)
    return dh, d_wqkv, d_wo


def _pad_heads(w, axis):
    shape = list(w.shape)
    n = shape[axis] // SWA_HEAD_DIM
    w = w.reshape(shape[:axis] + [n, SWA_HEAD_DIM] + shape[axis + 1:])
    pad = [(0, 0)] * w.ndim
    pad[axis + 1] = (0, LANES - SWA_HEAD_DIM)
    shape[axis] = n * LANES
    return jnp.pad(w, pad).reshape(shape)


def _unpad_heads(w, axis):
    shape = list(w.shape)
    n = shape[axis] // LANES
    w = w.reshape(shape[:axis] + [n, LANES] + shape[axis + 1:])
    w = lax.slice_in_dim(w, 0, SWA_HEAD_DIM, axis=axis + 1)
    shape[axis] = n * SWA_HEAD_DIM
    return w.reshape(shape)


def _swa_mixer_fwd(h, w_qkv_p, b_qkv_p, sinks_b, w_o_p, b_o, x, tabs, wg=None):
    nq = w_o_p.shape[0] // LANES
    nkv = nq // SWA_GROUP
    qkv = _mm(h, w_qkv_p, bias=b_qkv_p, name="swa_qkv", ex=wg)
    qk = _rope(qkv, tabs[0], tabs[1], nq + nkv, 8, "swa_rope")
    o, lse = _band_fwd(qk, qk, qkv, n_cls=1, n_steps=nkv, hpb=SWA_GROUP, group=SWA_GROUP, q_blk=lambda n, st: st,
                       k_blk=lambda n, st: nq + st, v_blk=lambda n, st: nq + nkv + st,
                       max_dist=SWA_WINDOW - 1, scale=SWA_HEAD_DIM ** -0.5, sinks=sinks_b, name="swa_fwd")
    return _mm(o, w_o_p, res=x, bias=b_o, name="swa_out"), (qkv, qk, o, lse)


def _swa_mixer_bwd(dy, h, w_qkv_p, sinks_b, w_o_p, saved, tabs_bwd, ex=None):
    qkv, qk, o, lse = saved
    nq = w_o_p.shape[0] // LANES
    nkv = nq // SWA_GROUP
    do = _mm(dy, w_o_p, tb=True, name="swa_do", ex=ex)
    d_wo_p = _mm(o, dy, ta=True, out_dtype=BF16, name="swa_dwo", ex=ex)
    d_bo = _colsum(dy, "swa_dbo")
    dq, dkc, dkp, dvc, dvp, dsink = _band_bwd(
        qk, qk, qkv, o, do, lse, None, n_cls=1, n_steps=nkv, hpb=SWA_GROUP, group=SWA_GROUP, q_blk=lambda n, st: st,
        k_blk=lambda n, st: nq + st, v_blk=lambda n, st: nq + nkv + st, max_dist=SWA_WINDOW - 1,
        scale=SWA_HEAD_DIM ** -0.5, sinks=sinks_b, name="swa_bwd")
    dk = _band_fold(dkc, dkp, n_cls=1, n_heads=nkv, group=1, name="swa_foldk")
    dv = _band_fold(dvc, dvp, n_cls=1, n_heads=nkv, group=1, name="swa_foldv")
    dqk = _rope(jnp.concatenate([dq, dk], axis=1), tabs_bwd[0], tabs_bwd[1], nq + nkv, 8, "swa_rope_bwd")
    dqkv = jnp.concatenate([dqk, dv.astype(BF16)], axis=1)
    d_bqkv_p = _colsum(dqkv, "swa_dbqkv")
    dh = _mm(dqkv, w_qkv_p, tb=True, name="swa_dh", ex=ex)
    d_wqkv_p = _mm(h, dqkv, ta=True, out_dtype=BF16, name="swa_dwqkv", ex=ex)
    return dh, d_wqkv_p, d_bqkv_p, dsink, d_wo_p, d_bo


def _xattn_layer_fwd(x, mem, g_x, g_m, w_q, w_kv, w_o):
    hq = _rms_fwd(x, g_x, "rms_fwd")
    hm = _rms_fwd(mem, g_m, "rms_mem_fwd")
    q = _mm(hq, _val(w_q), out_dtype=BF16, name="xa_q")
    kv = _mm(hm, _val(w_kv), out_dtype=BF16, name="xa_kv")
    o = _xattn_fwd(q, kv)
    return _mm(o, _val(w_o), res=x, name="xa_out"), (hq, hm, q, kv, o)


def _xattn_layer_bwd(dy, x, mem, g_x, g_m, w_q, w_kv, w_o, saved):
    hq, hm, q, kv, o = saved
    do = _mm(dy, w_o, tb=True, out_dtype=BF16, name="xa_do")
    d_wo = _mm(o, dy, ta=True, out_dtype=BF16, name="xa_dwo")
    dq, dk, dv = _xattn_bwd(q, kv, do)
    dkv = jnp.concatenate([dk, dv], axis=1).astype(BF16)
    dhq = _mm(dq, w_q, tb=True, name="xa_dhq")
    d_wq = _mm(hq, dq, ta=True, out_dtype=BF16, name="xa_dwq")
    dhm = _mm(dkv, w_kv, tb=True, name="xa_dhm")
    d_wkv = _mm(hm, dkv, ta=True, out_dtype=BF16, name="xa_dwkv")
    dx, dg_x = _rms_bwd(x, g_x, dhq, dy)
    _, dg_m = _rms_bwd(mem, g_m, dhm, None, "rms_mem_bwd")
    return dx, dg_x, dg_m, d_wq, d_wkv, d_wo


def _to_full(gathered, axis):
    t = jnp.moveaxis(gathered, 0, axis)
    shape = list(t.shape)
    return t.reshape(shape[:axis] + [shape[axis] * shape[axis + 1]] + shape[axis + 2:])


def _to_blocks(full, axis):
    shape = list(full.shape)
    t = full.reshape(shape[:axis] + [N_DEV, shape[axis] // N_DEV] + shape[axis + 1:])
    return jnp.moveaxis(t, axis, 0)


SHARD_AXIS = {
    "ffn1_w_gate_up": 2, "ffn1_w_down": 1, "sb_w_qkv": 2, "sb_w_o": 1, "dil_w_qkv": 2, "dil_w_o": 2,
    "swa_w_qkv": 2, "swa_b_qkv": 1, "swa_w_o": 1, "swa_b_o": 1, "xattn_w_q": 1, "xattn_w_kv": 1, "xattn_w_o": 2,
    "ffn2_w_gate_up": 2, "ffn2_w_down": 1,
}
SMALL = ("ffn1_norm", "mix_norm", "xattn_norm", "mem_norm", "ffn2_norm", "final_norm", "swa_sinks")
WEIGHTS = ("ffn1_norm", "ffn1_w_gate_up", "ffn1_w_down", "mix_norm", "sb_w_qkv", "sb_w_o", "dil_w_qkv", "dil_w_o",
           "swa_w_qkv", "swa_b_qkv", "swa_sinks", "swa_w_o", "swa_b_o", "xattn_norm", "mem_norm", "xattn_w_q",
           "xattn_w_kv", "xattn_w_o", "ffn2_norm", "ffn2_w_gate_up", "ffn2_w_down", "final_norm")


def _flat2(a):
    return a.reshape(-1, a.shape[-1])


def _pack_small(vals, d):
    rows = [vals[n].reshape(-1, d) for n in SMALL[:5]] + [vals["final_norm"].reshape(1, d)]
    sk = vals["swa_sinks"].reshape(1, -1)
    rows.append(jnp.pad(sk, ((0, 0), (0, d - sk.shape[1]))))
    rows.append(jnp.zeros((2, d), F32))
    return jnp.concatenate(rows, axis=0)


def _unpack_small(packed, like):
    out, r = {}, 0
    for n in SMALL[:5]:
        k = like[n].shape[0]
        out[n] = packed[r:r + k]
        r += k
    out["final_norm"] = packed[r]
    out["swa_sinks"] = packed[r + 1:r + 2, :like["swa_sinks"].shape[1]]
    return out


def _local_step(x0, mem0, positions, target, full, norm, ex=None, wg=None):
    d = x0.shape[1]
    names = list(SHARD_AXIS)
    sinks_b = jnp.repeat(norm["swa_sinks"], LANES, axis=1)
    tabs_dil, tabs_dil_bwd = _rope_tables(positions, 32), _rope_tables(positions, 32, -1.0)
    tabs_swa, tabs_swa_bwd = _rope_tables(positions, 16), _rope_tables(positions, 16, -1.0)

    def vec(name, i):
        return norm[name][i:i + 1]

    saved = []
    xc = x0
    for i in range(DEPTH):
        kind, j = i % 3, i // 3
        rec = {"x0": xc}
        xc, rec["ffn1"] = _ffn_fwd(xc, vec("ffn1_norm", i), lambda: full["ffn1_w_gate_up"][i],
                                   lambda: full["ffn1_w_down"][i], "ffn", wg)
        rec["x1"] = xc
        h = _rms_fwd(xc, vec("mix_norm", i))
        rec["h"] = h
        if kind == 0:
            xc, rec["mix"] = _sb_mixer_fwd(h, lambda: full["sb_w_qkv"][j], lambda: full["sb_w_o"][j], xc, wg)
        elif kind == 1:
            xc, rec["mix"] = _dil_mixer_fwd(h, lambda: full["dil_w_qkv"][j], lambda: full["dil_w_o"][j], xc, tabs_dil, wg)
        else:
            swa_w_qkv_p = _pad_heads(full["swa_w_qkv"][0], 1)
            swa_b_qkv_p = _pad_heads(full["swa_b_qkv"][0][None], 1)
            swa_w_o_p = _pad_heads(full["swa_w_o"][0], 0)
            xc, rec["mix"] = _swa_mixer_fwd(h, swa_w_qkv_p, swa_b_qkv_p, sinks_b, swa_w_o_p, full["swa_b_o"][0][None],
                                            xc, tabs_swa, wg)
        rec["x2"] = xc
        xc, rec["xa"] = _xattn_layer_fwd(xc, mem0, vec("xattn_norm", i), vec("mem_norm", i),
                                         lambda: full["xattn_w_q"][i], lambda: full["xattn_w_kv"][i],
                                         lambda: full["xattn_w_o"][i])
        rec["x3"] = xc
        xc, rec["ffn2"] = _ffn_fwd(xc, vec("ffn2_norm", i), lambda: full["ffn2_w_gate_up"][i],
                                   lambda: full["ffn2_w_down"][i], "ffn", wg)
        saved.append(rec)

    loss_part, dx, dg_final = _loss_head(xc, norm["final_norm"].reshape(1, d), target)

    gfull = {n: [None] * full[n].shape[0] for n in names}
    gsmall = {n: [None] * DEPTH for n in SMALL[:5]}
    gsmall["final_norm"] = dg_final
    gsmall["swa_sinks"] = jnp.zeros_like(norm["swa_sinks"])
    for i in reversed(range(DEPTH)):
        kind, j = i % 3, i // 3
        rec = saved[i]
        dx, gsmall["ffn2_norm"][i], gfull["ffn2_w_gate_up"][i], gfull["ffn2_w_down"][i] = _ffn_bwd(
            dx, rec["x3"], vec("ffn2_norm", i), full["ffn2_w_gate_up"][i], full["ffn2_w_down"][i], rec["ffn2"], "ffn",
            ex, "ffn2", i)
        (dx, gsmall["xattn_norm"][i], gsmall["mem_norm"][i], gfull["xattn_w_q"][i], gfull["xattn_w_kv"][i],
         gfull["xattn_w_o"][i]) = _xattn_layer_bwd(dx, rec["x2"], mem0, vec("xattn_norm", i), vec("mem_norm", i),
                                                   full["xattn_w_q"][i], full["xattn_w_kv"][i], full["xattn_w_o"][i],
                                                   rec["xa"])
        for n in ("xattn_w_q", "xattn_w_kv", "xattn_w_o"):
            _put(ex, n, i, gfull[n][i])
        if kind == 0:
            dh, gfull["sb_w_qkv"][j], gfull["sb_w_o"][j] = _sb_mixer_bwd(
                dx, rec["h"], full["sb_w_qkv"][j], full["sb_w_o"][j], rec["mix"], ex, j)
        elif kind == 1:
            dh, gfull["dil_w_qkv"][j], gfull["dil_w_o"][j] = _dil_mixer_bwd(
                dx, rec["h"], full["dil_w_qkv"][j], full["dil_w_o"][j], rec["mix"], tabs_dil_bwd, ex)
        else:
            dh, d_wqkv_p, d_bqkv_p, dsink, d_wo_p, d_bo = _swa_mixer_bwd(
                dx, rec["h"], swa_w_qkv_p, sinks_b, swa_w_o_p, rec["mix"], tabs_swa_bwd, ex)
            gfull["swa_w_qkv"][j] = _unpad_heads(d_wqkv_p, 1)
            gfull["swa_b_qkv"][j] = _unpad_heads(d_bqkv_p, 1)[0]
            gfull["swa_w_o"][j] = _unpad_heads(d_wo_p, 0)
            gfull["swa_b_o"][j] = d_bo[0]
            gsmall["swa_sinks"] = dsink.reshape(1, -1, LANES)[:, :, 0]
            _put(ex, "swa_w_o", j, gfull["swa_w_o"][j])
        dx, gsmall["mix_norm"][i] = _rms_bwd(rec["x1"], vec("mix_norm", i), dh, dx)
        dx, gsmall["ffn1_norm"][i], gfull["ffn1_w_gate_up"][i], gfull["ffn1_w_down"][i] = _ffn_bwd(
            dx, rec["x0"], vec("ffn1_norm", i), full["ffn1_w_gate_up"][i], full["ffn1_w_down"][i], rec["ffn1"], "ffn",
            ex, "ffn1", i)
    for n in SMALL[:5]:
        gsmall[n] = jnp.concatenate(gsmall[n], axis=0)
    return loss_part[0, 0], dx, gfull, gsmall


def _train_step(x, mem, positions, loss_target, w, m, v):
    d = x.shape[2]
    names = list(SHARD_AXIS)
    norm = {n: w[n] for n in SMALL}

    def carried(n):
        if w[n].ndim != 3:
            return False
        return w[n].shape[2] % LANES == 0 if SHARD_AXIS[n] == 2 else w[n].shape[1] % ROW_ALIGN == 0

    first = [n for n in names if not carried(n)]
    gathered = _all_gather([_flat2(w[n].astype(BF16) if w[n].ndim == 3 else w[n]) for n in first], "gather_weights")
    stacked = {n: _to_full(g.reshape((N_DEV,) + w[n].shape), SHARD_AXIS[n]) for n, g in zip(first, gathered)}
    wg = _WeightGather()
    for i in range(DEPTH):
        mixer = (("sb_w_qkv", "sb_w_o"), ("dil_w_qkv", "dil_w_o"), ("swa_w_qkv", "swa_w_o"))[i % 3]
        for n in ("ffn1_w_gate_up", "ffn1_w_down") + mixer + ("xattn_w_q", "xattn_w_kv", "xattn_w_o",
                                                               "ffn2_w_gate_up", "ffn2_w_down"):
            layer = i // 3 if n in mixer else i
            if carried(n):
                wg.add(n, layer, w[n][layer].astype(BF16))

    class Layers:
        def __init__(self, n):
            self.n, self.shape = n, w[n].shape[:1]

        def __getitem__(self, layer):
            return wg.get(self.n, layer) if carried(self.n) else stacked[self.n][layer]

    full = {n: Layers(n) for n in names}
    ex = _GradExchange()
    loss_part, dx, gfull, gsmall = _local_step(x[0], mem[0], positions, loss_target[0], full, norm, ex, wg)
    loss = lax.psum(loss_part, MESH_AXES)
    grad_x = dx[None]

    ex.flush()
    parts = {n: ex.pieces_of(n) for n in names}
    rest = [n for n in names if not parts[n]]
    blocks = [_to_blocks(jnp.stack(gfull[n], axis=0), SHARD_AXIS[n]) for n in rest]
    blocks = [b.reshape(N_DEV, -1, b.shape[-1]) for b in blocks]
    for n, received in zip(rest, _exchange(blocks, "exchange_grads")):
        parts[n] = [received]
    grad, delta, new_m, new_v = {}, {}, {}, {}
    for n in names:
        outs = _adamw(parts[n], _flat2(w[n]), _flat2(m[n]), _flat2(v[n]), "adamw")
        grad[n], delta[n], new_m[n], new_v[n] = (o.reshape(w[n].shape) for o in outs)

    small_parts = _all_gather([_pack_small(gsmall, d)], "gather_small_grads")[0]
    outs = _adamw([small_parts], _pack_small(norm, d), _pack_small({n: m[n] for n in SMALL}, d),
                  _pack_small({n: v[n] for n in SMALL}, d), "adamw_small")
    for res, o in zip((grad, delta, new_m, new_v), outs):
        res.update(_unpack_small(o, norm))
    return loss, grad_x, grad, delta, new_m, new_v


def kernel(x, mem, positions, ffn1_norm, ffn1_w_gate_up, ffn1_w_down, mix_norm, sb_w_qkv, sb_w_o, dil_w_qkv, dil_w_o, swa_w_qkv, swa_b_qkv, swa_sinks, swa_w_o, swa_b_o, xattn_norm, mem_norm, xattn_w_q, xattn_w_kv, xattn_w_o, ffn2_norm, ffn2_w_gate_up, ffn2_w_down, final_norm, loss_target, m_ffn1_norm, m_ffn1_w_gate_up, m_ffn1_w_down, m_mix_norm, m_sb_w_qkv, m_sb_w_o, m_dil_w_qkv, m_dil_w_o, m_swa_w_qkv, m_swa_b_qkv, m_swa_sinks, m_swa_w_o, m_swa_b_o, m_xattn_norm, m_mem_norm, m_xattn_w_q, m_xattn_w_kv, m_xattn_w_o, m_ffn2_norm, m_ffn2_w_gate_up, m_ffn2_w_down, m_final_norm, v_ffn1_norm, v_ffn1_w_gate_up, v_ffn1_w_down, v_mix_norm, v_sb_w_qkv, v_sb_w_o, v_dil_w_qkv, v_dil_w_o, v_swa_w_qkv, v_swa_b_qkv, v_swa_sinks, v_swa_w_o, v_swa_b_o, v_xattn_norm, v_mem_norm, v_xattn_w_q, v_xattn_w_kv, v_xattn_w_o, v_ffn2_norm, v_ffn2_w_gate_up, v_ffn2_w_down, v_final_norm):
    args = dict(locals())
    w = {n: args[n] for n in WEIGHTS}
    m = {n: args["m_" + n] for n in WEIGHTS}
    v = {n: args["v_" + n] for n in WEIGHTS}
    loss, grad_x, grad, delta, new_m, new_v = _train_step(x, mem, positions, loss_target, w, m, v)
    return (loss, grad_x, *[grad[n] for n in WEIGHTS], *[delta[n] for n in WEIGHTS],
            *[new_m[n] for n in WEIGHTS], *[new_v[n] for n in WEIGHTS])
```

```python
import jax
import jax.numpy as jnp
from jax import lax
from jax.experimental import pallas as pl
from jax.experimental.pallas import tpu as pltpu

F32 = jnp.float32
BF16 = jnp.bfloat16

N_DEV = 8
MESH_AXES = ("x", "y", "c")
LANES = 128
BAND = 128
NORM_EPS = 1e-6
ROPE_THETA = 500000.0
DIL_PATTERNS = ((128, 1), (512, 4), (2048, 16))
SWA_HEAD_DIM = 64
SWA_GROUP = 8
SWA_WINDOW = 128
XA_HEADS = 4
DEPTH = 4
ADAM_LR, ADAM_B1, ADAM_B2, ADAM_EPS, ADAM_WD, ADAM_STEP = 0.001, 0.9, 0.999, 1e-08, 0.01, 10
VMEM_LIMIT = 56 * 1024 * 1024
NEG = -1e30

NT = (((1,), (1,)), ((), ()))
TN = (((0,), (0,)), ((), ()))


def _tile(n, prefs):
    for p in prefs:
        if n % p == 0:
            return p
    return n


def _cp(sem):
    return pltpu.CompilerParams(dimension_semantics=sem, vmem_limit_bytes=VMEM_LIMIT)


MM_TILE_SIZES = (2816, 2048, 1408, 1024, 512, 256, 128)
MM_VMEM_BUDGET = 40 * 1024 * 1024
MM_STEP_BYTES = 1.2e6


def _mm_tiles(m, n, k, ea, eb, eo, has_res):
    best = None
    for bm in [c for c in MM_TILE_SIZES if m % c == 0] or [m]:
        for bn in [c for c in MM_TILE_SIZES if n % c == 0] or [n]:
            for bk in [c for c in MM_TILE_SIZES if k % c == 0] or [k]:
                vmem = 2 * (bm * bk * ea + bk * bn * eb) + bm * bn * 4 + 2 * bm * bn * (eo + (4 if has_res else 0))
                vmem += (bm * bk * 2 if ea == 4 else 0) + (bk * bn * 2 if eb == 4 else 0)
                if vmem > MM_VMEM_BUDGET:
                    continue
                ni, nj, nk = m // bm, n // bn, k // bk
                traffic = (m * k * ea * (nj if nk > 1 else 1) + k * n * eb * (ni if nk > 1 or nj > 1 else 1)
                           + m * n * (eo + (4 if has_res else 0)) + ni * nj * nk * MM_STEP_BYTES)
                if best is None or traffic < best[0]:
                    best = (traffic, (bm, bn, bk))
    return best[1]
def _mm(a, b, *, ta=False, tb=False, out_dtype=F32, alpha=1.0, res=None, bias=None, name, ex=None):
    kdim, m = a.shape if ta else a.shape[::-1]
    kdim2, n = b.shape[::-1] if tb else b.shape
    assert kdim == kdim2, (a.shape, b.shape, ta, tb)
    bm, bn, bk = _mm_tiles(m, n, kdim, a.dtype.itemsize, b.dtype.itemsize, jnp.dtype(out_dtype).itemsize, res is not None)
    nk = kdim // bk
    grid = (m // bm, n // bn, nk)
    dn = (((0 if ta else 1,), (1 if tb else 0,)), ((), ()))
    has_res, has_bias = res is not None, bias is not None
    jobs = ex.take(2.0 * m * n * kdim * EXCHANGE_BYTES_PER_FLOP) if ex is not None else []
    nj = len(jobs)
    n_in = 2 + has_res + has_bias

    def body(*refs):
        a_ref, b_ref = refs[0], refs[1]
        res_ref = refs[2] if has_res else None
        bias_ref = refs[2 + has_res] if has_bias else None
        o_ref, acc_ref = refs[n_in + nj], refs[n_in + 2 * nj + 1]
        k = pl.program_id(2)
        if nj:
            job_refs = (refs[n_in:n_in + nj], refs[n_in + nj + 1:n_in + 2 * nj + 1], refs[n_in + 2 * nj + 2:])
            ids = [pl.program_id(t) for t in range(3)]

            @pl.when((ids[0] == 0) & (ids[1] == 0) & (ids[2] == 0))
            def _():
                ex.start(jobs, *job_refs)

            @pl.when((ids[0] == grid[0] - 1) & (ids[1] == grid[1] - 1) & (ids[2] == grid[2] - 1))
            def _():
                ex.wait(jobs, *job_refs)

        @pl.when(k == 0)
        def _():
            acc_ref[...] = jnp.zeros_like(acc_ref)

        acc_ref[...] += lax.dot_general(a_ref[...].astype(BF16), b_ref[...].astype(BF16), dn,
                                        preferred_element_type=F32)

        @pl.when(k == nk - 1)
        def _():
            r = acc_ref[...]
            if alpha != 1.0:
                r = r * alpha
            if has_bias:
                r = r + bias_ref[...]
            if has_res:
                r = r + res_ref[...]
            o_ref[...] = r.astype(o_ref.dtype)

    a_spec = pl.BlockSpec((bk, bm), lambda i, j, k: (k, i)) if ta else pl.BlockSpec((bm, bk), lambda i, j, k: (i, k))
    b_spec = pl.BlockSpec((bn, bk), lambda i, j, k: (j, k)) if tb else pl.BlockSpec((bk, bn), lambda i, j, k: (k, j))
    ins, specs = [a, b], [a_spec, b_spec]
    if has_res:
        ins.append(res)
        specs.append(pl.BlockSpec((bm, bn), lambda i, j, k: (i, j)))
    if has_bias:
        ins.append(bias)
        specs.append(pl.BlockSpec((1, bn), lambda i, j, k: (0, j)))
    out_spec = pl.BlockSpec((bm, bn), lambda i, j, k: (i, j))
    out_shape = jax.ShapeDtypeStruct((m, n), out_dtype)
    scratch = [pltpu.VMEM((bm, bn), F32)]
    if not nj:
        return pl.pallas_call(
            body, name=name, grid=grid, in_specs=specs, out_specs=out_spec, out_shape=out_shape,
            scratch_shapes=scratch, compiler_params=_cp(("parallel", "parallel", "arbitrary")),
        )(*ins)
    outs = pl.pallas_call(
        body, name=name + "_carry", grid=grid, in_specs=specs + [ANY] * nj, out_specs=[out_spec] + [ANY] * nj,
        out_shape=[out_shape] + [j.recv_shape for j in jobs], scratch_shapes=scratch + _piece_sems(nj),
        compiler_params=_cp(("arbitrary", "arbitrary", "arbitrary")),
    )(*ins, *[j.g for j in jobs])
    ex.landed(jobs, outs[1:])
    return outs[0]


def _rms_fwd(x, g, name="rms_fwd"):
    s, d = x.shape
    bm = _tile(s, (256, 128, 8))

    def body(x_ref, g_ref, o_ref):
        xv = x_ref[...]
        r = lax.rsqrt(jnp.mean(xv * xv, axis=-1, keepdims=True) + NORM_EPS)
        o_ref[...] = (xv * r * g_ref[...]).astype(o_ref.dtype)

    return pl.pallas_call(
        body, name=name, grid=(s // bm,),
        in_specs=[pl.BlockSpec((bm, d), lambda i: (i, 0)), pl.BlockSpec((1, d), lambda i: (0, 0))],
        out_specs=pl.BlockSpec((bm, d), lambda i: (i, 0)),
        out_shape=jax.ShapeDtypeStruct((s, d), BF16), compiler_params=_cp(("parallel",)),
    )(x, g)


def _rms_bwd(x, g, dn, dy=None, name="rms_bwd"):
    s, d = x.shape
    bm = _tile(s, (256, 128, 8))
    has_dy = dy is not None

    def body(*refs):
        x_ref, g_ref, dn_ref = refs[:3]
        dy_ref = refs[3] if has_dy else None
        dx_ref, dg_ref = refs[-2], refs[-1]

        @pl.when(pl.program_id(0) == 0)
        def _():
            dg_ref[...] = jnp.zeros_like(dg_ref)

        xv = x_ref[...]
        r = lax.rsqrt(jnp.mean(xv * xv, axis=-1, keepdims=True) + NORM_EPS)
        xh = xv * r
        dnv = dn_ref[...].astype(F32)
        dxh = dnv * g_ref[...]
        dx = r * (dxh - xh * jnp.mean(dxh * xh, axis=-1, keepdims=True))
        if has_dy:
            dx = dx + dy_ref[...]
        dx_ref[...] = dx
        dg_ref[...] += jnp.sum(dnv * xh, axis=0, keepdims=True)

    row = pl.BlockSpec((bm, d), lambda i: (i, 0))
    vec = pl.BlockSpec((1, d), lambda i: (0, 0))
    ins, specs = [x, g, dn], [row, vec, row]
    if has_dy:
        ins.append(dy)
        specs.append(row)
    return pl.pallas_call(
        body, name=name, grid=(s // bm,), in_specs=specs, out_specs=[row, vec],
        out_shape=[jax.ShapeDtypeStruct((s, d), F32), jax.ShapeDtypeStruct((1, d), F32)],
        compiler_params=_cp(("arbitrary",)),
    )(*ins)


def _loss_head(x, g, target):
    s, d = x.shape
    bm = _tile(s, (256, 128, 8))

    def body(x_ref, g_ref, t_ref, loss_ref, dx_ref, dg_ref):
        @pl.when(pl.program_id(0) == 0)
        def _():
            dg_ref[...] = jnp.zeros_like(dg_ref)
            loss_ref[...] = jnp.zeros_like(loss_ref)

        xv = x_ref[...]
        gv = g_ref[...]
        r = lax.rsqrt(jnp.mean(xv * xv, axis=-1, keepdims=True) + NORM_EPS)
        xh = xv * r
        err = xh * gv - t_ref[...]
        part = 0.5 * jnp.sum(jnp.mean(err * err, axis=-1, keepdims=True), axis=0, keepdims=True)
        loss_ref[...] += jnp.broadcast_to(part, loss_ref.shape)
        dyv = err * (1.0 / d)
        dxh = dyv * gv
        dx_ref[...] = r * (dxh - xh * jnp.mean(dxh * xh, axis=-1, keepdims=True))
        dg_ref[...] += jnp.sum(dyv * xh, axis=0, keepdims=True)

    row = pl.BlockSpec((bm, d), lambda i: (i, 0))
    vec = pl.BlockSpec((1, d), lambda i: (0, 0))
    return pl.pallas_call(
        body, name="loss_head", grid=(s // bm,), in_specs=[row, vec, row],
        out_specs=[pl.BlockSpec((1, LANES), lambda i: (0, 0)), row, vec],
        out_shape=[jax.ShapeDtypeStruct((1, LANES), F32), jax.ShapeDtypeStruct((s, d), F32),
                   jax.ShapeDtypeStruct((1, d), F32)],
        compiler_params=_cp(("arbitrary",)),
    )(x, g, target)


def _colsum(x, name="colsum"):
    s, n = x.shape
    bm = _tile(s, (256, 128, 8))

    def body(x_ref, o_ref):
        @pl.when(pl.program_id(0) == 0)
        def _():
            o_ref[...] = jnp.zeros_like(o_ref)

        o_ref[...] += jnp.sum(x_ref[...].astype(F32), axis=0, keepdims=True)

    return pl.pallas_call(
        body, name=name, grid=(s // bm,), in_specs=[pl.BlockSpec((bm, n), lambda i: (i, 0))],
        out_specs=pl.BlockSpec((1, n), lambda i: (0, 0)), out_shape=jax.ShapeDtypeStruct((1, n), F32),
        compiler_params=_cp(("arbitrary",)),
    )(x)


def _swiglu_fwd(gu):
    s, f2 = gu.shape
    f = f2 // 2
    bm, bf = _tile(s, (256, 128, 8)), _tile(f, (512, 256, 128))

    def body(gu_ref, o_ref):
        for c in range(0, f, bf):
            gv = gu_ref[:, c:c + bf].astype(F32)
            o_ref[:, c:c + bf] = (gv / (1.0 + jnp.exp(-gv)) * gu_ref[:, f + c:f + c + bf].astype(F32)).astype(o_ref.dtype)

    return pl.pallas_call(
        body, name="swiglu_fwd", grid=(s // bm,), in_specs=[pl.BlockSpec((bm, f2), lambda i: (i, 0))],
        out_specs=pl.BlockSpec((bm, f), lambda i: (i, 0)),
        out_shape=jax.ShapeDtypeStruct((s, f), BF16), compiler_params=_cp(("parallel",)),
    )(gu)


def _swiglu_bwd(gu, da):
    s, f2 = gu.shape
    f = f2 // 2
    bm, bf = _tile(s, (128, 8)), _tile(f, (512, 256, 128))

    def body(gu_ref, da_ref, o_ref):
        for c in range(0, f, bf):
            gv = gu_ref[:, c:c + bf].astype(F32)
            dav = da_ref[:, c:c + bf].astype(F32)
            sig = 1.0 / (1.0 + jnp.exp(-gv))
            o_ref[:, c:c + bf] = (dav * gu_ref[:, f + c:f + c + bf].astype(F32)
                                  * (sig * (1.0 + gv * (1.0 - sig)))).astype(o_ref.dtype)
            o_ref[:, f + c:f + c + bf] = (dav * gv * sig).astype(o_ref.dtype)

    return pl.pallas_call(
        body, name="swiglu_bwd", grid=(s // bm,),
        in_specs=[pl.BlockSpec((bm, f2), lambda i: (i, 0)), pl.BlockSpec((bm, f), lambda i: (i, 0))],
        out_specs=pl.BlockSpec((bm, f2), lambda i: (i, 0)),
        out_shape=jax.ShapeDtypeStruct((s, f2), BF16), compiler_params=_cp(("parallel",)),
    )(gu, da)


def _rope_tables(positions, rot, sign=1.0):
    half = rot // 2
    inv_freq = jnp.power(F32(ROPE_THETA), -jnp.arange(half, dtype=F32) * 2.0 / rot)
    ang = positions.reshape(-1).astype(F32)[:, None] * inv_freq
    cos, sin = jnp.cos(ang), jnp.sin(ang) * sign
    s = ang.shape[0]
    c_tab = jnp.concatenate([cos, cos, jnp.ones((s, LANES - rot), F32)], axis=1)
    s_tab = jnp.concatenate([-sin, sin, jnp.zeros((s, LANES - rot), F32)], axis=1)
    return c_tab, s_tab


def _rope(x, c_tab, s_tab, nblk, half, name):
    s = x.shape[0]
    bm = _tile(s, (512, 256, 128, 8))

    def body(x_ref, c_ref, s_ref, o_ref):
        xv = x_ref[...].astype(F32)
        lane = lax.broadcasted_iota(jnp.int32, xv.shape, 1)
        sw = jnp.where(lane < half, pltpu.roll(xv, LANES - half, 1), pltpu.roll(xv, half, 1))
        o_ref[...] = (xv * c_ref[...] + sw * s_ref[...]).astype(o_ref.dtype)

    blk = pl.BlockSpec((bm, LANES), lambda i, j: (i, j))
    tab = pl.BlockSpec((bm, LANES), lambda i, j: (i, 0))
    return pl.pallas_call(
        body, name=name, grid=(s // bm, nblk), in_specs=[blk, tab, tab], out_specs=blk,
        out_shape=jax.ShapeDtypeStruct((s, nblk * LANES), BF16), compiler_params=_cp(("parallel", "parallel")),
    )(x, c_tab, s_tab)


def _split_dot(x, t):
    hi = x.astype(BF16)
    lo = (x - hi.astype(F32)).astype(BF16)
    return jnp.dot(hi, t, preferred_element_type=F32) + jnp.dot(lo, t, preferred_element_type=F32)


def _sb_terms(q, kb, scale):
    z = lax.dot_general(q, kb, NT, preferred_element_type=F32) * scale
    u = jnp.log(1.0 + jnp.exp(-jnp.abs(z)))
    return jnp.minimum(-z, 0.0) - u, jnp.minimum(z, 0.0) - u


def _sb_heads_per_step(n_heads):
    return 2 if n_heads % 2 == 0 else 1


def _sb_fwd(qkv, n_heads, ex=None):
    s = qkv.shape[0]
    tq = _tile(s, (256, 128))
    hp = _sb_heads_per_step(n_heads)
    w = hp * LANES
    ng = n_heads // hp
    scale = LANES ** -0.5
    jobs = ex.take(4 * 4 * n_heads * s * s * LANES * EXCHANGE_BYTES_PER_FLOP) if ex is not None else []
    nj = len(jobs)

    def body(*refs):
        q_ref, k_ref, v_ref = refs[:3]
        o_ref, lt_ref = refs[3 + nj:5 + nj]
        i = pl.program_id(1)
        if nj:
            job_refs = (refs[3:3 + nj], refs[5 + nj:5 + 2 * nj], refs[5 + 2 * nj:])
            gi = pl.program_id(0)
            pl.when((gi == 0) & (i == 0))(lambda: ex.start(jobs, *job_refs))
            pl.when((gi == ng - 1) & (i == s // tq - 1))(lambda: ex.wait(jobs, *job_refs))
        row = lax.broadcasted_iota(jnp.int32, (tq, tq), 0)
        col = lax.broadcasted_iota(jnp.int32, (tq, tq), 1)
        below = col < row
        later = (row > col).astype(BF16)
        qs = [q_ref[:, h * LANES:(h + 1) * LANES] for h in range(hp)]

        def block(h, k0, c, acc, diag):
            sl = slice(h * LANES, (h + 1) * LANES)
            lk, logsig = _sb_terms(qs[h], k_ref[pl.ds(k0, tq), sl], scale)
            if diag:
                lk = jnp.where(below, lk, 0.0)
            a = jnp.exp(logsig + _split_dot(lk, later) + c)
            if diag:
                a = jnp.where(below, a, 0.0)
            acc = acc + jnp.dot(a.astype(BF16), v_ref[pl.ds(k0, tq), sl], preferred_element_type=F32)
            return c + jnp.sum(lk, axis=1, keepdims=True), acc

        d0 = pl.multiple_of(i * tq, tq)
        carry = []
        for h in range(hp):
            carry += list(block(h, d0, jnp.zeros((tq, 1), F32), jnp.zeros((tq, LANES), F32), True))

        def step(t, carry):
            k0 = pl.multiple_of((i - 1 - t) * tq, tq)
            out = []
            for h in range(hp):
                out += list(block(h, k0, carry[2 * h], carry[2 * h + 1], False))
            return tuple(out)

        carry = lax.fori_loop(0, i, step, tuple(carry))
        for h in range(hp):
            sl = slice(h * LANES, (h + 1) * LANES)
            o_ref[:, sl] = carry[2 * h + 1].astype(o_ref.dtype)
            lt_ref[:, sl] = jnp.broadcast_to(carry[2 * h], (tq, LANES))

    blk = pl.BlockSpec((tq, w), lambda g, i: (i, g))
    outs = pl.pallas_call(
        body, name="sb_fwd_carry" if nj else "sb_fwd", grid=(ng, s // tq),
        in_specs=[blk, pl.BlockSpec((s, w), lambda g, i: (0, ng + g)), pl.BlockSpec((s, w), lambda g, i: (0, 2 * ng + g))]
        + [ANY] * nj,
        out_specs=[blk, blk] + [ANY] * nj,
        out_shape=[jax.ShapeDtypeStruct((s, n_heads * LANES), BF16), jax.ShapeDtypeStruct((s, n_heads * LANES), F32)]
        + [j.recv_shape for j in jobs],
        scratch_shapes=_piece_sems(nj) if nj else [],
        compiler_params=_cp(("arbitrary", "arbitrary")),
    )(qkv, qkv, qkv, *[j.g for j in jobs])
    if nj:
        ex.landed(jobs, outs[2:])
    return outs[:2]


def _sb_bwd(qkv, do, ltot, n_heads, ex=None):
    s = qkv.shape[0]
    tq = _tile(s, (256, 128))
    hp = _sb_heads_per_step(n_heads)
    w = hp * LANES
    ng = n_heads // hp
    scale = LANES ** -0.5
    jobs = ex.take(4 * 9 * n_heads * s * s * LANES * EXCHANGE_BYTES_PER_FLOP) if ex is not None else []
    nj = len(jobs)

    def body(*refs):
        q_ref, k_ref, v_ref, do_ref, lt_ref = refs[:5]
        dq_ref, dk_ref, dv_ref = refs[5 + nj:8 + nj]
        i = pl.program_id(1)
        if nj:
            job_refs = (refs[5:5 + nj], refs[8 + nj:8 + 2 * nj], refs[8 + 2 * nj:])
            gi = pl.program_id(0)
            pl.when((gi == 0) & (i == 0))(lambda: ex.start(jobs, *job_refs))
            pl.when((gi == ng - 1) & (i == s // tq - 1))(lambda: ex.wait(jobs, *job_refs))

        @pl.when(i == 0)
        def _():
            dk_ref[...] = jnp.zeros_like(dk_ref)
            dv_ref[...] = jnp.zeros_like(dv_ref)

        row = lax.broadcasted_iota(jnp.int32, (tq, tq), 0)
        col = lax.broadcasted_iota(jnp.int32, (tq, tq), 1)
        below = col < row
        upto = (row <= col).astype(BF16)
        before = (row < col).astype(BF16)
        qs = [q_ref[:, h * LANES:(h + 1) * LANES] for h in range(hp)]
        dos = [do_ref[:, h * LANES:(h + 1) * LANES] for h in range(hp)]
        lts = [lt_ref[:, h * LANES:h * LANES + 1] for h in range(hp)]

        def block(h, k0, cpre, ce, dq, diag):
            sl = slice(h * LANES, (h + 1) * LANES)
            kb = k_ref[pl.ds(k0, tq), sl]
            vb = v_ref[pl.ds(k0, tq), sl]
            lk, logsig = _sb_terms(qs[h], kb, scale)
            if diag:
                lk = jnp.where(below, lk, 0.0)
            a = jnp.exp(logsig + (lts[h] - cpre) - _split_dot(lk, upto))
            if diag:
                a = jnp.where(below, a, 0.0)
            e = a * lax.dot_general(dos[h], vb, NT, preferred_element_type=F32)
            e_before = ce + jnp.dot(e.astype(BF16), before, preferred_element_type=F32)
            sig = jnp.exp(logsig)
            dz = (e - sig * (e + e_before)) * scale
            if diag:
                dz = jnp.where(below, dz, 0.0)
            dzb = dz.astype(BF16)
            dq = dq + jnp.dot(dzb, kb, preferred_element_type=F32)
            dk_ref[pl.ds(k0, tq), sl] += lax.dot_general(dzb, qs[h], TN, preferred_element_type=F32)
            dv_ref[pl.ds(k0, tq), sl] += lax.dot_general(a.astype(BF16), dos[h], TN, preferred_element_type=F32)
            return cpre + jnp.sum(lk, axis=1, keepdims=True), ce + jnp.sum(e, axis=1, keepdims=True), dq

        def step(j, carry):
            k0 = pl.multiple_of(j * tq, tq)
            out = []
            for h in range(hp):
                out += list(block(h, k0, *carry[3 * h:3 * h + 3], False))
            return tuple(out)

        z1 = jnp.zeros((tq, 1), F32)
        carry = lax.fori_loop(0, i, step, (z1, z1, jnp.zeros((tq, LANES), F32)) * hp)
        d0 = pl.multiple_of(i * tq, tq)
        for h in range(hp):
            _, _, dq = block(h, d0, *carry[3 * h:3 * h + 3], True)
            dq_ref[:, h * LANES:(h + 1) * LANES] = dq.astype(dq_ref.dtype)

    blk = pl.BlockSpec((tq, w), lambda g, i: (i, g))
    full = pl.BlockSpec((s, w), lambda g, i: (0, g))
    wt = n_heads * LANES
    outs = pl.pallas_call(
        body, name="sb_bwd_carry" if nj else "sb_bwd", grid=(ng, s // tq),
        in_specs=[blk, pl.BlockSpec((s, w), lambda g, i: (0, ng + g)), pl.BlockSpec((s, w), lambda g, i: (0, 2 * ng + g)),
                  blk, blk] + [ANY] * nj,
        out_specs=[blk, full, full] + [ANY] * nj,
        out_shape=[jax.ShapeDtypeStruct((s, wt), BF16), jax.ShapeDtypeStruct((s, wt), F32),
                   jax.ShapeDtypeStruct((s, wt), F32)] + [j.recv_shape for j in jobs],
        scratch_shapes=_piece_sems(nj) if nj else [],
        compiler_params=_cp(("arbitrary", "arbitrary")),
    )(qkv, qkv, qkv, do, ltot, *[j.g for j in jobs])
    if nj:
        ex.landed(jobs, outs[3:])
    return outs[:3]


def _band_masks(b, max_dist):
    qi = lax.broadcasted_iota(jnp.int32, (BAND, BAND), 0)
    kj = lax.broadcasted_iota(jnp.int32, (BAND, BAND), 1)
    dist = qi - kj
    return ((BAND + dist) <= max_dist) & (b > 0), (dist >= 0) & (dist <= max_dist)


def _band_fwd(qa, ka, va, *, n_cls, n_steps, hpb, group, q_blk, k_blk, v_blk, max_dist, scale, sinks, name):
    length = qa.shape[0]
    nb = length // BAND
    has_sink = sinks is not None
    qw, kw = hpb * LANES, (hpb // group) * LANES

    def body(*refs):
        q_ref, kp_ref, kc_ref, vp_ref, vc_ref = refs[:5]
        o_ref, lse_ref = refs[-2], refs[-1]
        mask_p, mask_c = _band_masks(pl.program_id(2), max_dist)
        for hh in range(hpb):
            qs = slice(hh * LANES, (hh + 1) * LANES)
            ks = slice((hh // group) * LANES, (hh // group + 1) * LANES)
            q = q_ref[:, qs].astype(BF16)
            s_p = lax.dot_general(q, kp_ref[:, ks].astype(BF16), NT, preferred_element_type=F32) * scale
            s_c = lax.dot_general(q, kc_ref[:, ks].astype(BF16), NT, preferred_element_type=F32) * scale
            s_p = jnp.where(mask_p, s_p, NEG)
            s_c = jnp.where(mask_c, s_c, NEG)
            m = jnp.maximum(jnp.max(s_p, axis=1, keepdims=True), jnp.max(s_c, axis=1, keepdims=True))
            l = jnp.sum(jnp.exp(s_p - m), axis=1, keepdims=True) + jnp.sum(jnp.exp(s_c - m), axis=1, keepdims=True)
            lse = m + jnp.log(l)
            if has_sink:
                sk = refs[5][:, hh * LANES:hh * LANES + 1]
                lse = jnp.maximum(lse, sk) + jnp.log(1.0 + jnp.exp(-jnp.abs(lse - sk)))
            p_p = jnp.exp(s_p - lse).astype(BF16)
            p_c = jnp.exp(s_c - lse).astype(BF16)
            o_ref[:, qs] = (jnp.dot(p_p, vp_ref[:, ks].astype(BF16), preferred_element_type=F32)
                            + jnp.dot(p_c, vc_ref[:, ks].astype(BF16), preferred_element_type=F32))
            lse_ref[:, qs] = jnp.broadcast_to(lse, (BAND, LANES))

    def prev(b):
        return jnp.maximum(b - 1, 0)

    specs = [pl.BlockSpec((BAND, qw), lambda n, st, b: (b, q_blk(n, st))),
             pl.BlockSpec((BAND, kw), lambda n, st, b: (prev(b), k_blk(n, st))),
             pl.BlockSpec((BAND, kw), lambda n, st, b: (b, k_blk(n, st))),
             pl.BlockSpec((BAND, kw), lambda n, st, b: (prev(b), v_blk(n, st))),
             pl.BlockSpec((BAND, kw), lambda n, st, b: (b, v_blk(n, st)))]
    ins = [qa, ka, ka, va, va]
    if has_sink:
        ins.append(sinks)
        specs.append(pl.BlockSpec((1, qw), lambda n, st, b: (0, st)))
    out = pl.BlockSpec((BAND, qw), lambda n, st, b: (b, n * n_steps + st))
    w = n_cls * n_steps * qw
    return pl.pallas_call(
        body, name=name, grid=(n_cls, n_steps, nb), in_specs=specs, out_specs=[out, out],
        out_shape=[jax.ShapeDtypeStruct((length, w), F32), jax.ShapeDtypeStruct((length, w), F32)],
        compiler_params=_cp(("parallel", "parallel", "parallel")),
    )(*ins)


def _band_bwd(qa, ka, va, o, do, lse, dlse, *, n_cls, n_steps, hpb, group, q_blk, k_blk, v_blk, max_dist, scale, sinks,
              name):
    length = qa.shape[0]
    nb = length // BAND
    has_sink, has_dlse = sinks is not None, dlse is not None
    qw, kw = hpb * LANES, (hpb // group) * LANES

    def body(*refs):
        q_ref, kp_ref, kc_ref, vp_ref, vc_ref, o_ref, do_ref, lse_ref = refs[:8]
        pos = 8
        dlse_ref = refs[pos] if has_dlse else None
        pos += has_dlse
        sink_ref = refs[pos] if has_sink else None
        pos += has_sink
        dq_ref, dkc_ref, dkp_ref, dvc_ref, dvp_ref = refs[pos:pos + 5]
        b = pl.program_id(2)
        mask_p, mask_c = _band_masks(b, max_dist)
        if has_sink:
            dsink_ref = refs[pos + 5]

            @pl.when(b == 0)
            def _():
                dsink_ref[...] = jnp.zeros_like(dsink_ref)

        for hh in range(hpb):
            qs = slice(hh * LANES, (hh + 1) * LANES)
            ks = slice((hh // group) * LANES, (hh // group + 1) * LANES)
            q = q_ref[:, qs].astype(BF16)
            kp, kc = kp_ref[:, ks].astype(BF16), kc_ref[:, ks].astype(BF16)
            vp, vc = vp_ref[:, ks].astype(BF16), vc_ref[:, ks].astype(BF16)
            dov = do_ref[:, qs].astype(F32)
            dob = dov.astype(BF16)
            lse_v = lse_ref[:, hh * LANES:hh * LANES + 1]
            s_p = lax.dot_general(q, kp, NT, preferred_element_type=F32) * scale
            s_c = lax.dot_general(q, kc, NT, preferred_element_type=F32) * scale
            p_p = jnp.where(mask_p, jnp.exp(jnp.where(mask_p, s_p, NEG) - lse_v), 0.0)
            p_c = jnp.where(mask_c, jnp.exp(jnp.where(mask_c, s_c, NEG) - lse_v), 0.0)
            delta = jnp.sum(dov * o_ref[:, qs], axis=1, keepdims=True)
            shift = -delta
            if has_dlse:
                shift = shift + dlse_ref[:, hh * LANES:hh * LANES + 1]
            dp_p = lax.dot_general(dob, vp, NT, preferred_element_type=F32)
            dp_c = lax.dot_general(dob, vc, NT, preferred_element_type=F32)
            ds_p = (p_p * (dp_p + shift) * scale).astype(BF16)
            ds_c = (p_c * (dp_c + shift) * scale).astype(BF16)
            dq_ref[:, qs] = (jnp.dot(ds_p, kp, preferred_element_type=F32)
                             + jnp.dot(ds_c, kc, preferred_element_type=F32))
            parts = (lax.dot_general(ds_c, q, TN, preferred_element_type=F32),
                     lax.dot_general(ds_p, q, TN, preferred_element_type=F32),
                     lax.dot_general(p_c.astype(BF16), dob, TN, preferred_element_type=F32),
                     lax.dot_general(p_p.astype(BF16), dob, TN, preferred_element_type=F32))
            for ref, part in zip((dkc_ref, dkp_ref, dvc_ref, dvp_ref), parts):
                if hh % group == 0:
                    ref[:, ks] = part
                else:
                    ref[:, ks] += part
            if has_sink:
                p_sink = jnp.exp(sink_ref[:, hh * LANES:hh * LANES + 1] - lse_v)
                dsink_ref[:, qs] += jnp.broadcast_to(jnp.sum(-p_sink * delta, axis=0, keepdims=True), (1, LANES))

    def prev(b):
        return jnp.maximum(b - 1, 0)

    per_q = pl.BlockSpec((BAND, qw), lambda n, st, b: (b, n * n_steps + st))
    per_k = pl.BlockSpec((BAND, kw), lambda n, st, b: (b, n * n_steps + st))
    specs = [pl.BlockSpec((BAND, qw), lambda n, st, b: (b, q_blk(n, st))),
             pl.BlockSpec((BAND, kw), lambda n, st, b: (prev(b), k_blk(n, st))),
             pl.BlockSpec((BAND, kw), lambda n, st, b: (b, k_blk(n, st))),
             pl.BlockSpec((BAND, kw), lambda n, st, b: (prev(b), v_blk(n, st))),
             pl.BlockSpec((BAND, kw), lambda n, st, b: (b, v_blk(n, st))),
             per_q, per_q, per_q]
    ins = [qa, ka, ka, va, va, o, do, lse]
    if has_dlse:
        ins.append(dlse)
        specs.append(per_q)
    out_specs = [per_q] + [per_k] * 4
    out_shape = ([jax.ShapeDtypeStruct((length, n_cls * n_steps * qw), F32)]
                 + [jax.ShapeDtypeStruct((length, n_cls * n_steps * kw), F32)] * 4)
    if has_sink:
        ins.append(sinks)
        specs.append(pl.BlockSpec((1, qw), lambda n, st, b: (0, st)))
        out_specs = out_specs + [pl.BlockSpec((1, qw), lambda n, st, b: (0, st))]
        out_shape = out_shape + [jax.ShapeDtypeStruct((1, n_steps * qw), F32)]
    return pl.pallas_call(
        body, name=name, grid=(n_cls, n_steps, nb), in_specs=specs, out_specs=out_specs, out_shape=out_shape,
        compiler_params=_cp(("parallel", "parallel", "arbitrary")),
    )(*ins)


def _band_fold(cur, prv, *, n_cls, n_heads, group, name):
    length = cur.shape[0]
    nb = length // BAND
    n_kv = n_heads // group

    def body(c_ref, p_ref, o_ref):
        b, gq = pl.program_id(2), pl.program_id(3)

        @pl.when(gq == 0)
        def _():
            o_ref[...] = jnp.zeros_like(o_ref)

        o_ref[...] += c_ref[...] + jnp.where(b + 1 < nb, p_ref[...], 0.0)

    blk = (BAND, LANES)
    return pl.pallas_call(
        body, name=name, grid=(n_cls, n_kv, nb, group),
        in_specs=[pl.BlockSpec(blk, lambda n, h, b, gq: (b, n * n_heads + h * group + gq)),
                  pl.BlockSpec(blk, lambda n, h, b, gq: (jnp.minimum(b + 1, nb - 1), n * n_heads + h * group + gq))],
        out_specs=pl.BlockSpec(blk, lambda n, h, b, gq: (b, n * n_kv + h)),
        out_shape=jax.ShapeDtypeStruct((length, n_cls * n_kv * LANES), F32),
        compiler_params=_cp(("parallel", "parallel", "parallel", "arbitrary")),
    )(cur, prv)


def _dil_mix_fwd(os_, lses):
    s, w = os_[0].shape
    bm = _tile(s, (256, 128, 8))

    def body(o0, o1, o2, l0, l1, l2, out_ref):
        ls = [l0[...], l1[...], l2[...]]
        m = jnp.maximum(jnp.maximum(ls[0], ls[1]), ls[2])
        es = [jnp.exp(v - m) for v in ls]
        inv = 1.0 / (es[0] + es[1] + es[2])
        for gi, o_ref in enumerate((o0, o1, o2)):
            out_ref[:, gi * w:(gi + 1) * w] = (o_ref[...] * (es[gi] * inv)).astype(out_ref.dtype)

    blk = pl.BlockSpec((bm, w), lambda i: (i, 0))
    return pl.pallas_call(
        body, name="dil_mix_fwd", grid=(s // bm,), in_specs=[blk] * 6,
        out_specs=pl.BlockSpec((bm, 3 * w), lambda i: (i, 0)),
        out_shape=jax.ShapeDtypeStruct((s, 3 * w), BF16), compiler_params=_cp(("parallel",)),
    )(*os_, *lses)


def _dil_mix_bwd(os_, lses, dmixed):
    s, w = os_[0].shape
    bm = _tile(s, (256, 128, 8))
    hg = w // LANES

    def body(o0, o1, o2, l0, l1, l2, dm_ref, do0, do1, do2, dl0, dl1, dl2):
        ls = [l0[...], l1[...], l2[...]]
        m = jnp.maximum(jnp.maximum(ls[0], ls[1]), ls[2])
        es = [jnp.exp(v - m) for v in ls]
        inv = 1.0 / (es[0] + es[1] + es[2])
        alphas = [e * inv for e in es]
        dalphas = []
        for gi, (o_ref, do_ref) in enumerate(((o0, do0), (o1, do1), (o2, do2))):
            dm = dm_ref[:, gi * w:(gi + 1) * w].astype(F32)
            do_ref[...] = (dm * alphas[gi]).astype(do_ref.dtype)
            prod = dm * o_ref[...]
            parts = [jnp.broadcast_to(jnp.sum(prod[:, j * LANES:(j + 1) * LANES], axis=1, keepdims=True), (bm, LANES))
                     for j in range(hg)]
            dalphas.append(jnp.concatenate(parts, axis=1) if hg > 1 else parts[0])
        mean = alphas[0] * dalphas[0] + alphas[1] * dalphas[1] + alphas[2] * dalphas[2]
        for gi, dl_ref in enumerate((dl0, dl1, dl2)):
            dl_ref[...] = alphas[gi] * (dalphas[gi] - mean)

    blk = pl.BlockSpec((bm, w), lambda i: (i, 0))
    return pl.pallas_call(
        body, name="dil_mix_bwd", grid=(s // bm,), in_specs=[blk] * 6 + [pl.BlockSpec((bm, 3 * w), lambda i: (i, 0))],
        out_specs=[blk] * 6,
        out_shape=[jax.ShapeDtypeStruct((s, w), BF16)] * 3 + [jax.ShapeDtypeStruct((s, w), F32)] * 3,
        compiler_params=_cp(("parallel",)),
    )(*os_, *lses, dmixed)


def _xattn_fwd(q, kv):
    s, w = q.shape
    mlen = kv.shape[0]
    tq = _tile(s, (512, 256, 128))
    scale = LANES ** -0.5

    def body(q_ref, k_ref, v_ref, o_ref):
        for h in range(XA_HEADS):
            sl = slice(h * LANES, (h + 1) * LANES)
            sc = lax.dot_general(q_ref[:, sl], k_ref[:, sl], NT, preferred_element_type=F32) * scale
            m = jnp.max(sc, axis=1, keepdims=True)
            e = jnp.exp(sc - m)
            p = e / jnp.sum(e, axis=1, keepdims=True)
            o_ref[:, sl] = jnp.dot(p.astype(BF16), v_ref[:, sl], preferred_element_type=F32).astype(o_ref.dtype)

    return pl.pallas_call(
        body, name="xattn_fwd", grid=(s // tq,),
        in_specs=[pl.BlockSpec((tq, w), lambda i: (i, 0)), pl.BlockSpec((mlen, w), lambda i: (0, 0)),
                  pl.BlockSpec((mlen, w), lambda i: (0, 1))],
        out_specs=pl.BlockSpec((tq, w), lambda i: (i, 0)),
        out_shape=jax.ShapeDtypeStruct((s, w), BF16), compiler_params=_cp(("parallel",)),
    )(q, kv, kv)


def _xattn_bwd(q, kv, do):
    s, w = q.shape
    mlen = kv.shape[0]
    tq = _tile(s, (512, 256, 128))
    scale = LANES ** -0.5

    def body(q_ref, k_ref, v_ref, do_ref, dq_ref, dk_ref, dv_ref):
        @pl.when(pl.program_id(0) == 0)
        def _():
            dk_ref[...] = jnp.zeros_like(dk_ref)
            dv_ref[...] = jnp.zeros_like(dv_ref)

        for h in range(XA_HEADS):
            sl = slice(h * LANES, (h + 1) * LANES)
            qh, kh, vh, doh = q_ref[:, sl], k_ref[:, sl], v_ref[:, sl], do_ref[:, sl]
            sc = lax.dot_general(qh, kh, NT, preferred_element_type=F32) * scale
            m = jnp.max(sc, axis=1, keepdims=True)
            e = jnp.exp(sc - m)
            p = e / jnp.sum(e, axis=1, keepdims=True)
            dp = lax.dot_general(doh, vh, NT, preferred_element_type=F32)
            ds = (p * (dp - jnp.sum(p * dp, axis=1, keepdims=True)) * scale).astype(BF16)
            dq_ref[:, sl] = jnp.dot(ds, kh, preferred_element_type=F32).astype(dq_ref.dtype)
            dk_ref[:, sl] += lax.dot_general(ds, qh, TN, preferred_element_type=F32)
            dv_ref[:, sl] += lax.dot_general(p.astype(BF16), doh, TN, preferred_element_type=F32)

    row = pl.BlockSpec((tq, w), lambda i: (i, 0))
    acc = pl.BlockSpec((mlen, w), lambda i: (0, 0))
    return pl.pallas_call(
        body, name="xattn_bwd", grid=(s // tq,),
        in_specs=[row, acc, pl.BlockSpec((mlen, w), lambda i: (0, 1)), row],
        out_specs=[row, acc, acc],
        out_shape=[jax.ShapeDtypeStruct((s, w), BF16), jax.ShapeDtypeStruct((mlen, w), F32),
                   jax.ShapeDtypeStruct((mlen, w), F32)],
        compiler_params=_cp(("arbitrary",)),
    )(q, kv, kv, do)


def _adamw(parts, w, m, v, name, ex=None):
    r, c = w.shape
    npc = len(parts)
    rp = r // npc
    br = _tile(rp, tuple(p for p in (256, 128, 64, 32, 16, 8) if p * c * 4 <= 1024 * 1024))
    steps = rp // br
    jobs = ex.take(r * c * ADAMW_EXCHANGE_BYTES_PER_PARAM) if ex is not None else []
    nj = len(jobs)

    def body(*refs):
        w_ref, m_ref, v_ref = refs[npc:npc + 3]
        g_ref, d_ref, nm_ref, nv_ref = refs[npc + 3 + nj:npc + 7 + nj]
        if nj:
            job_refs = (refs[npc + 3:npc + 3 + nj], refs[npc + 7 + nj:npc + 7 + 2 * nj], refs[npc + 7 + 2 * nj:])
            p_id, i_id = pl.program_id(0), pl.program_id(1)
            pl.when((p_id == 0) & (i_id == 0))(lambda: ex.start(jobs, *job_refs))
            pl.when((p_id == npc - 1) & (i_id == steps - 1))(lambda: ex.wait(jobs, *job_refs))

        def update(p_ref):
            g = p_ref[0].astype(F32)
            for t in range(1, N_DEV):
                g = g + p_ref[t].astype(F32)
            nm = ADAM_B1 * m_ref[...] + (1.0 - ADAM_B1) * g
            nv = ADAM_B2 * v_ref[...] + (1.0 - ADAM_B2) * (g * g)
            m_hat = nm / (1.0 - ADAM_B1 ** ADAM_STEP)
            v_hat = nv / (1.0 - ADAM_B2 ** ADAM_STEP)
            g_ref[...] = g
            d_ref[...] = -ADAM_LR * (m_hat / (jnp.sqrt(v_hat) + ADAM_EPS) + ADAM_WD * w_ref[...])
            nm_ref[...] = nm
            nv_ref[...] = nv

        for k in range(npc):
            pl.when(pl.program_id(0) == k)(lambda k=k: update(refs[k]))

    blk = pl.BlockSpec((br, c), lambda p, i: (p * steps + i, 0))
    part_specs = [pl.BlockSpec((N_DEV, br, c), lambda p, i, k=k: (0, jnp.where(p == k, i, 0), 0)) for k in range(npc)]
    outs = pl.pallas_call(
        body, name=name + "_carry" if nj else name, grid=(npc, steps),
        in_specs=part_specs + [blk, blk, blk] + [ANY] * nj, out_specs=[blk] * 4 + [ANY] * nj,
        out_shape=[jax.ShapeDtypeStruct((r, c), F32)] * 4 + [j.recv_shape for j in jobs],
        scratch_shapes=_piece_sems(nj) if nj else [],
        compiler_params=_cp(("arbitrary", "arbitrary") if nj else ("parallel", "parallel")),
    )(*parts, w, m, v, *[j.g for j in jobs])
    if nj:
        ex.landed(jobs, outs[4:])
    return outs[:4]


MESH_ID = pl.DeviceIdType.MESH
ANY = pl.BlockSpec(memory_space=pl.ANY)


def _my_place():
    return lax.axis_index("x"), lax.axis_index("y"), lax.axis_index("c")


def _all_gather(xs, name):
    nt = len(xs)

    def body(*refs):
        x_refs, out_refs = refs[:nt], refs[nt:2 * nt]
        send_sems, recv_sems, local_sems = refs[2 * nt:]
        x, y, c = _my_place()
        me, sibling = (x, y, c), (x, y, 1 - c)
        chips = [(1 - x, y), (x, 1 - y), (1 - x, 1 - y)]

        def slot(t, p):
            return out_refs[t].at[4 * p[0] + 2 * p[1] + p[2]]

        def copy(t, k, block, to, src=None):
            return pltpu.make_async_remote_copy(
                src_ref=slot(t, block) if src is None else src, dst_ref=slot(t, block),
                send_sem=send_sems.at[7 * t + k], recv_sem=recv_sems.at[7 * t + k], device_id=to,
                device_id_type=MESH_ID)

        mine, first, passed = [], [], []
        for t in range(nt):
            cp = pltpu.make_async_copy(x_refs[t], slot(t, me), local_sems.at[t])
            cp.start()
            mine.append(cp)
            group = [copy(t, 0, me, sibling, src=x_refs[t])]
            group += [copy(t, 1 + j, me, (*chip, c), src=x_refs[t]) for j, chip in enumerate(chips)]
            for cp in group:
                cp.start()
            first += group
        for t in range(nt):
            for j, chip in enumerate(chips):
                copy(t, 1 + j, (*chip, c), me).wait_recv()
                fw = copy(t, 4 + j, (*chip, c), sibling)
                fw.start()
                passed.append(fw)
        for t in range(nt):
            copy(t, 0, sibling, me).wait_recv()
            for j, chip in enumerate(chips):
                copy(t, 4 + j, (*chip, 1 - c), me).wait_recv()
        for cp in first + passed:
            cp.wait_send()
        for cp in mine:
            cp.wait()

    return pl.pallas_call(
        body, name=name, in_specs=[ANY] * nt, out_specs=[ANY] * nt,
        out_shape=[jax.ShapeDtypeStruct((N_DEV,) + tuple(v.shape), v.dtype) for v in xs],
        scratch_shapes=[pltpu.SemaphoreType.DMA((7 * nt,)), pltpu.SemaphoreType.DMA((7 * nt,)),
                        pltpu.SemaphoreType.DMA((nt,))],
    )(*xs)


def _exchange(gs, name):
    nt = len(gs)

    def body(*refs):
        g_refs, out_refs = refs[:nt], refs[nt:2 * nt]
        send_sems, recv_sems, local_sems = refs[2 * nt:]
        x, y, c = _my_place()
        my_slot = 4 * x + 2 * y + c
        mine, sent = [], []
        for t in range(nt):
            cp = pltpu.make_async_copy(g_refs[t].at[my_slot], out_refs[t].at[my_slot], local_sems.at[t])
            cp.start()
            mine.append(cp)
            for rel in range(1, N_DEV):
                px, py, pc = x ^ ((rel >> 2) & 1), y ^ ((rel >> 1) & 1), c ^ (rel & 1)
                cp = pltpu.make_async_remote_copy(
                    src_ref=g_refs[t].at[4 * px + 2 * py + pc], dst_ref=out_refs[t].at[my_slot],
                    send_sem=send_sems.at[7 * t + rel - 1], recv_sem=recv_sems.at[7 * t + rel - 1],
                    device_id=(px, py, pc), device_id_type=MESH_ID)
                cp.start()
                sent.append(cp)
        for cp in sent:
            cp.wait_recv()
        for cp in sent:
            cp.wait_send()
        for cp in mine:
            cp.wait()

    return pl.pallas_call(
        body, name=name, in_specs=[ANY] * nt, out_specs=[ANY] * nt,
        out_shape=[jax.ShapeDtypeStruct(tuple(v.shape), v.dtype) for v in gs],
        scratch_shapes=[pltpu.SemaphoreType.DMA((7 * nt,)), pltpu.SemaphoreType.DMA((7 * nt,)),
                        pltpu.SemaphoreType.DMA((nt,))],
    )(*gs)


EXCHANGE_BYTES_PER_FLOP = 1.3e-4
ADAMW_EXCHANGE_BYTES_PER_PARAM = 1.2
PIECE_BYTES = 24 * 1024 * 1024
CARRIER_MIN_BYTES = 6 * 1024 * 1024
ROW_ALIGN = 16


class _Piece:
    def __init__(self, key, g, axis, lo, hi):
        self.key, self.g, self.axis, self.lo, self.hi = key, g, axis, lo, hi
        cols = g.shape[1] if axis == 0 else g.shape[1] // N_DEV
        self.recv_shape = jax.ShapeDtypeStruct((N_DEV, hi - lo, cols), g.dtype)
        self.nbytes = N_DEV * (hi - lo) * cols * g.dtype.itemsize


def _piece_sems(nj):
    return [pltpu.SemaphoreType.DMA((7 * nj,)), pltpu.SemaphoreType.DMA((7 * nj,)), pltpu.SemaphoreType.DMA((nj,))]


def _piece_copies(jobs, g_refs, recv_refs, sems):
    send_sems, recv_sems, local_sems = sems
    x, y, c = _my_place()
    me = 4 * x + 2 * y + c
    local, remote = [], []
    for t, (job, g, r) in enumerate(zip(jobs, g_refs, recv_refs)):
        rows = job.hi - job.lo

        def block(slot, job=job, g=g, r=r, rows=rows):
            if job.axis == 0:
                start = pl.multiple_of(slot * (g.shape[0] // N_DEV) + job.lo, ROW_ALIGN)
                return g.at[pl.ds(start, rows), :]
            cols = r.shape[2]
            return g.at[pl.ds(job.lo, rows), pl.ds(pl.multiple_of(slot * cols, LANES), cols)]

        local.append(pltpu.make_async_copy(block(me), r.at[me], local_sems.at[t]))
        for rel in range(1, N_DEV):
            px, py, pc = x ^ ((rel >> 2) & 1), y ^ ((rel >> 1) & 1), c ^ (rel & 1)
            remote.append(pltpu.make_async_remote_copy(
                src_ref=block(4 * px + 2 * py + pc), dst_ref=r.at[me], send_sem=send_sems.at[7 * t + rel - 1],
                recv_sem=recv_sems.at[7 * t + rel - 1], device_id=(px, py, pc), device_id_type=MESH_ID))
    return local, remote


def _pieces_start(jobs, g_refs, recv_refs, sems):
    local, remote = _piece_copies(jobs, g_refs, recv_refs, sems)
    for cp in local + remote:
        cp.start()


def _pieces_wait(jobs, g_refs, recv_refs, sems):
    local, remote = _piece_copies(jobs, g_refs, recv_refs, sems)
    for cp in remote:
        cp.wait_recv()
    for cp in remote:
        cp.wait_send()
    for cp in local:
        cp.wait()


def _exchange_pieces(jobs, name):
    nj = len(jobs)

    def body(*refs):
        job_refs = (refs[:nj], refs[nj:2 * nj], refs[2 * nj:])
        _pieces_start(jobs, *job_refs)
        _pieces_wait(jobs, *job_refs)

    return pl.pallas_call(
        body, name=name, in_specs=[ANY] * nj, out_specs=[ANY] * nj, out_shape=[j.recv_shape for j in jobs],
        scratch_shapes=_piece_sems(nj),
    )(*[j.g for j in jobs])


class _GradExchange:
    def __init__(self):
        self.queue, self.recv = [], {}

    def put(self, name, layer, g):
        axis = SHARD_AXIS[name] - 1
        rows = g.shape[0] // N_DEV if axis == 0 else g.shape[0]
        if g.dtype != BF16 or rows % ROW_ALIGN or (axis == 1 and (g.shape[1] // N_DEV) % LANES):
            return False
        n_split = max(1, round(g.size * g.dtype.itemsize / PIECE_BYTES))
        while rows % (n_split * ROW_ALIGN):
            n_split -= 1
        for k in range(n_split):
            self.queue.append(_Piece((name, layer, k), g, axis, k * rows // n_split, (k + 1) * rows // n_split))
        return True

    def take(self, capacity):
        jobs, used = [], 0
        if capacity >= CARRIER_MIN_BYTES and self.queue and self.queue[0].nbytes <= 2.2 * capacity:
            jobs.append(self.queue.pop(0))
            used = jobs[0].nbytes
            while self.queue and used + self.queue[0].nbytes <= capacity:
                used += self.queue[0].nbytes
                jobs.append(self.queue.pop(0))
        return jobs

    def landed(self, jobs, recvs):
        for j, r in zip(jobs, recvs):
            self.recv[j.key] = r

    def flush(self):
        if self.queue:
            jobs, self.queue = self.queue, []
            self.landed(jobs, _exchange_pieces(jobs, "exchange_rest"))

    def pieces_of(self, name):
        return [self.recv[k] for k in sorted(k for k in self.recv if k[0] == name)]

    start = staticmethod(_pieces_start)
    wait = staticmethod(_pieces_wait)


GATHER_SPEEDUP = 2.0


class _WeightPiece:
    def __init__(self, key, local, axis):
        self.key, self.g, self.axis = key, local, axis
        r, c = local.shape
        self.recv_shape = jax.ShapeDtypeStruct((r * N_DEV, c) if axis == 0 else (r, c * N_DEV), local.dtype)
        self.nbytes = N_DEV * r * c * local.dtype.itemsize


def _gather_copies(jobs, x_refs, full_refs, sems):
    send_sems, recv_sems, local_sems = sems
    x, y, c = _my_place()
    me, sibling = (x, y, c), (x, y, 1 - c)
    chips = [(1 - x, y), (x, 1 - y), (1 - x, 1 - y)]
    plans = []
    for t, (job, xr, fr) in enumerate(zip(jobs, x_refs, full_refs)):
        def blk(p, job=job, xr=xr, fr=fr):
            slot = 4 * p[0] + 2 * p[1] + p[2]
            if job.axis == 0:
                return fr.at[pl.ds(pl.multiple_of(slot * xr.shape[0], ROW_ALIGN), xr.shape[0]), :]
            return fr.at[:, pl.ds(pl.multiple_of(slot * xr.shape[1], LANES), xr.shape[1])]

        def copy(k, block, to, src=None, t=t, blk=blk):
            return pltpu.make_async_remote_copy(
                src_ref=blk(block) if src is None else src, dst_ref=blk(block), send_sem=send_sems.at[7 * t + k],
                recv_sem=recv_sems.at[7 * t + k], device_id=to, device_id_type=MESH_ID)

        plans.append(dict(
            mine=pltpu.make_async_copy(xr, blk(me), local_sems.at[t]),
            first=[copy(0, me, sibling, src=xr)] + [copy(1 + j, me, (*chip, c), src=xr) for j, chip in enumerate(chips)],
            landed=[copy(1 + j, (*chip, c), me) for j, chip in enumerate(chips)],
            passed=[copy(4 + j, (*chip, c), sibling) for j, chip in enumerate(chips)],
            from_sibling=[copy(0, sibling, me)] + [copy(4 + j, (*chip, 1 - c), me) for j, chip in enumerate(chips)]))
    return plans


def _gather_start(jobs, x_refs, full_refs, sems):
    for plan in _gather_copies(jobs, x_refs, full_refs, sems):
        plan["mine"].start()
        for cp in plan["first"]:
            cp.start()


def _gather_wait(jobs, x_refs, full_refs, sems):
    plans = _gather_copies(jobs, x_refs, full_refs, sems)
    for plan in plans:
        for landed, passed in zip(plan["landed"], plan["passed"]):
            landed.wait_recv()
            passed.start()
    for plan in plans:
        for cp in plan["from_sibling"]:
            cp.wait_recv()
        for cp in plan["first"] + plan["passed"]:
            cp.wait_send()
        plan["mine"].wait()


def _gather_pieces(jobs, name):
    nj = len(jobs)

    def body(*refs):
        job_refs = (refs[:nj], refs[nj:2 * nj], refs[2 * nj:])
        _gather_start(jobs, *job_refs)
        _gather_wait(jobs, *job_refs)

    return pl.pallas_call(
        body, name=name, in_specs=[ANY] * nj, out_specs=[ANY] * nj, out_shape=[j.recv_shape for j in jobs],
        scratch_shapes=_piece_sems(nj),
    )(*[j.g for j in jobs])


class _WeightGather:
    def __init__(self):
        self.queue, self.full = [], {}

    def add(self, name, layer, local):
        self.queue.append(_WeightPiece((name, layer), local, SHARD_AXIS[name] - 1))

    def take(self, capacity):
        capacity *= GATHER_SPEEDUP
        jobs, used = [], 0
        if capacity >= CARRIER_MIN_BYTES and self.queue and self.queue[0].nbytes <= 2.2 * capacity:
            jobs.append(self.queue.pop(0))
            used = jobs[0].nbytes
            while self.queue and used + self.queue[0].nbytes <= capacity:
                used += self.queue[0].nbytes
                jobs.append(self.queue.pop(0))
        return jobs

    def landed(self, jobs, fulls):
        for j, f in zip(jobs, fulls):
            self.full[j.key] = f

    def get(self, name, layer):
        if (name, layer) not in self.full:
            at = [j.key for j in self.queue].index((name, layer))
            jobs, self.queue = self.queue[:at + 1], self.queue[at + 1:]
            self.landed(jobs, _gather_pieces(jobs, "gather_now"))
        return self.full[(name, layer)]

    start = staticmethod(_gather_start)
    wait = staticmethod(_gather_wait)


def _val(w):
    return w() if callable(w) else w


def _ffn_fwd(x, g, w_gu, w_d, tag, wg=None):
    n = _rms_fwd(x, g)
    gu = _mm(n, _val(w_gu), out_dtype=BF16, name=f"{tag}_gu", ex=wg)
    a = _swiglu_fwd(gu)
    return _mm(a, _val(w_d), alpha=0.5, res=x, name=f"{tag}_down", ex=wg), (n, gu, a)


def _put(ex, name, layer, g):
    if ex is not None:
        ex.put(name, layer, g)


def _ffn_bwd(dy, x, g, w_gu, w_d, saved, tag, ex=None, which="ffn1", layer=0):
    n, gu, a = saved
    da = _mm(dy, w_d, tb=True, alpha=0.5, out_dtype=BF16, name=f"{tag}_da", ex=ex)
    d_wd = _mm(a, dy, ta=True, alpha=0.5, out_dtype=BF16, name=f"{tag}_dwd", ex=ex)
    _put(ex, f"{which}_w_down", layer, d_wd)
    dgu = _swiglu_bwd(gu, da)
    dn = _mm(dgu, w_gu, tb=True, name=f"{tag}_dn", ex=ex)
    d_wgu = _mm(n, dgu, ta=True, out_dtype=BF16, name=f"{tag}_dwgu", ex=ex)
    _put(ex, f"{which}_w_gate_up", layer, d_wgu)
    dx, dg = _rms_bwd(x, g, dn, dy)
    return dx, dg, d_wgu, d_wd


def _sb_mixer_fwd(h, w_qkv, w_o, x, wg=None):
    qkv = _mm(h, _val(w_qkv), out_dtype=BF16, name="sb_qkv", ex=wg)
    o, ltot = _sb_fwd(qkv, qkv.shape[1] // (3 * LANES), wg)
    return _mm(o, _val(w_o), res=x, name="sb_out", ex=wg), (qkv, o, ltot)


def _sb_mixer_bwd(dy, h, w_qkv, w_o, saved, ex=None, layer=0):
    qkv, o, ltot = saved
    n_heads = w_o.shape[0] // LANES
    do = _mm(dy, w_o, tb=True, out_dtype=BF16, name="sb_do", ex=ex)
    d_wo = _mm(o, dy, ta=True, out_dtype=BF16, name="sb_dwo", ex=ex)
    _put(ex, "sb_w_o", layer, d_wo)
    dq, dk, dv = _sb_bwd(qkv, do, ltot, n_heads, ex)
    dqkv = jnp.concatenate([dq, dk.astype(BF16), dv.astype(BF16)], axis=1)
    dh = _mm(dqkv, w_qkv, tb=True, name="sb_dh", ex=ex)
    d_wqkv = _mm(h, dqkv, ta=True, out_dtype=BF16, name="sb_dwqkv", ex=ex)
    _put(ex, "sb_w_qkv", layer, d_wqkv)
    return dh, d_wqkv, d_wo


def _dil_cols(gi):
    ng = len(DIL_PATTERNS)
    return dict(q_blk=lambda n, st: n * 2 * ng + gi, k_blk=lambda n, st: n * 2 * ng + ng + gi,
                v_blk=lambda n, st: n * 3 * ng + 2 * ng + gi)


def _dil_mixer_fwd(h, w_qkv, w_o, x, tabs, wg=None):
    s = h.shape[0]
    qkv = _mm(h, _val(w_qkv), name="dil_qkv", ex=wg)
    n_all = qkv.shape[1] // (3 * LANES)
    hg = n_all // len(DIL_PATTERNS)
    qk = _rope(qkv, tabs[0], tabs[1], 2 * n_all, 16, "dil_rope")
    os_, lses = [], []
    for gi, (window, dil) in enumerate(DIL_PATTERNS):
        o, lse = _band_fwd(qk.reshape(s // dil, -1), qk.reshape(s // dil, -1), qkv.reshape(s // dil, -1),
                           n_cls=dil, n_steps=1, hpb=hg, group=1, **_dil_cols(gi),
                           max_dist=window // dil, scale=LANES ** -0.5, sinks=None, name=f"dil_fwd{gi}")
        os_.append(o.reshape(s, hg * LANES))
        lses.append(lse.reshape(s, hg * LANES))
    mixed = _dil_mix_fwd(os_, lses)
    return _mm(mixed, _val(w_o), res=x, name="dil_out", ex=wg), (qkv, qk, os_, lses, mixed)


def _dil_mixer_bwd(dy, h, w_qkv, w_o, saved, tabs_bwd, ex=None):
    qkv, qk, os_, lses, mixed = saved
    s = h.shape[0]
    n_all = w_o.shape[0] // LANES
    hg = n_all // len(DIL_PATTERNS)
    dmixed = _mm(dy, w_o, tb=True, out_dtype=BF16, name="dil_dmix", ex=ex)
    d_wo = _mm(mixed, dy, ta=True, out_dtype=BF16, name="dil_dwo", ex=ex)
    _put(ex, "dil_w_o", 0, d_wo)
    mix_out = _dil_mix_bwd(os_, lses, dmixed)
    dos, dlses = mix_out[:3], mix_out[3:]
    dqs, dks, dvs = [], [], []
    for gi, (window, dil) in enumerate(DIL_PATTERNS):
        length = s // dil
        dq, dkc, dkp, dvc, dvp = _band_bwd(
            qk.reshape(length, -1), qk.reshape(length, -1), qkv.reshape(length, -1), os_[gi].reshape(length, -1),
            dos[gi].reshape(length, -1), lses[gi].reshape(length, -1), dlses[gi].reshape(length, -1),
            n_cls=dil, n_steps=1, hpb=hg, group=1, **_dil_cols(gi), max_dist=window // dil,
            scale=LANES ** -0.5, sinks=None, name=f"dil_bwd{gi}")
        dqs.append(dq.reshape(s, -1))
        dks.append(_band_fold(dkc, dkp, n_cls=dil, n_heads=hg, group=1, name=f"dil_foldk{gi}").reshape(s, -1))
        dvs.append(_band_fold(dvc, dvp, n_cls=dil, n_heads=hg, group=1, name=f"dil_foldv{gi}").reshape(s, -1))
    dqk_rot = jnp.concatenate(dqs + dks, axis=1)
    dqk = _rope(dqk_rot, tabs_bwd[0], tabs_bwd[1], 2 * n_all, 16, "dil_rope_bwd")
    dqkv = jnp.concatenate([dqk] + [t.astype(BF16) for t in dvs], axis=1)
    dh = _mm(dqkv, w_qkv, tb=True, name="dil_dh", ex=ex)
    d_wqkv = _mm(h, dqkv, ta=True, out_dtype=BF16, name="dil_dwqkv", ex=ex)
    return dh, d_wqkv, d_wo


def _pad_heads(w, axis):
    shape = list(w.shape)
    n = shape[axis] // SWA_HEAD_DIM
    w = w.reshape(shape[:axis] + [n, SWA_HEAD_DIM] + shape[axis + 1:])
    pad = [(0, 0)] * w.ndim
    pad[axis + 1] = (0, LANES - SWA_HEAD_DIM)
    shape[axis] = n * LANES
    return jnp.pad(w, pad).reshape(shape)


def _unpad_heads(w, axis):
    shape = list(w.shape)
    n = shape[axis] // LANES
    w = w.reshape(shape[:axis] + [n, LANES] + shape[axis + 1:])
    w = lax.slice_in_dim(w, 0, SWA_HEAD_DIM, axis=axis + 1)
    shape[axis] = n * SWA_HEAD_DIM
    return w.reshape(shape)


def _swa_mixer_fwd(h, w_qkv_p, b_qkv_p, sinks_b, w_o_p, b_o, x, tabs, wg=None):
    nq = w_o_p.shape[0] // LANES
    nkv = nq // SWA_GROUP
    qkv = _mm(h, w_qkv_p, bias=b_qkv_p, name="swa_qkv", ex=wg)
    qk = _rope(qkv, tabs[0], tabs[1], nq + nkv, 8, "swa_rope")
    o, lse = _band_fwd(qk, qk, qkv, n_cls=1, n_steps=nkv, hpb=SWA_GROUP, group=SWA_GROUP, q_blk=lambda n, st: st,
                       k_blk=lambda n, st: nq + st, v_blk=lambda n, st: nq + nkv + st,
                       max_dist=SWA_WINDOW - 1, scale=SWA_HEAD_DIM ** -0.5, sinks=sinks_b, name="swa_fwd")
    return _mm(o, w_o_p, res=x, bias=b_o, name="swa_out"), (qkv, qk, o, lse)


def _swa_mixer_bwd(dy, h, w_qkv_p, sinks_b, w_o_p, saved, tabs_bwd, ex=None):
    qkv, qk, o, lse = saved
    nq = w_o_p.shape[0] // LANES
    nkv = nq // SWA_GROUP
    do = _mm(dy, w_o_p, tb=True, name="swa_do", ex=ex)
    d_wo_p = _mm(o, dy, ta=True, out_dtype=BF16, name="swa_dwo", ex=ex)
    d_bo = _colsum(dy, "swa_dbo")
    dq, dkc, dkp, dvc, dvp, dsink = _band_bwd(
        qk, qk, qkv, o, do, lse, None, n_cls=1, n_steps=nkv, hpb=SWA_GROUP, group=SWA_GROUP, q_blk=lambda n, st: st,
        k_blk=lambda n, st: nq + st, v_blk=lambda n, st: nq + nkv + st, max_dist=SWA_WINDOW - 1,
        scale=SWA_HEAD_DIM ** -0.5, sinks=sinks_b, name="swa_bwd")
    dk = _band_fold(dkc, dkp, n_cls=1, n_heads=nkv, group=1, name="swa_foldk")
    dv = _band_fold(dvc, dvp, n_cls=1, n_heads=nkv, group=1, name="swa_foldv")
    dqk = _rope(jnp.concatenate([dq, dk], axis=1), tabs_bwd[0], tabs_bwd[1], nq + nkv, 8, "swa_rope_bwd")
    dqkv = jnp.concatenate([dqk, dv.astype(BF16)], axis=1)
    d_bqkv_p = _colsum(dqkv, "swa_dbqkv")
    dh = _mm(dqkv, w_qkv_p, tb=True, name="swa_dh", ex=ex)
    d_wqkv_p = _mm(h, dqkv, ta=True, out_dtype=BF16, name="swa_dwqkv", ex=ex)
    return dh, d_wqkv_p, d_bqkv_p, dsink, d_wo_p, d_bo


def _xattn_layer_fwd(x, mem, g_x, g_m, w_q, w_kv, w_o):
    hq = _rms_fwd(x, g_x, "rms_fwd")
    hm = _rms_fwd(mem, g_m, "rms_mem_fwd")
    q = _mm(hq, _val(w_q), out_dtype=BF16, name="xa_q")
    kv = _mm(hm, _val(w_kv), out_dtype=BF16, name="xa_kv")
    o = _xattn_fwd(q, kv)
    return _mm(o, _val(w_o), res=x, name="xa_out"), (hq, hm, q, kv, o)


def _xattn_layer_bwd(dy, x, mem, g_x, g_m, w_q, w_kv, w_o, saved):
    hq, hm, q, kv, o = saved
    do = _mm(dy, w_o, tb=True, out_dtype=BF16, name="xa_do")
    d_wo = _mm(o, dy, ta=True, out_dtype=BF16, name="xa_dwo")
    dq, dk, dv = _xattn_bwd(q, kv, do)
    dkv = jnp.concatenate([dk, dv], axis=1).astype(BF16)
    dhq = _mm(dq, w_q, tb=True, name="xa_dhq")
    d_wq = _mm(hq, dq, ta=True, out_dtype=BF16, name="xa_dwq")
    dhm = _mm(dkv, w_kv, tb=True, name="xa_dhm")
    d_wkv = _mm(hm, dkv, ta=True, out_dtype=BF16, name="xa_dwkv")
    dx, dg_x = _rms_bwd(x, g_x, dhq, dy)
    _, dg_m = _rms_bwd(mem, g_m, dhm, None, "rms_mem_bwd")
    return dx, dg_x, dg_m, d_wq, d_wkv, d_wo


def _to_full(gathered, axis):
    t = jnp.moveaxis(gathered, 0, axis)
    shape = list(t.shape)
    return t.reshape(shape[:axis] + [shape[axis] * shape[axis + 1]] + shape[axis + 2:])


def _to_blocks(full, axis):
    shape = list(full.shape)
    t = full.reshape(shape[:axis] + [N_DEV, shape[axis] // N_DEV] + shape[axis + 1:])
    return jnp.moveaxis(t, axis, 0)


SHARD_AXIS = {
    "ffn1_w_gate_up": 2, "ffn1_w_down": 1, "sb_w_qkv": 2, "sb_w_o": 1, "dil_w_qkv": 2, "dil_w_o": 2,
    "swa_w_qkv": 2, "swa_b_qkv": 1, "swa_w_o": 1, "swa_b_o": 1, "xattn_w_q": 1, "xattn_w_kv": 1, "xattn_w_o": 2,
    "ffn2_w_gate_up": 2, "ffn2_w_down": 1,
}
SMALL = ("ffn1_norm", "mix_norm", "xattn_norm", "mem_norm", "ffn2_norm", "final_norm", "swa_sinks")
WEIGHTS = ("ffn1_norm", "ffn1_w_gate_up", "ffn1_w_down", "mix_norm", "sb_w_qkv", "sb_w_o", "dil_w_qkv", "dil_w_o",
           "swa_w_qkv", "swa_b_qkv", "swa_sinks", "swa_w_o", "swa_b_o", "xattn_norm", "mem_norm", "xattn_w_q",
           "xattn_w_kv", "xattn_w_o", "ffn2_norm", "ffn2_w_gate_up", "ffn2_w_down", "final_norm")


def _flat2(a):
    return a.reshape(-1, a.shape[-1])


def _pack_small(vals, d):
    rows = [vals[n].reshape(-1, d) for n in SMALL[:5]] + [vals["final_norm"].reshape(1, d)]
    sk = vals["swa_sinks"].reshape(1, -1)
    rows.append(jnp.pad(sk, ((0, 0), (0, d - sk.shape[1]))))
    rows.append(jnp.zeros((2, d), F32))
    return jnp.concatenate(rows, axis=0)


def _unpack_small(packed, like):
    out, r = {}, 0
    for n in SMALL[:5]:
        k = like[n].shape[0]
        out[n] = packed[r:r + k]
        r += k
    out["final_norm"] = packed[r]
    out["swa_sinks"] = packed[r + 1:r + 2, :like["swa_sinks"].shape[1]]
    return out


def _local_step(x0, mem0, positions, target, full, norm, ex=None, wg=None):
    d = x0.shape[1]
    names = list(SHARD_AXIS)
    sinks_b = jnp.repeat(norm["swa_sinks"], LANES, axis=1)
    tabs_dil, tabs_dil_bwd = _rope_tables(positions, 32), _rope_tables(positions, 32, -1.0)
    tabs_swa, tabs_swa_bwd = _rope_tables(positions, 16), _rope_tables(positions, 16, -1.0)

    def vec(name, i):
        return norm[name][i:i + 1]

    saved = []
    xc = x0
    for i in range(DEPTH):
        kind, j = i % 3, i // 3
        rec = {"x0": xc}
        xc, rec["ffn1"] = _ffn_fwd(xc, vec("ffn1_norm", i), lambda: full["ffn1_w_gate_up"][i],
                                   lambda: full["ffn1_w_down"][i], "ffn", wg)
        rec["x1"] = xc
        h = _rms_fwd(xc, vec("mix_norm", i))
        rec["h"] = h
        if kind == 0:
            xc, rec["mix"] = _sb_mixer_fwd(h, lambda: full["sb_w_qkv"][j], lambda: full["sb_w_o"][j], xc, wg)
        elif kind == 1:
            xc, rec["mix"] = _dil_mixer_fwd(h, lambda: full["dil_w_qkv"][j], lambda: full["dil_w_o"][j], xc, tabs_dil, wg)
        else:
            swa_w_qkv_p = _pad_heads(full["swa_w_qkv"][0], 1)
            swa_b_qkv_p = _pad_heads(full["swa_b_qkv"][0][None], 1)
            swa_w_o_p = _pad_heads(full["swa_w_o"][0], 0)
            xc, rec["mix"] = _swa_mixer_fwd(h, swa_w_qkv_p, swa_b_qkv_p, sinks_b, swa_w_o_p, full["swa_b_o"][0][None],
                                            xc, tabs_swa, wg)
        rec["x2"] = xc
        xc, rec["xa"] = _xattn_layer_fwd(xc, mem0, vec("xattn_norm", i), vec("mem_norm", i),
                                         lambda: full["xattn_w_q"][i], lambda: full["xattn_w_kv"][i],
                                         lambda: full["xattn_w_o"][i])
        rec["x3"] = xc
        xc, rec["ffn2"] = _ffn_fwd(xc, vec("ffn2_norm", i), lambda: full["ffn2_w_gate_up"][i],
                                   lambda: full["ffn2_w_down"][i], "ffn", wg)
        saved.append(rec)

    loss_part, dx, dg_final = _loss_head(xc, norm["final_norm"].reshape(1, d), target)

    gfull = {n: [None] * full[n].shape[0] for n in names}
    gsmall = {n: [None] * DEPTH for n in SMALL[:5]}
    gsmall["final_norm"] = dg_final
    gsmall["swa_sinks"] = jnp.zeros_like(norm["swa_sinks"])
    for i in reversed(range(DEPTH)):
        kind, j = i % 3, i // 3
        rec = saved[i]
        dx, gsmall["ffn2_norm"][i], gfull["ffn2_w_gate_up"][i], gfull["ffn2_w_down"][i] = _ffn_bwd(
            dx, rec["x3"], vec("ffn2_norm", i), full["ffn2_w_gate_up"][i], full["ffn2_w_down"][i], rec["ffn2"], "ffn",
            ex, "ffn2", i)
        (dx, gsmall["xattn_norm"][i], gsmall["mem_norm"][i], gfull["xattn_w_q"][i], gfull["xattn_w_kv"][i],
         gfull["xattn_w_o"][i]) = _xattn_layer_bwd(dx, rec["x2"], mem0, vec("xattn_norm", i), vec("mem_norm", i),
                                                   full["xattn_w_q"][i], full["xattn_w_kv"][i], full["xattn_w_o"][i],
                                                   rec["xa"])
        for n in ("xattn_w_q", "xattn_w_kv", "xattn_w_o"):
            _put(ex, n, i, gfull[n][i])
        if kind == 0:
            dh, gfull["sb_w_qkv"][j], gfull["sb_w_o"][j] = _sb_mixer_bwd(
                dx, rec["h"], full["sb_w_qkv"][j], full["sb_w_o"][j], rec["mix"], ex, j)
        elif kind == 1:
            dh, gfull["dil_w_qkv"][j], gfull["dil_w_o"][j] = _dil_mixer_bwd(
                dx, rec["h"], full["dil_w_qkv"][j], full["dil_w_o"][j], rec["mix"], tabs_dil_bwd, ex)
        else:
            dh, d_wqkv_p, d_bqkv_p, dsink, d_wo_p, d_bo = _swa_mixer_bwd(
                dx, rec["h"], swa_w_qkv_p, sinks_b, swa_w_o_p, rec["mix"], tabs_swa_bwd, ex)
            gfull["swa_w_qkv"][j] = _unpad_heads(d_wqkv_p, 1)
            gfull["swa_b_qkv"][j] = _unpad_heads(d_bqkv_p, 1)[0]
            gfull["swa_w_o"][j] = _unpad_heads(d_wo_p, 0)
            gfull["swa_b_o"][j] = d_bo[0]
            gsmall["swa_sinks"] = dsink.reshape(1, -1, LANES)[:, :, 0]
            _put(ex, "swa_w_o", j, gfull["swa_w_o"][j])
        dx, gsmall["mix_norm"][i] = _rms_bwd(rec["x1"], vec("mix_norm", i), dh, dx)
        dx, gsmall["ffn1_norm"][i], gfull["ffn1_w_gate_up"][i], gfull["ffn1_w_down"][i] = _ffn_bwd(
            dx, rec["x0"], vec("ffn1_norm", i), full["ffn1_w_gate_up"][i], full["ffn1_w_down"][i], rec["ffn1"], "ffn",
            ex, "ffn1", i)
    for n in SMALL[:5]:
        gsmall[n] = jnp.concatenate(gsmall[n], axis=0)
    return loss_part[0, 0], dx, gfull, gsmall


def _train_step(x, mem, positions, loss_target, w, m, v):
    d = x.shape[2]
    names = list(SHARD_AXIS)
    norm = {n: w[n] for n in SMALL}

    def carried(n):
        if w[n].ndim != 3:
            return False
        return w[n].shape[2] % LANES == 0 if SHARD_AXIS[n] == 2 else w[n].shape[1] % ROW_ALIGN == 0

    first = [n for n in names if not carried(n)]
    gathered = _all_gather([_flat2(w[n].astype(BF16) if w[n].ndim == 3 else w[n]) for n in first], "gather_weights")
    stacked = {n: _to_full(g.reshape((N_DEV,) + w[n].shape), SHARD_AXIS[n]) for n, g in zip(first, gathered)}
    wg = _WeightGather()
    for i in range(DEPTH):
        mixer = (("sb_w_qkv", "sb_w_o"), ("dil_w_qkv", "dil_w_o"), ("swa_w_qkv", "swa_w_o"))[i % 3]
        for n in ("ffn1_w_gate_up", "ffn1_w_down") + mixer + ("xattn_w_q", "xattn_w_kv", "xattn_w_o",
                                                               "ffn2_w_gate_up", "ffn2_w_down"):
            layer = i // 3 if n in mixer else i
            if carried(n):
                wg.add(n, layer, w[n][layer].astype(BF16))

    class Layers:
        def __init__(self, n):
            self.n, self.shape = n, w[n].shape[:1]

        def __getitem__(self, layer):
            return wg.get(self.n, layer) if carried(self.n) else stacked[self.n][layer]

    full = {n: Layers(n) for n in names}
    ex = _GradExchange()
    loss_part, dx, gfull, gsmall = _local_step(x[0], mem[0], positions, loss_target[0], full, norm, ex, wg)
    loss = lax.psum(loss_part, MESH_AXES)
    grad_x = dx[None]

    grad, delta, new_m, new_v = {}, {}, {}, {}

    def update(n, parts, carrier):
        outs = _adamw(parts, _flat2(w[n]), _flat2(m[n]), _flat2(v[n]), "adamw", carrier)
        grad[n], delta[n], new_m[n], new_v[n] = (o.reshape(w[n].shape) for o in outs)

    taken = {k[0] for k in ex.recv} | {p.key[0] for p in ex.queue}
    late = {p.key[0] for p in ex.queue}
    for n in names:
        if n in taken and n not in late:
            update(n, ex.pieces_of(n), ex)
    ex.flush()
    for n in names:
        if n in late:
            update(n, ex.pieces_of(n), None)
    rest = [n for n in names if n not in taken]
    blocks = [_to_blocks(jnp.stack(gfull[n], axis=0), SHARD_AXIS[n]) for n in rest]
    blocks = [b.reshape(N_DEV, -1, b.shape[-1]) for b in blocks]
    for n, received in zip(rest, _exchange(blocks, "exchange_grads")):
        update(n, [received], None)

    small_parts = _all_gather([_pack_small(gsmall, d)], "gather_small_grads")[0]
    outs = _adamw([small_parts], _pack_small(norm, d), _pack_small({n: m[n] for n in SMALL}, d),
                  _pack_small({n: v[n] for n in SMALL}, d), "adamw_small")
    for res, o in zip((grad, delta, new_m, new_v), outs):
        res.update(_unpack_small(o, norm))
    return loss, grad_x, grad, delta, new_m, new_v


def kernel(x, mem, positions, ffn1_norm, ffn1_w_gate_up, ffn1_w_down, mix_norm, sb_w_qkv, sb_w_o, dil_w_qkv, dil_w_o, swa_w_qkv, swa_b_qkv, swa_sinks, swa_w_o, swa_b_o, xattn_norm, mem_norm, xattn_w_q, xattn_w_kv, xattn_w_o, ffn2_norm, ffn2_w_gate_up, ffn2_w_down, final_norm, loss_target, m_ffn1_norm, m_ffn1_w_gate_up, m_ffn1_w_down, m_mix_norm, m_sb_w_qkv, m_sb_w_o, m_dil_w_qkv, m_dil_w_o, m_swa_w_qkv, m_swa_b_qkv, m_swa_sinks, m_swa_w_o, m_swa_b_o, m_xattn_norm, m_mem_norm, m_xattn_w_q, m_xattn_w_kv, m_xattn_w_o, m_ffn2_norm, m_ffn2_w_gate_up, m_ffn2_w_down, m_final_norm, v_ffn1_norm, v_ffn1_w_gate_up, v_ffn1_w_down, v_mix_norm, v_sb_w_qkv, v_sb_w_o, v_dil_w_qkv, v_dil_w_o, v_swa_w_qkv, v_swa_b_qkv, v_swa_sinks, v_swa_w_o, v_swa_b_o, v_xattn_norm, v_mem_norm, v_xattn_w_q, v_xattn_w_kv, v_xattn_w_o, v_ffn2_norm, v_ffn2_w_gate_up, v_ffn2_w_down, v_final_norm):
    args = dict(locals())
    w = {n: args[n] for n in WEIGHTS}
    m = {n: args["m_" + n] for n in WEIGHTS}
    v = {n: args["v_" + n] for n in WEIGHTS}
    loss, grad_x, grad, delta, new_m, new_v = _train_step(x, mem, positions, loss_target, w, m, v)
    return (loss, grad_x, *[grad[n] for n in WEIGHTS], *[delta[n] for n in WEIGHTS],
            *[new_m[n] for n in WEIGHTS], *[new_v[n] for n in WEIGHTS])
```

```python
import jax
import jax.numpy as jnp
from jax import lax
from jax.experimental import pallas as pl
from jax.experimental.pallas import tpu as pltpu

F32 = jnp.float32
BF16 = jnp.bfloat16

N_DEV = 8
MESH_AXES = ("x", "y", "c")
LANES = 128
BAND = 128
NORM_EPS = 1e-6
ROPE_THETA = 500000.0
DIL_PATTERNS = ((128, 1), (512, 4), (2048, 16))
SWA_HEAD_DIM = 64
SWA_GROUP = 8
SWA_WINDOW = 128
XA_HEADS = 4
DEPTH = 4
ADAM_LR, ADAM_B1, ADAM_B2, ADAM_EPS, ADAM_WD, ADAM_STEP = 0.001, 0.9, 0.999, 1e-08, 0.01, 10
VMEM_LIMIT = 56 * 1024 * 1024
NEG = -1e30

NT = (((1,), (1,)), ((), ()))
TN = (((0,), (0,)), ((), ()))


def _tile(n, prefs):
    for p in prefs:
        if n % p == 0:
            return p
    return n


def _cp(sem):
    return pltpu.CompilerParams(dimension_semantics=sem, vmem_limit_bytes=VMEM_LIMIT)


MM_TILE_SIZES = (2816, 2048, 1408, 1024, 512, 256, 128)
MM_VMEM_BUDGET = 40 * 1024 * 1024
MM_STEP_BYTES = 1.2e6


def _mm_tiles(m, n, k, ea, eb, eo, has_res):
    best = None
    for bm in [c for c in MM_TILE_SIZES if m % c == 0] or [m]:
        for bn in [c for c in MM_TILE_SIZES if n % c == 0] or [n]:
            for bk in [c for c in MM_TILE_SIZES if k % c == 0] or [k]:
                vmem = 2 * (bm * bk * ea + bk * bn * eb) + bm * bn * 4 + 2 * bm * bn * (eo + (4 if has_res else 0))
                vmem += (bm * bk * 2 if ea == 4 else 0) + (bk * bn * 2 if eb == 4 else 0)
                if vmem > MM_VMEM_BUDGET:
                    continue
                ni, nj, nk = m // bm, n // bn, k // bk
                traffic = (m * k * ea * (nj if nk > 1 else 1) + k * n * eb * (ni if nk > 1 or nj > 1 else 1)
                           + m * n * (eo + (4 if has_res else 0)) + ni * nj * nk * MM_STEP_BYTES)
                if best is None or traffic < best[0]:
                    best = (traffic, (bm, bn, bk))
    return best[1]
def _mm(a, b, *, ta=False, tb=False, out_dtype=F32, alpha=1.0, res=None, bias=None, name, ex=None):
    kdim, m = a.shape if ta else a.shape[::-1]
    kdim2, n = b.shape[::-1] if tb else b.shape
    assert kdim == kdim2, (a.shape, b.shape, ta, tb)
    bm, bn, bk = _mm_tiles(m, n, kdim, a.dtype.itemsize, b.dtype.itemsize, jnp.dtype(out_dtype).itemsize, res is not None)
    nk = kdim // bk
    grid = (m // bm, n // bn, nk)
    dn = (((0 if ta else 1,), (1 if tb else 0,)), ((), ()))
    has_res, has_bias = res is not None, bias is not None
    jobs = ex.take(2.0 * m * n * kdim * EXCHANGE_BYTES_PER_FLOP) if ex is not None else []
    nj = len(jobs)
    n_in = 2 + has_res + has_bias

    def body(*refs):
        a_ref, b_ref = refs[0], refs[1]
        res_ref = refs[2] if has_res else None
        bias_ref = refs[2 + has_res] if has_bias else None
        o_ref, acc_ref = refs[n_in + nj], refs[n_in + 2 * nj + 1]
        k = pl.program_id(2)
        if nj:
            job_refs = (refs[n_in:n_in + nj], refs[n_in + nj + 1:n_in + 2 * nj + 1], refs[n_in + 2 * nj + 2:])
            ids = [pl.program_id(t) for t in range(3)]

            @pl.when((ids[0] == 0) & (ids[1] == 0) & (ids[2] == 0))
            def _():
                ex.start(jobs, *job_refs)

            @pl.when((ids[0] == grid[0] - 1) & (ids[1] == grid[1] - 1) & (ids[2] == grid[2] - 1))
            def _():
                ex.wait(jobs, *job_refs)

        @pl.when(k == 0)
        def _():
            acc_ref[...] = jnp.zeros_like(acc_ref)

        acc_ref[...] += lax.dot_general(a_ref[...].astype(BF16), b_ref[...].astype(BF16), dn,
                                        preferred_element_type=F32)

        @pl.when(k == nk - 1)
        def _():
            r = acc_ref[...]
            if alpha != 1.0:
                r = r * alpha
            if has_bias:
                r = r + bias_ref[...]
            if has_res:
                r = r + res_ref[...]
            o_ref[...] = r.astype(o_ref.dtype)

    a_spec = pl.BlockSpec((bk, bm), lambda i, j, k: (k, i)) if ta else pl.BlockSpec((bm, bk), lambda i, j, k: (i, k))
    b_spec = pl.BlockSpec((bn, bk), lambda i, j, k: (j, k)) if tb else pl.BlockSpec((bk, bn), lambda i, j, k: (k, j))
    ins, specs = [a, b], [a_spec, b_spec]
    if has_res:
        ins.append(res)
        specs.append(pl.BlockSpec((bm, bn), lambda i, j, k: (i, j)))
    if has_bias:
        ins.append(bias)
        specs.append(pl.BlockSpec((1, bn), lambda i, j, k: (0, j)))
    out_spec = pl.BlockSpec((bm, bn), lambda i, j, k: (i, j))
    out_shape = jax.ShapeDtypeStruct((m, n), out_dtype)
    scratch = [pltpu.VMEM((bm, bn), F32)]
    if not nj:
        return pl.pallas_call(
            body, name=name, grid=grid, in_specs=specs, out_specs=out_spec, out_shape=out_shape,
            scratch_shapes=scratch, compiler_params=_cp(("parallel", "parallel", "arbitrary")),
        )(*ins)
    outs = pl.pallas_call(
        body, name=name + "_carry", grid=grid, in_specs=specs + [ANY] * nj, out_specs=[out_spec] + [ANY] * nj,
        out_shape=[out_shape] + [j.recv_shape for j in jobs], scratch_shapes=scratch + _piece_sems(nj),
        compiler_params=_cp(("arbitrary", "arbitrary", "arbitrary")),
    )(*ins, *[j.g for j in jobs])
    ex.landed(jobs, outs[1:])
    return outs[0]


def _rms_fwd(x, g, name="rms_fwd"):
    s, d = x.shape
    bm = _tile(s, (256, 128, 8))

    def body(x_ref, g_ref, o_ref):
        xv = x_ref[...]
        r = lax.rsqrt(jnp.mean(xv * xv, axis=-1, keepdims=True) + NORM_EPS)
        o_ref[...] = (xv * r * g_ref[...]).astype(o_ref.dtype)

    return pl.pallas_call(
        body, name=name, grid=(s // bm,),
        in_specs=[pl.BlockSpec((bm, d), lambda i: (i, 0)), pl.BlockSpec((1, d), lambda i: (0, 0))],
        out_specs=pl.BlockSpec((bm, d), lambda i: (i, 0)),
        out_shape=jax.ShapeDtypeStruct((s, d), BF16), compiler_params=_cp(("parallel",)),
    )(x, g)


def _rms_bwd(x, g, dn, dy=None, name="rms_bwd"):
    s, d = x.shape
    bm = _tile(s, (256, 128, 8))
    has_dy = dy is not None

    def body(*refs):
        x_ref, g_ref, dn_ref = refs[:3]
        dy_ref = refs[3] if has_dy else None
        dx_ref, dg_ref = refs[-2], refs[-1]

        @pl.when(pl.program_id(0) == 0)
        def _():
            dg_ref[...] = jnp.zeros_like(dg_ref)

        xv = x_ref[...]
        r = lax.rsqrt(jnp.mean(xv * xv, axis=-1, keepdims=True) + NORM_EPS)
        xh = xv * r
        dnv = dn_ref[...].astype(F32)
        dxh = dnv * g_ref[...]
        dx = r * (dxh - xh * jnp.mean(dxh * xh, axis=-1, keepdims=True))
        if has_dy:
            dx = dx + dy_ref[...]
        dx_ref[...] = dx
        dg_ref[...] += jnp.sum(dnv * xh, axis=0, keepdims=True)

    row = pl.BlockSpec((bm, d), lambda i: (i, 0))
    vec = pl.BlockSpec((1, d), lambda i: (0, 0))
    ins, specs = [x, g, dn], [row, vec, row]
    if has_dy:
        ins.append(dy)
        specs.append(row)
    return pl.pallas_call(
        body, name=name, grid=(s // bm,), in_specs=specs, out_specs=[row, vec],
        out_shape=[jax.ShapeDtypeStruct((s, d), F32), jax.ShapeDtypeStruct((1, d), F32)],
        compiler_params=_cp(("arbitrary",)),
    )(*ins)


def _loss_head(x, g, target):
    s, d = x.shape
    bm = _tile(s, (256, 128, 8))

    def body(x_ref, g_ref, t_ref, loss_ref, dx_ref, dg_ref):
        @pl.when(pl.program_id(0) == 0)
        def _():
            dg_ref[...] = jnp.zeros_like(dg_ref)
            loss_ref[...] = jnp.zeros_like(loss_ref)

        xv = x_ref[...]
        gv = g_ref[...]
        r = lax.rsqrt(jnp.mean(xv * xv, axis=-1, keepdims=True) + NORM_EPS)
        xh = xv * r
        err = xh * gv - t_ref[...]
        part = 0.5 * jnp.sum(jnp.mean(err * err, axis=-1, keepdims=True), axis=0, keepdims=True)
        loss_ref[...] += jnp.broadcast_to(part, loss_ref.shape)
        dyv = err * (1.0 / d)
        dxh = dyv * gv
        dx_ref[...] = r * (dxh - xh * jnp.mean(dxh * xh, axis=-1, keepdims=True))
        dg_ref[...] += jnp.sum(dyv * xh, axis=0, keepdims=True)

    row = pl.BlockSpec((bm, d), lambda i: (i, 0))
    vec = pl.BlockSpec((1, d), lambda i: (0, 0))
    return pl.pallas_call(
        body, name="loss_head", grid=(s // bm,), in_specs=[row, vec, row],
        out_specs=[pl.BlockSpec((1, LANES), lambda i: (0, 0)), row, vec],
        out_shape=[jax.ShapeDtypeStruct((1, LANES), F32), jax.ShapeDtypeStruct((s, d), F32),
                   jax.ShapeDtypeStruct((1, d), F32)],
        compiler_params=_cp(("arbitrary",)),
    )(x, g, target)


def _colsum(x, name="colsum"):
    s, n = x.shape
    bm = _tile(s, (256, 128, 8))

    def body(x_ref, o_ref):
        @pl.when(pl.program_id(0) == 0)
        def _():
            o_ref[...] = jnp.zeros_like(o_ref)

        o_ref[...] += jnp.sum(x_ref[...].astype(F32), axis=0, keepdims=True)

    return pl.pallas_call(
        body, name=name, grid=(s // bm,), in_specs=[pl.BlockSpec((bm, n), lambda i: (i, 0))],
        out_specs=pl.BlockSpec((1, n), lambda i: (0, 0)), out_shape=jax.ShapeDtypeStruct((1, n), F32),
        compiler_params=_cp(("arbitrary",)),
    )(x)


def _swiglu_fwd(gu):
    s, f2 = gu.shape
    f = f2 // 2
    bm, bf = _tile(s, (256, 128, 8)), _tile(f, (512, 256, 128))

    def body(gu_ref, o_ref):
        for c in range(0, f, bf):
            gv = gu_ref[:, c:c + bf].astype(F32)
            o_ref[:, c:c + bf] = (gv / (1.0 + jnp.exp(-gv)) * gu_ref[:, f + c:f + c + bf].astype(F32)).astype(o_ref.dtype)

    return pl.pallas_call(
        body, name="swiglu_fwd", grid=(s // bm,), in_specs=[pl.BlockSpec((bm, f2), lambda i: (i, 0))],
        out_specs=pl.BlockSpec((bm, f), lambda i: (i, 0)),
        out_shape=jax.ShapeDtypeStruct((s, f), BF16), compiler_params=_cp(("parallel",)),
    )(gu)


def _swiglu_bwd(gu, da):
    s, f2 = gu.shape
    f = f2 // 2
    bm, bf = _tile(s, (128, 8)), _tile(f, (512, 256, 128))

    def body(gu_ref, da_ref, o_ref):
        for c in range(0, f, bf):
            gv = gu_ref[:, c:c + bf].astype(F32)
            dav = da_ref[:, c:c + bf].astype(F32)
            sig = 1.0 / (1.0 + jnp.exp(-gv))
            o_ref[:, c:c + bf] = (dav * gu_ref[:, f + c:f + c + bf].astype(F32)
                                  * (sig * (1.0 + gv * (1.0 - sig)))).astype(o_ref.dtype)
            o_ref[:, f + c:f + c + bf] = (dav * gv * sig).astype(o_ref.dtype)

    return pl.pallas_call(
        body, name="swiglu_bwd", grid=(s // bm,),
        in_specs=[pl.BlockSpec((bm, f2), lambda i: (i, 0)), pl.BlockSpec((bm, f), lambda i: (i, 0))],
        out_specs=pl.BlockSpec((bm, f2), lambda i: (i, 0)),
        out_shape=jax.ShapeDtypeStruct((s, f2), BF16), compiler_params=_cp(("parallel",)),
    )(gu, da)


def _rope_tables(positions, rot, sign=1.0):
    half = rot // 2
    inv_freq = jnp.power(F32(ROPE_THETA), -jnp.arange(half, dtype=F32) * 2.0 / rot)
    ang = positions.reshape(-1).astype(F32)[:, None] * inv_freq
    cos, sin = jnp.cos(ang), jnp.sin(ang) * sign
    s = ang.shape[0]
    c_tab = jnp.concatenate([cos, cos, jnp.ones((s, LANES - rot), F32)], axis=1)
    s_tab = jnp.concatenate([-sin, sin, jnp.zeros((s, LANES - rot), F32)], axis=1)
    return c_tab, s_tab


def _rope(x, c_tab, s_tab, nblk, half, name):
    s = x.shape[0]
    bm = _tile(s, (256, 128, 8))
    hb = _tile(nblk, (12, 8, 6, 4, 3, 2))

    def body(x_ref, c_ref, s_ref, o_ref):
        lane = lax.broadcasted_iota(jnp.int32, (bm, LANES), 1)
        for h in range(hb):
            sl = slice(h * LANES, (h + 1) * LANES)
            xv = x_ref[:, sl].astype(F32)
            sw = jnp.where(lane < half, pltpu.roll(xv, LANES - half, 1), pltpu.roll(xv, half, 1))
            o_ref[:, sl] = (xv * c_ref[...] + sw * s_ref[...]).astype(o_ref.dtype)

    blk = pl.BlockSpec((bm, hb * LANES), lambda i, j: (i, j))
    tab = pl.BlockSpec((bm, LANES), lambda i, j: (i, 0))
    return pl.pallas_call(
        body, name=name, grid=(s // bm, nblk // hb), in_specs=[blk, tab, tab], out_specs=blk,
        out_shape=jax.ShapeDtypeStruct((s, nblk * LANES), BF16), compiler_params=_cp(("parallel", "parallel")),
    )(x, c_tab, s_tab)


def _sb_terms(q, kb, scale):
    z = lax.dot_general(q, kb, NT, preferred_element_type=F32) * scale
    u = jnp.log(1.0 + jnp.exp(-jnp.abs(z)))
    return jnp.minimum(-z, 0.0) - u, jnp.minimum(z, 0.0) - u


def _sb_heads_per_step(n_heads):
    return 2 if n_heads % 2 == 0 else 1


def _sb_fwd(qkv, n_heads, ex=None):
    s = qkv.shape[0]
    tq = _tile(s, (256, 128))
    hp = _sb_heads_per_step(n_heads)
    w = hp * LANES
    ng = n_heads // hp
    scale = LANES ** -0.5
    jobs = ex.take(4 * 4 * n_heads * s * s * LANES * EXCHANGE_BYTES_PER_FLOP) if ex is not None else []
    nj = len(jobs)

    def body(*refs):
        q_ref, k_ref, v_ref = refs[:3]
        o_ref, lt_ref = refs[3 + nj:5 + nj]
        i = pl.program_id(1)
        if nj:
            job_refs = (refs[3:3 + nj], refs[5 + nj:5 + 2 * nj], refs[5 + 2 * nj:])
            gi = pl.program_id(0)
            pl.when((gi == 0) & (i == 0))(lambda: ex.start(jobs, *job_refs))
            pl.when((gi == ng - 1) & (i == s // tq - 1))(lambda: ex.wait(jobs, *job_refs))
        row = lax.broadcasted_iota(jnp.int32, (tq, tq), 0)
        col = lax.broadcasted_iota(jnp.int32, (tq, tq), 1)
        below = col < row
        later = (row > col).astype(BF16)
        qs = [q_ref[:, h * LANES:(h + 1) * LANES] for h in range(hp)]

        def block(h, k0, c, acc, diag):
            sl = slice(h * LANES, (h + 1) * LANES)
            lk, logsig = _sb_terms(qs[h], k_ref[pl.ds(k0, tq), sl], scale)
            if diag:
                lk = jnp.where(below, lk, 0.0)
            a = jnp.exp(logsig + jnp.dot(lk.astype(BF16), later, preferred_element_type=F32) + c)
            if diag:
                a = jnp.where(below, a, 0.0)
            acc = acc + jnp.dot(a.astype(BF16), v_ref[pl.ds(k0, tq), sl], preferred_element_type=F32)
            return c + jnp.sum(lk, axis=1, keepdims=True), acc

        d0 = pl.multiple_of(i * tq, tq)
        carry = []
        for h in range(hp):
            carry += list(block(h, d0, jnp.zeros((tq, 1), F32), jnp.zeros((tq, LANES), F32), True))

        def step(t, carry):
            k0 = pl.multiple_of((i - 1 - t) * tq, tq)
            out = []
            for h in range(hp):
                out += list(block(h, k0, carry[2 * h], carry[2 * h + 1], False))
            return tuple(out)

        carry = lax.fori_loop(0, i, step, tuple(carry))
        for h in range(hp):
            sl = slice(h * LANES, (h + 1) * LANES)
            o_ref[:, sl] = carry[2 * h + 1].astype(o_ref.dtype)
            lt_ref[:, sl] = jnp.broadcast_to(carry[2 * h], (tq, LANES))

    blk = pl.BlockSpec((tq, w), lambda g, i: (i, g))
    outs = pl.pallas_call(
        body, name="sb_fwd_carry" if nj else "sb_fwd", grid=(ng, s // tq),
        in_specs=[blk, pl.BlockSpec((s, w), lambda g, i: (0, ng + g)), pl.BlockSpec((s, w), lambda g, i: (0, 2 * ng + g))]
        + [ANY] * nj,
        out_specs=[blk, blk] + [ANY] * nj,
        out_shape=[jax.ShapeDtypeStruct((s, n_heads * LANES), BF16), jax.ShapeDtypeStruct((s, n_heads * LANES), F32)]
        + [j.recv_shape for j in jobs],
        scratch_shapes=_piece_sems(nj) if nj else [],
        compiler_params=_cp(("arbitrary", "arbitrary")),
    )(qkv, qkv, qkv, *[j.g for j in jobs])
    if nj:
        ex.landed(jobs, outs[2:])
    return outs[:2]


def _sb_bwd(qkv, do, ltot, n_heads, ex=None):
    s = qkv.shape[0]
    tq = _tile(s, (256, 128))
    hp = _sb_heads_per_step(n_heads)
    w = hp * LANES
    ng = n_heads // hp
    scale = LANES ** -0.5
    jobs = ex.take(4 * 9 * n_heads * s * s * LANES * EXCHANGE_BYTES_PER_FLOP) if ex is not None else []
    nj = len(jobs)

    def body(*refs):
        q_ref, k_ref, v_ref, do_ref, lt_ref = refs[:5]
        dq_ref, dk_ref, dv_ref = refs[5 + nj:8 + nj]
        i = pl.program_id(1)
        if nj:
            job_refs = (refs[5:5 + nj], refs[8 + nj:8 + 2 * nj], refs[8 + 2 * nj:])
            gi = pl.program_id(0)
            pl.when((gi == 0) & (i == 0))(lambda: ex.start(jobs, *job_refs))
            pl.when((gi == ng - 1) & (i == s // tq - 1))(lambda: ex.wait(jobs, *job_refs))

        @pl.when(i == 0)
        def _():
            dk_ref[...] = jnp.zeros_like(dk_ref)
            dv_ref[...] = jnp.zeros_like(dv_ref)

        row = lax.broadcasted_iota(jnp.int32, (tq, tq), 0)
        col = lax.broadcasted_iota(jnp.int32, (tq, tq), 1)
        below = col < row
        later = (row > col).astype(BF16)
        before = (row < col).astype(BF16)
        qs = [q_ref[:, h * LANES:(h + 1) * LANES] for h in range(hp)]
        dos = [do_ref[:, h * LANES:(h + 1) * LANES] for h in range(hp)]
        lts = [lt_ref[:, h * LANES:h * LANES + 1] for h in range(hp)]

        def block(h, k0, cpre, ce, dq, diag):
            sl = slice(h * LANES, (h + 1) * LANES)
            kb = k_ref[pl.ds(k0, tq), sl]
            vb = v_ref[pl.ds(k0, tq), sl]
            lk, logsig = _sb_terms(qs[h], kb, scale)
            if diag:
                lk = jnp.where(below, lk, 0.0)
            cnext = cpre + jnp.sum(lk, axis=1, keepdims=True)
            a = jnp.exp(logsig + (lts[h] - cnext) + jnp.dot(lk.astype(BF16), later, preferred_element_type=F32))
            if diag:
                a = jnp.where(below, a, 0.0)
            e = a * lax.dot_general(dos[h], vb, NT, preferred_element_type=F32)
            e_before = ce + jnp.dot(e.astype(BF16), before, preferred_element_type=F32)
            sig = jnp.exp(logsig)
            dz = (e - sig * (e + e_before)) * scale
            if diag:
                dz = jnp.where(below, dz, 0.0)
            dzb = dz.astype(BF16)
            dq = dq + jnp.dot(dzb, kb, preferred_element_type=F32)
            dk_ref[pl.ds(k0, tq), sl] += lax.dot_general(dzb, qs[h], TN, preferred_element_type=F32)
            dv_ref[pl.ds(k0, tq), sl] += lax.dot_general(a.astype(BF16), dos[h], TN, preferred_element_type=F32)
            return cnext, ce + jnp.sum(e, axis=1, keepdims=True), dq

        def step(j, carry):
            k0 = pl.multiple_of(j * tq, tq)
            out = []
            for h in range(hp):
                out += list(block(h, k0, *carry[3 * h:3 * h + 3], False))
            return tuple(out)

        z1 = jnp.zeros((tq, 1), F32)
        carry = lax.fori_loop(0, i, step, (z1, z1, jnp.zeros((tq, LANES), F32)) * hp)
        d0 = pl.multiple_of(i * tq, tq)
        for h in range(hp):
            _, _, dq = block(h, d0, *carry[3 * h:3 * h + 3], True)
            dq_ref[:, h * LANES:(h + 1) * LANES] = dq.astype(dq_ref.dtype)

    blk = pl.BlockSpec((tq, w), lambda g, i: (i, g))
    full = pl.BlockSpec((s, w), lambda g, i: (0, g))
    wt = n_heads * LANES
    outs = pl.pallas_call(
        body, name="sb_bwd_carry" if nj else "sb_bwd", grid=(ng, s // tq),
        in_specs=[blk, pl.BlockSpec((s, w), lambda g, i: (0, ng + g)), pl.BlockSpec((s, w), lambda g, i: (0, 2 * ng + g)),
                  blk, blk] + [ANY] * nj,
        out_specs=[blk, full, full] + [ANY] * nj,
        out_shape=[jax.ShapeDtypeStruct((s, wt), BF16), jax.ShapeDtypeStruct((s, wt), F32),
                   jax.ShapeDtypeStruct((s, wt), F32)] + [j.recv_shape for j in jobs],
        scratch_shapes=_piece_sems(nj) if nj else [],
        compiler_params=_cp(("arbitrary", "arbitrary")),
    )(qkv, qkv, qkv, do, ltot, *[j.g for j in jobs])
    if nj:
        ex.landed(jobs, outs[3:])
    return outs[:3]


def _band_masks(b, max_dist):
    qi = lax.broadcasted_iota(jnp.int32, (BAND, BAND), 0)
    kj = lax.broadcasted_iota(jnp.int32, (BAND, BAND), 1)
    dist = qi - kj
    return ((BAND + dist) <= max_dist) & (b > 0), (dist >= 0) & (dist <= max_dist)


def _band_fwd(qa, ka, va, *, n_cls, n_steps, hpb, group, q_blk, k_blk, v_blk, max_dist, scale, sinks, name):
    length = qa.shape[0]
    nb = length // BAND
    has_sink = sinks is not None
    qw, kw = hpb * LANES, (hpb // group) * LANES

    def body(*refs):
        q_ref, kp_ref, kc_ref, vp_ref, vc_ref = refs[:5]
        o_ref, lse_ref = refs[-2], refs[-1]
        mask_p, mask_c = _band_masks(pl.program_id(2), max_dist)
        for hh in range(hpb):
            qs = slice(hh * LANES, (hh + 1) * LANES)
            ks = slice((hh // group) * LANES, (hh // group + 1) * LANES)
            q = q_ref[:, qs].astype(BF16)
            s_p = lax.dot_general(q, kp_ref[:, ks].astype(BF16), NT, preferred_element_type=F32) * scale
            s_c = lax.dot_general(q, kc_ref[:, ks].astype(BF16), NT, preferred_element_type=F32) * scale
            s_p = jnp.where(mask_p, s_p, NEG)
            s_c = jnp.where(mask_c, s_c, NEG)
            m = jnp.maximum(jnp.max(s_p, axis=1, keepdims=True), jnp.max(s_c, axis=1, keepdims=True))
            l = jnp.sum(jnp.exp(s_p - m), axis=1, keepdims=True) + jnp.sum(jnp.exp(s_c - m), axis=1, keepdims=True)
            lse = m + jnp.log(l)
            if has_sink:
                sk = refs[5][:, hh * LANES:hh * LANES + 1]
                lse = jnp.maximum(lse, sk) + jnp.log(1.0 + jnp.exp(-jnp.abs(lse - sk)))
            p_p = jnp.exp(s_p - lse).astype(BF16)
            p_c = jnp.exp(s_c - lse).astype(BF16)
            o_ref[:, qs] = (jnp.dot(p_p, vp_ref[:, ks].astype(BF16), preferred_element_type=F32)
                            + jnp.dot(p_c, vc_ref[:, ks].astype(BF16), preferred_element_type=F32))
            lse_ref[:, qs] = jnp.broadcast_to(lse, (BAND, LANES))

    def prev(b):
        return jnp.maximum(b - 1, 0)

    specs = [pl.BlockSpec((BAND, qw), lambda n, st, b: (b, q_blk(n, st))),
             pl.BlockSpec((BAND, kw), lambda n, st, b: (prev(b), k_blk(n, st))),
             pl.BlockSpec((BAND, kw), lambda n, st, b: (b, k_blk(n, st))),
             pl.BlockSpec((BAND, kw), lambda n, st, b: (prev(b), v_blk(n, st))),
             pl.BlockSpec((BAND, kw), lambda n, st, b: (b, v_blk(n, st)))]
    ins = [qa, ka, ka, va, va]
    if has_sink:
        ins.append(sinks)
        specs.append(pl.BlockSpec((1, qw), lambda n, st, b: (0, st)))
    out = pl.BlockSpec((BAND, qw), lambda n, st, b: (b, n * n_steps + st))
    w = n_cls * n_steps * qw
    return pl.pallas_call(
        body, name=name, grid=(n_cls, n_steps, nb), in_specs=specs, out_specs=[out, out],
        out_shape=[jax.ShapeDtypeStruct((length, w), F32), jax.ShapeDtypeStruct((length, w), F32)],
        compiler_params=_cp(("parallel", "parallel", "parallel")),
    )(*ins)


def _band_bwd(qa, ka, va, o, do, lse, dlse, *, n_cls, n_steps, hpb, group, q_blk, k_blk, v_blk, max_dist, scale, sinks,
              name):
    length = qa.shape[0]
    nb = length // BAND
    has_sink, has_dlse = sinks is not None, dlse is not None
    qw, kw = hpb * LANES, (hpb // group) * LANES

    def body(*refs):
        q_ref, kp_ref, kc_ref, vp_ref, vc_ref, o_ref, do_ref, lse_ref = refs[:8]
        pos = 8
        dlse_ref = refs[pos] if has_dlse else None
        pos += has_dlse
        sink_ref = refs[pos] if has_sink else None
        pos += has_sink
        dq_ref, dkc_ref, dkp_ref, dvc_ref, dvp_ref = refs[pos:pos + 5]
        b = pl.program_id(2)
        mask_p, mask_c = _band_masks(b, max_dist)
        if has_sink:
            dsink_ref = refs[pos + 5]

            @pl.when(b == 0)
            def _():
                dsink_ref[...] = jnp.zeros_like(dsink_ref)

        for hh in range(hpb):
            qs = slice(hh * LANES, (hh + 1) * LANES)
            ks = slice((hh // group) * LANES, (hh // group + 1) * LANES)
            q = q_ref[:, qs].astype(BF16)
            kp, kc = kp_ref[:, ks].astype(BF16), kc_ref[:, ks].astype(BF16)
            vp, vc = vp_ref[:, ks].astype(BF16), vc_ref[:, ks].astype(BF16)
            dov = do_ref[:, qs].astype(F32)
            dob = dov.astype(BF16)
            lse_v = lse_ref[:, hh * LANES:hh * LANES + 1]
            s_p = lax.dot_general(q, kp, NT, preferred_element_type=F32) * scale
            s_c = lax.dot_general(q, kc, NT, preferred_element_type=F32) * scale
            p_p = jnp.where(mask_p, jnp.exp(jnp.where(mask_p, s_p, NEG) - lse_v), 0.0)
            p_c = jnp.where(mask_c, jnp.exp(jnp.where(mask_c, s_c, NEG) - lse_v), 0.0)
            delta = jnp.sum(dov * o_ref[:, qs], axis=1, keepdims=True)
            shift = -delta
            if has_dlse:
                shift = shift + dlse_ref[:, hh * LANES:hh * LANES + 1]
            dp_p = lax.dot_general(dob, vp, NT, preferred_element_type=F32)
            dp_c = lax.dot_general(dob, vc, NT, preferred_element_type=F32)
            ds_p = (p_p * (dp_p + shift) * scale).astype(BF16)
            ds_c = (p_c * (dp_c + shift) * scale).astype(BF16)
            dq_ref[:, qs] = (jnp.dot(ds_p, kp, preferred_element_type=F32)
                             + jnp.dot(ds_c, kc, preferred_element_type=F32))
            parts = (lax.dot_general(ds_c, q, TN, preferred_element_type=F32),
                     lax.dot_general(ds_p, q, TN, preferred_element_type=F32),
                     lax.dot_general(p_c.astype(BF16), dob, TN, preferred_element_type=F32),
                     lax.dot_general(p_p.astype(BF16), dob, TN, preferred_element_type=F32))
            for ref, part in zip((dkc_ref, dkp_ref, dvc_ref, dvp_ref), parts):
                if hh % group == 0:
                    ref[:, ks] = part
                else:
                    ref[:, ks] += part
            if has_sink:
                p_sink = jnp.exp(sink_ref[:, hh * LANES:hh * LANES + 1] - lse_v)
                dsink_ref[:, qs] += jnp.broadcast_to(jnp.sum(-p_sink * delta, axis=0, keepdims=True), (1, LANES))

    def prev(b):
        return jnp.maximum(b - 1, 0)

    per_q = pl.BlockSpec((BAND, qw), lambda n, st, b: (b, n * n_steps + st))
    per_k = pl.BlockSpec((BAND, kw), lambda n, st, b: (b, n * n_steps + st))
    specs = [pl.BlockSpec((BAND, qw), lambda n, st, b: (b, q_blk(n, st))),
             pl.BlockSpec((BAND, kw), lambda n, st, b: (prev(b), k_blk(n, st))),
             pl.BlockSpec((BAND, kw), lambda n, st, b: (b, k_blk(n, st))),
             pl.BlockSpec((BAND, kw), lambda n, st, b: (prev(b), v_blk(n, st))),
             pl.BlockSpec((BAND, kw), lambda n, st, b: (b, v_blk(n, st))),
             per_q, per_q, per_q]
    ins = [qa, ka, ka, va, va, o, do, lse]
    if has_dlse:
        ins.append(dlse)
        specs.append(per_q)
    out_specs = [per_q] + [per_k] * 4
    out_shape = ([jax.ShapeDtypeStruct((length, n_cls * n_steps * qw), F32)]
                 + [jax.ShapeDtypeStruct((length, n_cls * n_steps * kw), F32)] * 4)
    if has_sink:
        ins.append(sinks)
        specs.append(pl.BlockSpec((1, qw), lambda n, st, b: (0, st)))
        out_specs = out_specs + [pl.BlockSpec((1, qw), lambda n, st, b: (0, st))]
        out_shape = out_shape + [jax.ShapeDtypeStruct((1, n_steps * qw), F32)]
    return pl.pallas_call(
        body, name=name, grid=(n_cls, n_steps, nb), in_specs=specs, out_specs=out_specs, out_shape=out_shape,
        compiler_params=_cp(("parallel", "parallel", "arbitrary")),
    )(*ins)


def _band_fold(cur, prv, *, n_cls, n_heads, group, name):
    length = cur.shape[0]
    nb = length // BAND
    n_kv = n_heads // group

    def body(c_ref, p_ref, o_ref):
        b, gq = pl.program_id(2), pl.program_id(3)

        @pl.when(gq == 0)
        def _():
            o_ref[...] = jnp.zeros_like(o_ref)

        o_ref[...] += c_ref[...] + jnp.where(b + 1 < nb, p_ref[...], 0.0)

    blk = (BAND, LANES)
    return pl.pallas_call(
        body, name=name, grid=(n_cls, n_kv, nb, group),
        in_specs=[pl.BlockSpec(blk, lambda n, h, b, gq: (b, n * n_heads + h * group + gq)),
                  pl.BlockSpec(blk, lambda n, h, b, gq: (jnp.minimum(b + 1, nb - 1), n * n_heads + h * group + gq))],
        out_specs=pl.BlockSpec(blk, lambda n, h, b, gq: (b, n * n_kv + h)),
        out_shape=jax.ShapeDtypeStruct((length, n_cls * n_kv * LANES), F32),
        compiler_params=_cp(("parallel", "parallel", "parallel", "arbitrary")),
    )(cur, prv)


def _dil_mix_fwd(os_, lses):
    s, w = os_[0].shape
    bm = _tile(s, (256, 128, 8))

    def body(o0, o1, o2, l0, l1, l2, out_ref):
        ls = [l0[...], l1[...], l2[...]]
        m = jnp.maximum(jnp.maximum(ls[0], ls[1]), ls[2])
        es = [jnp.exp(v - m) for v in ls]
        inv = 1.0 / (es[0] + es[1] + es[2])
        for gi, o_ref in enumerate((o0, o1, o2)):
            out_ref[:, gi * w:(gi + 1) * w] = (o_ref[...] * (es[gi] * inv)).astype(out_ref.dtype)

    blk = pl.BlockSpec((bm, w), lambda i: (i, 0))
    return pl.pallas_call(
        body, name="dil_mix_fwd", grid=(s // bm,), in_specs=[blk] * 6,
        out_specs=pl.BlockSpec((bm, 3 * w), lambda i: (i, 0)),
        out_shape=jax.ShapeDtypeStruct((s, 3 * w), BF16), compiler_params=_cp(("parallel",)),
    )(*os_, *lses)


def _dil_mix_bwd(os_, lses, dmixed):
    s, w = os_[0].shape
    bm = _tile(s, (256, 128, 8))
    hg = w // LANES

    def body(o0, o1, o2, l0, l1, l2, dm_ref, do0, do1, do2, dl0, dl1, dl2):
        ls = [l0[...], l1[...], l2[...]]
        m = jnp.maximum(jnp.maximum(ls[0], ls[1]), ls[2])
        es = [jnp.exp(v - m) for v in ls]
        inv = 1.0 / (es[0] + es[1] + es[2])
        alphas = [e * inv for e in es]
        dalphas = []
        for gi, (o_ref, do_ref) in enumerate(((o0, do0), (o1, do1), (o2, do2))):
            dm = dm_ref[:, gi * w:(gi + 1) * w].astype(F32)
            do_ref[...] = (dm * alphas[gi]).astype(do_ref.dtype)
            prod = dm * o_ref[...]
            parts = [jnp.broadcast_to(jnp.sum(prod[:, j * LANES:(j + 1) * LANES], axis=1, keepdims=True), (bm, LANES))
                     for j in range(hg)]
            dalphas.append(jnp.concatenate(parts, axis=1) if hg > 1 else parts[0])
        mean = alphas[0] * dalphas[0] + alphas[1] * dalphas[1] + alphas[2] * dalphas[2]
        for gi, dl_ref in enumerate((dl0, dl1, dl2)):
            dl_ref[...] = alphas[gi] * (dalphas[gi] - mean)

    blk = pl.BlockSpec((bm, w), lambda i: (i, 0))
    return pl.pallas_call(
        body, name="dil_mix_bwd", grid=(s // bm,), in_specs=[blk] * 6 + [pl.BlockSpec((bm, 3 * w), lambda i: (i, 0))],
        out_specs=[blk] * 6,
        out_shape=[jax.ShapeDtypeStruct((s, w), BF16)] * 3 + [jax.ShapeDtypeStruct((s, w), F32)] * 3,
        compiler_params=_cp(("parallel",)),
    )(*os_, *lses, dmixed)


def _xattn_fwd(q, kv):
    s, w = q.shape
    mlen = kv.shape[0]
    tq = _tile(s, (512, 256, 128))
    scale = LANES ** -0.5

    def body(q_ref, k_ref, v_ref, o_ref):
        for h in range(XA_HEADS):
            sl = slice(h * LANES, (h + 1) * LANES)
            sc = lax.dot_general(q_ref[:, sl], k_ref[:, sl], NT, preferred_element_type=F32) * scale
            m = jnp.max(sc, axis=1, keepdims=True)
            e = jnp.exp(sc - m)
            p = e / jnp.sum(e, axis=1, keepdims=True)
            o_ref[:, sl] = jnp.dot(p.astype(BF16), v_ref[:, sl], preferred_element_type=F32).astype(o_ref.dtype)

    return pl.pallas_call(
        body, name="xattn_fwd", grid=(s // tq,),
        in_specs=[pl.BlockSpec((tq, w), lambda i: (i, 0)), pl.BlockSpec((mlen, w), lambda i: (0, 0)),
                  pl.BlockSpec((mlen, w), lambda i: (0, 1))],
        out_specs=pl.BlockSpec((tq, w), lambda i: (i, 0)),
        out_shape=jax.ShapeDtypeStruct((s, w), BF16), compiler_params=_cp(("parallel",)),
    )(q, kv, kv)


def _xattn_bwd(q, kv, do):
    s, w = q.shape
    mlen = kv.shape[0]
    tq = _tile(s, (512, 256, 128))
    scale = LANES ** -0.5

    def body(q_ref, k_ref, v_ref, do_ref, dq_ref, dk_ref, dv_ref):
        @pl.when(pl.program_id(0) == 0)
        def _():
            dk_ref[...] = jnp.zeros_like(dk_ref)
            dv_ref[...] = jnp.zeros_like(dv_ref)

        for h in range(XA_HEADS):
            sl = slice(h * LANES, (h + 1) * LANES)
            qh, kh, vh, doh = q_ref[:, sl], k_ref[:, sl], v_ref[:, sl], do_ref[:, sl]
            sc = lax.dot_general(qh, kh, NT, preferred_element_type=F32) * scale
            m = jnp.max(sc, axis=1, keepdims=True)
            e = jnp.exp(sc - m)
            p = e / jnp.sum(e, axis=1, keepdims=True)
            dp = lax.dot_general(doh, vh, NT, preferred_element_type=F32)
            ds = (p * (dp - jnp.sum(p * dp, axis=1, keepdims=True)) * scale).astype(BF16)
            dq_ref[:, sl] = jnp.dot(ds, kh, preferred_element_type=F32).astype(dq_ref.dtype)
            dk_ref[:, sl] += lax.dot_general(ds, qh, TN, preferred_element_type=F32)
            dv_ref[:, sl] += lax.dot_general(p.astype(BF16), doh, TN, preferred_element_type=F32)

    row = pl.BlockSpec((tq, w), lambda i: (i, 0))
    acc = pl.BlockSpec((mlen, w), lambda i: (0, 0))
    return pl.pallas_call(
        body, name="xattn_bwd", grid=(s // tq,),
        in_specs=[row, acc, pl.BlockSpec((mlen, w), lambda i: (0, 1)), row],
        out_specs=[row, acc, acc],
        out_shape=[jax.ShapeDtypeStruct((s, w), BF16), jax.ShapeDtypeStruct((mlen, w), F32),
                   jax.ShapeDtypeStruct((mlen, w), F32)],
        compiler_params=_cp(("arbitrary",)),
    )(q, kv, kv, do)


def _adamw(parts, w, m, v, name, ex=None):
    r, c = w.shape
    npc = len(parts)
    rp = r // npc
    row_bytes = c * (2 * N_DEV * npc * parts[0].dtype.itemsize + 2 * 7 * 4)
    br = _tile(rp, tuple(p for p in (256, 128, 64, 32, 16, 8) if p * c * 4 <= 1024 * 1024 and p * row_bytes <= MM_VMEM_BUDGET))
    steps = rp // br
    jobs = ex.take(r * c * ADAMW_EXCHANGE_BYTES_PER_PARAM) if ex is not None else []
    nj = len(jobs)

    def body(*refs):
        w_ref, m_ref, v_ref = refs[npc:npc + 3]
        g_ref, d_ref, nm_ref, nv_ref = refs[npc + 3 + nj:npc + 7 + nj]
        if nj:
            job_refs = (refs[npc + 3:npc + 3 + nj], refs[npc + 7 + nj:npc + 7 + 2 * nj], refs[npc + 7 + 2 * nj:])
            p_id, i_id = pl.program_id(0), pl.program_id(1)
            pl.when((p_id == 0) & (i_id == 0))(lambda: ex.start(jobs, *job_refs))
            pl.when((p_id == npc - 1) & (i_id == steps - 1))(lambda: ex.wait(jobs, *job_refs))

        def update(p_ref):
            g = p_ref[0].astype(F32)
            for t in range(1, N_DEV):
                g = g + p_ref[t].astype(F32)
            nm = ADAM_B1 * m_ref[...] + (1.0 - ADAM_B1) * g
            nv = ADAM_B2 * v_ref[...] + (1.0 - ADAM_B2) * (g * g)
            m_hat = nm / (1.0 - ADAM_B1 ** ADAM_STEP)
            v_hat = nv / (1.0 - ADAM_B2 ** ADAM_STEP)
            g_ref[...] = g
            d_ref[...] = -ADAM_LR * (m_hat / (jnp.sqrt(v_hat) + ADAM_EPS) + ADAM_WD * w_ref[...])
            nm_ref[...] = nm
            nv_ref[...] = nv

        for k in range(npc):
            pl.when(pl.program_id(0) == k)(lambda k=k: update(refs[k]))

    blk = pl.BlockSpec((br, c), lambda p, i: (p * steps + i, 0))
    part_specs = [pl.BlockSpec((N_DEV, br, c), lambda p, i, k=k: (0, jnp.where(p == k, i, 0), 0)) for k in range(npc)]
    outs = pl.pallas_call(
        body, name=name + "_carry" if nj else name, grid=(npc, steps),
        in_specs=part_specs + [blk, blk, blk] + [ANY] * nj, out_specs=[blk] * 4 + [ANY] * nj,
        out_shape=[jax.ShapeDtypeStruct((r, c), F32)] * 4 + [j.recv_shape for j in jobs],
        scratch_shapes=_piece_sems(nj) if nj else [],
        compiler_params=_cp(("arbitrary", "arbitrary") if nj else ("parallel", "parallel")),
    )(*parts, w, m, v, *[j.g for j in jobs])
    if nj:
        ex.landed(jobs, outs[4:])
    return outs[:4]


MESH_ID = pl.DeviceIdType.MESH
ANY = pl.BlockSpec(memory_space=pl.ANY)


def _my_place():
    return lax.axis_index("x"), lax.axis_index("y"), lax.axis_index("c")


def _all_gather(xs, name):
    nt = len(xs)

    def body(*refs):
        x_refs, out_refs = refs[:nt], refs[nt:2 * nt]
        send_sems, recv_sems, local_sems = refs[2 * nt:]
        x, y, c = _my_place()
        me, sibling = (x, y, c), (x, y, 1 - c)
        chips = [(1 - x, y), (x, 1 - y), (1 - x, 1 - y)]

        def slot(t, p):
            return out_refs[t].at[4 * p[0] + 2 * p[1] + p[2]]

        def copy(t, k, block, to, src=None):
            return pltpu.make_async_remote_copy(
                src_ref=slot(t, block) if src is None else src, dst_ref=slot(t, block),
                send_sem=send_sems.at[7 * t + k], recv_sem=recv_sems.at[7 * t + k], device_id=to,
                device_id_type=MESH_ID)

        mine, first, passed = [], [], []
        for t in range(nt):
            cp = pltpu.make_async_copy(x_refs[t], slot(t, me), local_sems.at[t])
            cp.start()
            mine.append(cp)
            group = [copy(t, 0, me, sibling, src=x_refs[t])]
            group += [copy(t, 1 + j, me, (*chip, c), src=x_refs[t]) for j, chip in enumerate(chips)]
            for cp in group:
                cp.start()
            first += group
        for t in range(nt):
            for j, chip in enumerate(chips):
                copy(t, 1 + j, (*chip, c), me).wait_recv()
                fw = copy(t, 4 + j, (*chip, c), sibling)
                fw.start()
                passed.append(fw)
        for t in range(nt):
            copy(t, 0, sibling, me).wait_recv()
            for j, chip in enumerate(chips):
                copy(t, 4 + j, (*chip, 1 - c), me).wait_recv()
        for cp in first + passed:
            cp.wait_send()
        for cp in mine:
            cp.wait()

    return pl.pallas_call(
        body, name=name, in_specs=[ANY] * nt, out_specs=[ANY] * nt,
        out_shape=[jax.ShapeDtypeStruct((N_DEV,) + tuple(v.shape), v.dtype) for v in xs],
        scratch_shapes=[pltpu.SemaphoreType.DMA((7 * nt,)), pltpu.SemaphoreType.DMA((7 * nt,)),
                        pltpu.SemaphoreType.DMA((nt,))],
    )(*xs)


def _exchange(gs, name):
    nt = len(gs)

    def body(*refs):
        g_refs, out_refs = refs[:nt], refs[nt:2 * nt]
        send_sems, recv_sems, local_sems = refs[2 * nt:]
        x, y, c = _my_place()
        my_slot = 4 * x + 2 * y + c
        mine, sent = [], []
        for t in range(nt):
            cp = pltpu.make_async_copy(g_refs[t].at[my_slot], out_refs[t].at[my_slot], local_sems.at[t])
            cp.start()
            mine.append(cp)
            for rel in range(1, N_DEV):
                px, py, pc = x ^ ((rel >> 2) & 1), y ^ ((rel >> 1) & 1), c ^ (rel & 1)
                cp = pltpu.make_async_remote_copy(
                    src_ref=g_refs[t].at[4 * px + 2 * py + pc], dst_ref=out_refs[t].at[my_slot],
                    send_sem=send_sems.at[7 * t + rel - 1], recv_sem=recv_sems.at[7 * t + rel - 1],
                    device_id=(px, py, pc), device_id_type=MESH_ID)
                cp.start()
                sent.append(cp)
        for cp in sent:
            cp.wait_recv()
        for cp in sent:
            cp.wait_send()
        for cp in mine:
            cp.wait()

    return pl.pallas_call(
        body, name=name, in_specs=[ANY] * nt, out_specs=[ANY] * nt,
        out_shape=[jax.ShapeDtypeStruct(tuple(v.shape), v.dtype) for v in gs],
        scratch_shapes=[pltpu.SemaphoreType.DMA((7 * nt,)), pltpu.SemaphoreType.DMA((7 * nt,)),
                        pltpu.SemaphoreType.DMA((nt,))],
    )(*gs)


EXCHANGE_BYTES_PER_FLOP = 1.1e-4
CARRIER_OVERFILL = 1.15
ADAMW_EXCHANGE_BYTES_PER_PARAM = 1.2
PIECE_BYTES = 8 * 1024 * 1024
CARRIER_MIN_BYTES = 6 * 1024 * 1024
ROW_ALIGN = 16


class _Piece:
    def __init__(self, key, g, axis, lo, hi):
        self.key, self.g, self.axis, self.lo, self.hi = key, g, axis, lo, hi
        cols = g.shape[1] if axis == 0 else g.shape[1] // N_DEV
        self.recv_shape = jax.ShapeDtypeStruct((N_DEV, hi - lo, cols), g.dtype)
        self.nbytes = N_DEV * (hi - lo) * cols * g.dtype.itemsize


def _piece_sems(nj):
    return [pltpu.SemaphoreType.DMA((7 * nj,)), pltpu.SemaphoreType.DMA((7 * nj,)), pltpu.SemaphoreType.DMA((nj,))]


def _piece_copies(jobs, g_refs, recv_refs, sems):
    send_sems, recv_sems, local_sems = sems
    x, y, c = _my_place()
    me = 4 * x + 2 * y + c
    local, remote = [], []
    for t, (job, g, r) in enumerate(zip(jobs, g_refs, recv_refs)):
        rows = job.hi - job.lo

        def block(slot, job=job, g=g, r=r, rows=rows):
            if job.axis == 0:
                start = pl.multiple_of(slot * (g.shape[0] // N_DEV) + job.lo, ROW_ALIGN)
                return g.at[pl.ds(start, rows), :]
            cols = r.shape[2]
            return g.at[pl.ds(job.lo, rows), pl.ds(pl.multiple_of(slot * cols, LANES), cols)]

        local.append(pltpu.make_async_copy(block(me), r.at[me], local_sems.at[t]))
        for rel in range(1, N_DEV):
            px, py, pc = x ^ ((rel >> 2) & 1), y ^ ((rel >> 1) & 1), c ^ (rel & 1)
            remote.append(pltpu.make_async_remote_copy(
                src_ref=block(4 * px + 2 * py + pc), dst_ref=r.at[me], send_sem=send_sems.at[7 * t + rel - 1],
                recv_sem=recv_sems.at[7 * t + rel - 1], device_id=(px, py, pc), device_id_type=MESH_ID))
    return local, remote


def _pieces_start(jobs, g_refs, recv_refs, sems):
    local, remote = _piece_copies(jobs, g_refs, recv_refs, sems)
    for cp in local + remote:
        cp.start()


def _pieces_wait(jobs, g_refs, recv_refs, sems):
    local, remote = _piece_copies(jobs, g_refs, recv_refs, sems)
    for cp in remote:
        cp.wait_recv()
    for cp in remote:
        cp.wait_send()
    for cp in local:
        cp.wait()


def _exchange_pieces(jobs, name):
    nj = len(jobs)

    def body(*refs):
        job_refs = (refs[:nj], refs[nj:2 * nj], refs[2 * nj:])
        _pieces_start(jobs, *job_refs)
        _pieces_wait(jobs, *job_refs)

    return pl.pallas_call(
        body, name=name, in_specs=[ANY] * nj, out_specs=[ANY] * nj, out_shape=[j.recv_shape for j in jobs],
        scratch_shapes=_piece_sems(nj),
    )(*[j.g for j in jobs])


class _GradExchange:
    def __init__(self):
        self.queue, self.recv = [], {}

    def put(self, name, layer, g):
        axis = SHARD_AXIS[name] - 1
        rows = g.shape[0] // N_DEV if axis == 0 else g.shape[0]
        if g.dtype != BF16 or rows % ROW_ALIGN or (axis == 1 and (g.shape[1] // N_DEV) % LANES):
            return False
        n_split = max(1, round(g.size * g.dtype.itemsize / PIECE_BYTES))
        while rows % (n_split * ROW_ALIGN):
            n_split -= 1
        for k in range(n_split):
            self.queue.append(_Piece((name, layer, k), g, axis, k * rows // n_split, (k + 1) * rows // n_split))
        return True

    def take(self, capacity):
        jobs, used = [], 0
        while capacity >= CARRIER_MIN_BYTES and self.queue and used + self.queue[0].nbytes <= CARRIER_OVERFILL * capacity:
            used += self.queue[0].nbytes
            jobs.append(self.queue.pop(0))
        return jobs

    def landed(self, jobs, recvs):
        for j, r in zip(jobs, recvs):
            self.recv[j.key] = r

    def flush(self):
        if self.queue:
            jobs, self.queue = self.queue, []
            self.landed(jobs, _exchange_pieces(jobs, "exchange_rest"))

    def pieces_of(self, name):
        return [self.recv[k] for k in sorted(k for k in self.recv if k[0] == name)]

    start = staticmethod(_pieces_start)
    wait = staticmethod(_pieces_wait)


GATHER_SPEEDUP = 2.0


class _WeightPiece:
    def __init__(self, key, local, axis):
        self.key, self.g, self.axis = key, local, axis
        r, c = local.shape
        self.recv_shape = jax.ShapeDtypeStruct((r * N_DEV, c) if axis == 0 else (r, c * N_DEV), local.dtype)
        self.nbytes = N_DEV * r * c * local.dtype.itemsize


def _gather_copies(jobs, x_refs, full_refs, sems):
    send_sems, recv_sems, local_sems = sems
    x, y, c = _my_place()
    me, sibling = (x, y, c), (x, y, 1 - c)
    chips = [(1 - x, y), (x, 1 - y), (1 - x, 1 - y)]
    plans = []
    for t, (job, xr, fr) in enumerate(zip(jobs, x_refs, full_refs)):
        def blk(p, job=job, xr=xr, fr=fr):
            slot = 4 * p[0] + 2 * p[1] + p[2]
            if job.axis == 0:
                return fr.at[pl.ds(pl.multiple_of(slot * xr.shape[0], ROW_ALIGN), xr.shape[0]), :]
            return fr.at[:, pl.ds(pl.multiple_of(slot * xr.shape[1], LANES), xr.shape[1])]

        def copy(k, block, to, src=None, t=t, blk=blk):
            return pltpu.make_async_remote_copy(
                src_ref=blk(block) if src is None else src, dst_ref=blk(block), send_sem=send_sems.at[7 * t + k],
                recv_sem=recv_sems.at[7 * t + k], device_id=to, device_id_type=MESH_ID)

        plans.append(dict(
            mine=pltpu.make_async_copy(xr, blk(me), local_sems.at[t]),
            first=[copy(0, me, sibling, src=xr)] + [copy(1 + j, me, (*chip, c), src=xr) for j, chip in enumerate(chips)],
            landed=[copy(1 + j, (*chip, c), me) for j, chip in enumerate(chips)],
            passed=[copy(4 + j, (*chip, c), sibling) for j, chip in enumerate(chips)],
            from_sibling=[copy(0, sibling, me)] + [copy(4 + j, (*chip, 1 - c), me) for j, chip in enumerate(chips)]))
    return plans


def _gather_start(jobs, x_refs, full_refs, sems):
    for plan in _gather_copies(jobs, x_refs, full_refs, sems):
        plan["mine"].start()
        for cp in plan["first"]:
            cp.start()


def _gather_wait(jobs, x_refs, full_refs, sems):
    plans = _gather_copies(jobs, x_refs, full_refs, sems)
    for plan in plans:
        for landed, passed in zip(plan["landed"], plan["passed"]):
            landed.wait_recv()
            passed.start()
    for plan in plans:
        for cp in plan["from_sibling"]:
            cp.wait_recv()
        for cp in plan["first"] + plan["passed"]:
            cp.wait_send()
        plan["mine"].wait()


def _gather_pieces(jobs, name):
    nj = len(jobs)

    def body(*refs):
        job_refs = (refs[:nj], refs[nj:2 * nj], refs[2 * nj:])
        _gather_start(jobs, *job_refs)
        _gather_wait(jobs, *job_refs)

    return pl.pallas_call(
        body, name=name, in_specs=[ANY] * nj, out_specs=[ANY] * nj, out_shape=[j.recv_shape for j in jobs],
        scratch_shapes=_piece_sems(nj),
    )(*[j.g for j in jobs])


class _WeightGather:
    def __init__(self):
        self.queue, self.full = [], {}

    def add(self, name, layer, local):
        self.queue.append(_WeightPiece((name, layer), local, SHARD_AXIS[name] - 1))

    def take(self, capacity):
        capacity *= GATHER_SPEEDUP
        jobs, used = [], 0
        while capacity >= CARRIER_MIN_BYTES and self.queue and used + self.queue[0].nbytes <= CARRIER_OVERFILL * capacity:
            used += self.queue[0].nbytes
            jobs.append(self.queue.pop(0))
        return jobs

    def landed(self, jobs, fulls):
        for j, f in zip(jobs, fulls):
            self.full[j.key] = f

    def get(self, name, layer):
        if (name, layer) not in self.full:
            at = [j.key for j in self.queue].index((name, layer))
            jobs, self.queue = self.queue[:at + 1], self.queue[at + 1:]
            self.landed(jobs, _gather_pieces(jobs, "gather_now"))
        return self.full[(name, layer)]

    start = staticmethod(_gather_start)
    wait = staticmethod(_gather_wait)


def _val(w):
    return w() if callable(w) else w


def _ffn_fwd(x, g, w_gu, w_d, tag, wg=None):
    n = _rms_fwd(x, g)
    gu = _mm(n, _val(w_gu), out_dtype=BF16, name=f"{tag}_gu", ex=wg)
    a = _swiglu_fwd(gu)
    return _mm(a, _val(w_d), alpha=0.5, res=x, name=f"{tag}_down", ex=wg), (n, gu, a)


def _put(ex, name, layer, g):
    if ex is not None:
        ex.put(name, layer, g)


def _ffn_bwd(dy, x, g, w_gu, w_d, saved, tag, ex=None, which="ffn1", layer=0):
    n, gu, a = saved
    da = _mm(dy, w_d, tb=True, alpha=0.5, out_dtype=BF16, name=f"{tag}_da", ex=ex)
    d_wd = _mm(a, dy, ta=True, alpha=0.5, out_dtype=BF16, name=f"{tag}_dwd", ex=ex)
    _put(ex, f"{which}_w_down", layer, d_wd)
    dgu = _swiglu_bwd(gu, da)
    dn = _mm(dgu, w_gu, tb=True, name=f"{tag}_dn", ex=ex)
    d_wgu = _mm(n, dgu, ta=True, out_dtype=BF16, name=f"{tag}_dwgu", ex=ex)
    _put(ex, f"{which}_w_gate_up", layer, d_wgu)
    dx, dg = _rms_bwd(x, g, dn, dy)
    return dx, dg, d_wgu, d_wd


def _sb_mixer_fwd(h, w_qkv, w_o, x, wg=None):
    qkv = _mm(h, _val(w_qkv), out_dtype=BF16, name="sb_qkv", ex=wg)
    o, ltot = _sb_fwd(qkv, qkv.shape[1] // (3 * LANES), wg)
    return _mm(o, _val(w_o), res=x, name="sb_out", ex=wg), (qkv, o, ltot)


def _sb_mixer_bwd(dy, h, w_qkv, w_o, saved, ex=None, layer=0):
    qkv, o, ltot = saved
    n_heads = w_o.shape[0] // LANES
    do = _mm(dy, w_o, tb=True, out_dtype=BF16, name="sb_do", ex=ex)
    d_wo = _mm(o, dy, ta=True, out_dtype=BF16, name="sb_dwo", ex=ex)
    _put(ex, "sb_w_o", layer, d_wo)
    dq, dk, dv = _sb_bwd(qkv, do, ltot, n_heads, ex)
    dqkv = jnp.concatenate([dq, dk.astype(BF16), dv.astype(BF16)], axis=1)
    dh = _mm(dqkv, w_qkv, tb=True, name="sb_dh", ex=ex)
    d_wqkv = _mm(h, dqkv, ta=True, out_dtype=BF16, name="sb_dwqkv", ex=ex)
    _put(ex, "sb_w_qkv", layer, d_wqkv)
    return dh, d_wqkv, d_wo


def _dil_cols(gi):
    ng = len(DIL_PATTERNS)
    return dict(q_blk=lambda n, st: n * 2 * ng + gi, k_blk=lambda n, st: n * 2 * ng + ng + gi,
                v_blk=lambda n, st: n * 3 * ng + 2 * ng + gi)


def _dil_mixer_fwd(h, w_qkv, w_o, x, tabs, wg=None):
    s = h.shape[0]
    qkv = _mm(h, _val(w_qkv), name="dil_qkv", ex=wg)
    n_all = qkv.shape[1] // (3 * LANES)
    hg = n_all // len(DIL_PATTERNS)
    qk = _rope(qkv, tabs[0], tabs[1], 2 * n_all, 16, "dil_rope")
    os_, lses = [], []
    for gi, (window, dil) in enumerate(DIL_PATTERNS):
        o, lse = _band_fwd(qk.reshape(s // dil, -1), qk.reshape(s // dil, -1), qkv.reshape(s // dil, -1),
                           n_cls=dil, n_steps=1, hpb=hg, group=1, **_dil_cols(gi),
                           max_dist=window // dil, scale=LANES ** -0.5, sinks=None, name=f"dil_fwd{gi}")
        os_.append(o.reshape(s, hg * LANES))
        lses.append(lse.reshape(s, hg * LANES))
    mixed = _dil_mix_fwd(os_, lses)
    return _mm(mixed, _val(w_o), res=x, name="dil_out", ex=wg), (qkv, qk, os_, lses, mixed)


def _dil_mixer_bwd(dy, h, w_qkv, w_o, saved, tabs_bwd, ex=None):
    qkv, qk, os_, lses, mixed = saved
    s = h.shape[0]
    n_all = w_o.shape[0] // LANES
    hg = n_all // len(DIL_PATTERNS)
    dmixed = _mm(dy, w_o, tb=True, out_dtype=BF16, name="dil_dmix", ex=ex)
    d_wo = _mm(mixed, dy, ta=True, out_dtype=BF16, name="dil_dwo", ex=ex)
    _put(ex, "dil_w_o", 0, d_wo)
    mix_out = _dil_mix_bwd(os_, lses, dmixed)
    dos, dlses = mix_out[:3], mix_out[3:]
    dqs, dks, dvs = [], [], []
    for gi, (window, dil) in enumerate(DIL_PATTERNS):
        length = s // dil
        dq, dkc, dkp, dvc, dvp = _band_bwd(
            qk.reshape(length, -1), qk.reshape(length, -1), qkv.reshape(length, -1), os_[gi].reshape(length, -1),
            dos[gi].reshape(length, -1), lses[gi].reshape(length, -1), dlses[gi].reshape(length, -1),
            n_cls=dil, n_steps=1, hpb=hg, group=1, **_dil_cols(gi), max_dist=window // dil,
            scale=LANES ** -0.5, sinks=None, name=f"dil_bwd{gi}")
        dqs.append(dq.reshape(s, -1))
        dks.append(_band_fold(dkc, dkp, n_cls=dil, n_heads=hg, group=1, name=f"dil_foldk{gi}").reshape(s, -1))
        dvs.append(_band_fold(dvc, dvp, n_cls=dil, n_heads=hg, group=1, name=f"dil_foldv{gi}").reshape(s, -1))
    dqk_rot = jnp.concatenate(dqs + dks, axis=1)
    dqk = _rope(dqk_rot, tabs_bwd[0], tabs_bwd[1], 2 * n_all, 16, "dil_rope_bwd")
    dqkv = jnp.concatenate([dqk] + [t.astype(BF16) for t in dvs], axis=1)
    dh = _mm(dqkv, w_qkv, tb=True, name="dil_dh", ex=ex)
    d_wqkv = _mm(h, dqkv, ta=True, out_dtype=BF16, name="dil_dwqkv", ex=ex)
    return dh, d_wqkv, d_wo


def _pad_heads(w, axis):
    shape = list(w.shape)
    n = shape[axis] // SWA_HEAD_DIM
    w = w.reshape(shape[:axis] + [n, SWA_HEAD_DIM] + shape[axis + 1:])
    pad = [(0, 0)] * w.ndim
    pad[axis + 1] = (0, LANES - SWA_HEAD_DIM)
    shape[axis] = n * LANES
    return jnp.pad(w, pad).reshape(shape)


def _unpad_heads(w, axis):
    shape = list(w.shape)
    n = shape[axis] // LANES
    w = w.reshape(shape[:axis] + [n, LANES] + shape[axis + 1:])
    w = lax.slice_in_dim(w, 0, SWA_HEAD_DIM, axis=axis + 1)
    shape[axis] = n * SWA_HEAD_DIM
    return w.reshape(shape)


def _swa_mixer_fwd(h, w_qkv_p, b_qkv_p, sinks_b, w_o_p, b_o, x, tabs, wg=None):
    nq = w_o_p.shape[0] // LANES
    nkv = nq // SWA_GROUP
    qkv = _mm(h, w_qkv_p, bias=b_qkv_p, name="swa_qkv", ex=wg)
    qk = _rope(qkv, tabs[0], tabs[1], nq + nkv, 8, "swa_rope")
    o, lse = _band_fwd(qk, qk, qkv, n_cls=1, n_steps=nkv, hpb=SWA_GROUP, group=SWA_GROUP, q_blk=lambda n, st: st,
                       k_blk=lambda n, st: nq + st, v_blk=lambda n, st: nq + nkv + st,
                       max_dist=SWA_WINDOW - 1, scale=SWA_HEAD_DIM ** -0.5, sinks=sinks_b, name="swa_fwd")
    return _mm(o, w_o_p, res=x, bias=b_o, name="swa_out"), (qkv, qk, o, lse)


def _swa_mixer_bwd(dy, h, w_qkv_p, sinks_b, w_o_p, saved, tabs_bwd, ex=None):
    qkv, qk, o, lse = saved
    nq = w_o_p.shape[0] // LANES
    nkv = nq // SWA_GROUP
    do = _mm(dy, w_o_p, tb=True, name="swa_do", ex=ex)
    d_wo_p = _mm(o, dy, ta=True, out_dtype=BF16, name="swa_dwo", ex=ex)
    d_bo = _colsum(dy, "swa_dbo")
    dq, dkc, dkp, dvc, dvp, dsink = _band_bwd(
        qk, qk, qkv, o, do, lse, None, n_cls=1, n_steps=nkv, hpb=SWA_GROUP, group=SWA_GROUP, q_blk=lambda n, st: st,
        k_blk=lambda n, st: nq + st, v_blk=lambda n, st: nq + nkv + st, max_dist=SWA_WINDOW - 1,
        scale=SWA_HEAD_DIM ** -0.5, sinks=sinks_b, name="swa_bwd")
    dk = _band_fold(dkc, dkp, n_cls=1, n_heads=nkv, group=1, name="swa_foldk")
    dv = _band_fold(dvc, dvp, n_cls=1, n_heads=nkv, group=1, name="swa_foldv")
    dqk = _rope(jnp.concatenate([dq, dk], axis=1), tabs_bwd[0], tabs_bwd[1], nq + nkv, 8, "swa_rope_bwd")
    dqkv = jnp.concatenate([dqk, dv.astype(BF16)], axis=1)
    d_bqkv_p = _colsum(dqkv, "swa_dbqkv")
    dh = _mm(dqkv, w_qkv_p, tb=True, name="swa_dh", ex=ex)
    d_wqkv_p = _mm(h, dqkv, ta=True, out_dtype=BF16, name="swa_dwqkv", ex=ex)
    return dh, d_wqkv_p, d_bqkv_p, dsink, d_wo_p, d_bo


def _xattn_layer_fwd(x, mem, g_x, g_m, w_q, w_kv, w_o):
    hq = _rms_fwd(x, g_x, "rms_fwd")
    hm = _rms_fwd(mem, g_m, "rms_mem_fwd")
    q = _mm(hq, _val(w_q), out_dtype=BF16, name="xa_q")
    kv = _mm(hm, _val(w_kv), out_dtype=BF16, name="xa_kv")
    o = _xattn_fwd(q, kv)
    return _mm(o, _val(w_o), res=x, name="xa_out"), (hq, hm, q, kv, o)


def _xattn_layer_bwd(dy, x, mem, g_x, g_m, w_q, w_kv, w_o, saved):
    hq, hm, q, kv, o = saved
    do = _mm(dy, w_o, tb=True, out_dtype=BF16, name="xa_do")
    d_wo = _mm(o, dy, ta=True, out_dtype=BF16, name="xa_dwo")
    dq, dk, dv = _xattn_bwd(q, kv, do)
    dkv = jnp.concatenate([dk, dv], axis=1).astype(BF16)
    dhq = _mm(dq, w_q, tb=True, name="xa_dhq")
    d_wq = _mm(hq, dq, ta=True, out_dtype=BF16, name="xa_dwq")
    dhm = _mm(dkv, w_kv, tb=True, name="xa_dhm")
    d_wkv = _mm(hm, dkv, ta=True, out_dtype=BF16, name="xa_dwkv")
    dx, dg_x = _rms_bwd(x, g_x, dhq, dy)
    _, dg_m = _rms_bwd(mem, g_m, dhm, None, "rms_mem_bwd")
    return dx, dg_x, dg_m, d_wq, d_wkv, d_wo


def _to_full(gathered, axis):
    t = jnp.moveaxis(gathered, 0, axis)
    shape = list(t.shape)
    return t.reshape(shape[:axis] + [shape[axis] * shape[axis + 1]] + shape[axis + 2:])


def _to_blocks(full, axis):
    shape = list(full.shape)
    t = full.reshape(shape[:axis] + [N_DEV, shape[axis] // N_DEV] + shape[axis + 1:])
    return jnp.moveaxis(t, axis, 0)


SHARD_AXIS = {
    "ffn1_w_gate_up": 2, "ffn1_w_down": 1, "sb_w_qkv": 2, "sb_w_o": 1, "dil_w_qkv": 2, "dil_w_o": 2,
    "swa_w_qkv": 2, "swa_b_qkv": 1, "swa_w_o": 1, "swa_b_o": 1, "xattn_w_q": 1, "xattn_w_kv": 1, "xattn_w_o": 2,
    "ffn2_w_gate_up": 2, "ffn2_w_down": 1,
}
SMALL = ("ffn1_norm", "mix_norm", "xattn_norm", "mem_norm", "ffn2_norm", "final_norm", "swa_sinks")
WEIGHTS = ("ffn1_norm", "ffn1_w_gate_up", "ffn1_w_down", "mix_norm", "sb_w_qkv", "sb_w_o", "dil_w_qkv", "dil_w_o",
           "swa_w_qkv", "swa_b_qkv", "swa_sinks", "swa_w_o", "swa_b_o", "xattn_norm", "mem_norm", "xattn_w_q",
           "xattn_w_kv", "xattn_w_o", "ffn2_norm", "ffn2_w_gate_up", "ffn2_w_down", "final_norm")


def _flat2(a):
    return a.reshape(-1, a.shape[-1])


def _pack_small(vals, d):
    rows = [vals[n].reshape(-1, d) for n in SMALL[:5]] + [vals["final_norm"].reshape(1, d)]
    sk = vals["swa_sinks"].reshape(1, -1)
    rows.append(jnp.pad(sk, ((0, 0), (0, d - sk.shape[1]))))
    rows.append(jnp.zeros((2, d), F32))
    return jnp.concatenate(rows, axis=0)


def _unpack_small(packed, like):
    out, r = {}, 0
    for n in SMALL[:5]:
        k = like[n].shape[0]
        out[n] = packed[r:r + k]
        r += k
    out["final_norm"] = packed[r]
    out["swa_sinks"] = packed[r + 1:r + 2, :like["swa_sinks"].shape[1]]
    return out


def _local_step(x0, mem0, positions, target, full, norm, ex=None, wg=None):
    d = x0.shape[1]
    names = list(SHARD_AXIS)
    sinks_b = jnp.repeat(norm["swa_sinks"], LANES, axis=1)
    tabs_dil, tabs_dil_bwd = _rope_tables(positions, 32), _rope_tables(positions, 32, -1.0)
    tabs_swa, tabs_swa_bwd = _rope_tables(positions, 16), _rope_tables(positions, 16, -1.0)

    def vec(name, i):
        return norm[name][i:i + 1]

    saved = []
    xc = x0
    for i in range(DEPTH):
        kind, j = i % 3, i // 3
        rec = {"x0": xc}
        xc, rec["ffn1"] = _ffn_fwd(xc, vec("ffn1_norm", i), lambda: full["ffn1_w_gate_up"][i],
                                   lambda: full["ffn1_w_down"][i], "ffn", wg)
        rec["x1"] = xc
        h = _rms_fwd(xc, vec("mix_norm", i))
        rec["h"] = h
        if kind == 0:
            xc, rec["mix"] = _sb_mixer_fwd(h, lambda: full["sb_w_qkv"][j], lambda: full["sb_w_o"][j], xc, wg)
        elif kind == 1:
            xc, rec["mix"] = _dil_mixer_fwd(h, lambda: full["dil_w_qkv"][j], lambda: full["dil_w_o"][j], xc, tabs_dil, wg)
        else:
            swa_w_qkv_p = _pad_heads(full["swa_w_qkv"][0], 1)
            swa_b_qkv_p = _pad_heads(full["swa_b_qkv"][0][None], 1)
            swa_w_o_p = _pad_heads(full["swa_w_o"][0], 0)
            xc, rec["mix"] = _swa_mixer_fwd(h, swa_w_qkv_p, swa_b_qkv_p, sinks_b, swa_w_o_p, full["swa_b_o"][0][None],
                                            xc, tabs_swa, wg)
        rec["x2"] = xc
        xc, rec["xa"] = _xattn_layer_fwd(xc, mem0, vec("xattn_norm", i), vec("mem_norm", i),
                                         lambda: full["xattn_w_q"][i], lambda: full["xattn_w_kv"][i],
                                         lambda: full["xattn_w_o"][i])
        rec["x3"] = xc
        xc, rec["ffn2"] = _ffn_fwd(xc, vec("ffn2_norm", i), lambda: full["ffn2_w_gate_up"][i],
                                   lambda: full["ffn2_w_down"][i], "ffn", wg)
        saved.append(rec)

    loss_part, dx, dg_final = _loss_head(xc, norm["final_norm"].reshape(1, d), target)

    gfull = {n: [None] * full[n].shape[0] for n in names}
    gsmall = {n: [None] * DEPTH for n in SMALL[:5]}
    gsmall["final_norm"] = dg_final
    gsmall["swa_sinks"] = jnp.zeros_like(norm["swa_sinks"])
    for i in reversed(range(DEPTH)):
        kind, j = i % 3, i // 3
        rec = saved[i]
        dx, gsmall["ffn2_norm"][i], gfull["ffn2_w_gate_up"][i], gfull["ffn2_w_down"][i] = _ffn_bwd(
            dx, rec["x3"], vec("ffn2_norm", i), full["ffn2_w_gate_up"][i], full["ffn2_w_down"][i], rec["ffn2"], "ffn",
            ex, "ffn2", i)
        (dx, gsmall["xattn_norm"][i], gsmall["mem_norm"][i], gfull["xattn_w_q"][i], gfull["xattn_w_kv"][i],
         gfull["xattn_w_o"][i]) = _xattn_layer_bwd(dx, rec["x2"], mem0, vec("xattn_norm", i), vec("mem_norm", i),
                                                   full["xattn_w_q"][i], full["xattn_w_kv"][i], full["xattn_w_o"][i],
                                                   rec["xa"])
        for n in ("xattn_w_q", "xattn_w_kv", "xattn_w_o"):
            _put(ex, n, i, gfull[n][i])
        if kind == 0:
            dh, gfull["sb_w_qkv"][j], gfull["sb_w_o"][j] = _sb_mixer_bwd(
                dx, rec["h"], full["sb_w_qkv"][j], full["sb_w_o"][j], rec["mix"], ex, j)
        elif kind == 1:
            dh, gfull["dil_w_qkv"][j], gfull["dil_w_o"][j] = _dil_mixer_bwd(
                dx, rec["h"], full["dil_w_qkv"][j], full["dil_w_o"][j], rec["mix"], tabs_dil_bwd, ex)
        else:
            dh, d_wqkv_p, d_bqkv_p, dsink, d_wo_p, d_bo = _swa_mixer_bwd(
                dx, rec["h"], swa_w_qkv_p, sinks_b, swa_w_o_p, rec["mix"], tabs_swa_bwd, ex)
            gfull["swa_w_qkv"][j] = _unpad_heads(d_wqkv_p, 1)
            gfull["swa_b_qkv"][j] = _unpad_heads(d_bqkv_p, 1)[0]
            gfull["swa_w_o"][j] = _unpad_heads(d_wo_p, 0)
            gfull["swa_b_o"][j] = d_bo[0]
            gsmall["swa_sinks"] = dsink.reshape(1, -1, LANES)[:, :, 0]
            _put(ex, "swa_w_o", j, gfull["swa_w_o"][j])
        dx, gsmall["mix_norm"][i] = _rms_bwd(rec["x1"], vec("mix_norm", i), dh, dx)
        dx, gsmall["ffn1_norm"][i], gfull["ffn1_w_gate_up"][i], gfull["ffn1_w_down"][i] = _ffn_bwd(
            dx, rec["x0"], vec("ffn1_norm", i), full["ffn1_w_gate_up"][i], full["ffn1_w_down"][i], rec["ffn1"], "ffn",
            ex, "ffn1", i)
    for n in SMALL[:5]:
        gsmall[n] = jnp.concatenate(gsmall[n], axis=0)
    return loss_part[0, 0], dx, gfull, gsmall


def _train_step(x, mem, positions, loss_target, w, m, v):
    d = x.shape[2]
    names = list(SHARD_AXIS)
    norm = {n: w[n] for n in SMALL}

    def carried(n):
        if w[n].ndim != 3:
            return False
        return w[n].shape[2] % LANES == 0 if SHARD_AXIS[n] == 2 else w[n].shape[1] % ROW_ALIGN == 0

    first = [n for n in names if not carried(n)]
    gathered = _all_gather([_flat2(w[n].astype(BF16) if w[n].ndim == 3 else w[n]) for n in first], "gather_weights")
    stacked = {n: _to_full(g.reshape((N_DEV,) + w[n].shape), SHARD_AXIS[n]) for n, g in zip(first, gathered)}
    wg = _WeightGather()
    for i in range(DEPTH):
        mixer = (("sb_w_qkv", "sb_w_o"), ("dil_w_qkv", "dil_w_o"), ("swa_w_qkv", "swa_w_o"))[i % 3]
        for n in ("ffn1_w_gate_up", "ffn1_w_down") + mixer + ("xattn_w_q", "xattn_w_kv", "xattn_w_o",
                                                               "ffn2_w_gate_up", "ffn2_w_down"):
            layer = i // 3 if n in mixer else i
            if carried(n):
                wg.add(n, layer, w[n][layer].astype(BF16))

    class Layers:
        def __init__(self, n):
            self.n, self.shape = n, w[n].shape[:1]

        def __getitem__(self, layer):
            return wg.get(self.n, layer) if carried(self.n) else stacked[self.n][layer]

    full = {n: Layers(n) for n in names}
    ex = _GradExchange()
    loss_part, dx, gfull, gsmall = _local_step(x[0], mem[0], positions, loss_target[0], full, norm, ex, wg)
    loss = lax.psum(loss_part, MESH_AXES)
    grad_x = dx[None]

    grad, delta, new_m, new_v = {}, {}, {}, {}

    def update(n, parts, carrier):
        outs = _adamw(parts, _flat2(w[n]), _flat2(m[n]), _flat2(v[n]), "adamw", carrier)
        grad[n], delta[n], new_m[n], new_v[n] = (o.reshape(w[n].shape) for o in outs)

    taken = {k[0] for k in ex.recv} | {p.key[0] for p in ex.queue}
    late = {p.key[0] for p in ex.queue}
    for n in names:
        if n in taken and n not in late:
            update(n, ex.pieces_of(n), ex)
    ex.flush()
    for n in names:
        if n in late:
            update(n, ex.pieces_of(n), None)
    rest = [n for n in names if n not in taken]
    blocks = [_to_blocks(jnp.stack(gfull[n], axis=0), SHARD_AXIS[n]) for n in rest]
    blocks = [b.reshape(N_DEV, -1, b.shape[-1]) for b in blocks]
    for n, received in zip(rest, _exchange(blocks, "exchange_grads")):
        update(n, [received], None)

    small_parts = _all_gather([_pack_small(gsmall, d)], "gather_small_grads")[0]
    outs = _adamw([small_parts], _pack_small(norm, d), _pack_small({n: m[n] for n in SMALL}, d),
                  _pack_small({n: v[n] for n in SMALL}, d), "adamw_small")
    for res, o in zip((grad, delta, new_m, new_v), outs):
        res.update(_unpack_small(o, norm))
    return loss, grad_x, grad, delta, new_m, new_v


def kernel(x, mem, positions, ffn1_norm, ffn1_w_gate_up, ffn1_w_down, mix_norm, sb_w_qkv, sb_w_o, dil_w_qkv, dil_w_o, swa_w_qkv, swa_b_qkv, swa_sinks, swa_w_o, swa_b_o, xattn_norm, mem_norm, xattn_w_q, xattn_w_kv, xattn_w_o, ffn2_norm, ffn2_w_gate_up, ffn2_w_down, final_norm, loss_target, m_ffn1_norm, m_ffn1_w_gate_up, m_ffn1_w_down, m_mix_norm, m_sb_w_qkv, m_sb_w_o, m_dil_w_qkv, m_dil_w_o, m_swa_w_qkv, m_swa_b_qkv, m_swa_sinks, m_swa_w_o, m_swa_b_o, m_xattn_norm, m_mem_norm, m_xattn_w_q, m_xattn_w_kv, m_xattn_w_o, m_ffn2_norm, m_ffn2_w_gate_up, m_ffn2_w_down, m_final_norm, v_ffn1_norm, v_ffn1_w_gate_up, v_ffn1_w_down, v_mix_norm, v_sb_w_qkv, v_sb_w_o, v_dil_w_qkv, v_dil_w_o, v_swa_w_qkv, v_swa_b_qkv, v_swa_sinks, v_swa_w_o, v_swa_b_o, v_xattn_norm, v_mem_norm, v_xattn_w_q, v_xattn_w_kv, v_xattn_w_o, v_ffn2_norm, v_ffn2_w_gate_up, v_ffn2_w_down, v_final_norm):
    args = dict(locals())
    w = {n: args[n] for n in WEIGHTS}
    m = {n: args["m_" + n] for n in WEIGHTS}
    v = {n: args["v_" + n] for n in WEIGHTS}
    loss, grad_x, grad, delta, new_m, new_v = _train_step(x, mem, positions, loss_target, w, m, v)
    return (loss, grad_x, *[grad[n] for n in WEIGHTS], *[delta[n] for n in WEIGHTS],
            *[new_m[n] for n in WEIGHTS], *[new_v[n] for n in WEIGHTS])
```

```python
import jax
import jax.numpy as jnp
from jax import lax
from jax.experimental import pallas as pl
from jax.experimental.pallas import tpu as pltpu

F32 = jnp.float32
BF16 = jnp.bfloat16

N_DEV = 8
MESH_AXES = ("x", "y", "c")
LANES = 128
BAND = 128
NORM_EPS = 1e-6
ROPE_THETA = 500000.0
DIL_PATTERNS = ((128, 1), (512, 4), (2048, 16))
SWA_HEAD_DIM = 64
SWA_GROUP = 8
SWA_WINDOW = 128
XA_HEADS = 4
DEPTH = 4
ADAM_LR, ADAM_B1, ADAM_B2, ADAM_EPS, ADAM_WD, ADAM_STEP = 0.001, 0.9, 0.999, 1e-08, 0.01, 10
VMEM_LIMIT = 56 * 1024 * 1024
NEG = -1e30

NT = (((1,), (1,)), ((), ()))
TN = (((0,), (0,)), ((), ()))


def _tile(n, prefs):
    for p in prefs:
        if n % p == 0:
            return p
    return n


def _cp(sem):
    return pltpu.CompilerParams(dimension_semantics=sem, vmem_limit_bytes=VMEM_LIMIT)


MM_TILE_SIZES = (2816, 2048, 1408, 1024, 512, 256, 128)
MM_VMEM_BUDGET = 40 * 1024 * 1024
MM_STEP_BYTES = 1.2e6


def _mm_tiles(m, n, k, ea, eb, eo, has_res):
    best = None
    for bm in [c for c in MM_TILE_SIZES if m % c == 0] or [m]:
        for bn in [c for c in MM_TILE_SIZES if n % c == 0] or [n]:
            for bk in [c for c in MM_TILE_SIZES if k % c == 0] or [k]:
                vmem = 2 * (bm * bk * ea + bk * bn * eb) + bm * bn * 4 + 2 * bm * bn * (eo + (4 if has_res else 0))
                vmem += (bm * bk * 2 if ea == 4 else 0) + (bk * bn * 2 if eb == 4 else 0)
                if vmem > MM_VMEM_BUDGET:
                    continue
                ni, nj, nk = m // bm, n // bn, k // bk
                traffic = (m * k * ea * (nj if nk > 1 else 1) + k * n * eb * (ni if nk > 1 or nj > 1 else 1)
                           + m * n * (eo + (4 if has_res else 0)) + ni * nj * nk * MM_STEP_BYTES)
                if best is None or traffic < best[0]:
                    best = (traffic, (bm, bn, bk))
    return best[1]
def _mm(a, b, *, ta=False, tb=False, out_dtype=F32, alpha=1.0, res=None, bias=None, name, ex=None):
    kdim, m = a.shape if ta else a.shape[::-1]
    kdim2, n = b.shape[::-1] if tb else b.shape
    assert kdim == kdim2, (a.shape, b.shape, ta, tb)
    bm, bn, bk = _mm_tiles(m, n, kdim, a.dtype.itemsize, b.dtype.itemsize, jnp.dtype(out_dtype).itemsize, res is not None)
    nk = kdim // bk
    grid = (m // bm, n // bn, nk)
    dn = (((0 if ta else 1,), (1 if tb else 0,)), ((), ()))
    has_res, has_bias = res is not None, bias is not None
    jobs = ex.take(2.0 * m * n * kdim * EXCHANGE_BYTES_PER_FLOP) if ex is not None else []
    nj = len(jobs)
    n_in = 2 + has_res + has_bias

    def body(*refs):
        a_ref, b_ref = refs[0], refs[1]
        res_ref = refs[2] if has_res else None
        bias_ref = refs[2 + has_res] if has_bias else None
        o_ref, acc_ref = refs[n_in + nj], refs[n_in + 2 * nj + 1]
        k = pl.program_id(2)
        if nj:
            job_refs = (refs[n_in:n_in + nj], refs[n_in + nj + 1:n_in + 2 * nj + 1], refs[n_in + 2 * nj + 2:])
            ids = [pl.program_id(t) for t in range(3)]

            @pl.when((ids[0] == 0) & (ids[1] == 0) & (ids[2] == 0))
            def _():
                ex.start(jobs, *job_refs)

            @pl.when((ids[0] == grid[0] - 1) & (ids[1] == grid[1] - 1) & (ids[2] == grid[2] - 1))
            def _():
                ex.wait(jobs, *job_refs)

        @pl.when(k == 0)
        def _():
            acc_ref[...] = jnp.zeros_like(acc_ref)

        acc_ref[...] += lax.dot_general(a_ref[...].astype(BF16), b_ref[...].astype(BF16), dn,
                                        preferred_element_type=F32)

        @pl.when(k == nk - 1)
        def _():
            r = acc_ref[...]
            if alpha != 1.0:
                r = r * alpha
            if has_bias:
                r = r + bias_ref[...]
            if has_res:
                r = r + res_ref[...]
            o_ref[...] = r.astype(o_ref.dtype)

    a_spec = pl.BlockSpec((bk, bm), lambda i, j, k: (k, i)) if ta else pl.BlockSpec((bm, bk), lambda i, j, k: (i, k))
    b_spec = pl.BlockSpec((bn, bk), lambda i, j, k: (j, k)) if tb else pl.BlockSpec((bk, bn), lambda i, j, k: (k, j))
    ins, specs = [a, b], [a_spec, b_spec]
    if has_res:
        ins.append(res)
        specs.append(pl.BlockSpec((bm, bn), lambda i, j, k: (i, j)))
    if has_bias:
        ins.append(bias)
        specs.append(pl.BlockSpec((1, bn), lambda i, j, k: (0, j)))
    out_spec = pl.BlockSpec((bm, bn), lambda i, j, k: (i, j))
    out_shape = jax.ShapeDtypeStruct((m, n), out_dtype)
    scratch = [pltpu.VMEM((bm, bn), F32)]
    if not nj:
        return pl.pallas_call(
            body, name=name, grid=grid, in_specs=specs, out_specs=out_spec, out_shape=out_shape,
            scratch_shapes=scratch, compiler_params=_cp(("parallel", "parallel", "arbitrary")),
        )(*ins)
    outs = pl.pallas_call(
        body, name=name + "_carry", grid=grid, in_specs=specs + [ANY] * nj, out_specs=[out_spec] + [ANY] * nj,
        out_shape=[out_shape] + [j.recv_shape for j in jobs], scratch_shapes=scratch + _piece_sems(nj),
        compiler_params=_cp(("arbitrary", "arbitrary", "arbitrary")),
    )(*ins, *[j.g for j in jobs])
    ex.landed(jobs, outs[1:])
    return outs[0]


def _rms_fwd(x, g, name="rms_fwd"):
    s, d = x.shape
    bm = _tile(s, (256, 128, 8))

    def body(x_ref, g_ref, o_ref):
        xv = x_ref[...]
        r = lax.rsqrt(jnp.mean(xv * xv, axis=-1, keepdims=True) + NORM_EPS)
        o_ref[...] = (xv * r * g_ref[...]).astype(o_ref.dtype)

    return pl.pallas_call(
        body, name=name, grid=(s // bm,),
        in_specs=[pl.BlockSpec((bm, d), lambda i: (i, 0)), pl.BlockSpec((1, d), lambda i: (0, 0))],
        out_specs=pl.BlockSpec((bm, d), lambda i: (i, 0)),
        out_shape=jax.ShapeDtypeStruct((s, d), BF16), compiler_params=_cp(("parallel",)),
    )(x, g)


def _rms_bwd(x, g, dn, dy=None, name="rms_bwd"):
    s, d = x.shape
    bm = _tile(s, (256, 128, 8))
    has_dy = dy is not None

    def body(*refs):
        x_ref, g_ref, dn_ref = refs[:3]
        dy_ref = refs[3] if has_dy else None
        dx_ref, dg_ref = refs[-2], refs[-1]

        @pl.when(pl.program_id(0) == 0)
        def _():
            dg_ref[...] = jnp.zeros_like(dg_ref)

        xv = x_ref[...]
        r = lax.rsqrt(jnp.mean(xv * xv, axis=-1, keepdims=True) + NORM_EPS)
        xh = xv * r
        dnv = dn_ref[...].astype(F32)
        dxh = dnv * g_ref[...]
        dx = r * (dxh - xh * jnp.mean(dxh * xh, axis=-1, keepdims=True))
        if has_dy:
            dx = dx + dy_ref[...]
        dx_ref[...] = dx
        dg_ref[...] += jnp.sum(dnv * xh, axis=0, keepdims=True)

    row = pl.BlockSpec((bm, d), lambda i: (i, 0))
    vec = pl.BlockSpec((1, d), lambda i: (0, 0))
    ins, specs = [x, g, dn], [row, vec, row]
    if has_dy:
        ins.append(dy)
        specs.append(row)
    return pl.pallas_call(
        body, name=name, grid=(s // bm,), in_specs=specs, out_specs=[row, vec],
        out_shape=[jax.ShapeDtypeStruct((s, d), F32), jax.ShapeDtypeStruct((1, d), F32)],
        compiler_params=_cp(("arbitrary",)),
    )(*ins)


def _loss_head(x, g, target):
    s, d = x.shape
    bm = _tile(s, (256, 128, 8))

    def body(x_ref, g_ref, t_ref, loss_ref, dx_ref, dg_ref):
        @pl.when(pl.program_id(0) == 0)
        def _():
            dg_ref[...] = jnp.zeros_like(dg_ref)
            loss_ref[...] = jnp.zeros_like(loss_ref)

        xv = x_ref[...]
        gv = g_ref[...]
        r = lax.rsqrt(jnp.mean(xv * xv, axis=-1, keepdims=True) + NORM_EPS)
        xh = xv * r
        err = xh * gv - t_ref[...]
        part = 0.5 * jnp.sum(jnp.mean(err * err, axis=-1, keepdims=True), axis=0, keepdims=True)
        loss_ref[...] += jnp.broadcast_to(part, loss_ref.shape)
        dyv = err * (1.0 / d)
        dxh = dyv * gv
        dx_ref[...] = r * (dxh - xh * jnp.mean(dxh * xh, axis=-1, keepdims=True))
        dg_ref[...] += jnp.sum(dyv * xh, axis=0, keepdims=True)

    row = pl.BlockSpec((bm, d), lambda i: (i, 0))
    vec = pl.BlockSpec((1, d), lambda i: (0, 0))
    return pl.pallas_call(
        body, name="loss_head", grid=(s // bm,), in_specs=[row, vec, row],
        out_specs=[pl.BlockSpec((1, LANES), lambda i: (0, 0)), row, vec],
        out_shape=[jax.ShapeDtypeStruct((1, LANES), F32), jax.ShapeDtypeStruct((s, d), F32),
                   jax.ShapeDtypeStruct((1, d), F32)],
        compiler_params=_cp(("arbitrary",)),
    )(x, g, target)


def _colsum(x, name="colsum"):
    s, n = x.shape
    bm = _tile(s, (256, 128, 8))

    def body(x_ref, o_ref):
        @pl.when(pl.program_id(0) == 0)
        def _():
            o_ref[...] = jnp.zeros_like(o_ref)

        o_ref[...] += jnp.sum(x_ref[...].astype(F32), axis=0, keepdims=True)

    return pl.pallas_call(
        body, name=name, grid=(s // bm,), in_specs=[pl.BlockSpec((bm, n), lambda i: (i, 0))],
        out_specs=pl.BlockSpec((1, n), lambda i: (0, 0)), out_shape=jax.ShapeDtypeStruct((1, n), F32),
        compiler_params=_cp(("arbitrary",)),
    )(x)


def _swiglu_fwd(gu):
    s, f2 = gu.shape
    f = f2 // 2
    bm, bf = _tile(s, (256, 128, 8)), _tile(f, (512, 256, 128))

    def body(gu_ref, o_ref):
        for c in range(0, f, bf):
            gv = gu_ref[:, c:c + bf].astype(F32)
            o_ref[:, c:c + bf] = (gv / (1.0 + jnp.exp(-gv)) * gu_ref[:, f + c:f + c + bf].astype(F32)).astype(o_ref.dtype)

    return pl.pallas_call(
        body, name="swiglu_fwd", grid=(s // bm,), in_specs=[pl.BlockSpec((bm, f2), lambda i: (i, 0))],
        out_specs=pl.BlockSpec((bm, f), lambda i: (i, 0)),
        out_shape=jax.ShapeDtypeStruct((s, f), BF16), compiler_params=_cp(("parallel",)),
    )(gu)


def _swiglu_bwd(gu, da):
    s, f2 = gu.shape
    f = f2 // 2
    bm, bf = _tile(s, (128, 8)), _tile(f, (512, 256, 128))

    def body(gu_ref, da_ref, o_ref):
        for c in range(0, f, bf):
            gv = gu_ref[:, c:c + bf].astype(F32)
            dav = da_ref[:, c:c + bf].astype(F32)
            sig = 1.0 / (1.0 + jnp.exp(-gv))
            o_ref[:, c:c + bf] = (dav * gu_ref[:, f + c:f + c + bf].astype(F32)
                                  * (sig * (1.0 + gv * (1.0 - sig)))).astype(o_ref.dtype)
            o_ref[:, f + c:f + c + bf] = (dav * gv * sig).astype(o_ref.dtype)

    return pl.pallas_call(
        body, name="swiglu_bwd", grid=(s // bm,),
        in_specs=[pl.BlockSpec((bm, f2), lambda i: (i, 0)), pl.BlockSpec((bm, f), lambda i: (i, 0))],
        out_specs=pl.BlockSpec((bm, f2), lambda i: (i, 0)),
        out_shape=jax.ShapeDtypeStruct((s, f2), BF16), compiler_params=_cp(("parallel",)),
    )(gu, da)


def _rope_tables(positions, rot, sign=1.0):
    half = rot // 2
    inv_freq = jnp.power(F32(ROPE_THETA), -jnp.arange(half, dtype=F32) * 2.0 / rot)
    ang = positions.reshape(-1).astype(F32)[:, None] * inv_freq
    cos, sin = jnp.cos(ang), jnp.sin(ang) * sign
    s = ang.shape[0]
    c_tab = jnp.concatenate([cos, cos, jnp.ones((s, LANES - rot), F32)], axis=1)
    s_tab = jnp.concatenate([-sin, sin, jnp.zeros((s, LANES - rot), F32)], axis=1)
    return c_tab, s_tab


def _rope(x, c_tab, s_tab, nblk, half, name):
    s = x.shape[0]
    bm = _tile(s, (256, 128, 8))
    hb = _tile(nblk, (12, 8, 6, 4, 3, 2))

    def body(x_ref, c_ref, s_ref, o_ref):
        lane = lax.broadcasted_iota(jnp.int32, (bm, LANES), 1)
        for h in range(hb):
            sl = slice(h * LANES, (h + 1) * LANES)
            xv = x_ref[:, sl].astype(F32)
            sw = jnp.where(lane < half, pltpu.roll(xv, LANES - half, 1), pltpu.roll(xv, half, 1))
            o_ref[:, sl] = (xv * c_ref[...] + sw * s_ref[...]).astype(o_ref.dtype)

    blk = pl.BlockSpec((bm, hb * LANES), lambda i, j: (i, j))
    tab = pl.BlockSpec((bm, LANES), lambda i, j: (i, 0))
    return pl.pallas_call(
        body, name=name, grid=(s // bm, nblk // hb), in_specs=[blk, tab, tab], out_specs=blk,
        out_shape=jax.ShapeDtypeStruct((s, nblk * LANES), BF16), compiler_params=_cp(("parallel", "parallel")),
    )(x, c_tab, s_tab)


def _sb_terms(q, kb, scale):
    z = lax.dot_general(q, kb, NT, preferred_element_type=F32) * scale
    u = jnp.log(1.0 + jnp.exp(-jnp.abs(z)))
    return jnp.minimum(-z, 0.0) - u, jnp.minimum(z, 0.0) - u


def _sb_heads_per_step(n_heads):
    return 2 if n_heads % 2 == 0 else 1


def _sb_fwd(qkv, n_heads, ex=None):
    s = qkv.shape[0]
    tq = _tile(s, (256, 128))
    hp = _sb_heads_per_step(n_heads)
    w = hp * LANES
    ng = n_heads // hp
    scale = LANES ** -0.5
    jobs = ex.take(4 * 4 * n_heads * s * s * LANES * EXCHANGE_BYTES_PER_FLOP) if ex is not None else []
    nj = len(jobs)

    def body(*refs):
        q_ref, k_ref, v_ref = refs[:3]
        o_ref, lt_ref = refs[3 + nj:5 + nj]
        i = pl.program_id(1)
        if nj:
            job_refs = (refs[3:3 + nj], refs[5 + nj:5 + 2 * nj], refs[5 + 2 * nj:])
            gi = pl.program_id(0)
            pl.when((gi == 0) & (i == 0))(lambda: ex.start(jobs, *job_refs))
            pl.when((gi == ng - 1) & (i == s // tq - 1))(lambda: ex.wait(jobs, *job_refs))
        row = lax.broadcasted_iota(jnp.int32, (tq, tq), 0)
        col = lax.broadcasted_iota(jnp.int32, (tq, tq), 1)
        below = col < row
        later = (row > col).astype(BF16)
        qs = [q_ref[:, h * LANES:(h + 1) * LANES] for h in range(hp)]

        def block(h, k0, c, acc, diag):
            sl = slice(h * LANES, (h + 1) * LANES)
            lk, logsig = _sb_terms(qs[h], k_ref[pl.ds(k0, tq), sl], scale)
            if diag:
                lk = jnp.where(below, lk, 0.0)
            a = jnp.exp(logsig + jnp.dot(lk.astype(BF16), later, preferred_element_type=F32) + c)
            if diag:
                a = jnp.where(below, a, 0.0)
            acc = acc + jnp.dot(a.astype(BF16), v_ref[pl.ds(k0, tq), sl], preferred_element_type=F32)
            return c + jnp.sum(lk, axis=1, keepdims=True), acc

        d0 = pl.multiple_of(i * tq, tq)
        carry = []
        for h in range(hp):
            carry += list(block(h, d0, jnp.zeros((tq, 1), F32), jnp.zeros((tq, LANES), F32), True))

        def step(t, carry):
            k0 = pl.multiple_of((i - 1 - t) * tq, tq)
            out = []
            for h in range(hp):
                out += list(block(h, k0, carry[2 * h], carry[2 * h + 1], False))
            return tuple(out)

        carry = lax.fori_loop(0, i, step, tuple(carry))
        for h in range(hp):
            sl = slice(h * LANES, (h + 1) * LANES)
            o_ref[:, sl] = carry[2 * h + 1].astype(o_ref.dtype)
            lt_ref[:, sl] = jnp.broadcast_to(carry[2 * h], (tq, LANES))

    blk = pl.BlockSpec((tq, w), lambda g, i: (i, g))
    outs = pl.pallas_call(
        body, name="sb_fwd_carry" if nj else "sb_fwd", grid=(ng, s // tq),
        in_specs=[blk, pl.BlockSpec((s, w), lambda g, i: (0, ng + g)), pl.BlockSpec((s, w), lambda g, i: (0, 2 * ng + g))]
        + [ANY] * nj,
        out_specs=[blk, blk] + [ANY] * nj,
        out_shape=[jax.ShapeDtypeStruct((s, n_heads * LANES), BF16), jax.ShapeDtypeStruct((s, n_heads * LANES), F32)]
        + [j.recv_shape for j in jobs],
        scratch_shapes=_piece_sems(nj) if nj else [],
        compiler_params=_cp(("arbitrary", "arbitrary")),
    )(qkv, qkv, qkv, *[j.g for j in jobs])
    if nj:
        ex.landed(jobs, outs[2:])
    return outs[:2]


def _sb_bwd(qkv, do, ltot, n_heads, ex=None):
    s = qkv.shape[0]
    tq = _tile(s, (256, 128))
    hp = _sb_heads_per_step(n_heads)
    w = hp * LANES
    ng = n_heads // hp
    scale = LANES ** -0.5
    jobs = ex.take(4 * 9 * n_heads * s * s * LANES * EXCHANGE_BYTES_PER_FLOP) if ex is not None else []
    nj = len(jobs)

    def body(*refs):
        q_ref, k_ref, v_ref, do_ref, lt_ref = refs[:5]
        dq_ref, dk_ref, dv_ref = refs[5 + nj:8 + nj]
        i = pl.program_id(1)
        if nj:
            job_refs = (refs[5:5 + nj], refs[8 + nj:8 + 2 * nj], refs[8 + 2 * nj:])
            gi = pl.program_id(0)
            pl.when((gi == 0) & (i == 0))(lambda: ex.start(jobs, *job_refs))
            pl.when((gi == ng - 1) & (i == s // tq - 1))(lambda: ex.wait(jobs, *job_refs))

        @pl.when(i == 0)
        def _():
            dk_ref[...] = jnp.zeros_like(dk_ref)
            dv_ref[...] = jnp.zeros_like(dv_ref)

        row = lax.broadcasted_iota(jnp.int32, (tq, tq), 0)
        col = lax.broadcasted_iota(jnp.int32, (tq, tq), 1)
        below = col < row
        later = (row > col).astype(BF16)
        before = (row < col).astype(BF16)
        qs = [q_ref[:, h * LANES:(h + 1) * LANES] for h in range(hp)]
        dos = [do_ref[:, h * LANES:(h + 1) * LANES] for h in range(hp)]
        lts = [lt_ref[:, h * LANES:h * LANES + 1] for h in range(hp)]

        def block(h, k0, cpre, ce, dq, diag):
            sl = slice(h * LANES, (h + 1) * LANES)
            kb = k_ref[pl.ds(k0, tq), sl]
            vb = v_ref[pl.ds(k0, tq), sl]
            lk, logsig = _sb_terms(qs[h], kb, scale)
            if diag:
                lk = jnp.where(below, lk, 0.0)
            cnext = cpre + jnp.sum(lk, axis=1, keepdims=True)
            a = jnp.exp(logsig + (lts[h] - cnext) + jnp.dot(lk.astype(BF16), later, preferred_element_type=F32))
            if diag:
                a = jnp.where(below, a, 0.0)
            e = a * lax.dot_general(dos[h], vb, NT, preferred_element_type=F32)
            e_before = ce + jnp.dot(e.astype(BF16), before, preferred_element_type=F32)
            sig = jnp.exp(logsig)
            dz = (e - sig * (e + e_before)) * scale
            if diag:
                dz = jnp.where(below, dz, 0.0)
            dzb = dz.astype(BF16)
            dq = dq + jnp.dot(dzb, kb, preferred_element_type=F32)
            dk_ref[pl.ds(k0, tq), sl] += lax.dot_general(dzb, qs[h], TN, preferred_element_type=F32)
            dv_ref[pl.ds(k0, tq), sl] += lax.dot_general(a.astype(BF16), dos[h], TN, preferred_element_type=F32)
            return cnext, ce + jnp.sum(e, axis=1, keepdims=True), dq

        def step(j, carry):
            k0 = pl.multiple_of(j * tq, tq)
            out = []
            for h in range(hp):
                out += list(block(h, k0, *carry[3 * h:3 * h + 3], False))
            return tuple(out)

        z1 = jnp.zeros((tq, 1), F32)
        carry = lax.fori_loop(0, i, step, (z1, z1, jnp.zeros((tq, LANES), F32)) * hp)
        d0 = pl.multiple_of(i * tq, tq)
        for h in range(hp):
            _, _, dq = block(h, d0, *carry[3 * h:3 * h + 3], True)
            dq_ref[:, h * LANES:(h + 1) * LANES] = dq.astype(dq_ref.dtype)

    blk = pl.BlockSpec((tq, w), lambda g, i: (i, g))
    full = pl.BlockSpec((s, w), lambda g, i: (0, g))
    wt = n_heads * LANES
    outs = pl.pallas_call(
        body, name="sb_bwd_carry" if nj else "sb_bwd", grid=(ng, s // tq),
        in_specs=[blk, pl.BlockSpec((s, w), lambda g, i: (0, ng + g)), pl.BlockSpec((s, w), lambda g, i: (0, 2 * ng + g)),
                  blk, blk] + [ANY] * nj,
        out_specs=[blk, full, full] + [ANY] * nj,
        out_shape=[jax.ShapeDtypeStruct((s, wt), BF16), jax.ShapeDtypeStruct((s, wt), F32),
                   jax.ShapeDtypeStruct((s, wt), F32)] + [j.recv_shape for j in jobs],
        scratch_shapes=_piece_sems(nj) if nj else [],
        compiler_params=_cp(("arbitrary", "arbitrary")),
    )(qkv, qkv, qkv, do, ltot, *[j.g for j in jobs])
    if nj:
        ex.landed(jobs, outs[3:])
    return outs[:3]


def _band_masks(b, max_dist):
    qi = lax.broadcasted_iota(jnp.int32, (BAND, BAND), 0)
    kj = lax.broadcasted_iota(jnp.int32, (BAND, BAND), 1)
    dist = qi - kj
    return ((BAND + dist) <= max_dist) & (b > 0), (dist >= 0) & (dist <= max_dist)


def _band_fwd(qa, ka, va, *, n_cls, n_steps, hpb, group, q_blk, k_blk, v_blk, max_dist, scale, sinks, name):
    length = qa.shape[0]
    nb = length // BAND
    has_sink = sinks is not None
    qw, kw = hpb * LANES, (hpb // group) * LANES

    def body(*refs):
        q_ref, kp_ref, kc_ref, vp_ref, vc_ref = refs[:5]
        o_ref, lse_ref = refs[-2], refs[-1]
        mask_p, mask_c = _band_masks(pl.program_id(2), max_dist)
        for hh in range(hpb):
            qs = slice(hh * LANES, (hh + 1) * LANES)
            ks = slice((hh // group) * LANES, (hh // group + 1) * LANES)
            q = q_ref[:, qs].astype(BF16)
            s_p = lax.dot_general(q, kp_ref[:, ks].astype(BF16), NT, preferred_element_type=F32) * scale
            s_c = lax.dot_general(q, kc_ref[:, ks].astype(BF16), NT, preferred_element_type=F32) * scale
            s_p = jnp.where(mask_p, s_p, NEG)
            s_c = jnp.where(mask_c, s_c, NEG)
            m = jnp.maximum(jnp.max(s_p, axis=1, keepdims=True), jnp.max(s_c, axis=1, keepdims=True))
            l = jnp.sum(jnp.exp(s_p - m), axis=1, keepdims=True) + jnp.sum(jnp.exp(s_c - m), axis=1, keepdims=True)
            lse = m + jnp.log(l)
            if has_sink:
                sk = refs[5][:, hh * LANES:hh * LANES + 1]
                lse = jnp.maximum(lse, sk) + jnp.log(1.0 + jnp.exp(-jnp.abs(lse - sk)))
            p_p = jnp.exp(s_p - lse).astype(BF16)
            p_c = jnp.exp(s_c - lse).astype(BF16)
            o_ref[:, qs] = (jnp.dot(p_p, vp_ref[:, ks].astype(BF16), preferred_element_type=F32)
                            + jnp.dot(p_c, vc_ref[:, ks].astype(BF16), preferred_element_type=F32))
            lse_ref[:, qs] = jnp.broadcast_to(lse, (BAND, LANES))

    def prev(b):
        return jnp.maximum(b - 1, 0)

    specs = [pl.BlockSpec((BAND, qw), lambda n, st, b: (b, q_blk(n, st))),
             pl.BlockSpec((BAND, kw), lambda n, st, b: (prev(b), k_blk(n, st))),
             pl.BlockSpec((BAND, kw), lambda n, st, b: (b, k_blk(n, st))),
             pl.BlockSpec((BAND, kw), lambda n, st, b: (prev(b), v_blk(n, st))),
             pl.BlockSpec((BAND, kw), lambda n, st, b: (b, v_blk(n, st)))]
    ins = [qa, ka, ka, va, va]
    if has_sink:
        ins.append(sinks)
        specs.append(pl.BlockSpec((1, qw), lambda n, st, b: (0, st)))
    out = pl.BlockSpec((BAND, qw), lambda n, st, b: (b, n * n_steps + st))
    w = n_cls * n_steps * qw
    return pl.pallas_call(
        body, name=name, grid=(n_cls, n_steps, nb), in_specs=specs, out_specs=[out, out],
        out_shape=[jax.ShapeDtypeStruct((length, w), F32), jax.ShapeDtypeStruct((length, w), F32)],
        compiler_params=_cp(("parallel", "parallel", "parallel")),
    )(*ins)


def _band_bwd(qa, ka, va, o, do, lse, dlse, *, n_cls, n_steps, hpb, group, q_blk, k_blk, v_blk, max_dist, scale, sinks,
              name):
    length = qa.shape[0]
    nb = length // BAND
    has_sink, has_dlse = sinks is not None, dlse is not None
    qw, kw = hpb * LANES, (hpb // group) * LANES

    def body(*refs):
        q_ref, kp_ref, kc_ref, vp_ref, vc_ref, o_ref, do_ref, lse_ref = refs[:8]
        pos = 8
        dlse_ref = refs[pos] if has_dlse else None
        pos += has_dlse
        sink_ref = refs[pos] if has_sink else None
        pos += has_sink
        dq_ref, dkc_ref, dkp_ref, dvc_ref, dvp_ref = refs[pos:pos + 5]
        b = pl.program_id(2)
        mask_p, mask_c = _band_masks(b, max_dist)
        if has_sink:
            dsink_ref = refs[pos + 5]

            @pl.when(b == 0)
            def _():
                dsink_ref[...] = jnp.zeros_like(dsink_ref)

        for hh in range(hpb):
            qs = slice(hh * LANES, (hh + 1) * LANES)
            ks = slice((hh // group) * LANES, (hh // group + 1) * LANES)
            q = q_ref[:, qs].astype(BF16)
            kp, kc = kp_ref[:, ks].astype(BF16), kc_ref[:, ks].astype(BF16)
            vp, vc = vp_ref[:, ks].astype(BF16), vc_ref[:, ks].astype(BF16)
            dov = do_ref[:, qs].astype(F32)
            dob = dov.astype(BF16)
            lse_v = lse_ref[:, hh * LANES:hh * LANES + 1]
            s_p = lax.dot_general(q, kp, NT, preferred_element_type=F32) * scale
            s_c = lax.dot_general(q, kc, NT, preferred_element_type=F32) * scale
            p_p = jnp.where(mask_p, jnp.exp(jnp.where(mask_p, s_p, NEG) - lse_v), 0.0)
            p_c = jnp.where(mask_c, jnp.exp(jnp.where(mask_c, s_c, NEG) - lse_v), 0.0)
            delta = jnp.sum(dov * o_ref[:, qs], axis=1, keepdims=True)
            shift = -delta
            if has_dlse:
                shift = shift + dlse_ref[:, hh * LANES:hh * LANES + 1]
            dp_p = lax.dot_general(dob, vp, NT, preferred_element_type=F32)
            dp_c = lax.dot_general(dob, vc, NT, preferred_element_type=F32)
            ds_p = (p_p * (dp_p + shift) * scale).astype(BF16)
            ds_c = (p_c * (dp_c + shift) * scale).astype(BF16)
            dq_ref[:, qs] = (jnp.dot(ds_p, kp, preferred_element_type=F32)
                             + jnp.dot(ds_c, kc, preferred_element_type=F32))
            parts = (lax.dot_general(ds_c, q, TN, preferred_element_type=F32),
                     lax.dot_general(ds_p, q, TN, preferred_element_type=F32),
                     lax.dot_general(p_c.astype(BF16), dob, TN, preferred_element_type=F32),
                     lax.dot_general(p_p.astype(BF16), dob, TN, preferred_element_type=F32))
            for ref, part in zip((dkc_ref, dkp_ref, dvc_ref, dvp_ref), parts):
                if hh % group == 0:
                    ref[:, ks] = part
                else:
                    ref[:, ks] += part
            if has_sink:
                p_sink = jnp.exp(sink_ref[:, hh * LANES:hh * LANES + 1] - lse_v)
                dsink_ref[:, qs] += jnp.broadcast_to(jnp.sum(-p_sink * delta, axis=0, keepdims=True), (1, LANES))

    def prev(b):
        return jnp.maximum(b - 1, 0)

    per_q = pl.BlockSpec((BAND, qw), lambda n, st, b: (b, n * n_steps + st))
    per_k = pl.BlockSpec((BAND, kw), lambda n, st, b: (b, n * n_steps + st))
    specs = [pl.BlockSpec((BAND, qw), lambda n, st, b: (b, q_blk(n, st))),
             pl.BlockSpec((BAND, kw), lambda n, st, b: (prev(b), k_blk(n, st))),
             pl.BlockSpec((BAND, kw), lambda n, st, b: (b, k_blk(n, st))),
             pl.BlockSpec((BAND, kw), lambda n, st, b: (prev(b), v_blk(n, st))),
             pl.BlockSpec((BAND, kw), lambda n, st, b: (b, v_blk(n, st))),
             per_q, per_q, per_q]
    ins = [qa, ka, ka, va, va, o, do, lse]
    if has_dlse:
        ins.append(dlse)
        specs.append(per_q)
    out_specs = [per_q] + [per_k] * 4
    out_shape = ([jax.ShapeDtypeStruct((length, n_cls * n_steps * qw), F32)]
                 + [jax.ShapeDtypeStruct((length, n_cls * n_steps * kw), F32)] * 4)
    if has_sink:
        ins.append(sinks)
        specs.append(pl.BlockSpec((1, qw), lambda n, st, b: (0, st)))
        out_specs = out_specs + [pl.BlockSpec((1, qw), lambda n, st, b: (0, st))]
        out_shape = out_shape + [jax.ShapeDtypeStruct((1, n_steps * qw), F32)]
    return pl.pallas_call(
        body, name=name, grid=(n_cls, n_steps, nb), in_specs=specs, out_specs=out_specs, out_shape=out_shape,
        compiler_params=_cp(("parallel", "parallel", "arbitrary")),
    )(*ins)


def _band_fold(cur, prv, *, n_cls, n_heads, group, name):
    length = cur.shape[0]
    nb = length // BAND
    n_kv = n_heads // group

    def body(c_ref, p_ref, o_ref):
        b, gq = pl.program_id(2), pl.program_id(3)

        @pl.when(gq == 0)
        def _():
            o_ref[...] = jnp.zeros_like(o_ref)

        o_ref[...] += c_ref[...] + jnp.where(b + 1 < nb, p_ref[...], 0.0)

    blk = (BAND, LANES)
    return pl.pallas_call(
        body, name=name, grid=(n_cls, n_kv, nb, group),
        in_specs=[pl.BlockSpec(blk, lambda n, h, b, gq: (b, n * n_heads + h * group + gq)),
                  pl.BlockSpec(blk, lambda n, h, b, gq: (jnp.minimum(b + 1, nb - 1), n * n_heads + h * group + gq))],
        out_specs=pl.BlockSpec(blk, lambda n, h, b, gq: (b, n * n_kv + h)),
        out_shape=jax.ShapeDtypeStruct((length, n_cls * n_kv * LANES), F32),
        compiler_params=_cp(("parallel", "parallel", "parallel", "arbitrary")),
    )(cur, prv)


def _dil_mix_fwd(os_, lses):
    s, w = os_[0].shape
    bm = _tile(s, (256, 128, 8))

    def body(o0, o1, o2, l0, l1, l2, out_ref):
        ls = [l0[...], l1[...], l2[...]]
        m = jnp.maximum(jnp.maximum(ls[0], ls[1]), ls[2])
        es = [jnp.exp(v - m) for v in ls]
        inv = 1.0 / (es[0] + es[1] + es[2])
        for gi, o_ref in enumerate((o0, o1, o2)):
            out_ref[:, gi * w:(gi + 1) * w] = (o_ref[...] * (es[gi] * inv)).astype(out_ref.dtype)

    blk = pl.BlockSpec((bm, w), lambda i: (i, 0))
    return pl.pallas_call(
        body, name="dil_mix_fwd", grid=(s // bm,), in_specs=[blk] * 6,
        out_specs=pl.BlockSpec((bm, 3 * w), lambda i: (i, 0)),
        out_shape=jax.ShapeDtypeStruct((s, 3 * w), BF16), compiler_params=_cp(("parallel",)),
    )(*os_, *lses)


def _dil_mix_bwd(os_, lses, dmixed):
    s, w = os_[0].shape
    bm = _tile(s, (256, 128, 8))
    hg = w // LANES

    def body(o0, o1, o2, l0, l1, l2, dm_ref, do0, do1, do2, dl0, dl1, dl2):
        ls = [l0[...], l1[...], l2[...]]
        m = jnp.maximum(jnp.maximum(ls[0], ls[1]), ls[2])
        es = [jnp.exp(v - m) for v in ls]
        inv = 1.0 / (es[0] + es[1] + es[2])
        alphas = [e * inv for e in es]
        dalphas = []
        for gi, (o_ref, do_ref) in enumerate(((o0, do0), (o1, do1), (o2, do2))):
            dm = dm_ref[:, gi * w:(gi + 1) * w].astype(F32)
            do_ref[...] = (dm * alphas[gi]).astype(do_ref.dtype)
            prod = dm * o_ref[...]
            parts = [jnp.broadcast_to(jnp.sum(prod[:, j * LANES:(j + 1) * LANES], axis=1, keepdims=True), (bm, LANES))
                     for j in range(hg)]
            dalphas.append(jnp.concatenate(parts, axis=1) if hg > 1 else parts[0])
        mean = alphas[0] * dalphas[0] + alphas[1] * dalphas[1] + alphas[2] * dalphas[2]
        for gi, dl_ref in enumerate((dl0, dl1, dl2)):
            dl_ref[...] = alphas[gi] * (dalphas[gi] - mean)

    blk = pl.BlockSpec((bm, w), lambda i: (i, 0))
    return pl.pallas_call(
        body, name="dil_mix_bwd", grid=(s // bm,), in_specs=[blk] * 6 + [pl.BlockSpec((bm, 3 * w), lambda i: (i, 0))],
        out_specs=[blk] * 6,
        out_shape=[jax.ShapeDtypeStruct((s, w), BF16)] * 3 + [jax.ShapeDtypeStruct((s, w), F32)] * 3,
        compiler_params=_cp(("parallel",)),
    )(*os_, *lses, dmixed)


def _xattn_fwd(q, kv):
    s, w = q.shape
    mlen = kv.shape[0]
    tq = _tile(s, (512, 256, 128))
    scale = LANES ** -0.5

    def body(q_ref, k_ref, v_ref, o_ref):
        for h in range(XA_HEADS):
            sl = slice(h * LANES, (h + 1) * LANES)
            sc = lax.dot_general(q_ref[:, sl], k_ref[:, sl], NT, preferred_element_type=F32) * scale
            m = jnp.max(sc, axis=1, keepdims=True)
            e = jnp.exp(sc - m)
            p = e / jnp.sum(e, axis=1, keepdims=True)
            o_ref[:, sl] = jnp.dot(p.astype(BF16), v_ref[:, sl], preferred_element_type=F32).astype(o_ref.dtype)

    return pl.pallas_call(
        body, name="xattn_fwd", grid=(s // tq,),
        in_specs=[pl.BlockSpec((tq, w), lambda i: (i, 0)), pl.BlockSpec((mlen, w), lambda i: (0, 0)),
                  pl.BlockSpec((mlen, w), lambda i: (0, 1))],
        out_specs=pl.BlockSpec((tq, w), lambda i: (i, 0)),
        out_shape=jax.ShapeDtypeStruct((s, w), BF16), compiler_params=_cp(("parallel",)),
    )(q, kv, kv)


def _xattn_bwd(q, kv, do):
    s, w = q.shape
    mlen = kv.shape[0]
    tq = _tile(s, (512, 256, 128))
    scale = LANES ** -0.5

    def body(q_ref, k_ref, v_ref, do_ref, dq_ref, dk_ref, dv_ref):
        @pl.when(pl.program_id(0) == 0)
        def _():
            dk_ref[...] = jnp.zeros_like(dk_ref)
            dv_ref[...] = jnp.zeros_like(dv_ref)

        for h in range(XA_HEADS):
            sl = slice(h * LANES, (h + 1) * LANES)
            qh, kh, vh, doh = q_ref[:, sl], k_ref[:, sl], v_ref[:, sl], do_ref[:, sl]
            sc = lax.dot_general(qh, kh, NT, preferred_element_type=F32) * scale
            m = jnp.max(sc, axis=1, keepdims=True)
            e = jnp.exp(sc - m)
            p = e / jnp.sum(e, axis=1, keepdims=True)
            dp = lax.dot_general(doh, vh, NT, preferred_element_type=F32)
            ds = (p * (dp - jnp.sum(p * dp, axis=1, keepdims=True)) * scale).astype(BF16)
            dq_ref[:, sl] = jnp.dot(ds, kh, preferred_element_type=F32).astype(dq_ref.dtype)
            dk_ref[:, sl] += lax.dot_general(ds, qh, TN, preferred_element_type=F32)
            dv_ref[:, sl] += lax.dot_general(p.astype(BF16), doh, TN, preferred_element_type=F32)

    row = pl.BlockSpec((tq, w), lambda i: (i, 0))
    acc = pl.BlockSpec((mlen, w), lambda i: (0, 0))
    return pl.pallas_call(
        body, name="xattn_bwd", grid=(s // tq,),
        in_specs=[row, acc, pl.BlockSpec((mlen, w), lambda i: (0, 1)), row],
        out_specs=[row, acc, acc],
        out_shape=[jax.ShapeDtypeStruct((s, w), BF16), jax.ShapeDtypeStruct((mlen, w), F32),
                   jax.ShapeDtypeStruct((mlen, w), F32)],
        compiler_params=_cp(("arbitrary",)),
    )(q, kv, kv, do)


def _adamw(parts, w, m, v, name, ex=None):
    r, c = w.shape
    npc = len(parts)
    rp = r // npc
    row_bytes = c * (2 * N_DEV * npc * parts[0].dtype.itemsize + 2 * 7 * 4)
    br = _tile(rp, tuple(p for p in (256, 128, 64, 32, 16, 8) if p * c * 4 <= 1024 * 1024 and p * row_bytes <= MM_VMEM_BUDGET))
    steps = rp // br
    jobs = ex.take(r * c * ADAMW_EXCHANGE_BYTES_PER_PARAM) if ex is not None else []
    nj = len(jobs)

    def body(*refs):
        w_ref, m_ref, v_ref = refs[npc:npc + 3]
        g_ref, d_ref, nm_ref, nv_ref = refs[npc + 3 + nj:npc + 7 + nj]
        if nj:
            job_refs = (refs[npc + 3:npc + 3 + nj], refs[npc + 7 + nj:npc + 7 + 2 * nj], refs[npc + 7 + 2 * nj:])
            p_id, i_id = pl.program_id(0), pl.program_id(1)
            pl.when((p_id == 0) & (i_id == 0))(lambda: ex.start(jobs, *job_refs))
            pl.when((p_id == npc - 1) & (i_id == steps - 1))(lambda: ex.wait(jobs, *job_refs))

        def update(p_ref):
            g = p_ref[0].astype(F32)
            for t in range(1, N_DEV):
                g = g + p_ref[t].astype(F32)
            nm = ADAM_B1 * m_ref[...] + (1.0 - ADAM_B1) * g
            nv = ADAM_B2 * v_ref[...] + (1.0 - ADAM_B2) * (g * g)
            m_hat = nm / (1.0 - ADAM_B1 ** ADAM_STEP)
            v_hat = nv / (1.0 - ADAM_B2 ** ADAM_STEP)
            g_ref[...] = g
            d_ref[...] = -ADAM_LR * (m_hat / (jnp.sqrt(v_hat) + ADAM_EPS) + ADAM_WD * w_ref[...])
            nm_ref[...] = nm
            nv_ref[...] = nv

        for k in range(npc):
            pl.when(pl.program_id(0) == k)(lambda k=k: update(refs[k]))

    blk = pl.BlockSpec((br, c), lambda p, i: (p * steps + i, 0))
    part_specs = [pl.BlockSpec((N_DEV, br, c), lambda p, i, k=k: (0, jnp.where(p == k, i, 0), 0)) for k in range(npc)]
    outs = pl.pallas_call(
        body, name=name + "_carry" if nj else name, grid=(npc, steps),
        in_specs=part_specs + [blk, blk, blk] + [ANY] * nj, out_specs=[blk] * 4 + [ANY] * nj,
        out_shape=[jax.ShapeDtypeStruct((r, c), F32)] * 4 + [j.recv_shape for j in jobs],
        scratch_shapes=_piece_sems(nj) if nj else [],
        compiler_params=_cp(("arbitrary", "arbitrary") if nj else ("parallel", "parallel")),
    )(*parts, w, m, v, *[j.g for j in jobs])
    if nj:
        ex.landed(jobs, outs[4:])
    return outs[:4]


MESH_ID = pl.DeviceIdType.MESH
ANY = pl.BlockSpec(memory_space=pl.ANY)


def _my_place():
    return lax.axis_index("x"), lax.axis_index("y"), lax.axis_index("c")


def _all_gather(xs, name):
    nt = len(xs)

    def body(*refs):
        x_refs, out_refs = refs[:nt], refs[nt:2 * nt]
        send_sems, recv_sems, local_sems = refs[2 * nt:]
        x, y, c = _my_place()
        me, sibling = (x, y, c), (x, y, 1 - c)
        chips = [(1 - x, y), (x, 1 - y), (1 - x, 1 - y)]

        def slot(t, p):
            return out_refs[t].at[4 * p[0] + 2 * p[1] + p[2]]

        def copy(t, k, block, to, src=None):
            return pltpu.make_async_remote_copy(
                src_ref=slot(t, block) if src is None else src, dst_ref=slot(t, block),
                send_sem=send_sems.at[7 * t + k], recv_sem=recv_sems.at[7 * t + k], device_id=to,
                device_id_type=MESH_ID)

        mine, first, passed = [], [], []
        for t in range(nt):
            cp = pltpu.make_async_copy(x_refs[t], slot(t, me), local_sems.at[t])
            cp.start()
            mine.append(cp)
            group = [copy(t, 0, me, sibling, src=x_refs[t])]
            group += [copy(t, 1 + j, me, (*chip, c), src=x_refs[t]) for j, chip in enumerate(chips)]
            for cp in group:
                cp.start()
            first += group
        for t in range(nt):
            for j, chip in enumerate(chips):
                copy(t, 1 + j, (*chip, c), me).wait_recv()
                fw = copy(t, 4 + j, (*chip, c), sibling)
                fw.start()
                passed.append(fw)
        for t in range(nt):
            copy(t, 0, sibling, me).wait_recv()
            for j, chip in enumerate(chips):
                copy(t, 4 + j, (*chip, 1 - c), me).wait_recv()
        for cp in first + passed:
            cp.wait_send()
        for cp in mine:
            cp.wait()

    return pl.pallas_call(
        body, name=name, in_specs=[ANY] * nt, out_specs=[ANY] * nt,
        out_shape=[jax.ShapeDtypeStruct((N_DEV,) + tuple(v.shape), v.dtype) for v in xs],
        scratch_shapes=[pltpu.SemaphoreType.DMA((7 * nt,)), pltpu.SemaphoreType.DMA((7 * nt,)),
                        pltpu.SemaphoreType.DMA((nt,))],
    )(*xs)


def _exchange(gs, name):
    nt = len(gs)

    def body(*refs):
        g_refs, out_refs = refs[:nt], refs[nt:2 * nt]
        send_sems, recv_sems, local_sems = refs[2 * nt:]
        x, y, c = _my_place()
        my_slot = 4 * x + 2 * y + c
        mine, sent = [], []
        for t in range(nt):
            cp = pltpu.make_async_copy(g_refs[t].at[my_slot], out_refs[t].at[my_slot], local_sems.at[t])
            cp.start()
            mine.append(cp)
            for rel in range(1, N_DEV):
                px, py, pc = x ^ ((rel >> 2) & 1), y ^ ((rel >> 1) & 1), c ^ (rel & 1)
                cp = pltpu.make_async_remote_copy(
                    src_ref=g_refs[t].at[4 * px + 2 * py + pc], dst_ref=out_refs[t].at[my_slot],
                    send_sem=send_sems.at[7 * t + rel - 1], recv_sem=recv_sems.at[7 * t + rel - 1],
                    device_id=(px, py, pc), device_id_type=MESH_ID)
                cp.start()
                sent.append(cp)
        for cp in sent:
            cp.wait_recv()
        for cp in sent:
            cp.wait_send()
        for cp in mine:
            cp.wait()

    return pl.pallas_call(
        body, name=name, in_specs=[ANY] * nt, out_specs=[ANY] * nt,
        out_shape=[jax.ShapeDtypeStruct(tuple(v.shape), v.dtype) for v in gs],
        scratch_shapes=[pltpu.SemaphoreType.DMA((7 * nt,)), pltpu.SemaphoreType.DMA((7 * nt,)),
                        pltpu.SemaphoreType.DMA((nt,))],
    )(*gs)


EXCHANGE_BYTES_PER_FLOP = 1.1e-4
CARRIER_OVERFILL = 1.15
ADAMW_EXCHANGE_BYTES_PER_PARAM = 2.0
PIECE_BYTES = 8 * 1024 * 1024
CARRIER_MIN_BYTES = 6 * 1024 * 1024
ROW_ALIGN = 16
BLOCKS = -1


class _Piece:
    def __init__(self, key, g, axis, lo, hi):
        self.key, self.g, self.axis, self.lo, self.hi = key, g, axis, lo, hi
        cols = g.shape[2] if axis == BLOCKS else g.shape[1] if axis == 0 else g.shape[1] // N_DEV
        self.recv_shape = jax.ShapeDtypeStruct((N_DEV, hi - lo, cols), g.dtype)
        self.nbytes = N_DEV * (hi - lo) * cols * g.dtype.itemsize


def _piece_sems(nj):
    return [pltpu.SemaphoreType.DMA((7 * nj,)), pltpu.SemaphoreType.DMA((7 * nj,)), pltpu.SemaphoreType.DMA((nj,))]


def _piece_copies(jobs, g_refs, recv_refs, sems):
    send_sems, recv_sems, local_sems = sems
    x, y, c = _my_place()
    me = 4 * x + 2 * y + c
    local, remote = [], []
    for t, (job, g, r) in enumerate(zip(jobs, g_refs, recv_refs)):
        rows = job.hi - job.lo

        def block(slot, job=job, g=g, r=r, rows=rows):
            if job.axis == BLOCKS:
                return g.at[slot, pl.ds(job.lo, rows), :]
            if job.axis == 0:
                start = pl.multiple_of(slot * (g.shape[0] // N_DEV) + job.lo, ROW_ALIGN)
                return g.at[pl.ds(start, rows), :]
            cols = r.shape[2]
            return g.at[pl.ds(job.lo, rows), pl.ds(pl.multiple_of(slot * cols, LANES), cols)]

        local.append(pltpu.make_async_copy(block(me), r.at[me], local_sems.at[t]))
        for rel in range(1, N_DEV):
            px, py, pc = x ^ ((rel >> 2) & 1), y ^ ((rel >> 1) & 1), c ^ (rel & 1)
            remote.append(pltpu.make_async_remote_copy(
                src_ref=block(4 * px + 2 * py + pc), dst_ref=r.at[me], send_sem=send_sems.at[7 * t + rel - 1],
                recv_sem=recv_sems.at[7 * t + rel - 1], device_id=(px, py, pc), device_id_type=MESH_ID))
    return local, remote


def _pieces_start(jobs, g_refs, recv_refs, sems):
    local, remote = _piece_copies(jobs, g_refs, recv_refs, sems)
    for cp in local + remote:
        cp.start()


def _pieces_wait(jobs, g_refs, recv_refs, sems):
    local, remote = _piece_copies(jobs, g_refs, recv_refs, sems)
    for cp in remote:
        cp.wait_recv()
    for cp in remote:
        cp.wait_send()
    for cp in local:
        cp.wait()


def _exchange_pieces(jobs, name):
    nj = len(jobs)

    def body(*refs):
        job_refs = (refs[:nj], refs[nj:2 * nj], refs[2 * nj:])
        _pieces_start(jobs, *job_refs)
        _pieces_wait(jobs, *job_refs)

    return pl.pallas_call(
        body, name=name, in_specs=[ANY] * nj, out_specs=[ANY] * nj, out_shape=[j.recv_shape for j in jobs],
        scratch_shapes=_piece_sems(nj),
    )(*[j.g for j in jobs])


class _GradExchange:
    def __init__(self):
        self.queue, self.recv = [], {}

    def put(self, name, layer, g):
        axis = SHARD_AXIS[name] - 1
        rows = g.shape[0] // N_DEV if axis == 0 else g.shape[0]
        if g.dtype != BF16 or g.ndim != 2:
            return False
        if rows % ROW_ALIGN or (axis == 1 and (g.shape[1] // N_DEV) % LANES):
            g = _to_blocks(g, axis)
            axis, rows = BLOCKS, g.shape[1]
        n_split = max(1, round(g.size * g.dtype.itemsize / PIECE_BYTES))
        while rows % (n_split * ROW_ALIGN):
            n_split -= 1
        for k in range(n_split):
            self.queue.append(_Piece((name, layer, k), g, axis, k * rows // n_split, (k + 1) * rows // n_split))
        return True

    def take(self, capacity):
        jobs, used = [], 0
        while capacity >= CARRIER_MIN_BYTES and self.queue and used + self.queue[0].nbytes <= CARRIER_OVERFILL * capacity:
            used += self.queue[0].nbytes
            jobs.append(self.queue.pop(0))
        return jobs

    def landed(self, jobs, recvs):
        for j, r in zip(jobs, recvs):
            self.recv[j.key] = r

    def flush(self):
        if self.queue:
            jobs, self.queue = self.queue, []
            self.landed(jobs, _exchange_pieces(jobs, "exchange_rest"))

    def pieces_of(self, name):
        return [self.recv[k] for k in sorted(k for k in self.recv if k[0] == name)]

    start = staticmethod(_pieces_start)
    wait = staticmethod(_pieces_wait)


GATHER_SPEEDUP = 2.0
GATHER_FIRST_OVERFILL = 2.2


class _WeightPiece:
    def __init__(self, key, local, axis):
        self.key, self.g, self.axis = key, local, axis
        r, c = local.shape
        self.recv_shape = jax.ShapeDtypeStruct(
            (N_DEV, r, c) if axis == BLOCKS else (r * N_DEV, c) if axis == 0 else (r, c * N_DEV), local.dtype)
        self.nbytes = N_DEV * r * c * local.dtype.itemsize


def _gather_copies(jobs, x_refs, full_refs, sems):
    send_sems, recv_sems, local_sems = sems
    x, y, c = _my_place()
    me, sibling = (x, y, c), (x, y, 1 - c)
    chips = [(1 - x, y), (x, 1 - y), (1 - x, 1 - y)]
    plans = []
    for t, (job, xr, fr) in enumerate(zip(jobs, x_refs, full_refs)):
        def blk(p, job=job, xr=xr, fr=fr):
            slot = 4 * p[0] + 2 * p[1] + p[2]
            if job.axis == BLOCKS:
                return fr.at[slot]
            if job.axis == 0:
                return fr.at[pl.ds(pl.multiple_of(slot * xr.shape[0], ROW_ALIGN), xr.shape[0]), :]
            return fr.at[:, pl.ds(pl.multiple_of(slot * xr.shape[1], LANES), xr.shape[1])]

        def copy(k, block, to, src=None, t=t, blk=blk):
            return pltpu.make_async_remote_copy(
                src_ref=blk(block) if src is None else src, dst_ref=blk(block), send_sem=send_sems.at[7 * t + k],
                recv_sem=recv_sems.at[7 * t + k], device_id=to, device_id_type=MESH_ID)

        plans.append(dict(
            mine=pltpu.make_async_copy(xr, blk(me), local_sems.at[t]),
            first=[copy(0, me, sibling, src=xr)] + [copy(1 + j, me, (*chip, c), src=xr) for j, chip in enumerate(chips)],
            landed=[copy(1 + j, (*chip, c), me) for j, chip in enumerate(chips)],
            passed=[copy(4 + j, (*chip, c), sibling) for j, chip in enumerate(chips)],
            from_sibling=[copy(0, sibling, me)] + [copy(4 + j, (*chip, 1 - c), me) for j, chip in enumerate(chips)]))
    return plans


def _gather_start(jobs, x_refs, full_refs, sems):
    for plan in _gather_copies(jobs, x_refs, full_refs, sems):
        plan["mine"].start()
        for cp in plan["first"]:
            cp.start()


def _gather_wait(jobs, x_refs, full_refs, sems):
    plans = _gather_copies(jobs, x_refs, full_refs, sems)
    for plan in plans:
        for landed, passed in zip(plan["landed"], plan["passed"]):
            landed.wait_recv()
            passed.start()
    for plan in plans:
        for cp in plan["from_sibling"]:
            cp.wait_recv()
        for cp in plan["first"] + plan["passed"]:
            cp.wait_send()
        plan["mine"].wait()


def _gather_pieces(jobs, name):
    nj = len(jobs)

    def body(*refs):
        job_refs = (refs[:nj], refs[nj:2 * nj], refs[2 * nj:])
        _gather_start(jobs, *job_refs)
        _gather_wait(jobs, *job_refs)

    return pl.pallas_call(
        body, name=name, in_specs=[ANY] * nj, out_specs=[ANY] * nj, out_shape=[j.recv_shape for j in jobs],
        scratch_shapes=_piece_sems(nj),
    )(*[j.g for j in jobs])


class _WeightGather:
    def __init__(self):
        self.queue, self.full = [], {}

    def add(self, name, layer, local, in_place):
        self.queue.append(_WeightPiece((name, layer), local, SHARD_AXIS[name] - 1 if in_place else BLOCKS))

    def take(self, capacity):
        capacity *= GATHER_SPEEDUP
        jobs, used = [], 0
        if capacity >= CARRIER_MIN_BYTES and self.queue and self.queue[0].nbytes <= GATHER_FIRST_OVERFILL * capacity:
            jobs.append(self.queue.pop(0))
            used = jobs[0].nbytes
            while self.queue and used + self.queue[0].nbytes <= CARRIER_OVERFILL * capacity:
                used += self.queue[0].nbytes
                jobs.append(self.queue.pop(0))
        return jobs

    def landed(self, jobs, fulls):
        for j, f in zip(jobs, fulls):
            self.full[j.key] = _to_full(f, SHARD_AXIS[j.key[0]] - 1) if j.axis == BLOCKS else f

    def get(self, name, layer):
        if (name, layer) not in self.full:
            at = [j.key for j in self.queue].index((name, layer))
            jobs, self.queue = self.queue[:at + 1], self.queue[at + 1:]
            self.landed(jobs, _gather_pieces(jobs, "gather_now"))
        return self.full[(name, layer)]

    start = staticmethod(_gather_start)
    wait = staticmethod(_gather_wait)


def _val(w):
    return w() if callable(w) else w


def _ffn_fwd(x, g, w_gu, w_d, tag, wg=None):
    n = _rms_fwd(x, g)
    gu = _mm(n, _val(w_gu), out_dtype=BF16, name=f"{tag}_gu", ex=wg)
    a = _swiglu_fwd(gu)
    return _mm(a, _val(w_d), alpha=0.5, res=x, name=f"{tag}_down", ex=wg), (n, gu, a)


def _put(ex, name, layer, g):
    if ex is not None:
        ex.put(name, layer, g)


def _ffn_bwd(dy, x, g, w_gu, w_d, saved, tag, ex=None, which="ffn1", layer=0):
    n, gu, a = saved
    da = _mm(dy, w_d, tb=True, alpha=0.5, out_dtype=BF16, name=f"{tag}_da", ex=ex)
    d_wd = _mm(a, dy, ta=True, alpha=0.5, out_dtype=BF16, name=f"{tag}_dwd", ex=ex)
    _put(ex, f"{which}_w_down", layer, d_wd)
    dgu = _swiglu_bwd(gu, da)
    dn = _mm(dgu, w_gu, tb=True, name=f"{tag}_dn", ex=ex)
    d_wgu = _mm(n, dgu, ta=True, out_dtype=BF16, name=f"{tag}_dwgu", ex=ex)
    _put(ex, f"{which}_w_gate_up", layer, d_wgu)
    dx, dg = _rms_bwd(x, g, dn, dy)
    return dx, dg, d_wgu, d_wd


def _sb_mixer_fwd(h, w_qkv, w_o, x, wg=None):
    qkv = _mm(h, _val(w_qkv), out_dtype=BF16, name="sb_qkv", ex=wg)
    o, ltot = _sb_fwd(qkv, qkv.shape[1] // (3 * LANES), wg)
    return _mm(o, _val(w_o), res=x, name="sb_out", ex=wg), (qkv, o, ltot)


def _sb_mixer_bwd(dy, h, w_qkv, w_o, saved, ex=None, layer=0):
    qkv, o, ltot = saved
    n_heads = w_o.shape[0] // LANES
    do = _mm(dy, w_o, tb=True, out_dtype=BF16, name="sb_do", ex=ex)
    d_wo = _mm(o, dy, ta=True, out_dtype=BF16, name="sb_dwo", ex=ex)
    _put(ex, "sb_w_o", layer, d_wo)
    dq, dk, dv = _sb_bwd(qkv, do, ltot, n_heads, ex)
    dqkv = jnp.concatenate([dq, dk.astype(BF16), dv.astype(BF16)], axis=1)
    dh = _mm(dqkv, w_qkv, tb=True, name="sb_dh", ex=ex)
    d_wqkv = _mm(h, dqkv, ta=True, out_dtype=BF16, name="sb_dwqkv", ex=ex)
    _put(ex, "sb_w_qkv", layer, d_wqkv)
    return dh, d_wqkv, d_wo


def _dil_cols(gi):
    ng = len(DIL_PATTERNS)
    return dict(q_blk=lambda n, st: n * 2 * ng + gi, k_blk=lambda n, st: n * 2 * ng + ng + gi,
                v_blk=lambda n, st: n * 3 * ng + 2 * ng + gi)


def _dil_mixer_fwd(h, w_qkv, w_o, x, tabs, wg=None):
    s = h.shape[0]
    qkv = _mm(h, _val(w_qkv), name="dil_qkv", ex=wg)
    n_all = qkv.shape[1] // (3 * LANES)
    hg = n_all // len(DIL_PATTERNS)
    qk = _rope(qkv, tabs[0], tabs[1], 2 * n_all, 16, "dil_rope")
    os_, lses = [], []
    for gi, (window, dil) in enumerate(DIL_PATTERNS):
        o, lse = _band_fwd(qk.reshape(s // dil, -1), qk.reshape(s // dil, -1), qkv.reshape(s // dil, -1),
                           n_cls=dil, n_steps=1, hpb=hg, group=1, **_dil_cols(gi),
                           max_dist=window // dil, scale=LANES ** -0.5, sinks=None, name=f"dil_fwd{gi}")
        os_.append(o.reshape(s, hg * LANES))
        lses.append(lse.reshape(s, hg * LANES))
    mixed = _dil_mix_fwd(os_, lses)
    return _mm(mixed, _val(w_o), res=x, name="dil_out", ex=wg), (qkv, qk, os_, lses, mixed)


def _dil_mixer_bwd(dy, h, w_qkv, w_o, saved, tabs_bwd, ex=None):
    qkv, qk, os_, lses, mixed = saved
    s = h.shape[0]
    n_all = w_o.shape[0] // LANES
    hg = n_all // len(DIL_PATTERNS)
    dmixed = _mm(dy, w_o, tb=True, out_dtype=BF16, name="dil_dmix", ex=ex)
    d_wo = _mm(mixed, dy, ta=True, out_dtype=BF16, name="dil_dwo", ex=ex)
    _put(ex, "dil_w_o", 0, d_wo)
    mix_out = _dil_mix_bwd(os_, lses, dmixed)
    dos, dlses = mix_out[:3], mix_out[3:]
    dqs, dks, dvs = [], [], []
    for gi, (window, dil) in enumerate(DIL_PATTERNS):
        length = s // dil
        dq, dkc, dkp, dvc, dvp = _band_bwd(
            qk.reshape(length, -1), qk.reshape(length, -1), qkv.reshape(length, -1), os_[gi].reshape(length, -1),
            dos[gi].reshape(length, -1), lses[gi].reshape(length, -1), dlses[gi].reshape(length, -1),
            n_cls=dil, n_steps=1, hpb=hg, group=1, **_dil_cols(gi), max_dist=window // dil,
            scale=LANES ** -0.5, sinks=None, name=f"dil_bwd{gi}")
        dqs.append(dq.reshape(s, -1))
        dks.append(_band_fold(dkc, dkp, n_cls=dil, n_heads=hg, group=1, name=f"dil_foldk{gi}").reshape(s, -1))
        dvs.append(_band_fold(dvc, dvp, n_cls=dil, n_heads=hg, group=1, name=f"dil_foldv{gi}").reshape(s, -1))
    dqk_rot = jnp.concatenate(dqs + dks, axis=1)
    dqk = _rope(dqk_rot, tabs_bwd[0], tabs_bwd[1], 2 * n_all, 16, "dil_rope_bwd")
    dqkv = jnp.concatenate([dqk] + [t.astype(BF16) for t in dvs], axis=1)
    dh = _mm(dqkv, w_qkv, tb=True, name="dil_dh", ex=ex)
    d_wqkv = _mm(h, dqkv, ta=True, out_dtype=BF16, name="dil_dwqkv", ex=ex)
    _put(ex, "dil_w_qkv", 0, d_wqkv)
    return dh, d_wqkv, d_wo


def _pad_heads(w, axis):
    shape = list(w.shape)
    n = shape[axis] // SWA_HEAD_DIM
    w = w.reshape(shape[:axis] + [n, SWA_HEAD_DIM] + shape[axis + 1:])
    pad = [(0, 0)] * w.ndim
    pad[axis + 1] = (0, LANES - SWA_HEAD_DIM)
    shape[axis] = n * LANES
    return jnp.pad(w, pad).reshape(shape)


def _unpad_heads(w, axis):
    shape = list(w.shape)
    n = shape[axis] // LANES
    w = w.reshape(shape[:axis] + [n, LANES] + shape[axis + 1:])
    w = lax.slice_in_dim(w, 0, SWA_HEAD_DIM, axis=axis + 1)
    shape[axis] = n * SWA_HEAD_DIM
    return w.reshape(shape)


def _swa_mixer_fwd(h, w_qkv_p, b_qkv_p, sinks_b, w_o_p, b_o, x, tabs, wg=None):
    nq = w_o_p.shape[0] // LANES
    nkv = nq // SWA_GROUP
    qkv = _mm(h, w_qkv_p, bias=b_qkv_p, name="swa_qkv", ex=wg)
    qk = _rope(qkv, tabs[0], tabs[1], nq + nkv, 8, "swa_rope")
    o, lse = _band_fwd(qk, qk, qkv, n_cls=1, n_steps=nkv, hpb=SWA_GROUP, group=SWA_GROUP, q_blk=lambda n, st: st,
                       k_blk=lambda n, st: nq + st, v_blk=lambda n, st: nq + nkv + st,
                       max_dist=SWA_WINDOW - 1, scale=SWA_HEAD_DIM ** -0.5, sinks=sinks_b, name="swa_fwd")
    return _mm(o, w_o_p, res=x, bias=b_o, name="swa_out"), (qkv, qk, o, lse)


def _swa_mixer_bwd(dy, h, w_qkv_p, sinks_b, w_o_p, saved, tabs_bwd, ex=None):
    qkv, qk, o, lse = saved
    nq = w_o_p.shape[0] // LANES
    nkv = nq // SWA_GROUP
    do = _mm(dy, w_o_p, tb=True, name="swa_do", ex=ex)
    d_wo_p = _mm(o, dy, ta=True, out_dtype=BF16, name="swa_dwo", ex=ex)
    d_bo = _colsum(dy, "swa_dbo")
    dq, dkc, dkp, dvc, dvp, dsink = _band_bwd(
        qk, qk, qkv, o, do, lse, None, n_cls=1, n_steps=nkv, hpb=SWA_GROUP, group=SWA_GROUP, q_blk=lambda n, st: st,
        k_blk=lambda n, st: nq + st, v_blk=lambda n, st: nq + nkv + st, max_dist=SWA_WINDOW - 1,
        scale=SWA_HEAD_DIM ** -0.5, sinks=sinks_b, name="swa_bwd")
    dk = _band_fold(dkc, dkp, n_cls=1, n_heads=nkv, group=1, name="swa_foldk")
    dv = _band_fold(dvc, dvp, n_cls=1, n_heads=nkv, group=1, name="swa_foldv")
    dqk = _rope(jnp.concatenate([dq, dk], axis=1), tabs_bwd[0], tabs_bwd[1], nq + nkv, 8, "swa_rope_bwd")
    dqkv = jnp.concatenate([dqk, dv.astype(BF16)], axis=1)
    d_bqkv_p = _colsum(dqkv, "swa_dbqkv")
    dh = _mm(dqkv, w_qkv_p, tb=True, name="swa_dh", ex=ex)
    d_wqkv_p = _mm(h, dqkv, ta=True, out_dtype=BF16, name="swa_dwqkv", ex=ex)
    return dh, d_wqkv_p, d_bqkv_p, dsink, d_wo_p, d_bo


def _xattn_layer_fwd(x, mem, g_x, g_m, w_q, w_kv, w_o):
    hq = _rms_fwd(x, g_x, "rms_fwd")
    hm = _rms_fwd(mem, g_m, "rms_mem_fwd")
    q = _mm(hq, _val(w_q), out_dtype=BF16, name="xa_q")
    kv = _mm(hm, _val(w_kv), out_dtype=BF16, name="xa_kv")
    o = _xattn_fwd(q, kv)
    return _mm(o, _val(w_o), res=x, name="xa_out"), (hq, hm, q, kv, o)


def _xattn_layer_bwd(dy, x, mem, g_x, g_m, w_q, w_kv, w_o, saved):
    hq, hm, q, kv, o = saved
    do = _mm(dy, w_o, tb=True, out_dtype=BF16, name="xa_do")
    d_wo = _mm(o, dy, ta=True, out_dtype=BF16, name="xa_dwo")
    dq, dk, dv = _xattn_bwd(q, kv, do)
    dkv = jnp.concatenate([dk, dv], axis=1).astype(BF16)
    dhq = _mm(dq, w_q, tb=True, name="xa_dhq")
    d_wq = _mm(hq, dq, ta=True, out_dtype=BF16, name="xa_dwq")
    dhm = _mm(dkv, w_kv, tb=True, name="xa_dhm")
    d_wkv = _mm(hm, dkv, ta=True, out_dtype=BF16, name="xa_dwkv")
    dx, dg_x = _rms_bwd(x, g_x, dhq, dy)
    _, dg_m = _rms_bwd(mem, g_m, dhm, None, "rms_mem_bwd")
    return dx, dg_x, dg_m, d_wq, d_wkv, d_wo


def _to_full(gathered, axis):
    t = jnp.moveaxis(gathered, 0, axis)
    shape = list(t.shape)
    return t.reshape(shape[:axis] + [shape[axis] * shape[axis + 1]] + shape[axis + 2:])


def _to_blocks(full, axis):
    shape = list(full.shape)
    t = full.reshape(shape[:axis] + [N_DEV, shape[axis] // N_DEV] + shape[axis + 1:])
    return jnp.moveaxis(t, axis, 0)


SHARD_AXIS = {
    "ffn1_w_gate_up": 2, "ffn1_w_down": 1, "sb_w_qkv": 2, "sb_w_o": 1, "dil_w_qkv": 2, "dil_w_o": 2,
    "swa_w_qkv": 2, "swa_b_qkv": 1, "swa_w_o": 1, "swa_b_o": 1, "xattn_w_q": 1, "xattn_w_kv": 1, "xattn_w_o": 2,
    "ffn2_w_gate_up": 2, "ffn2_w_down": 1,
}
SMALL = ("ffn1_norm", "mix_norm", "xattn_norm", "mem_norm", "ffn2_norm", "final_norm", "swa_sinks")
WEIGHTS = ("ffn1_norm", "ffn1_w_gate_up", "ffn1_w_down", "mix_norm", "sb_w_qkv", "sb_w_o", "dil_w_qkv", "dil_w_o",
           "swa_w_qkv", "swa_b_qkv", "swa_sinks", "swa_w_o", "swa_b_o", "xattn_norm", "mem_norm", "xattn_w_q",
           "xattn_w_kv", "xattn_w_o", "ffn2_norm", "ffn2_w_gate_up", "ffn2_w_down", "final_norm")


def _flat2(a):
    return a.reshape(-1, a.shape[-1])


def _pack_small(vals, d):
    rows = [vals[n].reshape(-1, d) for n in SMALL[:5]] + [vals["final_norm"].reshape(1, d)]
    sk = vals["swa_sinks"].reshape(1, -1)
    rows.append(jnp.pad(sk, ((0, 0), (0, d - sk.shape[1]))))
    rows.append(jnp.zeros((2, d), F32))
    return jnp.concatenate(rows, axis=0)


def _unpack_small(packed, like):
    out, r = {}, 0
    for n in SMALL[:5]:
        k = like[n].shape[0]
        out[n] = packed[r:r + k]
        r += k
    out["final_norm"] = packed[r]
    out["swa_sinks"] = packed[r + 1:r + 2, :like["swa_sinks"].shape[1]]
    return out


def _local_step(x0, mem0, positions, target, full, norm, ex=None, wg=None):
    d = x0.shape[1]
    names = list(SHARD_AXIS)
    sinks_b = jnp.repeat(norm["swa_sinks"], LANES, axis=1)
    tabs_dil, tabs_dil_bwd = _rope_tables(positions, 32), _rope_tables(positions, 32, -1.0)
    tabs_swa, tabs_swa_bwd = _rope_tables(positions, 16), _rope_tables(positions, 16, -1.0)

    def vec(name, i):
        return norm[name][i:i + 1]

    saved = []
    xc = x0
    for i in range(DEPTH):
        kind, j = i % 3, i // 3
        rec = {"x0": xc}
        xc, rec["ffn1"] = _ffn_fwd(xc, vec("ffn1_norm", i), lambda: full["ffn1_w_gate_up"][i],
                                   lambda: full["ffn1_w_down"][i], "ffn", wg)
        rec["x1"] = xc
        h = _rms_fwd(xc, vec("mix_norm", i))
        rec["h"] = h
        if kind == 0:
            xc, rec["mix"] = _sb_mixer_fwd(h, lambda: full["sb_w_qkv"][j], lambda: full["sb_w_o"][j], xc, wg)
        elif kind == 1:
            xc, rec["mix"] = _dil_mixer_fwd(h, lambda: full["dil_w_qkv"][j], lambda: full["dil_w_o"][j], xc, tabs_dil, wg)
        else:
            swa_w_qkv_p = _pad_heads(full["swa_w_qkv"][0], 1)
            swa_b_qkv_p = _pad_heads(full["swa_b_qkv"][0][None], 1)
            swa_w_o_p = _pad_heads(full["swa_w_o"][0], 0)
            xc, rec["mix"] = _swa_mixer_fwd(h, swa_w_qkv_p, swa_b_qkv_p, sinks_b, swa_w_o_p, full["swa_b_o"][0][None],
                                            xc, tabs_swa, wg)
        rec["x2"] = xc
        xc, rec["xa"] = _xattn_layer_fwd(xc, mem0, vec("xattn_norm", i), vec("mem_norm", i),
                                         lambda: full["xattn_w_q"][i], lambda: full["xattn_w_kv"][i],
                                         lambda: full["xattn_w_o"][i])
        rec["x3"] = xc
        xc, rec["ffn2"] = _ffn_fwd(xc, vec("ffn2_norm", i), lambda: full["ffn2_w_gate_up"][i],
                                   lambda: full["ffn2_w_down"][i], "ffn", wg)
        saved.append(rec)

    loss_part, dx, dg_final = _loss_head(xc, norm["final_norm"].reshape(1, d), target)

    gfull = {n: [None] * full[n].shape[0] for n in names}
    gsmall = {n: [None] * DEPTH for n in SMALL[:5]}
    gsmall["final_norm"] = dg_final
    gsmall["swa_sinks"] = jnp.zeros_like(norm["swa_sinks"])
    for i in reversed(range(DEPTH)):
        kind, j = i % 3, i // 3
        rec = saved[i]
        dx, gsmall["ffn2_norm"][i], gfull["ffn2_w_gate_up"][i], gfull["ffn2_w_down"][i] = _ffn_bwd(
            dx, rec["x3"], vec("ffn2_norm", i), full["ffn2_w_gate_up"][i], full["ffn2_w_down"][i], rec["ffn2"], "ffn",
            ex, "ffn2", i)
        (dx, gsmall["xattn_norm"][i], gsmall["mem_norm"][i], gfull["xattn_w_q"][i], gfull["xattn_w_kv"][i],
         gfull["xattn_w_o"][i]) = _xattn_layer_bwd(dx, rec["x2"], mem0, vec("xattn_norm", i), vec("mem_norm", i),
                                                   full["xattn_w_q"][i], full["xattn_w_kv"][i], full["xattn_w_o"][i],
                                                   rec["xa"])
        for n in ("xattn_w_q", "xattn_w_kv", "xattn_w_o"):
            _put(ex, n, i, gfull[n][i])
        if kind == 0:
            dh, gfull["sb_w_qkv"][j], gfull["sb_w_o"][j] = _sb_mixer_bwd(
                dx, rec["h"], full["sb_w_qkv"][j], full["sb_w_o"][j], rec["mix"], ex, j)
        elif kind == 1:
            dh, gfull["dil_w_qkv"][j], gfull["dil_w_o"][j] = _dil_mixer_bwd(
                dx, rec["h"], full["dil_w_qkv"][j], full["dil_w_o"][j], rec["mix"], tabs_dil_bwd, ex)
        else:
            dh, d_wqkv_p, d_bqkv_p, dsink, d_wo_p, d_bo = _swa_mixer_bwd(
                dx, rec["h"], swa_w_qkv_p, sinks_b, swa_w_o_p, rec["mix"], tabs_swa_bwd, ex)
            gfull["swa_w_qkv"][j] = _unpad_heads(d_wqkv_p, 1)
            gfull["swa_b_qkv"][j] = _unpad_heads(d_bqkv_p, 1)[0]
            gfull["swa_w_o"][j] = _unpad_heads(d_wo_p, 0)
            gfull["swa_b_o"][j] = d_bo[0]
            gsmall["swa_sinks"] = dsink.reshape(1, -1, LANES)[:, :, 0]
            _put(ex, "swa_w_o", j, gfull["swa_w_o"][j])
            _put(ex, "swa_w_qkv", j, gfull["swa_w_qkv"][j])
        dx, gsmall["mix_norm"][i] = _rms_bwd(rec["x1"], vec("mix_norm", i), dh, dx)
        dx, gsmall["ffn1_norm"][i], gfull["ffn1_w_gate_up"][i], gfull["ffn1_w_down"][i] = _ffn_bwd(
            dx, rec["x0"], vec("ffn1_norm", i), full["ffn1_w_gate_up"][i], full["ffn1_w_down"][i], rec["ffn1"], "ffn",
            ex, "ffn1", i)
    for n in SMALL[:5]:
        gsmall[n] = jnp.concatenate(gsmall[n], axis=0)
    return loss_part[0, 0], dx, gfull, gsmall


def _train_step(x, mem, positions, loss_target, w, m, v):
    d = x.shape[2]
    names = list(SHARD_AXIS)
    norm = {n: w[n] for n in SMALL}

    def in_place(n):
        return w[n].shape[2] % LANES == 0 if SHARD_AXIS[n] == 2 else w[n].shape[1] % ROW_ALIGN == 0

    first = [n for n in names if w[n].ndim != 3]
    gathered = _all_gather([_flat2(w[n]) for n in first], "gather_weights")
    stacked = {n: _to_full(g.reshape((N_DEV,) + w[n].shape), SHARD_AXIS[n]) for n, g in zip(first, gathered)}
    wg = _WeightGather()
    for i in range(DEPTH):
        mixer = (("sb_w_qkv", "sb_w_o"), ("dil_w_qkv", "dil_w_o"), ("swa_w_qkv", "swa_w_o"))[i % 3]
        for n in ("ffn1_w_gate_up", "ffn1_w_down") + mixer + ("xattn_w_q", "xattn_w_kv", "xattn_w_o",
                                                               "ffn2_w_gate_up", "ffn2_w_down"):
            layer = i // 3 if n in mixer else i
            wg.add(n, layer, w[n][layer].astype(BF16), in_place(n))

    class Layers:
        def __init__(self, n):
            self.n, self.shape = n, w[n].shape[:1]

        def __getitem__(self, layer):
            return wg.get(self.n, layer) if w[self.n].ndim == 3 else stacked[self.n][layer]

    full = {n: Layers(n) for n in names}
    ex = _GradExchange()
    loss_part, dx, gfull, gsmall = _local_step(x[0], mem[0], positions, loss_target[0], full, norm, ex, wg)
    loss = lax.psum(loss_part, MESH_AXES)
    grad_x = dx[None]

    grad, delta, new_m, new_v = {}, {}, {}, {}

    def update(n, parts, carrier):
        outs = _adamw(parts, _flat2(w[n]), _flat2(m[n]), _flat2(v[n]), "adamw", carrier)
        grad[n], delta[n], new_m[n], new_v[n] = (o.reshape(w[n].shape) for o in outs)

    taken = {k[0] for k in ex.recv} | {p.key[0] for p in ex.queue}
    late = {p.key[0] for p in ex.queue}
    for n in names:
        if n in taken and n not in late:
            update(n, ex.pieces_of(n), ex)
    ex.flush()
    for n in names:
        if n in late:
            update(n, ex.pieces_of(n), None)
    rest = [n for n in names if n not in taken]
    blocks = [_to_blocks(jnp.stack(gfull[n], axis=0), SHARD_AXIS[n]) for n in rest]
    blocks = [b.reshape(N_DEV, -1, b.shape[-1]) for b in blocks]
    for n, received in zip(rest, _exchange(blocks, "exchange_grads")):
        update(n, [received], None)

    small_parts = _all_gather([_pack_small(gsmall, d)], "gather_small_grads")[0]
    outs = _adamw([small_parts], _pack_small(norm, d), _pack_small({n: m[n] for n in SMALL}, d),
                  _pack_small({n: v[n] for n in SMALL}, d), "adamw_small")
    for res, o in zip((grad, delta, new_m, new_v), outs):
        res.update(_unpack_small(o, norm))
    return loss, grad_x, grad, delta, new_m, new_v


def kernel(x, mem, positions, ffn1_norm, ffn1_w_gate_up, ffn1_w_down, mix_norm, sb_w_qkv, sb_w_o, dil_w_qkv, dil_w_o, swa_w_qkv, swa_b_qkv, swa_sinks, swa_w_o, swa_b_o, xattn_norm, mem_norm, xattn_w_q, xattn_w_kv, xattn_w_o, ffn2_norm, ffn2_w_gate_up, ffn2_w_down, final_norm, loss_target, m_ffn1_norm, m_ffn1_w_gate_up, m_ffn1_w_down, m_mix_norm, m_sb_w_qkv, m_sb_w_o, m_dil_w_qkv, m_dil_w_o, m_swa_w_qkv, m_swa_b_qkv, m_swa_sinks, m_swa_w_o, m_swa_b_o, m_xattn_norm, m_mem_norm, m_xattn_w_q, m_xattn_w_kv, m_xattn_w_o, m_ffn2_norm, m_ffn2_w_gate_up, m_ffn2_w_down, m_final_norm, v_ffn1_norm, v_ffn1_w_gate_up, v_ffn1_w_down, v_mix_norm, v_sb_w_qkv, v_sb_w_o, v_dil_w_qkv, v_dil_w_o, v_swa_w_qkv, v_swa_b_qkv, v_swa_sinks, v_swa_w_o, v_swa_b_o, v_xattn_norm, v_mem_norm, v_xattn_w_q, v_xattn_w_kv, v_xattn_w_o, v_ffn2_norm, v_ffn2_w_gate_up, v_ffn2_w_down, v_final_norm):
    args = dict(locals())
    w = {n: args[n] for n in WEIGHTS}
    m = {n: args["m_" + n] for n in WEIGHTS}
    v = {n: args["v_" + n] for n in WEIGHTS}
    loss, grad_x, grad, delta, new_m, new_v = _train_step(x, mem, positions, loss_target, w, m, v)
    return (loss, grad_x, *[grad[n] for n in WEIGHTS], *[delta[n] for n in WEIGHTS],
            *[new_m[n] for n in WEIGHTS], *[new_v[n] for n in WEIGHTS])
```

```python
import jax
import jax.numpy as jnp
from jax import lax
from jax.experimental import pallas as pl
from jax.experimental.pallas import tpu as pltpu

F32 = jnp.float32
BF16 = jnp.bfloat16

N_DEV = 8
MESH_AXES = ("x", "y", "c")
LANES = 128
BAND = 128
NORM_EPS = 1e-6
ROPE_THETA = 500000.0
DIL_PATTERNS = ((128, 1), (512, 4), (2048, 16))
SWA_HEAD_DIM = 64
SWA_GROUP = 8
SWA_WINDOW = 128
XA_HEADS = 4
DEPTH = 4
ADAM_LR, ADAM_B1, ADAM_B2, ADAM_EPS, ADAM_WD, ADAM_STEP = 0.001, 0.9, 0.999, 1e-08, 0.01, 10
VMEM_LIMIT = 56 * 1024 * 1024
NEG = -1e30

NT = (((1,), (1,)), ((), ()))
TN = (((0,), (0,)), ((), ()))


def _tile(n, prefs):
    for p in prefs:
        if n % p == 0:
            return p
    return n


def _cp(sem):
    return pltpu.CompilerParams(dimension_semantics=sem, vmem_limit_bytes=VMEM_LIMIT)


MM_TILE_SIZES = (2816, 2048, 1408, 1024, 512, 256, 128)
MM_VMEM_BUDGET = 40 * 1024 * 1024
MM_STEP_BYTES = 1.2e6


def _mm_tiles(m, n, k, ea, eb, eo, has_res):
    best = None
    for bm in [c for c in MM_TILE_SIZES if m % c == 0] or [m]:
        for bn in [c for c in MM_TILE_SIZES if n % c == 0] or [n]:
            for bk in [c for c in MM_TILE_SIZES if k % c == 0] or [k]:
                vmem = 2 * (bm * bk * ea + bk * bn * eb) + bm * bn * 4 + 2 * bm * bn * (eo + (4 if has_res else 0))
                vmem += (bm * bk * 2 if ea == 4 else 0) + (bk * bn * 2 if eb == 4 else 0)
                if vmem > MM_VMEM_BUDGET:
                    continue
                ni, nj, nk = m // bm, n // bn, k // bk
                traffic = (m * k * ea * (nj if nk > 1 else 1) + k * n * eb * (ni if nk > 1 or nj > 1 else 1)
                           + m * n * (eo + (4 if has_res else 0)) + ni * nj * nk * MM_STEP_BYTES)
                if best is None or traffic < best[0]:
                    best = (traffic, (bm, bn, bk))
    return best[1]
def _mm(a, b, *, ta=False, tb=False, out_dtype=F32, alpha=1.0, res=None, bias=None, name, ex=None):
    kdim, m = a.shape if ta else a.shape[::-1]
    kdim2, n = b.shape[::-1] if tb else b.shape
    assert kdim == kdim2, (a.shape, b.shape, ta, tb)
    bm, bn, bk = _mm_tiles(m, n, kdim, a.dtype.itemsize, b.dtype.itemsize, jnp.dtype(out_dtype).itemsize, res is not None)
    nk = kdim // bk
    grid = (m // bm, n // bn, nk)
    dn = (((0 if ta else 1,), (1 if tb else 0,)), ((), ()))
    has_res, has_bias = res is not None, bias is not None
    jobs = ex.take(2.0 * m * n * kdim * EXCHANGE_BYTES_PER_FLOP) if ex is not None else []
    nj = len(jobs)
    n_in = 2 + has_res + has_bias

    def body(*refs):
        a_ref, b_ref = refs[0], refs[1]
        res_ref = refs[2] if has_res else None
        bias_ref = refs[2 + has_res] if has_bias else None
        o_ref, acc_ref = refs[n_in + nj], refs[n_in + 2 * nj + 1]
        k = pl.program_id(2)
        if nj:
            job_refs = (refs[n_in:n_in + nj], refs[n_in + nj + 1:n_in + 2 * nj + 1], refs[n_in + 2 * nj + 2:])
            ids = [pl.program_id(t) for t in range(3)]

            @pl.when((ids[0] == 0) & (ids[1] == 0) & (ids[2] == 0))
            def _():
                ex.start(jobs, *job_refs)

            @pl.when((ids[0] == grid[0] - 1) & (ids[1] == grid[1] - 1) & (ids[2] == grid[2] - 1))
            def _():
                ex.wait(jobs, *job_refs)

        def finish(r):
            if alpha != 1.0:
                r = r * alpha
            if has_bias:
                r = r + bias_ref[...]
            if has_res:
                r = r + res_ref[...]
            o_ref[...] = r.astype(o_ref.dtype)

        prod = lax.dot_general(a_ref[...].astype(BF16), b_ref[...].astype(BF16), dn, preferred_element_type=F32)
        if nk == 1:
            finish(prod)
        else:
            @pl.when(k == 0)
            def _():
                acc_ref[...] = prod

            @pl.when(k > 0)
            def _():
                acc_ref[...] += prod

            pl.when(k == nk - 1)(lambda: finish(acc_ref[...]))

    a_spec = pl.BlockSpec((bk, bm), lambda i, j, k: (k, i)) if ta else pl.BlockSpec((bm, bk), lambda i, j, k: (i, k))
    b_spec = pl.BlockSpec((bn, bk), lambda i, j, k: (j, k)) if tb else pl.BlockSpec((bk, bn), lambda i, j, k: (k, j))
    ins, specs = [a, b], [a_spec, b_spec]
    if has_res:
        ins.append(res)
        specs.append(pl.BlockSpec((bm, bn), lambda i, j, k: (i, j)))
    if has_bias:
        ins.append(bias)
        specs.append(pl.BlockSpec((1, bn), lambda i, j, k: (0, j)))
    out_spec = pl.BlockSpec((bm, bn), lambda i, j, k: (i, j))
    out_shape = jax.ShapeDtypeStruct((m, n), out_dtype)
    scratch = [pltpu.VMEM((bm, bn), F32)]
    if not nj:
        return pl.pallas_call(
            body, name=name, grid=grid, in_specs=specs, out_specs=out_spec, out_shape=out_shape,
            scratch_shapes=scratch, compiler_params=_cp(("parallel", "parallel", "arbitrary")),
        )(*ins)
    outs = pl.pallas_call(
        body, name=name + "_carry", grid=grid, in_specs=specs + [ANY] * nj, out_specs=[out_spec] + [ANY] * nj,
        out_shape=[out_shape] + [j.recv_shape for j in jobs], scratch_shapes=scratch + _piece_sems(nj),
        compiler_params=_cp(("arbitrary", "arbitrary", "arbitrary")),
    )(*ins, *[j.g for j in jobs])
    ex.landed(jobs, outs[1:])
    return outs[0]


def _rms_fwd(x, g, name="rms_fwd"):
    s, d = x.shape
    bm = _tile(s, (256, 128, 8))

    def body(x_ref, g_ref, o_ref):
        xv = x_ref[...]
        r = lax.rsqrt(jnp.mean(xv * xv, axis=-1, keepdims=True) + NORM_EPS)
        o_ref[...] = (xv * r * g_ref[...]).astype(o_ref.dtype)

    return pl.pallas_call(
        body, name=name, grid=(s // bm,),
        in_specs=[pl.BlockSpec((bm, d), lambda i: (i, 0)), pl.BlockSpec((1, d), lambda i: (0, 0))],
        out_specs=pl.BlockSpec((bm, d), lambda i: (i, 0)),
        out_shape=jax.ShapeDtypeStruct((s, d), BF16), compiler_params=_cp(("parallel",)),
    )(x, g)


def _rms_bwd(x, g, dn, dy=None, name="rms_bwd"):
    s, d = x.shape
    bm = _tile(s, (256, 128, 8))
    has_dy = dy is not None

    def body(*refs):
        x_ref, g_ref, dn_ref = refs[:3]
        dy_ref = refs[3] if has_dy else None
        dx_ref, dg_ref = refs[-2], refs[-1]

        @pl.when(pl.program_id(0) == 0)
        def _():
            dg_ref[...] = jnp.zeros_like(dg_ref)

        xv = x_ref[...]
        r = lax.rsqrt(jnp.mean(xv * xv, axis=-1, keepdims=True) + NORM_EPS)
        xh = xv * r
        dnv = dn_ref[...].astype(F32)
        dxh = dnv * g_ref[...]
        dx = r * (dxh - xh * jnp.mean(dxh * xh, axis=-1, keepdims=True))
        if has_dy:
            dx = dx + dy_ref[...]
        dx_ref[...] = dx
        dg_ref[...] += jnp.sum(dnv * xh, axis=0, keepdims=True)

    row = pl.BlockSpec((bm, d), lambda i: (i, 0))
    vec = pl.BlockSpec((1, d), lambda i: (0, 0))
    ins, specs = [x, g, dn], [row, vec, row]
    if has_dy:
        ins.append(dy)
        specs.append(row)
    return pl.pallas_call(
        body, name=name, grid=(s // bm,), in_specs=specs, out_specs=[row, vec],
        out_shape=[jax.ShapeDtypeStruct((s, d), F32), jax.ShapeDtypeStruct((1, d), F32)],
        compiler_params=_cp(("arbitrary",)),
    )(*ins)


def _loss_head(x, g, target):
    s, d = x.shape
    bm = _tile(s, (256, 128, 8))

    def body(x_ref, g_ref, t_ref, loss_ref, dx_ref, dg_ref):
        @pl.when(pl.program_id(0) == 0)
        def _():
            dg_ref[...] = jnp.zeros_like(dg_ref)
            loss_ref[...] = jnp.zeros_like(loss_ref)

        xv = x_ref[...]
        gv = g_ref[...]
        r = lax.rsqrt(jnp.mean(xv * xv, axis=-1, keepdims=True) + NORM_EPS)
        xh = xv * r
        err = xh * gv - t_ref[...]
        part = 0.5 * jnp.sum(jnp.mean(err * err, axis=-1, keepdims=True), axis=0, keepdims=True)
        loss_ref[...] += jnp.broadcast_to(part, loss_ref.shape)
        dyv = err * (1.0 / d)
        dxh = dyv * gv
        dx_ref[...] = r * (dxh - xh * jnp.mean(dxh * xh, axis=-1, keepdims=True))
        dg_ref[...] += jnp.sum(dyv * xh, axis=0, keepdims=True)

    row = pl.BlockSpec((bm, d), lambda i: (i, 0))
    vec = pl.BlockSpec((1, d), lambda i: (0, 0))
    return pl.pallas_call(
        body, name="loss_head", grid=(s // bm,), in_specs=[row, vec, row],
        out_specs=[pl.BlockSpec((1, LANES), lambda i: (0, 0)), row, vec],
        out_shape=[jax.ShapeDtypeStruct((1, LANES), F32), jax.ShapeDtypeStruct((s, d), F32),
                   jax.ShapeDtypeStruct((1, d), F32)],
        compiler_params=_cp(("arbitrary",)),
    )(x, g, target)


def _colsum(x, name="colsum"):
    s, n = x.shape
    bm = _tile(s, (256, 128, 8))

    def body(x_ref, o_ref):
        @pl.when(pl.program_id(0) == 0)
        def _():
            o_ref[...] = jnp.zeros_like(o_ref)

        o_ref[...] += jnp.sum(x_ref[...].astype(F32), axis=0, keepdims=True)

    return pl.pallas_call(
        body, name=name, grid=(s // bm,), in_specs=[pl.BlockSpec((bm, n), lambda i: (i, 0))],
        out_specs=pl.BlockSpec((1, n), lambda i: (0, 0)), out_shape=jax.ShapeDtypeStruct((1, n), F32),
        compiler_params=_cp(("arbitrary",)),
    )(x)


def _swiglu_fwd(gu):
    s, f2 = gu.shape
    f = f2 // 2
    bm, bf = _tile(s, (256, 128, 8)), _tile(f, (512, 256, 128))

    def body(gu_ref, o_ref):
        for c in range(0, f, bf):
            gv = gu_ref[:, c:c + bf].astype(F32)
            o_ref[:, c:c + bf] = (gv / (1.0 + jnp.exp(-gv)) * gu_ref[:, f + c:f + c + bf].astype(F32)).astype(o_ref.dtype)

    return pl.pallas_call(
        body, name="swiglu_fwd", grid=(s // bm,), in_specs=[pl.BlockSpec((bm, f2), lambda i: (i, 0))],
        out_specs=pl.BlockSpec((bm, f), lambda i: (i, 0)),
        out_shape=jax.ShapeDtypeStruct((s, f), BF16), compiler_params=_cp(("parallel",)),
    )(gu)


def _swiglu_bwd(gu, da):
    s, f2 = gu.shape
    f = f2 // 2
    bm, bf = _tile(s, (128, 8)), _tile(f, (512, 256, 128))

    def body(gu_ref, da_ref, o_ref):
        for c in range(0, f, bf):
            gv = gu_ref[:, c:c + bf].astype(F32)
            dav = da_ref[:, c:c + bf].astype(F32)
            sig = 1.0 / (1.0 + jnp.exp(-gv))
            o_ref[:, c:c + bf] = (dav * gu_ref[:, f + c:f + c + bf].astype(F32)
                                  * (sig * (1.0 + gv * (1.0 - sig)))).astype(o_ref.dtype)
            o_ref[:, f + c:f + c + bf] = (dav * gv * sig).astype(o_ref.dtype)

    return pl.pallas_call(
        body, name="swiglu_bwd", grid=(s // bm,),
        in_specs=[pl.BlockSpec((bm, f2), lambda i: (i, 0)), pl.BlockSpec((bm, f), lambda i: (i, 0))],
        out_specs=pl.BlockSpec((bm, f2), lambda i: (i, 0)),
        out_shape=jax.ShapeDtypeStruct((s, f2), BF16), compiler_params=_cp(("parallel",)),
    )(gu, da)


def _rope_tables(positions, rot, sign=1.0):
    half = rot // 2
    inv_freq = jnp.power(F32(ROPE_THETA), -jnp.arange(half, dtype=F32) * 2.0 / rot)
    ang = positions.reshape(-1).astype(F32)[:, None] * inv_freq
    cos, sin = jnp.cos(ang), jnp.sin(ang) * sign
    s = ang.shape[0]
    c_tab = jnp.concatenate([cos, cos, jnp.ones((s, LANES - rot), F32)], axis=1)
    s_tab = jnp.concatenate([-sin, sin, jnp.zeros((s, LANES - rot), F32)], axis=1)
    return c_tab, s_tab


def _rope(x, c_tab, s_tab, nblk, half, name):
    s = x.shape[0]
    bm = _tile(s, (256, 128, 8))
    hb = _tile(nblk, (12, 8, 6, 4, 3, 2))

    def body(x_ref, c_ref, s_ref, o_ref):
        lane = lax.broadcasted_iota(jnp.int32, (bm, LANES), 1)
        for h in range(hb):
            sl = slice(h * LANES, (h + 1) * LANES)
            xv = x_ref[:, sl].astype(F32)
            sw = jnp.where(lane < half, pltpu.roll(xv, LANES - half, 1), pltpu.roll(xv, half, 1))
            o_ref[:, sl] = (xv * c_ref[...] + sw * s_ref[...]).astype(o_ref.dtype)

    blk = pl.BlockSpec((bm, hb * LANES), lambda i, j: (i, j))
    tab = pl.BlockSpec((bm, LANES), lambda i, j: (i, 0))
    return pl.pallas_call(
        body, name=name, grid=(s // bm, nblk // hb), in_specs=[blk, tab, tab], out_specs=blk,
        out_shape=jax.ShapeDtypeStruct((s, nblk * LANES), BF16), compiler_params=_cp(("parallel", "parallel")),
    )(x, c_tab, s_tab)


def _sb_terms(q, kb, scale):
    z = lax.dot_general(q, kb, NT, preferred_element_type=F32) * scale
    u = jnp.log(1.0 + jnp.exp(-jnp.abs(z)))
    return jnp.minimum(-z, 0.0) - u, jnp.minimum(z, 0.0) - u


def _sb_heads_per_step(n_heads):
    return 2 if n_heads % 2 == 0 else 1


def _sb_fwd(qkv, n_heads, ex=None):
    s = qkv.shape[0]
    tq = _tile(s, (256, 128))
    hp = _sb_heads_per_step(n_heads)
    w = hp * LANES
    ng = n_heads // hp
    scale = LANES ** -0.5
    jobs = ex.take(4 * 4 * n_heads * s * s * LANES * EXCHANGE_BYTES_PER_FLOP) if ex is not None else []
    nj = len(jobs)

    def body(*refs):
        q_ref, k_ref, v_ref = refs[:3]
        o_ref, lt_ref = refs[3 + nj:5 + nj]
        i = pl.program_id(1)
        if nj:
            job_refs = (refs[3:3 + nj], refs[5 + nj:5 + 2 * nj], refs[5 + 2 * nj:])
            gi = pl.program_id(0)
            pl.when((gi == 0) & (i == 0))(lambda: ex.start(jobs, *job_refs))
            pl.when((gi == ng - 1) & (i == s // tq - 1))(lambda: ex.wait(jobs, *job_refs))
        row = lax.broadcasted_iota(jnp.int32, (tq, tq), 0)
        col = lax.broadcasted_iota(jnp.int32, (tq, tq), 1)
        below = col < row
        later = (row > col).astype(BF16)
        qs = [q_ref[:, h * LANES:(h + 1) * LANES] for h in range(hp)]

        def block(h, k0, c, acc, diag):
            sl = slice(h * LANES, (h + 1) * LANES)
            lk, logsig = _sb_terms(qs[h], k_ref[pl.ds(k0, tq), sl], scale)
            if diag:
                lk = jnp.where(below, lk, 0.0)
            a = jnp.exp(logsig + jnp.dot(lk.astype(BF16), later, preferred_element_type=F32) + c)
            if diag:
                a = jnp.where(below, a, 0.0)
            acc = acc + jnp.dot(a.astype(BF16), v_ref[pl.ds(k0, tq), sl], preferred_element_type=F32)
            return c + jnp.sum(lk, axis=1, keepdims=True), acc

        d0 = pl.multiple_of(i * tq, tq)
        carry = []
        for h in range(hp):
            carry += list(block(h, d0, jnp.zeros((tq, 1), F32), jnp.zeros((tq, LANES), F32), True))

        def step(t, carry):
            k0 = pl.multiple_of((i - 1 - t) * tq, tq)
            out = []
            for h in range(hp):
                out += list(block(h, k0, carry[2 * h], carry[2 * h + 1], False))
            return tuple(out)

        carry = lax.fori_loop(0, i, step, tuple(carry))
        for h in range(hp):
            sl = slice(h * LANES, (h + 1) * LANES)
            o_ref[:, sl] = carry[2 * h + 1].astype(o_ref.dtype)
            lt_ref[:, sl] = jnp.broadcast_to(carry[2 * h], (tq, LANES))

    blk = pl.BlockSpec((tq, w), lambda g, i: (i, g))
    outs = pl.pallas_call(
        body, name="sb_fwd_carry" if nj else "sb_fwd", grid=(ng, s // tq),
        in_specs=[blk, pl.BlockSpec((s, w), lambda g, i: (0, ng + g)), pl.BlockSpec((s, w), lambda g, i: (0, 2 * ng + g))]
        + [ANY] * nj,
        out_specs=[blk, blk] + [ANY] * nj,
        out_shape=[jax.ShapeDtypeStruct((s, n_heads * LANES), BF16), jax.ShapeDtypeStruct((s, n_heads * LANES), F32)]
        + [j.recv_shape for j in jobs],
        scratch_shapes=_piece_sems(nj) if nj else [],
        compiler_params=_cp(("arbitrary", "arbitrary")),
    )(qkv, qkv, qkv, *[j.g for j in jobs])
    if nj:
        ex.landed(jobs, outs[2:])
    return outs[:2]


def _sb_bwd(qkv, do, ltot, n_heads, ex=None):
    s = qkv.shape[0]
    tq = _tile(s, (256, 128))
    hp = _sb_heads_per_step(n_heads)
    w = hp * LANES
    ng = n_heads // hp
    scale = LANES ** -0.5
    jobs = ex.take(4 * 9 * n_heads * s * s * LANES * EXCHANGE_BYTES_PER_FLOP) if ex is not None else []
    nj = len(jobs)

    def body(*refs):
        q_ref, k_ref, v_ref, do_ref, lt_ref = refs[:5]
        dq_ref, dk_ref, dv_ref = refs[5 + nj:8 + nj]
        i = pl.program_id(1)
        if nj:
            job_refs = (refs[5:5 + nj], refs[8 + nj:8 + 2 * nj], refs[8 + 2 * nj:])
            gi = pl.program_id(0)
            pl.when((gi == 0) & (i == 0))(lambda: ex.start(jobs, *job_refs))
            pl.when((gi == ng - 1) & (i == s // tq - 1))(lambda: ex.wait(jobs, *job_refs))

        @pl.when(i == 0)
        def _():
            dk_ref[...] = jnp.zeros_like(dk_ref)
            dv_ref[...] = jnp.zeros_like(dv_ref)

        row = lax.broadcasted_iota(jnp.int32, (tq, tq), 0)
        col = lax.broadcasted_iota(jnp.int32, (tq, tq), 1)
        below = col < row
        later = (row > col).astype(BF16)
        before = (row < col).astype(BF16)
        qs = [q_ref[:, h * LANES:(h + 1) * LANES] for h in range(hp)]
        dos = [do_ref[:, h * LANES:(h + 1) * LANES] for h in range(hp)]
        lts = [lt_ref[:, h * LANES:h * LANES + 1] for h in range(hp)]

        def block(h, k0, cpre, ce, dq, diag):
            sl = slice(h * LANES, (h + 1) * LANES)
            kb = k_ref[pl.ds(k0, tq), sl]
            vb = v_ref[pl.ds(k0, tq), sl]
            lk, logsig = _sb_terms(qs[h], kb, scale)
            if diag:
                lk = jnp.where(below, lk, 0.0)
            cnext = cpre + jnp.sum(lk, axis=1, keepdims=True)
            a = jnp.exp(logsig + (lts[h] - cnext) + jnp.dot(lk.astype(BF16), later, preferred_element_type=F32))
            if diag:
                a = jnp.where(below, a, 0.0)
            e = a * lax.dot_general(dos[h], vb, NT, preferred_element_type=F32)
            e_before = ce + jnp.dot(e.astype(BF16), before, preferred_element_type=F32)
            sig = jnp.exp(logsig)
            dz = (e - sig * (e + e_before)) * scale
            if diag:
                dz = jnp.where(below, dz, 0.0)
            dzb = dz.astype(BF16)
            dq = dq + jnp.dot(dzb, kb, preferred_element_type=F32)
            dk_ref[pl.ds(k0, tq), sl] += lax.dot_general(dzb, qs[h], TN, preferred_element_type=F32)
            dv_ref[pl.ds(k0, tq), sl] += lax.dot_general(a.astype(BF16), dos[h], TN, preferred_element_type=F32)
            return cnext, ce + jnp.sum(e, axis=1, keepdims=True), dq

        def step(j, carry):
            k0 = pl.multiple_of(j * tq, tq)
            out = []
            for h in range(hp):
                out += list(block(h, k0, *carry[3 * h:3 * h + 3], False))
            return tuple(out)

        z1 = jnp.zeros((tq, 1), F32)
        carry = lax.fori_loop(0, i, step, (z1, z1, jnp.zeros((tq, LANES), F32)) * hp)
        d0 = pl.multiple_of(i * tq, tq)
        for h in range(hp):
            _, _, dq = block(h, d0, *carry[3 * h:3 * h + 3], True)
            dq_ref[:, h * LANES:(h + 1) * LANES] = dq.astype(dq_ref.dtype)

    blk = pl.BlockSpec((tq, w), lambda g, i: (i, g))
    full = pl.BlockSpec((s, w), lambda g, i: (0, g))
    wt = n_heads * LANES
    outs = pl.pallas_call(
        body, name="sb_bwd_carry" if nj else "sb_bwd", grid=(ng, s // tq),
        in_specs=[blk, pl.BlockSpec((s, w), lambda g, i: (0, ng + g)), pl.BlockSpec((s, w), lambda g, i: (0, 2 * ng + g)),
                  blk, blk] + [ANY] * nj,
        out_specs=[blk, full, full] + [ANY] * nj,
        out_shape=[jax.ShapeDtypeStruct((s, wt), BF16), jax.ShapeDtypeStruct((s, wt), F32),
                   jax.ShapeDtypeStruct((s, wt), F32)] + [j.recv_shape for j in jobs],
        scratch_shapes=_piece_sems(nj) if nj else [],
        compiler_params=_cp(("arbitrary", "arbitrary")),
    )(qkv, qkv, qkv, do, ltot, *[j.g for j in jobs])
    if nj:
        ex.landed(jobs, outs[3:])
    return outs[:3]


def _band_masks(b, max_dist):
    qi = lax.broadcasted_iota(jnp.int32, (BAND, BAND), 0)
    kj = lax.broadcasted_iota(jnp.int32, (BAND, BAND), 1)
    dist = qi - kj
    return ((BAND + dist) <= max_dist) & (b > 0), (dist >= 0) & (dist <= max_dist)


def _band_fwd(qa, ka, va, *, n_cls, n_steps, hpb, group, q_blk, k_blk, v_blk, max_dist, scale, sinks, name):
    length = qa.shape[0]
    nb = length // BAND
    has_sink = sinks is not None
    qw, kw = hpb * LANES, (hpb // group) * LANES

    def body(*refs):
        q_ref, kp_ref, kc_ref, vp_ref, vc_ref = refs[:5]
        o_ref, lse_ref = refs[-2], refs[-1]
        mask_p, mask_c = _band_masks(pl.program_id(2), max_dist)
        for hh in range(hpb):
            qs = slice(hh * LANES, (hh + 1) * LANES)
            ks = slice((hh // group) * LANES, (hh // group + 1) * LANES)
            q = q_ref[:, qs].astype(BF16)
            s_p = lax.dot_general(q, kp_ref[:, ks].astype(BF16), NT, preferred_element_type=F32) * scale
            s_c = lax.dot_general(q, kc_ref[:, ks].astype(BF16), NT, preferred_element_type=F32) * scale
            s_p = jnp.where(mask_p, s_p, NEG)
            s_c = jnp.where(mask_c, s_c, NEG)
            m = jnp.maximum(jnp.max(s_p, axis=1, keepdims=True), jnp.max(s_c, axis=1, keepdims=True))
            l = jnp.sum(jnp.exp(s_p - m), axis=1, keepdims=True) + jnp.sum(jnp.exp(s_c - m), axis=1, keepdims=True)
            lse = m + jnp.log(l)
            if has_sink:
                sk = refs[5][:, hh * LANES:hh * LANES + 1]
                lse = jnp.maximum(lse, sk) + jnp.log(1.0 + jnp.exp(-jnp.abs(lse - sk)))
            p_p = jnp.exp(s_p - lse).astype(BF16)
            p_c = jnp.exp(s_c - lse).astype(BF16)
            o_ref[:, qs] = (jnp.dot(p_p, vp_ref[:, ks].astype(BF16), preferred_element_type=F32)
                            + jnp.dot(p_c, vc_ref[:, ks].astype(BF16), preferred_element_type=F32))
            lse_ref[:, qs] = jnp.broadcast_to(lse, (BAND, LANES))

    def prev(b):
        return jnp.maximum(b - 1, 0)

    specs = [pl.BlockSpec((BAND, qw), lambda n, st, b: (b, q_blk(n, st))),
             pl.BlockSpec((BAND, kw), lambda n, st, b: (prev(b), k_blk(n, st))),
             pl.BlockSpec((BAND, kw), lambda n, st, b: (b, k_blk(n, st))),
             pl.BlockSpec((BAND, kw), lambda n, st, b: (prev(b), v_blk(n, st))),
             pl.BlockSpec((BAND, kw), lambda n, st, b: (b, v_blk(n, st)))]
    ins = [qa, ka, ka, va, va]
    if has_sink:
        ins.append(sinks)
        specs.append(pl.BlockSpec((1, qw), lambda n, st, b: (0, st)))
    out = pl.BlockSpec((BAND, qw), lambda n, st, b: (b, n * n_steps + st))
    w = n_cls * n_steps * qw
    return pl.pallas_call(
        body, name=name, grid=(n_cls, n_steps, nb), in_specs=specs, out_specs=[out, out],
        out_shape=[jax.ShapeDtypeStruct((length, w), F32), jax.ShapeDtypeStruct((length, w), F32)],
        compiler_params=_cp(("parallel", "parallel", "parallel")),
    )(*ins)


def _band_bwd(qa, ka, va, o, do, lse, dlse, *, n_cls, n_steps, hpb, group, q_blk, k_blk, v_blk, max_dist, scale, sinks,
              name):
    length = qa.shape[0]
    nb = length // BAND
    has_sink, has_dlse = sinks is not None, dlse is not None
    qw, kw = hpb * LANES, (hpb // group) * LANES

    def body(*refs):
        q_ref, kp_ref, kc_ref, vp_ref, vc_ref, o_ref, do_ref, lse_ref = refs[:8]
        pos = 8
        dlse_ref = refs[pos] if has_dlse else None
        pos += has_dlse
        sink_ref = refs[pos] if has_sink else None
        pos += has_sink
        dq_ref, dkc_ref, dkp_ref, dvc_ref, dvp_ref = refs[pos:pos + 5]
        b = pl.program_id(2)
        mask_p, mask_c = _band_masks(b, max_dist)
        if has_sink:
            dsink_ref = refs[pos + 5]

            @pl.when(b == 0)
            def _():
                dsink_ref[...] = jnp.zeros_like(dsink_ref)

        for hh in range(hpb):
            qs = slice(hh * LANES, (hh + 1) * LANES)
            ks = slice((hh // group) * LANES, (hh // group + 1) * LANES)
            q = q_ref[:, qs].astype(BF16)
            kp, kc = kp_ref[:, ks].astype(BF16), kc_ref[:, ks].astype(BF16)
            vp, vc = vp_ref[:, ks].astype(BF16), vc_ref[:, ks].astype(BF16)
            dov = do_ref[:, qs].astype(F32)
            dob = dov.astype(BF16)
            lse_v = lse_ref[:, hh * LANES:hh * LANES + 1]
            s_p = lax.dot_general(q, kp, NT, preferred_element_type=F32) * scale
            s_c = lax.dot_general(q, kc, NT, preferred_element_type=F32) * scale
            p_p = jnp.where(mask_p, jnp.exp(jnp.where(mask_p, s_p, NEG) - lse_v), 0.0)
            p_c = jnp.where(mask_c, jnp.exp(jnp.where(mask_c, s_c, NEG) - lse_v), 0.0)
            delta = jnp.sum(dov * o_ref[:, qs], axis=1, keepdims=True)
            shift = -delta
            if has_dlse:
                shift = shift + dlse_ref[:, hh * LANES:hh * LANES + 1]
            dp_p = lax.dot_general(dob, vp, NT, preferred_element_type=F32)
            dp_c = lax.dot_general(dob, vc, NT, preferred_element_type=F32)
            ds_p = (p_p * (dp_p + shift) * scale).astype(BF16)
            ds_c = (p_c * (dp_c + shift) * scale).astype(BF16)
            dq_ref[:, qs] = (jnp.dot(ds_p, kp, preferred_element_type=F32)
                             + jnp.dot(ds_c, kc, preferred_element_type=F32))
            parts = (lax.dot_general(ds_c, q, TN, preferred_element_type=F32),
                     lax.dot_general(ds_p, q, TN, preferred_element_type=F32),
                     lax.dot_general(p_c.astype(BF16), dob, TN, preferred_element_type=F32),
                     lax.dot_general(p_p.astype(BF16), dob, TN, preferred_element_type=F32))
            for ref, part in zip((dkc_ref, dkp_ref, dvc_ref, dvp_ref), parts):
                if hh % group == 0:
                    ref[:, ks] = part
                else:
                    ref[:, ks] += part
            if has_sink:
                p_sink = jnp.exp(sink_ref[:, hh * LANES:hh * LANES + 1] - lse_v)
                dsink_ref[:, qs] += jnp.broadcast_to(jnp.sum(-p_sink * delta, axis=0, keepdims=True), (1, LANES))

    def prev(b):
        return jnp.maximum(b - 1, 0)

    per_q = pl.BlockSpec((BAND, qw), lambda n, st, b: (b, n * n_steps + st))
    per_k = pl.BlockSpec((BAND, kw), lambda n, st, b: (b, n * n_steps + st))
    specs = [pl.BlockSpec((BAND, qw), lambda n, st, b: (b, q_blk(n, st))),
             pl.BlockSpec((BAND, kw), lambda n, st, b: (prev(b), k_blk(n, st))),
             pl.BlockSpec((BAND, kw), lambda n, st, b: (b, k_blk(n, st))),
             pl.BlockSpec((BAND, kw), lambda n, st, b: (prev(b), v_blk(n, st))),
             pl.BlockSpec((BAND, kw), lambda n, st, b: (b, v_blk(n, st))),
             per_q, per_q, per_q]
    ins = [qa, ka, ka, va, va, o, do, lse]
    if has_dlse:
        ins.append(dlse)
        specs.append(per_q)
    out_specs = [per_q] + [per_k] * 4
    out_shape = ([jax.ShapeDtypeStruct((length, n_cls * n_steps * qw), F32)]
                 + [jax.ShapeDtypeStruct((length, n_cls * n_steps * kw), F32)] * 4)
    if has_sink:
        ins.append(sinks)
        specs.append(pl.BlockSpec((1, qw), lambda n, st, b: (0, st)))
        out_specs = out_specs + [pl.BlockSpec((1, qw), lambda n, st, b: (0, st))]
        out_shape = out_shape + [jax.ShapeDtypeStruct((1, n_steps * qw), F32)]
    return pl.pallas_call(
        body, name=name, grid=(n_cls, n_steps, nb), in_specs=specs, out_specs=out_specs, out_shape=out_shape,
        compiler_params=_cp(("parallel", "parallel", "arbitrary")),
    )(*ins)


def _band_fold(cur, prv, *, n_cls, n_heads, group, name):
    length = cur.shape[0]
    nb = length // BAND
    n_kv = n_heads // group

    def body(c_ref, p_ref, o_ref):
        b, gq = pl.program_id(2), pl.program_id(3)

        @pl.when(gq == 0)
        def _():
            o_ref[...] = jnp.zeros_like(o_ref)

        o_ref[...] += c_ref[...] + jnp.where(b + 1 < nb, p_ref[...], 0.0)

    blk = (BAND, LANES)
    return pl.pallas_call(
        body, name=name, grid=(n_cls, n_kv, nb, group),
        in_specs=[pl.BlockSpec(blk, lambda n, h, b, gq: (b, n * n_heads + h * group + gq)),
                  pl.BlockSpec(blk, lambda n, h, b, gq: (jnp.minimum(b + 1, nb - 1), n * n_heads + h * group + gq))],
        out_specs=pl.BlockSpec(blk, lambda n, h, b, gq: (b, n * n_kv + h)),
        out_shape=jax.ShapeDtypeStruct((length, n_cls * n_kv * LANES), F32),
        compiler_params=_cp(("parallel", "parallel", "parallel", "arbitrary")),
    )(cur, prv)


def _dil_mix_fwd(os_, lses):
    s, w = os_[0].shape
    bm = _tile(s, (256, 128, 8))

    def body(o0, o1, o2, l0, l1, l2, out_ref):
        ls = [l0[...], l1[...], l2[...]]
        m = jnp.maximum(jnp.maximum(ls[0], ls[1]), ls[2])
        es = [jnp.exp(v - m) for v in ls]
        inv = 1.0 / (es[0] + es[1] + es[2])
        for gi, o_ref in enumerate((o0, o1, o2)):
            out_ref[:, gi * w:(gi + 1) * w] = (o_ref[...] * (es[gi] * inv)).astype(out_ref.dtype)

    blk = pl.BlockSpec((bm, w), lambda i: (i, 0))
    return pl.pallas_call(
        body, name="dil_mix_fwd", grid=(s // bm,), in_specs=[blk] * 6,
        out_specs=pl.BlockSpec((bm, 3 * w), lambda i: (i, 0)),
        out_shape=jax.ShapeDtypeStruct((s, 3 * w), BF16), compiler_params=_cp(("parallel",)),
    )(*os_, *lses)


def _dil_mix_bwd(os_, lses, dmixed):
    s, w = os_[0].shape
    bm = _tile(s, (256, 128, 8))
    hg = w // LANES

    def body(o0, o1, o2, l0, l1, l2, dm_ref, do0, do1, do2, dl0, dl1, dl2):
        ls = [l0[...], l1[...], l2[...]]
        m = jnp.maximum(jnp.maximum(ls[0], ls[1]), ls[2])
        es = [jnp.exp(v - m) for v in ls]
        inv = 1.0 / (es[0] + es[1] + es[2])
        alphas = [e * inv for e in es]
        dalphas = []
        for gi, (o_ref, do_ref) in enumerate(((o0, do0), (o1, do1), (o2, do2))):
            dm = dm_ref[:, gi * w:(gi + 1) * w].astype(F32)
            do_ref[...] = (dm * alphas[gi]).astype(do_ref.dtype)
            prod = dm * o_ref[...]
            parts = [jnp.broadcast_to(jnp.sum(prod[:, j * LANES:(j + 1) * LANES], axis=1, keepdims=True), (bm, LANES))
                     for j in range(hg)]
            dalphas.append(jnp.concatenate(parts, axis=1) if hg > 1 else parts[0])
        mean = alphas[0] * dalphas[0] + alphas[1] * dalphas[1] + alphas[2] * dalphas[2]
        for gi, dl_ref in enumerate((dl0, dl1, dl2)):
            dl_ref[...] = alphas[gi] * (dalphas[gi] - mean)

    blk = pl.BlockSpec((bm, w), lambda i: (i, 0))
    return pl.pallas_call(
        body, name="dil_mix_bwd", grid=(s // bm,), in_specs=[blk] * 6 + [pl.BlockSpec((bm, 3 * w), lambda i: (i, 0))],
        out_specs=[blk] * 6,
        out_shape=[jax.ShapeDtypeStruct((s, w), BF16)] * 3 + [jax.ShapeDtypeStruct((s, w), F32)] * 3,
        compiler_params=_cp(("parallel",)),
    )(*os_, *lses, dmixed)


def _xattn_fwd(q, kv):
    s, w = q.shape
    mlen = kv.shape[0]
    tq = _tile(s, (512, 256, 128))
    scale = LANES ** -0.5

    def body(q_ref, k_ref, v_ref, o_ref):
        for h in range(XA_HEADS):
            sl = slice(h * LANES, (h + 1) * LANES)
            sc = lax.dot_general(q_ref[:, sl], k_ref[:, sl], NT, preferred_element_type=F32) * scale
            m = jnp.max(sc, axis=1, keepdims=True)
            e = jnp.exp(sc - m)
            p = e / jnp.sum(e, axis=1, keepdims=True)
            o_ref[:, sl] = jnp.dot(p.astype(BF16), v_ref[:, sl], preferred_element_type=F32).astype(o_ref.dtype)

    return pl.pallas_call(
        body, name="xattn_fwd", grid=(s // tq,),
        in_specs=[pl.BlockSpec((tq, w), lambda i: (i, 0)), pl.BlockSpec((mlen, w), lambda i: (0, 0)),
                  pl.BlockSpec((mlen, w), lambda i: (0, 1))],
        out_specs=pl.BlockSpec((tq, w), lambda i: (i, 0)),
        out_shape=jax.ShapeDtypeStruct((s, w), BF16), compiler_params=_cp(("parallel",)),
    )(q, kv, kv)


def _xattn_bwd(q, kv, do):
    s, w = q.shape
    mlen = kv.shape[0]
    tq = _tile(s, (512, 256, 128))
    scale = LANES ** -0.5

    def body(q_ref, k_ref, v_ref, do_ref, dq_ref, dk_ref, dv_ref):
        @pl.when(pl.program_id(0) == 0)
        def _():
            dk_ref[...] = jnp.zeros_like(dk_ref)
            dv_ref[...] = jnp.zeros_like(dv_ref)

        for h in range(XA_HEADS):
            sl = slice(h * LANES, (h + 1) * LANES)
            qh, kh, vh, doh = q_ref[:, sl], k_ref[:, sl], v_ref[:, sl], do_ref[:, sl]
            sc = lax.dot_general(qh, kh, NT, preferred_element_type=F32) * scale
            m = jnp.max(sc, axis=1, keepdims=True)
            e = jnp.exp(sc - m)
            p = e / jnp.sum(e, axis=1, keepdims=True)
            dp = lax.dot_general(doh, vh, NT, preferred_element_type=F32)
            ds = (p * (dp - jnp.sum(p * dp, axis=1, keepdims=True)) * scale).astype(BF16)
            dq_ref[:, sl] = jnp.dot(ds, kh, preferred_element_type=F32).astype(dq_ref.dtype)
            dk_ref[:, sl] += lax.dot_general(ds, qh, TN, preferred_element_type=F32)
            dv_ref[:, sl] += lax.dot_general(p.astype(BF16), doh, TN, preferred_element_type=F32)

    row = pl.BlockSpec((tq, w), lambda i: (i, 0))
    acc = pl.BlockSpec((mlen, w), lambda i: (0, 0))
    return pl.pallas_call(
        body, name="xattn_bwd", grid=(s // tq,),
        in_specs=[row, acc, pl.BlockSpec((mlen, w), lambda i: (0, 1)), row],
        out_specs=[row, acc, acc],
        out_shape=[jax.ShapeDtypeStruct((s, w), BF16), jax.ShapeDtypeStruct((mlen, w), F32),
                   jax.ShapeDtypeStruct((mlen, w), F32)],
        compiler_params=_cp(("arbitrary",)),
    )(q, kv, kv, do)


def _adamw(parts, w, m, v, name, ex=None):
    r, c = w.shape
    npc = len(parts)
    rp = r // npc
    row_bytes = c * (2 * N_DEV * npc * parts[0].dtype.itemsize + 2 * 7 * 4)
    br = _tile(rp, tuple(p for p in (256, 128, 64, 32, 16, 8) if p * c * 4 <= 1024 * 1024 and p * row_bytes <= MM_VMEM_BUDGET))
    steps = rp // br
    jobs = ex.take(r * c * ADAMW_EXCHANGE_BYTES_PER_PARAM) if ex is not None else []
    nj = len(jobs)

    def body(*refs):
        w_ref, m_ref, v_ref = refs[npc:npc + 3]
        g_ref, d_ref, nm_ref, nv_ref = refs[npc + 3 + nj:npc + 7 + nj]
        if nj:
            job_refs = (refs[npc + 3:npc + 3 + nj], refs[npc + 7 + nj:npc + 7 + 2 * nj], refs[npc + 7 + 2 * nj:])
            p_id, i_id = pl.program_id(0), pl.program_id(1)
            pl.when((p_id == 0) & (i_id == 0))(lambda: ex.start(jobs, *job_refs))
            pl.when((p_id == npc - 1) & (i_id == steps - 1))(lambda: ex.wait(jobs, *job_refs))

        def update(p_ref):
            g = p_ref[0].astype(F32)
            for t in range(1, N_DEV):
                g = g + p_ref[t].astype(F32)
            nm = ADAM_B1 * m_ref[...] + (1.0 - ADAM_B1) * g
            nv = ADAM_B2 * v_ref[...] + (1.0 - ADAM_B2) * (g * g)
            m_hat = nm / (1.0 - ADAM_B1 ** ADAM_STEP)
            v_hat = nv / (1.0 - ADAM_B2 ** ADAM_STEP)
            g_ref[...] = g
            d_ref[...] = -ADAM_LR * (m_hat / (jnp.sqrt(v_hat) + ADAM_EPS) + ADAM_WD * w_ref[...])
            nm_ref[...] = nm
            nv_ref[...] = nv

        for k in range(npc):
            pl.when(pl.program_id(0) == k)(lambda k=k: update(refs[k]))

    blk = pl.BlockSpec((br, c), lambda p, i: (p * steps + i, 0))
    part_specs = [pl.BlockSpec((N_DEV, br, c), lambda p, i, k=k: (0, jnp.where(p == k, i, 0), 0)) for k in range(npc)]
    outs = pl.pallas_call(
        body, name=name + "_carry" if nj else name, grid=(npc, steps),
        in_specs=part_specs + [blk, blk, blk] + [ANY] * nj, out_specs=[blk] * 4 + [ANY] * nj,
        out_shape=[jax.ShapeDtypeStruct((r, c), F32)] * 4 + [j.recv_shape for j in jobs],
        scratch_shapes=_piece_sems(nj) if nj else [],
        compiler_params=_cp(("arbitrary", "arbitrary") if nj else ("parallel", "parallel")),
    )(*parts, w, m, v, *[j.g for j in jobs])
    if nj:
        ex.landed(jobs, outs[4:])
    return outs[:4]


MESH_ID = pl.DeviceIdType.MESH
ANY = pl.BlockSpec(memory_space=pl.ANY)


def _my_place():
    return lax.axis_index("x"), lax.axis_index("y"), lax.axis_index("c")


def _all_gather(xs, name):
    nt = len(xs)

    def body(*refs):
        x_refs, out_refs = refs[:nt], refs[nt:2 * nt]
        send_sems, recv_sems, local_sems = refs[2 * nt:]
        x, y, c = _my_place()
        me, sibling = (x, y, c), (x, y, 1 - c)
        chips = [(1 - x, y), (x, 1 - y), (1 - x, 1 - y)]

        def slot(t, p):
            return out_refs[t].at[4 * p[0] + 2 * p[1] + p[2]]

        def copy(t, k, block, to, src=None):
            return pltpu.make_async_remote_copy(
                src_ref=slot(t, block) if src is None else src, dst_ref=slot(t, block),
                send_sem=send_sems.at[7 * t + k], recv_sem=recv_sems.at[7 * t + k], device_id=to,
                device_id_type=MESH_ID)

        mine, first, passed = [], [], []
        for t in range(nt):
            cp = pltpu.make_async_copy(x_refs[t], slot(t, me), local_sems.at[t])
            cp.start()
            mine.append(cp)
            group = [copy(t, 0, me, sibling, src=x_refs[t])]
            group += [copy(t, 1 + j, me, (*chip, c), src=x_refs[t]) for j, chip in enumerate(chips)]
            for cp in group:
                cp.start()
            first += group
        for t in range(nt):
            for j, chip in enumerate(chips):
                copy(t, 1 + j, (*chip, c), me).wait_recv()
                fw = copy(t, 4 + j, (*chip, c), sibling)
                fw.start()
                passed.append(fw)
        for t in range(nt):
            copy(t, 0, sibling, me).wait_recv()
            for j, chip in enumerate(chips):
                copy(t, 4 + j, (*chip, 1 - c), me).wait_recv()
        for cp in first + passed:
            cp.wait_send()
        for cp in mine:
            cp.wait()

    return pl.pallas_call(
        body, name=name, in_specs=[ANY] * nt, out_specs=[ANY] * nt,
        out_shape=[jax.ShapeDtypeStruct((N_DEV,) + tuple(v.shape), v.dtype) for v in xs],
        scratch_shapes=[pltpu.SemaphoreType.DMA((7 * nt,)), pltpu.SemaphoreType.DMA((7 * nt,)),
                        pltpu.SemaphoreType.DMA((nt,))],
    )(*xs)


def _exchange(gs, name):
    nt = len(gs)

    def body(*refs):
        g_refs, out_refs = refs[:nt], refs[nt:2 * nt]
        send_sems, recv_sems, local_sems = refs[2 * nt:]
        x, y, c = _my_place()
        my_slot = 4 * x + 2 * y + c
        mine, sent = [], []
        for t in range(nt):
            cp = pltpu.make_async_copy(g_refs[t].at[my_slot], out_refs[t].at[my_slot], local_sems.at[t])
            cp.start()
            mine.append(cp)
            for rel in range(1, N_DEV):
                px, py, pc = x ^ ((rel >> 2) & 1), y ^ ((rel >> 1) & 1), c ^ (rel & 1)
                cp = pltpu.make_async_remote_copy(
                    src_ref=g_refs[t].at[4 * px + 2 * py + pc], dst_ref=out_refs[t].at[my_slot],
                    send_sem=send_sems.at[7 * t + rel - 1], recv_sem=recv_sems.at[7 * t + rel - 1],
                    device_id=(px, py, pc), device_id_type=MESH_ID)
                cp.start()
                sent.append(cp)
        for cp in sent:
            cp.wait_recv()
        for cp in sent:
            cp.wait_send()
        for cp in mine:
            cp.wait()

    return pl.pallas_call(
        body, name=name, in_specs=[ANY] * nt, out_specs=[ANY] * nt,
        out_shape=[jax.ShapeDtypeStruct(tuple(v.shape), v.dtype) for v in gs],
        scratch_shapes=[pltpu.SemaphoreType.DMA((7 * nt,)), pltpu.SemaphoreType.DMA((7 * nt,)),
                        pltpu.SemaphoreType.DMA((nt,))],
    )(*gs)


EXCHANGE_BYTES_PER_FLOP = 1.1e-4
CARRIER_OVERFILL = 1.15
ADAMW_EXCHANGE_BYTES_PER_PARAM = 2.0
PIECE_BYTES = 8 * 1024 * 1024
CARRIER_MIN_BYTES = 6 * 1024 * 1024
ROW_ALIGN = 16
BLOCKS = -1


class _Piece:
    def __init__(self, key, g, axis, lo, hi):
        self.key, self.g, self.axis, self.lo, self.hi = key, g, axis, lo, hi
        cols = g.shape[2] if axis == BLOCKS else g.shape[1] if axis == 0 else g.shape[1] // N_DEV
        self.recv_shape = jax.ShapeDtypeStruct((N_DEV, hi - lo, cols), g.dtype)
        self.nbytes = N_DEV * (hi - lo) * cols * g.dtype.itemsize


def _piece_sems(nj):
    return [pltpu.SemaphoreType.DMA((7 * nj,)), pltpu.SemaphoreType.DMA((7 * nj,)), pltpu.SemaphoreType.DMA((nj,))]


def _piece_copies(jobs, g_refs, recv_refs, sems):
    send_sems, recv_sems, local_sems = sems
    x, y, c = _my_place()
    me = 4 * x + 2 * y + c
    local, remote = [], []
    for t, (job, g, r) in enumerate(zip(jobs, g_refs, recv_refs)):
        rows = job.hi - job.lo

        def block(slot, job=job, g=g, r=r, rows=rows):
            if job.axis == BLOCKS:
                return g.at[slot, pl.ds(job.lo, rows), :]
            if job.axis == 0:
                start = pl.multiple_of(slot * (g.shape[0] // N_DEV) + job.lo, ROW_ALIGN)
                return g.at[pl.ds(start, rows), :]
            cols = r.shape[2]
            return g.at[pl.ds(job.lo, rows), pl.ds(pl.multiple_of(slot * cols, LANES), cols)]

        local.append(pltpu.make_async_copy(block(me), r.at[me], local_sems.at[t]))
        for rel in range(1, N_DEV):
            px, py, pc = x ^ ((rel >> 2) & 1), y ^ ((rel >> 1) & 1), c ^ (rel & 1)
            remote.append(pltpu.make_async_remote_copy(
                src_ref=block(4 * px + 2 * py + pc), dst_ref=r.at[me], send_sem=send_sems.at[7 * t + rel - 1],
                recv_sem=recv_sems.at[7 * t + rel - 1], device_id=(px, py, pc), device_id_type=MESH_ID))
    return local, remote


def _pieces_start(jobs, g_refs, recv_refs, sems):
    local, remote = _piece_copies(jobs, g_refs, recv_refs, sems)
    for cp in local + remote:
        cp.start()


def _pieces_wait(jobs, g_refs, recv_refs, sems):
    local, remote = _piece_copies(jobs, g_refs, recv_refs, sems)
    for cp in remote:
        cp.wait_recv()
    for cp in remote:
        cp.wait_send()
    for cp in local:
        cp.wait()


def _exchange_pieces(jobs, name):
    nj = len(jobs)

    def body(*refs):
        job_refs = (refs[:nj], refs[nj:2 * nj], refs[2 * nj:])
        _pieces_start(jobs, *job_refs)
        _pieces_wait(jobs, *job_refs)

    return pl.pallas_call(
        body, name=name, in_specs=[ANY] * nj, out_specs=[ANY] * nj, out_shape=[j.recv_shape for j in jobs],
        scratch_shapes=_piece_sems(nj),
    )(*[j.g for j in jobs])


class _GradExchange:
    def __init__(self):
        self.queue, self.recv = [], {}

    def put(self, name, layer, g):
        axis = SHARD_AXIS[name] - 1
        rows = g.shape[0] // N_DEV if axis == 0 else g.shape[0]
        if g.dtype != BF16 or g.ndim != 2:
            return False
        if rows % ROW_ALIGN or (axis == 1 and (g.shape[1] // N_DEV) % LANES):
            g = _to_blocks(g, axis)
            axis, rows = BLOCKS, g.shape[1]
        n_split = max(1, round(g.size * g.dtype.itemsize / PIECE_BYTES))
        while rows % (n_split * ROW_ALIGN):
            n_split -= 1
        for k in range(n_split):
            self.queue.append(_Piece((name, layer, k), g, axis, k * rows // n_split, (k + 1) * rows // n_split))
        return True

    def take(self, capacity):
        jobs, used = [], 0
        while capacity >= CARRIER_MIN_BYTES and self.queue and used + self.queue[0].nbytes <= CARRIER_OVERFILL * capacity:
            used += self.queue[0].nbytes
            jobs.append(self.queue.pop(0))
        return jobs

    def landed(self, jobs, recvs):
        for j, r in zip(jobs, recvs):
            self.recv[j.key] = r

    def flush(self):
        if self.queue:
            jobs, self.queue = self.queue, []
            self.landed(jobs, _exchange_pieces(jobs, "exchange_rest"))

    def pieces_of(self, name):
        return [self.recv[k] for k in sorted(k for k in self.recv if k[0] == name)]

    start = staticmethod(_pieces_start)
    wait = staticmethod(_pieces_wait)


GATHER_SPEEDUP = 2.0
GATHER_FIRST_OVERFILL = 2.5


class _WeightPiece:
    def __init__(self, key, local, axis):
        self.key, self.g, self.axis = key, local, axis
        r, c = local.shape
        self.recv_shape = jax.ShapeDtypeStruct(
            (N_DEV, r, c) if axis == BLOCKS else (r * N_DEV, c) if axis == 0 else (r, c * N_DEV), local.dtype)
        self.nbytes = N_DEV * r * c * local.dtype.itemsize


def _gather_copies(jobs, x_refs, full_refs, sems):
    send_sems, recv_sems, local_sems = sems
    x, y, c = _my_place()
    me, sibling = (x, y, c), (x, y, 1 - c)
    chips = [(1 - x, y), (x, 1 - y), (1 - x, 1 - y)]
    plans = []
    for t, (job, xr, fr) in enumerate(zip(jobs, x_refs, full_refs)):
        def blk(p, job=job, xr=xr, fr=fr):
            slot = 4 * p[0] + 2 * p[1] + p[2]
            if job.axis == BLOCKS:
                return fr.at[slot]
            if job.axis == 0:
                return fr.at[pl.ds(pl.multiple_of(slot * xr.shape[0], ROW_ALIGN), xr.shape[0]), :]
            return fr.at[:, pl.ds(pl.multiple_of(slot * xr.shape[1], LANES), xr.shape[1])]

        def copy(k, block, to, src=None, t=t, blk=blk):
            return pltpu.make_async_remote_copy(
                src_ref=blk(block) if src is None else src, dst_ref=blk(block), send_sem=send_sems.at[7 * t + k],
                recv_sem=recv_sems.at[7 * t + k], device_id=to, device_id_type=MESH_ID)

        plans.append(dict(
            mine=pltpu.make_async_copy(xr, blk(me), local_sems.at[t]),
            first=[copy(0, me, sibling, src=xr)] + [copy(1 + j, me, (*chip, c), src=xr) for j, chip in enumerate(chips)],
            landed=[copy(1 + j, (*chip, c), me) for j, chip in enumerate(chips)],
            passed=[copy(4 + j, (*chip, c), sibling) for j, chip in enumerate(chips)],
            from_sibling=[copy(0, sibling, me)] + [copy(4 + j, (*chip, 1 - c), me) for j, chip in enumerate(chips)]))
    return plans


def _gather_start(jobs, x_refs, full_refs, sems):
    for plan in _gather_copies(jobs, x_refs, full_refs, sems):
        plan["mine"].start()
        for cp in plan["first"]:
            cp.start()


def _gather_wait(jobs, x_refs, full_refs, sems):
    plans = _gather_copies(jobs, x_refs, full_refs, sems)
    for plan in plans:
        for landed, passed in zip(plan["landed"], plan["passed"]):
            landed.wait_recv()
            passed.start()
    for plan in plans:
        for cp in plan["from_sibling"]:
            cp.wait_recv()
        for cp in plan["first"] + plan["passed"]:
            cp.wait_send()
        plan["mine"].wait()


def _gather_pieces(jobs, name):
    nj = len(jobs)

    def body(*refs):
        job_refs = (refs[:nj], refs[nj:2 * nj], refs[2 * nj:])
        _gather_start(jobs, *job_refs)
        _gather_wait(jobs, *job_refs)

    return pl.pallas_call(
        body, name=name, in_specs=[ANY] * nj, out_specs=[ANY] * nj, out_shape=[j.recv_shape for j in jobs],
        scratch_shapes=_piece_sems(nj),
    )(*[j.g for j in jobs])


class _WeightGather:
    def __init__(self):
        self.queue, self.full = [], {}

    def add(self, name, layer, local, in_place):
        self.queue.append(_WeightPiece((name, layer), local, SHARD_AXIS[name] - 1 if in_place else BLOCKS))

    def take(self, capacity):
        capacity *= GATHER_SPEEDUP
        jobs, used = [], 0
        if capacity >= CARRIER_MIN_BYTES and self.queue and self.queue[0].nbytes <= GATHER_FIRST_OVERFILL * capacity:
            jobs.append(self.queue.pop(0))
            used = jobs[0].nbytes
            while self.queue and used + self.queue[0].nbytes <= CARRIER_OVERFILL * capacity:
                used += self.queue[0].nbytes
                jobs.append(self.queue.pop(0))
        return jobs

    def landed(self, jobs, fulls):
        for j, f in zip(jobs, fulls):
            self.full[j.key] = _to_full(f, SHARD_AXIS[j.key[0]] - 1) if j.axis == BLOCKS else f

    def get(self, name, layer):
        if (name, layer) not in self.full:
            at = [j.key for j in self.queue].index((name, layer))
            jobs, self.queue = self.queue[:at + 1], self.queue[at + 1:]
            self.landed(jobs, _gather_pieces(jobs, "gather_now"))
        return self.full[(name, layer)]

    start = staticmethod(_gather_start)
    wait = staticmethod(_gather_wait)


def _val(w):
    return w() if callable(w) else w


def _ffn_fwd(x, g, w_gu, w_d, tag, wg=None):
    n = _rms_fwd(x, g)
    gu = _mm(n, _val(w_gu), out_dtype=BF16, name=f"{tag}_gu", ex=wg)
    a = _swiglu_fwd(gu)
    return _mm(a, _val(w_d), alpha=0.5, res=x, name=f"{tag}_down", ex=wg), (n, gu, a)


def _put(ex, name, layer, g):
    if ex is not None:
        ex.put(name, layer, g)


def _ffn_bwd(dy, x, g, w_gu, w_d, saved, tag, ex=None, which="ffn1", layer=0):
    n, gu, a = saved
    da = _mm(dy, w_d, tb=True, alpha=0.5, out_dtype=BF16, name=f"{tag}_da", ex=ex)
    dgu = _swiglu_bwd(gu, da)
    d_wgu = _mm(n, dgu, ta=True, out_dtype=BF16, name=f"{tag}_dwgu", ex=ex)
    _put(ex, f"{which}_w_gate_up", layer, d_wgu)
    d_wd = _mm(a, dy, ta=True, alpha=0.5, out_dtype=BF16, name=f"{tag}_dwd", ex=ex)
    _put(ex, f"{which}_w_down", layer, d_wd)
    dn = _mm(dgu, w_gu, tb=True, name=f"{tag}_dn", ex=ex)
    dx, dg = _rms_bwd(x, g, dn, dy)
    return dx, dg, d_wgu, d_wd


def _sb_mixer_fwd(h, w_qkv, w_o, x, wg=None):
    qkv = _mm(h, _val(w_qkv), out_dtype=BF16, name="sb_qkv", ex=wg)
    o, ltot = _sb_fwd(qkv, qkv.shape[1] // (3 * LANES), wg)
    return _mm(o, _val(w_o), res=x, name="sb_out", ex=wg), (qkv, o, ltot)


def _sb_mixer_bwd(dy, h, w_qkv, w_o, saved, ex=None, layer=0):
    qkv, o, ltot = saved
    n_heads = w_o.shape[0] // LANES
    do = _mm(dy, w_o, tb=True, out_dtype=BF16, name="sb_do", ex=ex)
    d_wo = _mm(o, dy, ta=True, out_dtype=BF16, name="sb_dwo", ex=ex)
    _put(ex, "sb_w_o", layer, d_wo)
    dq, dk, dv = _sb_bwd(qkv, do, ltot, n_heads, ex)
    dqkv = jnp.concatenate([dq, dk.astype(BF16), dv.astype(BF16)], axis=1)
    d_wqkv = _mm(h, dqkv, ta=True, out_dtype=BF16, name="sb_dwqkv", ex=ex)
    _put(ex, "sb_w_qkv", layer, d_wqkv)
    dh = _mm(dqkv, w_qkv, tb=True, name="sb_dh", ex=ex)
    return dh, d_wqkv, d_wo


def _dil_cols(gi):
    ng = len(DIL_PATTERNS)
    return dict(q_blk=lambda n, st: n * 2 * ng + gi, k_blk=lambda n, st: n * 2 * ng + ng + gi,
                v_blk=lambda n, st: n * 3 * ng + 2 * ng + gi)


def _dil_mixer_fwd(h, w_qkv, w_o, x, tabs, wg=None):
    s = h.shape[0]
    qkv = _mm(h, _val(w_qkv), name="dil_qkv", ex=wg)
    n_all = qkv.shape[1] // (3 * LANES)
    hg = n_all // len(DIL_PATTERNS)
    qk = _rope(qkv, tabs[0], tabs[1], 2 * n_all, 16, "dil_rope")
    os_, lses = [], []
    for gi, (window, dil) in enumerate(DIL_PATTERNS):
        o, lse = _band_fwd(qk.reshape(s // dil, -1), qk.reshape(s // dil, -1), qkv.reshape(s // dil, -1),
                           n_cls=dil, n_steps=1, hpb=hg, group=1, **_dil_cols(gi),
                           max_dist=window // dil, scale=LANES ** -0.5, sinks=None, name=f"dil_fwd{gi}")
        os_.append(o.reshape(s, hg * LANES))
        lses.append(lse.reshape(s, hg * LANES))
    mixed = _dil_mix_fwd(os_, lses)
    return _mm(mixed, _val(w_o), res=x, name="dil_out", ex=wg), (qkv, qk, os_, lses, mixed)


def _dil_mixer_bwd(dy, h, w_qkv, w_o, saved, tabs_bwd, ex=None):
    qkv, qk, os_, lses, mixed = saved
    s = h.shape[0]
    n_all = w_o.shape[0] // LANES
    hg = n_all // len(DIL_PATTERNS)
    dmixed = _mm(dy, w_o, tb=True, out_dtype=BF16, name="dil_dmix", ex=ex)
    d_wo = _mm(mixed, dy, ta=True, out_dtype=BF16, name="dil_dwo", ex=ex)
    _put(ex, "dil_w_o", 0, d_wo)
    mix_out = _dil_mix_bwd(os_, lses, dmixed)
    dos, dlses = mix_out[:3], mix_out[3:]
    dqs, dks, dvs = [], [], []
    for gi, (window, dil) in enumerate(DIL_PATTERNS):
        length = s // dil
        dq, dkc, dkp, dvc, dvp = _band_bwd(
            qk.reshape(length, -1), qk.reshape(length, -1), qkv.reshape(length, -1), os_[gi].reshape(length, -1),
            dos[gi].reshape(length, -1), lses[gi].reshape(length, -1), dlses[gi].reshape(length, -1),
            n_cls=dil, n_steps=1, hpb=hg, group=1, **_dil_cols(gi), max_dist=window // dil,
            scale=LANES ** -0.5, sinks=None, name=f"dil_bwd{gi}")
        dqs.append(dq.reshape(s, -1))
        dks.append(_band_fold(dkc, dkp, n_cls=dil, n_heads=hg, group=1, name=f"dil_foldk{gi}").reshape(s, -1))
        dvs.append(_band_fold(dvc, dvp, n_cls=dil, n_heads=hg, group=1, name=f"dil_foldv{gi}").reshape(s, -1))
    dqk_rot = jnp.concatenate(dqs + dks, axis=1)
    dqk = _rope(dqk_rot, tabs_bwd[0], tabs_bwd[1], 2 * n_all, 16, "dil_rope_bwd")
    dqkv = jnp.concatenate([dqk] + [t.astype(BF16) for t in dvs], axis=1)
    dh = _mm(dqkv, w_qkv, tb=True, name="dil_dh", ex=ex)
    d_wqkv = _mm(h, dqkv, ta=True, out_dtype=BF16, name="dil_dwqkv", ex=ex)
    _put(ex, "dil_w_qkv", 0, d_wqkv)
    return dh, d_wqkv, d_wo


def _pad_heads(w, axis):
    shape = list(w.shape)
    n = shape[axis] // SWA_HEAD_DIM
    w = w.reshape(shape[:axis] + [n, SWA_HEAD_DIM] + shape[axis + 1:])
    pad = [(0, 0)] * w.ndim
    pad[axis + 1] = (0, LANES - SWA_HEAD_DIM)
    shape[axis] = n * LANES
    return jnp.pad(w, pad).reshape(shape)


def _unpad_heads(w, axis):
    shape = list(w.shape)
    n = shape[axis] // LANES
    w = w.reshape(shape[:axis] + [n, LANES] + shape[axis + 1:])
    w = lax.slice_in_dim(w, 0, SWA_HEAD_DIM, axis=axis + 1)
    shape[axis] = n * SWA_HEAD_DIM
    return w.reshape(shape)


def _swa_mixer_fwd(h, w_qkv_p, b_qkv_p, sinks_b, w_o_p, b_o, x, tabs, wg=None):
    nq = w_o_p.shape[0] // LANES
    nkv = nq // SWA_GROUP
    qkv = _mm(h, w_qkv_p, bias=b_qkv_p, name="swa_qkv", ex=wg)
    qk = _rope(qkv, tabs[0], tabs[1], nq + nkv, 8, "swa_rope")
    o, lse = _band_fwd(qk, qk, qkv, n_cls=1, n_steps=nkv, hpb=SWA_GROUP, group=SWA_GROUP, q_blk=lambda n, st: st,
                       k_blk=lambda n, st: nq + st, v_blk=lambda n, st: nq + nkv + st,
                       max_dist=SWA_WINDOW - 1, scale=SWA_HEAD_DIM ** -0.5, sinks=sinks_b, name="swa_fwd")
    return _mm(o, w_o_p, res=x, bias=b_o, name="swa_out"), (qkv, qk, o, lse)


def _swa_mixer_bwd(dy, h, w_qkv_p, sinks_b, w_o_p, saved, tabs_bwd, ex=None):
    qkv, qk, o, lse = saved
    nq = w_o_p.shape[0] // LANES
    nkv = nq // SWA_GROUP
    do = _mm(dy, w_o_p, tb=True, name="swa_do", ex=ex)
    d_wo_p = _mm(o, dy, ta=True, out_dtype=BF16, name="swa_dwo", ex=ex)
    d_bo = _colsum(dy, "swa_dbo")
    dq, dkc, dkp, dvc, dvp, dsink = _band_bwd(
        qk, qk, qkv, o, do, lse, None, n_cls=1, n_steps=nkv, hpb=SWA_GROUP, group=SWA_GROUP, q_blk=lambda n, st: st,
        k_blk=lambda n, st: nq + st, v_blk=lambda n, st: nq + nkv + st, max_dist=SWA_WINDOW - 1,
        scale=SWA_HEAD_DIM ** -0.5, sinks=sinks_b, name="swa_bwd")
    dk = _band_fold(dkc, dkp, n_cls=1, n_heads=nkv, group=1, name="swa_foldk")
    dv = _band_fold(dvc, dvp, n_cls=1, n_heads=nkv, group=1, name="swa_foldv")
    dqk = _rope(jnp.concatenate([dq, dk], axis=1), tabs_bwd[0], tabs_bwd[1], nq + nkv, 8, "swa_rope_bwd")
    dqkv = jnp.concatenate([dqk, dv.astype(BF16)], axis=1)
    d_bqkv_p = _colsum(dqkv, "swa_dbqkv")
    dh = _mm(dqkv, w_qkv_p, tb=True, name="swa_dh", ex=ex)
    d_wqkv_p = _mm(h, dqkv, ta=True, out_dtype=BF16, name="swa_dwqkv", ex=ex)
    return dh, d_wqkv_p, d_bqkv_p, dsink, d_wo_p, d_bo


def _xattn_layer_fwd(x, mem, g_x, g_m, w_q, w_kv, w_o):
    hq = _rms_fwd(x, g_x, "rms_fwd")
    hm = _rms_fwd(mem, g_m, "rms_mem_fwd")
    q = _mm(hq, _val(w_q), out_dtype=BF16, name="xa_q")
    kv = _mm(hm, _val(w_kv), out_dtype=BF16, name="xa_kv")
    o = _xattn_fwd(q, kv)
    return _mm(o, _val(w_o), res=x, name="xa_out"), (hq, hm, q, kv, o)


def _xattn_layer_bwd(dy, x, mem, g_x, g_m, w_q, w_kv, w_o, saved):
    hq, hm, q, kv, o = saved
    do = _mm(dy, w_o, tb=True, out_dtype=BF16, name="xa_do")
    d_wo = _mm(o, dy, ta=True, out_dtype=BF16, name="xa_dwo")
    dq, dk, dv = _xattn_bwd(q, kv, do)
    dkv = jnp.concatenate([dk, dv], axis=1).astype(BF16)
    dhq = _mm(dq, w_q, tb=True, name="xa_dhq")
    d_wq = _mm(hq, dq, ta=True, out_dtype=BF16, name="xa_dwq")
    dhm = _mm(dkv, w_kv, tb=True, name="xa_dhm")
    d_wkv = _mm(hm, dkv, ta=True, out_dtype=BF16, name="xa_dwkv")
    dx, dg_x = _rms_bwd(x, g_x, dhq, dy)
    _, dg_m = _rms_bwd(mem, g_m, dhm, None, "rms_mem_bwd")
    return dx, dg_x, dg_m, d_wq, d_wkv, d_wo


def _to_full(gathered, axis):
    t = jnp.moveaxis(gathered, 0, axis)
    shape = list(t.shape)
    return t.reshape(shape[:axis] + [shape[axis] * shape[axis + 1]] + shape[axis + 2:])


def _to_blocks(full, axis):
    shape = list(full.shape)
    t = full.reshape(shape[:axis] + [N_DEV, shape[axis] // N_DEV] + shape[axis + 1:])
    return jnp.moveaxis(t, axis, 0)


SHARD_AXIS = {
    "ffn1_w_gate_up": 2, "ffn1_w_down": 1, "sb_w_qkv": 2, "sb_w_o": 1, "dil_w_qkv": 2, "dil_w_o": 2,
    "swa_w_qkv": 2, "swa_b_qkv": 1, "swa_w_o": 1, "swa_b_o": 1, "xattn_w_q": 1, "xattn_w_kv": 1, "xattn_w_o": 2,
    "ffn2_w_gate_up": 2, "ffn2_w_down": 1,
}
SMALL = ("ffn1_norm", "mix_norm", "xattn_norm", "mem_norm", "ffn2_norm", "final_norm", "swa_sinks")
WEIGHTS = ("ffn1_norm", "ffn1_w_gate_up", "ffn1_w_down", "mix_norm", "sb_w_qkv", "sb_w_o", "dil_w_qkv", "dil_w_o",
           "swa_w_qkv", "swa_b_qkv", "swa_sinks", "swa_w_o", "swa_b_o", "xattn_norm", "mem_norm", "xattn_w_q",
           "xattn_w_kv", "xattn_w_o", "ffn2_norm", "ffn2_w_gate_up", "ffn2_w_down", "final_norm")


def _flat2(a):
    return a.reshape(-1, a.shape[-1])


def _pack_small(vals, d):
    rows = [vals[n].reshape(-1, d) for n in SMALL[:5]] + [vals["final_norm"].reshape(1, d)]
    sk = vals["swa_sinks"].reshape(1, -1)
    rows.append(jnp.pad(sk, ((0, 0), (0, d - sk.shape[1]))))
    rows.append(jnp.zeros((2, d), F32))
    return jnp.concatenate(rows, axis=0)


def _unpack_small(packed, like):
    out, r = {}, 0
    for n in SMALL[:5]:
        k = like[n].shape[0]
        out[n] = packed[r:r + k]
        r += k
    out["final_norm"] = packed[r]
    out["swa_sinks"] = packed[r + 1:r + 2, :like["swa_sinks"].shape[1]]
    return out


def _local_step(x0, mem0, positions, target, full, norm, ex=None, wg=None):
    d = x0.shape[1]
    names = list(SHARD_AXIS)
    sinks_b = jnp.repeat(norm["swa_sinks"], LANES, axis=1)
    tabs_dil, tabs_dil_bwd = _rope_tables(positions, 32), _rope_tables(positions, 32, -1.0)
    tabs_swa, tabs_swa_bwd = _rope_tables(positions, 16), _rope_tables(positions, 16, -1.0)

    def vec(name, i):
        return norm[name][i:i + 1]

    saved = []
    xc = x0
    for i in range(DEPTH):
        kind, j = i % 3, i // 3
        rec = {"x0": xc}
        xc, rec["ffn1"] = _ffn_fwd(xc, vec("ffn1_norm", i), lambda: full["ffn1_w_gate_up"][i],
                                   lambda: full["ffn1_w_down"][i], "ffn", wg)
        rec["x1"] = xc
        h = _rms_fwd(xc, vec("mix_norm", i))
        rec["h"] = h
        if kind == 0:
            xc, rec["mix"] = _sb_mixer_fwd(h, lambda: full["sb_w_qkv"][j], lambda: full["sb_w_o"][j], xc, wg)
        elif kind == 1:
            xc, rec["mix"] = _dil_mixer_fwd(h, lambda: full["dil_w_qkv"][j], lambda: full["dil_w_o"][j], xc, tabs_dil, wg)
        else:
            swa_w_qkv_p = _pad_heads(full["swa_w_qkv"][0], 1)
            swa_b_qkv_p = _pad_heads(full["swa_b_qkv"][0][None], 1)
            swa_w_o_p = _pad_heads(full["swa_w_o"][0], 0)
            xc, rec["mix"] = _swa_mixer_fwd(h, swa_w_qkv_p, swa_b_qkv_p, sinks_b, swa_w_o_p, full["swa_b_o"][0][None],
                                            xc, tabs_swa, wg)
        rec["x2"] = xc
        xc, rec["xa"] = _xattn_layer_fwd(xc, mem0, vec("xattn_norm", i), vec("mem_norm", i),
                                         lambda: full["xattn_w_q"][i], lambda: full["xattn_w_kv"][i],
                                         lambda: full["xattn_w_o"][i])
        rec["x3"] = xc
        xc, rec["ffn2"] = _ffn_fwd(xc, vec("ffn2_norm", i), lambda: full["ffn2_w_gate_up"][i],
                                   lambda: full["ffn2_w_down"][i], "ffn", wg)
        saved.append(rec)

    loss_part, dx, dg_final = _loss_head(xc, norm["final_norm"].reshape(1, d), target)

    gfull = {n: [None] * full[n].shape[0] for n in names}
    gsmall = {n: [None] * DEPTH for n in SMALL[:5]}
    gsmall["final_norm"] = dg_final
    gsmall["swa_sinks"] = jnp.zeros_like(norm["swa_sinks"])
    for i in reversed(range(DEPTH)):
        kind, j = i % 3, i // 3
        rec = saved[i]
        dx, gsmall["ffn2_norm"][i], gfull["ffn2_w_gate_up"][i], gfull["ffn2_w_down"][i] = _ffn_bwd(
            dx, rec["x3"], vec("ffn2_norm", i), full["ffn2_w_gate_up"][i], full["ffn2_w_down"][i], rec["ffn2"], "ffn",
            ex, "ffn2", i)
        (dx, gsmall["xattn_norm"][i], gsmall["mem_norm"][i], gfull["xattn_w_q"][i], gfull["xattn_w_kv"][i],
         gfull["xattn_w_o"][i]) = _xattn_layer_bwd(dx, rec["x2"], mem0, vec("xattn_norm", i), vec("mem_norm", i),
                                                   full["xattn_w_q"][i], full["xattn_w_kv"][i], full["xattn_w_o"][i],
                                                   rec["xa"])
        for n in ("xattn_w_q", "xattn_w_kv", "xattn_w_o"):
            _put(ex, n, i, gfull[n][i])
        if kind == 0:
            dh, gfull["sb_w_qkv"][j], gfull["sb_w_o"][j] = _sb_mixer_bwd(
                dx, rec["h"], full["sb_w_qkv"][j], full["sb_w_o"][j], rec["mix"], ex, j)
        elif kind == 1:
            dh, gfull["dil_w_qkv"][j], gfull["dil_w_o"][j] = _dil_mixer_bwd(
                dx, rec["h"], full["dil_w_qkv"][j], full["dil_w_o"][j], rec["mix"], tabs_dil_bwd, ex)
        else:
            dh, d_wqkv_p, d_bqkv_p, dsink, d_wo_p, d_bo = _swa_mixer_bwd(
                dx, rec["h"], swa_w_qkv_p, sinks_b, swa_w_o_p, rec["mix"], tabs_swa_bwd, ex)
            gfull["swa_w_qkv"][j] = _unpad_heads(d_wqkv_p, 1)
            gfull["swa_b_qkv"][j] = _unpad_heads(d_bqkv_p, 1)[0]
            gfull["swa_w_o"][j] = _unpad_heads(d_wo_p, 0)
            gfull["swa_b_o"][j] = d_bo[0]
            gsmall["swa_sinks"] = dsink.reshape(1, -1, LANES)[:, :, 0]
            _put(ex, "swa_w_o", j, gfull["swa_w_o"][j])
            _put(ex, "swa_w_qkv", j, gfull["swa_w_qkv"][j])
        dx, gsmall["mix_norm"][i] = _rms_bwd(rec["x1"], vec("mix_norm", i), dh, dx)
        dx, gsmall["ffn1_norm"][i], gfull["ffn1_w_gate_up"][i], gfull["ffn1_w_down"][i] = _ffn_bwd(
            dx, rec["x0"], vec("ffn1_norm", i), full["ffn1_w_gate_up"][i], full["ffn1_w_down"][i], rec["ffn1"], "ffn",
            ex, "ffn1", i)
    for n in SMALL[:5]:
        gsmall[n] = jnp.concatenate(gsmall[n], axis=0)
    return loss_part[0, 0], dx, gfull, gsmall


def _train_step(x, mem, positions, loss_target, w, m, v):
    d = x.shape[2]
    names = list(SHARD_AXIS)
    norm = {n: w[n] for n in SMALL}

    def in_place(n):
        return w[n].shape[2] % LANES == 0 if SHARD_AXIS[n] == 2 else w[n].shape[1] % ROW_ALIGN == 0

    first = [n for n in names if w[n].ndim != 3]
    gathered = _all_gather([_flat2(w[n]) for n in first], "gather_weights")
    stacked = {n: _to_full(g.reshape((N_DEV,) + w[n].shape), SHARD_AXIS[n]) for n, g in zip(first, gathered)}
    wg = _WeightGather()
    for i in range(DEPTH):
        mixer = (("sb_w_qkv", "sb_w_o"), ("dil_w_qkv", "dil_w_o"), ("swa_w_qkv", "swa_w_o"))[i % 3]
        for n in ("ffn1_w_gate_up", "ffn1_w_down") + mixer + ("xattn_w_q", "xattn_w_kv", "xattn_w_o",
                                                               "ffn2_w_gate_up", "ffn2_w_down"):
            layer = i // 3 if n in mixer else i
            wg.add(n, layer, w[n][layer].astype(BF16), in_place(n))

    class Layers:
        def __init__(self, n):
            self.n, self.shape = n, w[n].shape[:1]

        def __getitem__(self, layer):
            return wg.get(self.n, layer) if w[self.n].ndim == 3 else stacked[self.n][layer]

    full = {n: Layers(n) for n in names}
    ex = _GradExchange()
    loss_part, dx, gfull, gsmall = _local_step(x[0], mem[0], positions, loss_target[0], full, norm, ex, wg)
    loss = lax.psum(loss_part, MESH_AXES)
    grad_x = dx[None]

    grad, delta, new_m, new_v = {}, {}, {}, {}

    def update(n, parts, carrier):
        outs = _adamw(parts, _flat2(w[n]), _flat2(m[n]), _flat2(v[n]), "adamw", carrier)
        grad[n], delta[n], new_m[n], new_v[n] = (o.reshape(w[n].shape) for o in outs)

    taken = {k[0] for k in ex.recv} | {p.key[0] for p in ex.queue}
    late = {p.key[0] for p in ex.queue}
    for n in names:
        if n in taken and n not in late:
            update(n, ex.pieces_of(n), ex)
    ex.flush()
    for n in names:
        if n in late:
            update(n, ex.pieces_of(n), None)
    rest = [n for n in names if n not in taken]
    blocks = [_to_blocks(jnp.stack(gfull[n], axis=0), SHARD_AXIS[n]) for n in rest]
    blocks = [b.reshape(N_DEV, -1, b.shape[-1]) for b in blocks]
    for n, received in zip(rest, _exchange(blocks, "exchange_grads")):
        update(n, [received], None)

    small_parts = _all_gather([_pack_small(gsmall, d)], "gather_small_grads")[0]
    outs = _adamw([small_parts], _pack_small(norm, d), _pack_small({n: m[n] for n in SMALL}, d),
                  _pack_small({n: v[n] for n in SMALL}, d), "adamw_small")
    for res, o in zip((grad, delta, new_m, new_v), outs):
        res.update(_unpack_small(o, norm))
    return loss, grad_x, grad, delta, new_m, new_v


def kernel(x, mem, positions, ffn1_norm, ffn1_w_gate_up, ffn1_w_down, mix_norm, sb_w_qkv, sb_w_o, dil_w_qkv, dil_w_o, swa_w_qkv, swa_b_qkv, swa_sinks, swa_w_o, swa_b_o, xattn_norm, mem_norm, xattn_w_q, xattn_w_kv, xattn_w_o, ffn2_norm, ffn2_w_gate_up, ffn2_w_down, final_norm, loss_target, m_ffn1_norm, m_ffn1_w_gate_up, m_ffn1_w_down, m_mix_norm, m_sb_w_qkv, m_sb_w_o, m_dil_w_qkv, m_dil_w_o, m_swa_w_qkv, m_swa_b_qkv, m_swa_sinks, m_swa_w_o, m_swa_b_o, m_xattn_norm, m_mem_norm, m_xattn_w_q, m_xattn_w_kv, m_xattn_w_o, m_ffn2_norm, m_ffn2_w_gate_up, m_ffn2_w_down, m_final_norm, v_ffn1_norm, v_ffn1_w_gate_up, v_ffn1_w_down, v_mix_norm, v_sb_w_qkv, v_sb_w_o, v_dil_w_qkv, v_dil_w_o, v_swa_w_qkv, v_swa_b_qkv, v_swa_sinks, v_swa_w_o, v_swa_b_o, v_xattn_norm, v_mem_norm, v_xattn_w_q, v_xattn_w_kv, v_xattn_w_o, v_ffn2_norm, v_ffn2_w_gate_up, v_ffn2_w_down, v_final_norm):
    args = dict(locals())
    w = {n: args[n] for n in WEIGHTS}
    m = {n: args["m_" + n] for n in WEIGHTS}
    v = {n: args["v_" + n] for n in WEIGHTS}
    loss, grad_x, grad, delta, new_m, new_v = _train_step(x, mem, positions, loss_target, w, m, v)
    return (loss, grad_x, *[grad[n] for n in WEIGHTS], *[delta[n] for n in WEIGHTS],
            *[new_m[n] for n in WEIGHTS], *[new_v[n] for n in WEIGHTS])
```

```python
import jax
import jax.numpy as jnp
from jax import lax
from jax.experimental import pallas as pl
from jax.experimental.pallas import tpu as pltpu

F32 = jnp.float32
BF16 = jnp.bfloat16

N_DEV = 8
MESH_AXES = ("x", "y", "c")
LANES = 128
BAND = 128
NORM_EPS = 1e-6
ROPE_THETA = 500000.0
DIL_PATTERNS = ((128, 1), (512, 4), (2048, 16))
SWA_HEAD_DIM = 64
SWA_GROUP = 8
SWA_WINDOW = 128
XA_HEADS = 4
DEPTH = 4
ADAM_LR, ADAM_B1, ADAM_B2, ADAM_EPS, ADAM_WD, ADAM_STEP = 0.001, 0.9, 0.999, 1e-08, 0.01, 10
VMEM_LIMIT = 56 * 1024 * 1024
NEG = -1e30

NT = (((1,), (1,)), ((), ()))
TN = (((0,), (0,)), ((), ()))


def _tile(n, prefs):
    for p in prefs:
        if n % p == 0:
            return p
    return n


def _cp(sem):
    return pltpu.CompilerParams(dimension_semantics=sem, vmem_limit_bytes=VMEM_LIMIT)


MM_TILE_SIZES = (2816, 2048, 1408, 1024, 512, 256, 128)
MM_VMEM_BUDGET = 40 * 1024 * 1024
MM_STEP_BYTES = 1.2e6


def _mm_tiles(m, n, k, ea, eb, eo, has_res):
    best = None
    for bm in [c for c in MM_TILE_SIZES if m % c == 0] or [m]:
        for bn in [c for c in MM_TILE_SIZES if n % c == 0] or [n]:
            for bk in [c for c in MM_TILE_SIZES if k % c == 0] or [k]:
                vmem = 2 * (bm * bk * ea + bk * bn * eb) + bm * bn * 4 + 2 * bm * bn * (eo + (4 if has_res else 0))
                vmem += (bm * bk * 2 if ea == 4 else 0) + (bk * bn * 2 if eb == 4 else 0)
                if vmem > MM_VMEM_BUDGET:
                    continue
                ni, nj, nk = m // bm, n // bn, k // bk
                traffic = (m * k * ea * (nj if nk > 1 else 1) + k * n * eb * (ni if nk > 1 or nj > 1 else 1)
                           + m * n * (eo + (4 if has_res else 0)) + ni * nj * nk * MM_STEP_BYTES)
                if best is None or traffic < best[0]:
                    best = (traffic, (bm, bn, bk))
    return best[1]
def _mm(a, b, *, ta=False, tb=False, out_dtype=F32, alpha=1.0, res=None, bias=None, name, ex=None):
    kdim, m = a.shape if ta else a.shape[::-1]
    kdim2, n = b.shape[::-1] if tb else b.shape
    assert kdim == kdim2, (a.shape, b.shape, ta, tb)
    bm, bn, bk = _mm_tiles(m, n, kdim, a.dtype.itemsize, b.dtype.itemsize, jnp.dtype(out_dtype).itemsize, res is not None)
    nk = kdim // bk
    grid = (m // bm, n // bn, nk)
    dn = (((0 if ta else 1,), (1 if tb else 0,)), ((), ()))
    has_res, has_bias = res is not None, bias is not None
    jobs = ex.take(2.0 * m * n * kdim * EXCHANGE_BYTES_PER_FLOP) if ex is not None else []
    nj = len(jobs)
    n_in = 2 + has_res + has_bias

    def body(*refs):
        a_ref, b_ref = refs[0], refs[1]
        res_ref = refs[2] if has_res else None
        bias_ref = refs[2 + has_res] if has_bias else None
        o_ref, acc_ref = refs[n_in + nj], refs[n_in + 2 * nj + 1]
        k = pl.program_id(2)
        if nj:
            job_refs = (refs[n_in:n_in + nj], refs[n_in + nj + 1:n_in + 2 * nj + 1], refs[n_in + 2 * nj + 2:])
            ids = [pl.program_id(t) for t in range(3)]

            @pl.when((ids[0] == 0) & (ids[1] == 0) & (ids[2] == 0))
            def _():
                ex.start(jobs, *job_refs)

            @pl.when((ids[0] == grid[0] - 1) & (ids[1] == grid[1] - 1) & (ids[2] == grid[2] - 1))
            def _():
                ex.wait(jobs, *job_refs)

        def finish(r):
            if alpha != 1.0:
                r = r * alpha
            if has_bias:
                r = r + bias_ref[...]
            if has_res:
                r = r + res_ref[...]
            o_ref[...] = r.astype(o_ref.dtype)

        prod = lax.dot_general(a_ref[...].astype(BF16), b_ref[...].astype(BF16), dn, preferred_element_type=F32)
        if nk == 1:
            finish(prod)
        else:
            @pl.when(k == 0)
            def _():
                acc_ref[...] = prod

            @pl.when(k > 0)
            def _():
                acc_ref[...] += prod

            pl.when(k == nk - 1)(lambda: finish(acc_ref[...]))

    a_spec = pl.BlockSpec((bk, bm), lambda i, j, k: (k, i)) if ta else pl.BlockSpec((bm, bk), lambda i, j, k: (i, k))
    b_spec = pl.BlockSpec((bn, bk), lambda i, j, k: (j, k)) if tb else pl.BlockSpec((bk, bn), lambda i, j, k: (k, j))
    ins, specs = [a, b], [a_spec, b_spec]
    if has_res:
        ins.append(res)
        specs.append(pl.BlockSpec((bm, bn), lambda i, j, k: (i, j)))
    if has_bias:
        ins.append(bias)
        specs.append(pl.BlockSpec((1, bn), lambda i, j, k: (0, j)))
    out_spec = pl.BlockSpec((bm, bn), lambda i, j, k: (i, j))
    out_shape = jax.ShapeDtypeStruct((m, n), out_dtype)
    scratch = [pltpu.VMEM((bm, bn), F32)]
    if not nj:
        return pl.pallas_call(
            body, name=name, grid=grid, in_specs=specs, out_specs=out_spec, out_shape=out_shape,
            scratch_shapes=scratch, compiler_params=_cp(("parallel", "parallel", "arbitrary")),
        )(*ins)
    outs = pl.pallas_call(
        body, name=name + "_carry", grid=grid, in_specs=specs + [ANY] * nj, out_specs=[out_spec] + [ANY] * nj,
        out_shape=[out_shape] + [j.recv_shape for j in jobs], scratch_shapes=scratch + _piece_sems(nj),
        compiler_params=_cp(("arbitrary", "arbitrary", "arbitrary")),
    )(*ins, *[j.g for j in jobs])
    ex.landed(jobs, outs[1:])
    return outs[0]


def _rms_fwd(x, g, name="rms_fwd"):
    s, d = x.shape
    bm = _tile(s, (256, 128, 8))

    def body(x_ref, g_ref, o_ref):
        xv = x_ref[...]
        r = lax.rsqrt(jnp.mean(xv * xv, axis=-1, keepdims=True) + NORM_EPS)
        o_ref[...] = (xv * r * g_ref[...]).astype(o_ref.dtype)

    return pl.pallas_call(
        body, name=name, grid=(s // bm,),
        in_specs=[pl.BlockSpec((bm, d), lambda i: (i, 0)), pl.BlockSpec((1, d), lambda i: (0, 0))],
        out_specs=pl.BlockSpec((bm, d), lambda i: (i, 0)),
        out_shape=jax.ShapeDtypeStruct((s, d), BF16), compiler_params=_cp(("parallel",)),
    )(x, g)


def _rms_bwd(x, g, dn, dy=None, name="rms_bwd"):
    s, d = x.shape
    bm = _tile(s, (256, 128, 8))
    has_dy = dy is not None

    def body(*refs):
        x_ref, g_ref, dn_ref = refs[:3]
        dy_ref = refs[3] if has_dy else None
        dx_ref, dg_ref = refs[-2], refs[-1]

        @pl.when(pl.program_id(0) == 0)
        def _():
            dg_ref[...] = jnp.zeros_like(dg_ref)

        xv = x_ref[...]
        r = lax.rsqrt(jnp.mean(xv * xv, axis=-1, keepdims=True) + NORM_EPS)
        xh = xv * r
        dnv = dn_ref[...].astype(F32)
        dxh = dnv * g_ref[...]
        dx = r * (dxh - xh * jnp.mean(dxh * xh, axis=-1, keepdims=True))
        if has_dy:
            dx = dx + dy_ref[...]
        dx_ref[...] = dx
        dg_ref[...] += jnp.sum(dnv * xh, axis=0, keepdims=True)

    row = pl.BlockSpec((bm, d), lambda i: (i, 0))
    vec = pl.BlockSpec((1, d), lambda i: (0, 0))
    ins, specs = [x, g, dn], [row, vec, row]
    if has_dy:
        ins.append(dy)
        specs.append(row)
    return pl.pallas_call(
        body, name=name, grid=(s // bm,), in_specs=specs, out_specs=[row, vec],
        out_shape=[jax.ShapeDtypeStruct((s, d), F32), jax.ShapeDtypeStruct((1, d), F32)],
        compiler_params=_cp(("arbitrary",)),
    )(*ins)


def _loss_head(x, g, target):
    s, d = x.shape
    bm = _tile(s, (256, 128, 8))

    def body(x_ref, g_ref, t_ref, loss_ref, dx_ref, dg_ref):
        @pl.when(pl.program_id(0) == 0)
        def _():
            dg_ref[...] = jnp.zeros_like(dg_ref)
            loss_ref[...] = jnp.zeros_like(loss_ref)

        xv = x_ref[...]
        gv = g_ref[...]
        r = lax.rsqrt(jnp.mean(xv * xv, axis=-1, keepdims=True) + NORM_EPS)
        xh = xv * r
        err = xh * gv - t_ref[...]
        part = 0.5 * jnp.sum(jnp.mean(err * err, axis=-1, keepdims=True), axis=0, keepdims=True)
        loss_ref[...] += jnp.broadcast_to(part, loss_ref.shape)
        dyv = err * (1.0 / d)
        dxh = dyv * gv
        dx_ref[...] = r * (dxh - xh * jnp.mean(dxh * xh, axis=-1, keepdims=True))
        dg_ref[...] += jnp.sum(dyv * xh, axis=0, keepdims=True)

    row = pl.BlockSpec((bm, d), lambda i: (i, 0))
    vec = pl.BlockSpec((1, d), lambda i: (0, 0))
    return pl.pallas_call(
        body, name="loss_head", grid=(s // bm,), in_specs=[row, vec, row],
        out_specs=[pl.BlockSpec((1, LANES), lambda i: (0, 0)), row, vec],
        out_shape=[jax.ShapeDtypeStruct((1, LANES), F32), jax.ShapeDtypeStruct((s, d), F32),
                   jax.ShapeDtypeStruct((1, d), F32)],
        compiler_params=_cp(("arbitrary",)),
    )(x, g, target)


def _colsum(x, name="colsum"):
    s, n = x.shape
    bm = _tile(s, (256, 128, 8))

    def body(x_ref, o_ref):
        @pl.when(pl.program_id(0) == 0)
        def _():
            o_ref[...] = jnp.zeros_like(o_ref)

        o_ref[...] += jnp.sum(x_ref[...].astype(F32), axis=0, keepdims=True)

    return pl.pallas_call(
        body, name=name, grid=(s // bm,), in_specs=[pl.BlockSpec((bm, n), lambda i: (i, 0))],
        out_specs=pl.BlockSpec((1, n), lambda i: (0, 0)), out_shape=jax.ShapeDtypeStruct((1, n), F32),
        compiler_params=_cp(("arbitrary",)),
    )(x)


def _swiglu_fwd(gu):
    s, f2 = gu.shape
    f = f2 // 2
    bm, bf = _tile(s, (256, 128, 8)), _tile(f, (512, 256, 128))

    def body(gu_ref, o_ref):
        for c in range(0, f, bf):
            gv = gu_ref[:, c:c + bf].astype(F32)
            o_ref[:, c:c + bf] = (gv / (1.0 + jnp.exp(-gv)) * gu_ref[:, f + c:f + c + bf].astype(F32)).astype(o_ref.dtype)

    return pl.pallas_call(
        body, name="swiglu_fwd", grid=(s // bm,), in_specs=[pl.BlockSpec((bm, f2), lambda i: (i, 0))],
        out_specs=pl.BlockSpec((bm, f), lambda i: (i, 0)),
        out_shape=jax.ShapeDtypeStruct((s, f), BF16), compiler_params=_cp(("parallel",)),
    )(gu)


def _swiglu_bwd(gu, da):
    s, f2 = gu.shape
    f = f2 // 2
    bm, bf = _tile(s, (128, 8)), _tile(f, (512, 256, 128))

    def body(gu_ref, da_ref, o_ref):
        for c in range(0, f, bf):
            gv = gu_ref[:, c:c + bf].astype(F32)
            dav = da_ref[:, c:c + bf].astype(F32)
            sig = 1.0 / (1.0 + jnp.exp(-gv))
            o_ref[:, c:c + bf] = (dav * gu_ref[:, f + c:f + c + bf].astype(F32)
                                  * (sig * (1.0 + gv * (1.0 - sig)))).astype(o_ref.dtype)
            o_ref[:, f + c:f + c + bf] = (dav * gv * sig).astype(o_ref.dtype)

    return pl.pallas_call(
        body, name="swiglu_bwd", grid=(s // bm,),
        in_specs=[pl.BlockSpec((bm, f2), lambda i: (i, 0)), pl.BlockSpec((bm, f), lambda i: (i, 0))],
        out_specs=pl.BlockSpec((bm, f2), lambda i: (i, 0)),
        out_shape=jax.ShapeDtypeStruct((s, f2), BF16), compiler_params=_cp(("parallel",)),
    )(gu, da)


def _rope_tables(positions, rot, sign=1.0):
    half = rot // 2
    inv_freq = jnp.power(F32(ROPE_THETA), -jnp.arange(half, dtype=F32) * 2.0 / rot)
    ang = positions.reshape(-1).astype(F32)[:, None] * inv_freq
    cos, sin = jnp.cos(ang), jnp.sin(ang) * sign
    s = ang.shape[0]
    c_tab = jnp.concatenate([cos, cos, jnp.ones((s, LANES - rot), F32)], axis=1)
    s_tab = jnp.concatenate([-sin, sin, jnp.zeros((s, LANES - rot), F32)], axis=1)
    return c_tab, s_tab


def _rope(x, c_tab, s_tab, nblk, half, name):
    s = x.shape[0]
    bm = _tile(s, (256, 128, 8))
    hb = _tile(nblk, (12, 8, 6, 4, 3, 2))

    def body(x_ref, c_ref, s_ref, o_ref):
        lane = lax.broadcasted_iota(jnp.int32, (bm, LANES), 1)
        for h in range(hb):
            sl = slice(h * LANES, (h + 1) * LANES)
            xv = x_ref[:, sl].astype(F32)
            sw = jnp.where(lane < half, pltpu.roll(xv, LANES - half, 1), pltpu.roll(xv, half, 1))
            o_ref[:, sl] = (xv * c_ref[...] + sw * s_ref[...]).astype(o_ref.dtype)

    blk = pl.BlockSpec((bm, hb * LANES), lambda i, j: (i, j))
    tab = pl.BlockSpec((bm, LANES), lambda i, j: (i, 0))
    return pl.pallas_call(
        body, name=name, grid=(s // bm, nblk // hb), in_specs=[blk, tab, tab], out_specs=blk,
        out_shape=jax.ShapeDtypeStruct((s, nblk * LANES), BF16), compiler_params=_cp(("parallel", "parallel")),
    )(x, c_tab, s_tab)


def _sb_terms(q, kb, scale):
    z = lax.dot_general(q, kb, NT, preferred_element_type=F32) * scale
    u = jnp.log(1.0 + jnp.exp(-jnp.abs(z)))
    return jnp.minimum(-z, 0.0) - u, jnp.minimum(z, 0.0) - u


def _sb_heads_per_step(n_heads):
    return 2 if n_heads % 2 == 0 else 1


def _sb_fwd(qkv, n_heads, ex=None):
    s = qkv.shape[0]
    tq = _tile(s, (256, 128))
    hp = _sb_heads_per_step(n_heads)
    w = hp * LANES
    ng = n_heads // hp
    scale = LANES ** -0.5
    jobs = ex.take(4 * 4 * n_heads * s * s * LANES * EXCHANGE_BYTES_PER_FLOP) if ex is not None else []
    nj = len(jobs)

    def body(*refs):
        q_ref, k_ref, v_ref = refs[:3]
        o_ref, lt_ref = refs[3 + nj:5 + nj]
        i = pl.program_id(1)
        if nj:
            job_refs = (refs[3:3 + nj], refs[5 + nj:5 + 2 * nj], refs[5 + 2 * nj:])
            gi = pl.program_id(0)
            pl.when((gi == 0) & (i == 0))(lambda: ex.start(jobs, *job_refs))
            pl.when((gi == ng - 1) & (i == s // tq - 1))(lambda: ex.wait(jobs, *job_refs))
        row = lax.broadcasted_iota(jnp.int32, (tq, tq), 0)
        col = lax.broadcasted_iota(jnp.int32, (tq, tq), 1)
        below = col < row
        later = (row > col).astype(BF16)
        qs = [q_ref[:, h * LANES:(h + 1) * LANES] for h in range(hp)]

        def block(h, k0, c, acc, diag):
            sl = slice(h * LANES, (h + 1) * LANES)
            lk, logsig = _sb_terms(qs[h], k_ref[pl.ds(k0, tq), sl], scale)
            if diag:
                lk = jnp.where(below, lk, 0.0)
            a = jnp.exp(logsig + jnp.dot(lk.astype(BF16), later, preferred_element_type=F32) + c)
            if diag:
                a = jnp.where(below, a, 0.0)
            acc = acc + jnp.dot(a.astype(BF16), v_ref[pl.ds(k0, tq), sl], preferred_element_type=F32)
            return c + jnp.sum(lk, axis=1, keepdims=True), acc

        d0 = pl.multiple_of(i * tq, tq)
        carry = []
        for h in range(hp):
            carry += list(block(h, d0, jnp.zeros((tq, 1), F32), jnp.zeros((tq, LANES), F32), True))

        def step(t, carry):
            k0 = pl.multiple_of((i - 1 - t) * tq, tq)
            out = []
            for h in range(hp):
                out += list(block(h, k0, carry[2 * h], carry[2 * h + 1], False))
            return tuple(out)

        carry = lax.fori_loop(0, i, step, tuple(carry))
        for h in range(hp):
            sl = slice(h * LANES, (h + 1) * LANES)
            o_ref[:, sl] = carry[2 * h + 1].astype(o_ref.dtype)
            lt_ref[:, sl] = jnp.broadcast_to(carry[2 * h], (tq, LANES))

    blk = pl.BlockSpec((tq, w), lambda g, i: (i, g))
    outs = pl.pallas_call(
        body, name="sb_fwd_carry" if nj else "sb_fwd", grid=(ng, s // tq),
        in_specs=[blk, pl.BlockSpec((s, w), lambda g, i: (0, ng + g)), pl.BlockSpec((s, w), lambda g, i: (0, 2 * ng + g))]
        + [ANY] * nj,
        out_specs=[blk, blk] + [ANY] * nj,
        out_shape=[jax.ShapeDtypeStruct((s, n_heads * LANES), BF16), jax.ShapeDtypeStruct((s, n_heads * LANES), F32)]
        + [j.recv_shape for j in jobs],
        scratch_shapes=_piece_sems(nj) if nj else [],
        compiler_params=_cp(("arbitrary", "arbitrary")),
    )(qkv, qkv, qkv, *[j.g for j in jobs])
    if nj:
        ex.landed(jobs, outs[2:])
    return outs[:2]


def _sb_bwd(qkv, do, ltot, n_heads, ex=None):
    s = qkv.shape[0]
    tq = _tile(s, (256, 128))
    hp = _sb_heads_per_step(n_heads)
    w = hp * LANES
    ng = n_heads // hp
    scale = LANES ** -0.5
    jobs = ex.take(4 * 9 * n_heads * s * s * LANES * EXCHANGE_BYTES_PER_FLOP) if ex is not None else []
    nj = len(jobs)

    def body(*refs):
        q_ref, k_ref, v_ref, do_ref, lt_ref = refs[:5]
        dq_ref, dk_ref, dv_ref = refs[5 + nj:8 + nj]
        i = pl.program_id(1)
        if nj:
            job_refs = (refs[5:5 + nj], refs[8 + nj:8 + 2 * nj], refs[8 + 2 * nj:])
            gi = pl.program_id(0)
            pl.when((gi == 0) & (i == 0))(lambda: ex.start(jobs, *job_refs))
            pl.when((gi == ng - 1) & (i == s // tq - 1))(lambda: ex.wait(jobs, *job_refs))

        @pl.when(i == 0)
        def _():
            dk_ref[...] = jnp.zeros_like(dk_ref)
            dv_ref[...] = jnp.zeros_like(dv_ref)

        row = lax.broadcasted_iota(jnp.int32, (tq, tq), 0)
        col = lax.broadcasted_iota(jnp.int32, (tq, tq), 1)
        below = col < row
        later = (row > col).astype(BF16)
        before = (row < col).astype(BF16)
        qs = [q_ref[:, h * LANES:(h + 1) * LANES] for h in range(hp)]
        dos = [do_ref[:, h * LANES:(h + 1) * LANES] for h in range(hp)]
        lts = [lt_ref[:, h * LANES:h * LANES + 1] for h in range(hp)]

        def block(h, k0, cpre, ce, dq, diag):
            sl = slice(h * LANES, (h + 1) * LANES)
            kb = k_ref[pl.ds(k0, tq), sl]
            vb = v_ref[pl.ds(k0, tq), sl]
            lk, logsig = _sb_terms(qs[h], kb, scale)
            if diag:
                lk = jnp.where(below, lk, 0.0)
            cnext = cpre + jnp.sum(lk, axis=1, keepdims=True)
            a = jnp.exp(logsig + (lts[h] - cnext) + jnp.dot(lk.astype(BF16), later, preferred_element_type=F32))
            if diag:
                a = jnp.where(below, a, 0.0)
            e = a * lax.dot_general(dos[h], vb, NT, preferred_element_type=F32)
            e_before = ce + jnp.dot(e.astype(BF16), before, preferred_element_type=F32)
            sig = jnp.exp(logsig)
            dz = (e - sig * (e + e_before)) * scale
            if diag:
                dz = jnp.where(below, dz, 0.0)
            dzb = dz.astype(BF16)
            dq = dq + jnp.dot(dzb, kb, preferred_element_type=F32)
            dk_ref[pl.ds(k0, tq), sl] += lax.dot_general(dzb, qs[h], TN, preferred_element_type=F32)
            dv_ref[pl.ds(k0, tq), sl] += lax.dot_general(a.astype(BF16), dos[h], TN, preferred_element_type=F32)
            return cnext, ce + jnp.sum(e, axis=1, keepdims=True), dq

        def step(j, carry):
            k0 = pl.multiple_of(j * tq, tq)
            out = []
            for h in range(hp):
                out += list(block(h, k0, *carry[3 * h:3 * h + 3], False))
            return tuple(out)

        z1 = jnp.zeros((tq, 1), F32)
        carry = lax.fori_loop(0, i, step, (z1, z1, jnp.zeros((tq, LANES), F32)) * hp)
        d0 = pl.multiple_of(i * tq, tq)
        for h in range(hp):
            _, _, dq = block(h, d0, *carry[3 * h:3 * h + 3], True)
            dq_ref[:, h * LANES:(h + 1) * LANES] = dq.astype(dq_ref.dtype)

    blk = pl.BlockSpec((tq, w), lambda g, i: (i, g))
    full = pl.BlockSpec((s, w), lambda g, i: (0, g))
    wt = n_heads * LANES
    outs = pl.pallas_call(
        body, name="sb_bwd_carry" if nj else "sb_bwd", grid=(ng, s // tq),
        in_specs=[blk, pl.BlockSpec((s, w), lambda g, i: (0, ng + g)), pl.BlockSpec((s, w), lambda g, i: (0, 2 * ng + g)),
                  blk, blk] + [ANY] * nj,
        out_specs=[blk, full, full] + [ANY] * nj,
        out_shape=[jax.ShapeDtypeStruct((s, wt), BF16), jax.ShapeDtypeStruct((s, wt), F32),
                   jax.ShapeDtypeStruct((s, wt), F32)] + [j.recv_shape for j in jobs],
        scratch_shapes=_piece_sems(nj) if nj else [],
        compiler_params=_cp(("arbitrary", "arbitrary")),
    )(qkv, qkv, qkv, do, ltot, *[j.g for j in jobs])
    if nj:
        ex.landed(jobs, outs[3:])
    return outs[:3]


def _stack_heads(ref, n):
    return jnp.concatenate([ref[:, h * LANES:(h + 1) * LANES] for h in range(n)], axis=0)


def _stack_head_scalars(ref, n):
    return jnp.concatenate([jnp.broadcast_to(ref[:, h * LANES:h * LANES + 1], (BAND, 1)) for h in range(n)], axis=0)


def _band_masks(b, max_dist):
    qi = lax.broadcasted_iota(jnp.int32, (BAND, BAND), 0)
    kj = lax.broadcasted_iota(jnp.int32, (BAND, BAND), 1)
    dist = qi - kj
    return ((BAND + dist) <= max_dist) & (b > 0), (dist >= 0) & (dist <= max_dist)


def _band_fwd(qa, ka, va, *, n_cls, n_steps, hpb, group, q_blk, k_blk, v_blk, max_dist, scale, sinks, name):
    length = qa.shape[0]
    nb = length // BAND
    has_sink = sinks is not None
    qw, kw = hpb * LANES, (hpb // group) * LANES
    stacked = group > 1 and group == hpb

    def body(*refs):
        q_ref, kp_ref, kc_ref, vp_ref, vc_ref = refs[:5]
        o_ref, lse_ref = refs[-2], refs[-1]
        mask_p, mask_c = _band_masks(pl.program_id(2), max_dist)
        if stacked:
            q = _stack_heads(q_ref, hpb).astype(BF16)
            mask_p, mask_c = jnp.tile(mask_p, (hpb, 1)), jnp.tile(mask_c, (hpb, 1))
            s_p = lax.dot_general(q, kp_ref[...].astype(BF16), NT, preferred_element_type=F32) * scale
            s_c = lax.dot_general(q, kc_ref[...].astype(BF16), NT, preferred_element_type=F32) * scale
            s_p = jnp.where(mask_p, s_p, NEG)
            s_c = jnp.where(mask_c, s_c, NEG)
            m = jnp.maximum(jnp.max(s_p, axis=1, keepdims=True), jnp.max(s_c, axis=1, keepdims=True))
            l = jnp.sum(jnp.exp(s_p - m), axis=1, keepdims=True) + jnp.sum(jnp.exp(s_c - m), axis=1, keepdims=True)
            lse = m + jnp.log(l)
            if has_sink:
                sk = _stack_head_scalars(refs[5], hpb)
                lse = jnp.maximum(lse, sk) + jnp.log(1.0 + jnp.exp(-jnp.abs(lse - sk)))
            p_p = jnp.exp(s_p - lse).astype(BF16)
            p_c = jnp.exp(s_c - lse).astype(BF16)
            o = (jnp.dot(p_p, vp_ref[...].astype(BF16), preferred_element_type=F32)
                 + jnp.dot(p_c, vc_ref[...].astype(BF16), preferred_element_type=F32))
            for hh in range(hpb):
                rows = slice(hh * BAND, (hh + 1) * BAND)
                o_ref[:, hh * LANES:(hh + 1) * LANES] = o[rows]
                lse_ref[:, hh * LANES:(hh + 1) * LANES] = jnp.broadcast_to(lse[rows], (BAND, LANES))
            return
        for hh in range(hpb):
            qs = slice(hh * LANES, (hh + 1) * LANES)
            ks = slice((hh // group) * LANES, (hh // group + 1) * LANES)
            q = q_ref[:, qs].astype(BF16)
            s_p = lax.dot_general(q, kp_ref[:, ks].astype(BF16), NT, preferred_element_type=F32) * scale
            s_c = lax.dot_general(q, kc_ref[:, ks].astype(BF16), NT, preferred_element_type=F32) * scale
            s_p = jnp.where(mask_p, s_p, NEG)
            s_c = jnp.where(mask_c, s_c, NEG)
            m = jnp.maximum(jnp.max(s_p, axis=1, keepdims=True), jnp.max(s_c, axis=1, keepdims=True))
            l = jnp.sum(jnp.exp(s_p - m), axis=1, keepdims=True) + jnp.sum(jnp.exp(s_c - m), axis=1, keepdims=True)
            lse = m + jnp.log(l)
            if has_sink:
                sk = refs[5][:, hh * LANES:hh * LANES + 1]
                lse = jnp.maximum(lse, sk) + jnp.log(1.0 + jnp.exp(-jnp.abs(lse - sk)))
            p_p = jnp.exp(s_p - lse).astype(BF16)
            p_c = jnp.exp(s_c - lse).astype(BF16)
            o_ref[:, qs] = (jnp.dot(p_p, vp_ref[:, ks].astype(BF16), preferred_element_type=F32)
                            + jnp.dot(p_c, vc_ref[:, ks].astype(BF16), preferred_element_type=F32))
            lse_ref[:, qs] = jnp.broadcast_to(lse, (BAND, LANES))

    def prev(b):
        return jnp.maximum(b - 1, 0)

    specs = [pl.BlockSpec((BAND, qw), lambda n, st, b: (b, q_blk(n, st))),
             pl.BlockSpec((BAND, kw), lambda n, st, b: (prev(b), k_blk(n, st))),
             pl.BlockSpec((BAND, kw), lambda n, st, b: (b, k_blk(n, st))),
             pl.BlockSpec((BAND, kw), lambda n, st, b: (prev(b), v_blk(n, st))),
             pl.BlockSpec((BAND, kw), lambda n, st, b: (b, v_blk(n, st)))]
    ins = [qa, ka, ka, va, va]
    if has_sink:
        ins.append(sinks)
        specs.append(pl.BlockSpec((1, qw), lambda n, st, b: (0, st)))
    out = pl.BlockSpec((BAND, qw), lambda n, st, b: (b, n * n_steps + st))
    w = n_cls * n_steps * qw
    return pl.pallas_call(
        body, name=name, grid=(n_cls, n_steps, nb), in_specs=specs, out_specs=[out, out],
        out_shape=[jax.ShapeDtypeStruct((length, w), F32), jax.ShapeDtypeStruct((length, w), F32)],
        compiler_params=_cp(("parallel", "parallel", "parallel")),
    )(*ins)


def _band_bwd(qa, ka, va, o, do, lse, dlse, *, n_cls, n_steps, hpb, group, q_blk, k_blk, v_blk, max_dist, scale, sinks,
              name):
    length = qa.shape[0]
    nb = length // BAND
    has_sink, has_dlse = sinks is not None, dlse is not None
    qw, kw = hpb * LANES, (hpb // group) * LANES
    stacked = group > 1 and group == hpb

    def body(*refs):
        q_ref, kp_ref, kc_ref, vp_ref, vc_ref, o_ref, do_ref, lse_ref = refs[:8]
        pos = 8
        dlse_ref = refs[pos] if has_dlse else None
        pos += has_dlse
        sink_ref = refs[pos] if has_sink else None
        pos += has_sink
        dq_ref, dkc_ref, dkp_ref, dvc_ref, dvp_ref = refs[pos:pos + 5]
        b = pl.program_id(2)
        mask_p, mask_c = _band_masks(b, max_dist)
        if has_sink:
            dsink_ref = refs[pos + 5]

            @pl.when(b == 0)
            def _():
                dsink_ref[...] = jnp.zeros_like(dsink_ref)

        if stacked:
            q = _stack_heads(q_ref, hpb).astype(BF16)
            kp, kc = kp_ref[...].astype(BF16), kc_ref[...].astype(BF16)
            vp, vc = vp_ref[...].astype(BF16), vc_ref[...].astype(BF16)
            dov = _stack_heads(do_ref, hpb).astype(F32)
            dob = dov.astype(BF16)
            lse_v = _stack_head_scalars(lse_ref, hpb)
            mask_p, mask_c = jnp.tile(mask_p, (hpb, 1)), jnp.tile(mask_c, (hpb, 1))
            s_p = lax.dot_general(q, kp, NT, preferred_element_type=F32) * scale
            s_c = lax.dot_general(q, kc, NT, preferred_element_type=F32) * scale
            p_p = jnp.where(mask_p, jnp.exp(jnp.where(mask_p, s_p, NEG) - lse_v), 0.0)
            p_c = jnp.where(mask_c, jnp.exp(jnp.where(mask_c, s_c, NEG) - lse_v), 0.0)
            delta = jnp.sum(dov * _stack_heads(o_ref, hpb), axis=1, keepdims=True)
            shift = -delta
            if has_dlse:
                shift = shift + _stack_head_scalars(dlse_ref, hpb)
            ds_p = (p_p * (lax.dot_general(dob, vp, NT, preferred_element_type=F32) + shift) * scale).astype(BF16)
            ds_c = (p_c * (lax.dot_general(dob, vc, NT, preferred_element_type=F32) + shift) * scale).astype(BF16)
            dq = jnp.dot(ds_p, kp, preferred_element_type=F32) + jnp.dot(ds_c, kc, preferred_element_type=F32)
            for hh in range(hpb):
                dq_ref[:, hh * LANES:(hh + 1) * LANES] = dq[hh * BAND:(hh + 1) * BAND]
            dkc_ref[...] = lax.dot_general(ds_c, q, TN, preferred_element_type=F32)
            dkp_ref[...] = lax.dot_general(ds_p, q, TN, preferred_element_type=F32)
            dvc_ref[...] = lax.dot_general(p_c.astype(BF16), dob, TN, preferred_element_type=F32)
            dvp_ref[...] = lax.dot_general(p_p.astype(BF16), dob, TN, preferred_element_type=F32)
            if has_sink:
                lost = -jnp.exp(_stack_head_scalars(sink_ref, hpb) - lse_v) * delta
                for hh in range(hpb):
                    dsink_ref[:, hh * LANES:(hh + 1) * LANES] += jnp.broadcast_to(
                        jnp.sum(lost[hh * BAND:(hh + 1) * BAND], axis=0, keepdims=True), (1, LANES))
            return

        for hh in range(hpb):
            qs = slice(hh * LANES, (hh + 1) * LANES)
            ks = slice((hh // group) * LANES, (hh // group + 1) * LANES)
            q = q_ref[:, qs].astype(BF16)
            kp, kc = kp_ref[:, ks].astype(BF16), kc_ref[:, ks].astype(BF16)
            vp, vc = vp_ref[:, ks].astype(BF16), vc_ref[:, ks].astype(BF16)
            dov = do_ref[:, qs].astype(F32)
            dob = dov.astype(BF16)
            lse_v = lse_ref[:, hh * LANES:hh * LANES + 1]
            s_p = lax.dot_general(q, kp, NT, preferred_element_type=F32) * scale
            s_c = lax.dot_general(q, kc, NT, preferred_element_type=F32) * scale
            p_p = jnp.where(mask_p, jnp.exp(jnp.where(mask_p, s_p, NEG) - lse_v), 0.0)
            p_c = jnp.where(mask_c, jnp.exp(jnp.where(mask_c, s_c, NEG) - lse_v), 0.0)
            delta = jnp.sum(dov * o_ref[:, qs], axis=1, keepdims=True)
            shift = -delta
            if has_dlse:
                shift = shift + dlse_ref[:, hh * LANES:hh * LANES + 1]
            dp_p = lax.dot_general(dob, vp, NT, preferred_element_type=F32)
            dp_c = lax.dot_general(dob, vc, NT, preferred_element_type=F32)
            ds_p = (p_p * (dp_p + shift) * scale).astype(BF16)
            ds_c = (p_c * (dp_c + shift) * scale).astype(BF16)
            dq_ref[:, qs] = (jnp.dot(ds_p, kp, preferred_element_type=F32)
                             + jnp.dot(ds_c, kc, preferred_element_type=F32))
            parts = (lax.dot_general(ds_c, q, TN, preferred_element_type=F32),
                     lax.dot_general(ds_p, q, TN, preferred_element_type=F32),
                     lax.dot_general(p_c.astype(BF16), dob, TN, preferred_element_type=F32),
                     lax.dot_general(p_p.astype(BF16), dob, TN, preferred_element_type=F32))
            for ref, part in zip((dkc_ref, dkp_ref, dvc_ref, dvp_ref), parts):
                if hh % group == 0:
                    ref[:, ks] = part
                else:
                    ref[:, ks] += part
            if has_sink:
                p_sink = jnp.exp(sink_ref[:, hh * LANES:hh * LANES + 1] - lse_v)
                dsink_ref[:, qs] += jnp.broadcast_to(jnp.sum(-p_sink * delta, axis=0, keepdims=True), (1, LANES))

    def prev(b):
        return jnp.maximum(b - 1, 0)

    per_q = pl.BlockSpec((BAND, qw), lambda n, st, b: (b, n * n_steps + st))
    per_k = pl.BlockSpec((BAND, kw), lambda n, st, b: (b, n * n_steps + st))
    specs = [pl.BlockSpec((BAND, qw), lambda n, st, b: (b, q_blk(n, st))),
             pl.BlockSpec((BAND, kw), lambda n, st, b: (prev(b), k_blk(n, st))),
             pl.BlockSpec((BAND, kw), lambda n, st, b: (b, k_blk(n, st))),
             pl.BlockSpec((BAND, kw), lambda n, st, b: (prev(b), v_blk(n, st))),
             pl.BlockSpec((BAND, kw), lambda n, st, b: (b, v_blk(n, st))),
             per_q, per_q, per_q]
    ins = [qa, ka, ka, va, va, o, do, lse]
    if has_dlse:
        ins.append(dlse)
        specs.append(per_q)
    out_specs = [per_q] + [per_k] * 4
    out_shape = ([jax.ShapeDtypeStruct((length, n_cls * n_steps * qw), F32)]
                 + [jax.ShapeDtypeStruct((length, n_cls * n_steps * kw), F32)] * 4)
    if has_sink:
        ins.append(sinks)
        specs.append(pl.BlockSpec((1, qw), lambda n, st, b: (0, st)))
        out_specs = out_specs + [pl.BlockSpec((1, qw), lambda n, st, b: (0, st))]
        out_shape = out_shape + [jax.ShapeDtypeStruct((1, n_steps * qw), F32)]
    return pl.pallas_call(
        body, name=name, grid=(n_cls, n_steps, nb), in_specs=specs, out_specs=out_specs, out_shape=out_shape,
        compiler_params=_cp(("parallel", "parallel", "arbitrary")),
    )(*ins)


def _band_fold(cur, prv, *, n_cls, n_heads, group, name):
    length = cur.shape[0]
    nb = length // BAND
    n_kv = n_heads // group

    def body(c_ref, p_ref, o_ref):
        b, gq = pl.program_id(2), pl.program_id(3)

        @pl.when(gq == 0)
        def _():
            o_ref[...] = jnp.zeros_like(o_ref)

        o_ref[...] += c_ref[...] + jnp.where(b + 1 < nb, p_ref[...], 0.0)

    blk = (BAND, LANES)
    return pl.pallas_call(
        body, name=name, grid=(n_cls, n_kv, nb, group),
        in_specs=[pl.BlockSpec(blk, lambda n, h, b, gq: (b, n * n_heads + h * group + gq)),
                  pl.BlockSpec(blk, lambda n, h, b, gq: (jnp.minimum(b + 1, nb - 1), n * n_heads + h * group + gq))],
        out_specs=pl.BlockSpec(blk, lambda n, h, b, gq: (b, n * n_kv + h)),
        out_shape=jax.ShapeDtypeStruct((length, n_cls * n_kv * LANES), F32),
        compiler_params=_cp(("parallel", "parallel", "parallel", "arbitrary")),
    )(cur, prv)


def _dil_mix_fwd(os_, lses):
    s, w = os_[0].shape
    bm = _tile(s, (256, 128, 8))

    def body(o0, o1, o2, l0, l1, l2, out_ref):
        ls = [l0[...], l1[...], l2[...]]
        m = jnp.maximum(jnp.maximum(ls[0], ls[1]), ls[2])
        es = [jnp.exp(v - m) for v in ls]
        inv = 1.0 / (es[0] + es[1] + es[2])
        for gi, o_ref in enumerate((o0, o1, o2)):
            out_ref[:, gi * w:(gi + 1) * w] = (o_ref[...] * (es[gi] * inv)).astype(out_ref.dtype)

    blk = pl.BlockSpec((bm, w), lambda i: (i, 0))
    return pl.pallas_call(
        body, name="dil_mix_fwd", grid=(s // bm,), in_specs=[blk] * 6,
        out_specs=pl.BlockSpec((bm, 3 * w), lambda i: (i, 0)),
        out_shape=jax.ShapeDtypeStruct((s, 3 * w), BF16), compiler_params=_cp(("parallel",)),
    )(*os_, *lses)


def _dil_mix_bwd(os_, lses, dmixed):
    s, w = os_[0].shape
    bm = _tile(s, (256, 128, 8))
    hg = w // LANES

    def body(o0, o1, o2, l0, l1, l2, dm_ref, do0, do1, do2, dl0, dl1, dl2):
        ls = [l0[...], l1[...], l2[...]]
        m = jnp.maximum(jnp.maximum(ls[0], ls[1]), ls[2])
        es = [jnp.exp(v - m) for v in ls]
        inv = 1.0 / (es[0] + es[1] + es[2])
        alphas = [e * inv for e in es]
        dalphas = []
        for gi, (o_ref, do_ref) in enumerate(((o0, do0), (o1, do1), (o2, do2))):
            dm = dm_ref[:, gi * w:(gi + 1) * w].astype(F32)
            do_ref[...] = (dm * alphas[gi]).astype(do_ref.dtype)
            prod = dm * o_ref[...]
            parts = [jnp.broadcast_to(jnp.sum(prod[:, j * LANES:(j + 1) * LANES], axis=1, keepdims=True), (bm, LANES))
                     for j in range(hg)]
            dalphas.append(jnp.concatenate(parts, axis=1) if hg > 1 else parts[0])
        mean = alphas[0] * dalphas[0] + alphas[1] * dalphas[1] + alphas[2] * dalphas[2]
        for gi, dl_ref in enumerate((dl0, dl1, dl2)):
            dl_ref[...] = alphas[gi] * (dalphas[gi] - mean)

    blk = pl.BlockSpec((bm, w), lambda i: (i, 0))
    return pl.pallas_call(
        body, name="dil_mix_bwd", grid=(s // bm,), in_specs=[blk] * 6 + [pl.BlockSpec((bm, 3 * w), lambda i: (i, 0))],
        out_specs=[blk] * 6,
        out_shape=[jax.ShapeDtypeStruct((s, w), BF16)] * 3 + [jax.ShapeDtypeStruct((s, w), F32)] * 3,
        compiler_params=_cp(("parallel",)),
    )(*os_, *lses, dmixed)


def _xattn_fwd(q, kv):
    s, w = q.shape
    mlen = kv.shape[0]
    tq = _tile(s, (512, 256, 128))
    scale = LANES ** -0.5

    def body(q_ref, k_ref, v_ref, o_ref):
        for h in range(XA_HEADS):
            sl = slice(h * LANES, (h + 1) * LANES)
            sc = lax.dot_general(q_ref[:, sl], k_ref[:, sl], NT, preferred_element_type=F32) * scale
            m = jnp.max(sc, axis=1, keepdims=True)
            e = jnp.exp(sc - m)
            p = e / jnp.sum(e, axis=1, keepdims=True)
            o_ref[:, sl] = jnp.dot(p.astype(BF16), v_ref[:, sl], preferred_element_type=F32).astype(o_ref.dtype)

    return pl.pallas_call(
        body, name="xattn_fwd", grid=(s // tq,),
        in_specs=[pl.BlockSpec((tq, w), lambda i: (i, 0)), pl.BlockSpec((mlen, w), lambda i: (0, 0)),
                  pl.BlockSpec((mlen, w), lambda i: (0, 1))],
        out_specs=pl.BlockSpec((tq, w), lambda i: (i, 0)),
        out_shape=jax.ShapeDtypeStruct((s, w), BF16), compiler_params=_cp(("parallel",)),
    )(q, kv, kv)


def _xattn_bwd(q, kv, do):
    s, w = q.shape
    mlen = kv.shape[0]
    tq = _tile(s, (512, 256, 128))
    scale = LANES ** -0.5

    def body(q_ref, k_ref, v_ref, do_ref, dq_ref, dk_ref, dv_ref):
        @pl.when(pl.program_id(0) == 0)
        def _():
            dk_ref[...] = jnp.zeros_like(dk_ref)
            dv_ref[...] = jnp.zeros_like(dv_ref)

        for h in range(XA_HEADS):
            sl = slice(h * LANES, (h + 1) * LANES)
            qh, kh, vh, doh = q_ref[:, sl], k_ref[:, sl], v_ref[:, sl], do_ref[:, sl]
            sc = lax.dot_general(qh, kh, NT, preferred_element_type=F32) * scale
            m = jnp.max(sc, axis=1, keepdims=True)
            e = jnp.exp(sc - m)
            p = e / jnp.sum(e, axis=1, keepdims=True)
            dp = lax.dot_general(doh, vh, NT, preferred_element_type=F32)
            ds = (p * (dp - jnp.sum(p * dp, axis=1, keepdims=True)) * scale).astype(BF16)
            dq_ref[:, sl] = jnp.dot(ds, kh, preferred_element_type=F32).astype(dq_ref.dtype)
            dk_ref[:, sl] += lax.dot_general(ds, qh, TN, preferred_element_type=F32)
            dv_ref[:, sl] += lax.dot_general(p.astype(BF16), doh, TN, preferred_element_type=F32)

    row = pl.BlockSpec((tq, w), lambda i: (i, 0))
    acc = pl.BlockSpec((mlen, w), lambda i: (0, 0))
    return pl.pallas_call(
        body, name="xattn_bwd", grid=(s // tq,),
        in_specs=[row, acc, pl.BlockSpec((mlen, w), lambda i: (0, 1)), row],
        out_specs=[row, acc, acc],
        out_shape=[jax.ShapeDtypeStruct((s, w), BF16), jax.ShapeDtypeStruct((mlen, w), F32),
                   jax.ShapeDtypeStruct((mlen, w), F32)],
        compiler_params=_cp(("arbitrary",)),
    )(q, kv, kv, do)


def _adamw(parts, w, m, v, name, ex=None):
    r, c = w.shape
    npc = len(parts)
    rp = r // npc
    row_bytes = c * (2 * N_DEV * npc * parts[0].dtype.itemsize + 2 * 7 * 4)
    br = _tile(rp, tuple(p for p in (256, 128, 64, 32, 16, 8) if p * c * 4 <= 1024 * 1024 and p * row_bytes <= MM_VMEM_BUDGET))
    steps = rp // br
    jobs = ex.take(r * c * ADAMW_EXCHANGE_BYTES_PER_PARAM) if ex is not None else []
    nj = len(jobs)

    def body(*refs):
        w_ref, m_ref, v_ref = refs[npc:npc + 3]
        g_ref, d_ref, nm_ref, nv_ref = refs[npc + 3 + nj:npc + 7 + nj]
        if nj:
            job_refs = (refs[npc + 3:npc + 3 + nj], refs[npc + 7 + nj:npc + 7 + 2 * nj], refs[npc + 7 + 2 * nj:])
            p_id, i_id = pl.program_id(0), pl.program_id(1)
            pl.when((p_id == 0) & (i_id == 0))(lambda: ex.start(jobs, *job_refs))
            pl.when((p_id == npc - 1) & (i_id == steps - 1))(lambda: ex.wait(jobs, *job_refs))

        def update(p_ref):
            g = p_ref[0].astype(F32)
            for t in range(1, N_DEV):
                g = g + p_ref[t].astype(F32)
            nm = ADAM_B1 * m_ref[...] + (1.0 - ADAM_B1) * g
            nv = ADAM_B2 * v_ref[...] + (1.0 - ADAM_B2) * (g * g)
            m_hat = nm / (1.0 - ADAM_B1 ** ADAM_STEP)
            v_hat = nv / (1.0 - ADAM_B2 ** ADAM_STEP)
            g_ref[...] = g
            d_ref[...] = -ADAM_LR * (m_hat / (jnp.sqrt(v_hat) + ADAM_EPS) + ADAM_WD * w_ref[...])
            nm_ref[...] = nm
            nv_ref[...] = nv

        for k in range(npc):
            pl.when(pl.program_id(0) == k)(lambda k=k: update(refs[k]))

    blk = pl.BlockSpec((br, c), lambda p, i: (p * steps + i, 0))
    part_specs = [pl.BlockSpec((N_DEV, br, c), lambda p, i, k=k: (0, jnp.where(p == k, i, 0), 0)) for k in range(npc)]
    outs = pl.pallas_call(
        body, name=name + "_carry" if nj else name, grid=(npc, steps),
        in_specs=part_specs + [blk, blk, blk] + [ANY] * nj, out_specs=[blk] * 4 + [ANY] * nj,
        out_shape=[jax.ShapeDtypeStruct((r, c), F32)] * 4 + [j.recv_shape for j in jobs],
        scratch_shapes=_piece_sems(nj) if nj else [],
        compiler_params=_cp(("arbitrary", "arbitrary") if nj else ("parallel", "parallel")),
    )(*parts, w, m, v, *[j.g for j in jobs])
    if nj:
        ex.landed(jobs, outs[4:])
    return outs[:4]


MESH_ID = pl.DeviceIdType.MESH
ANY = pl.BlockSpec(memory_space=pl.ANY)


def _my_place():
    return lax.axis_index("x"), lax.axis_index("y"), lax.axis_index("c")


def _all_gather(xs, name):
    nt = len(xs)

    def body(*refs):
        x_refs, out_refs = refs[:nt], refs[nt:2 * nt]
        send_sems, recv_sems, local_sems = refs[2 * nt:]
        x, y, c = _my_place()
        me, sibling = (x, y, c), (x, y, 1 - c)
        chips = [(1 - x, y), (x, 1 - y), (1 - x, 1 - y)]

        def slot(t, p):
            return out_refs[t].at[4 * p[0] + 2 * p[1] + p[2]]

        def copy(t, k, block, to, src=None):
            return pltpu.make_async_remote_copy(
                src_ref=slot(t, block) if src is None else src, dst_ref=slot(t, block),
                send_sem=send_sems.at[7 * t + k], recv_sem=recv_sems.at[7 * t + k], device_id=to,
                device_id_type=MESH_ID)

        mine, first, passed = [], [], []
        for t in range(nt):
            cp = pltpu.make_async_copy(x_refs[t], slot(t, me), local_sems.at[t])
            cp.start()
            mine.append(cp)
            group = [copy(t, 0, me, sibling, src=x_refs[t])]
            group += [copy(t, 1 + j, me, (*chip, c), src=x_refs[t]) for j, chip in enumerate(chips)]
            for cp in group:
                cp.start()
            first += group
        for t in range(nt):
            for j, chip in enumerate(chips):
                copy(t, 1 + j, (*chip, c), me).wait_recv()
                fw = copy(t, 4 + j, (*chip, c), sibling)
                fw.start()
                passed.append(fw)
        for t in range(nt):
            copy(t, 0, sibling, me).wait_recv()
            for j, chip in enumerate(chips):
                copy(t, 4 + j, (*chip, 1 - c), me).wait_recv()
        for cp in first + passed:
            cp.wait_send()
        for cp in mine:
            cp.wait()

    return pl.pallas_call(
        body, name=name, in_specs=[ANY] * nt, out_specs=[ANY] * nt,
        out_shape=[jax.ShapeDtypeStruct((N_DEV,) + tuple(v.shape), v.dtype) for v in xs],
        scratch_shapes=[pltpu.SemaphoreType.DMA((7 * nt,)), pltpu.SemaphoreType.DMA((7 * nt,)),
                        pltpu.SemaphoreType.DMA((nt,))],
    )(*xs)


def _exchange(gs, name):
    nt = len(gs)

    def body(*refs):
        g_refs, out_refs = refs[:nt], refs[nt:2 * nt]
        send_sems, recv_sems, local_sems = refs[2 * nt:]
        x, y, c = _my_place()
        my_slot = 4 * x + 2 * y + c
        mine, sent = [], []
        for t in range(nt):
            cp = pltpu.make_async_copy(g_refs[t].at[my_slot], out_refs[t].at[my_slot], local_sems.at[t])
            cp.start()
            mine.append(cp)
            for rel in range(1, N_DEV):
                px, py, pc = x ^ ((rel >> 2) & 1), y ^ ((rel >> 1) & 1), c ^ (rel & 1)
                cp = pltpu.make_async_remote_copy(
                    src_ref=g_refs[t].at[4 * px + 2 * py + pc], dst_ref=out_refs[t].at[my_slot],
                    send_sem=send_sems.at[7 * t + rel - 1], recv_sem=recv_sems.at[7 * t + rel - 1],
                    device_id=(px, py, pc), device_id_type=MESH_ID)
                cp.start()
                sent.append(cp)
        for cp in sent:
            cp.wait_recv()
        for cp in sent:
            cp.wait_send()
        for cp in mine:
            cp.wait()

    return pl.pallas_call(
        body, name=name, in_specs=[ANY] * nt, out_specs=[ANY] * nt,
        out_shape=[jax.ShapeDtypeStruct(tuple(v.shape), v.dtype) for v in gs],
        scratch_shapes=[pltpu.SemaphoreType.DMA((7 * nt,)), pltpu.SemaphoreType.DMA((7 * nt,)),
                        pltpu.SemaphoreType.DMA((nt,))],
    )(*gs)


EXCHANGE_BYTES_PER_FLOP = 1.1e-4
CARRIER_OVERFILL = 1.15
ADAMW_EXCHANGE_BYTES_PER_PARAM = 2.0
PIECE_BYTES = 8 * 1024 * 1024
CARRIER_MIN_BYTES = 6 * 1024 * 1024
ROW_ALIGN = 16
BLOCKS = -1


class _Piece:
    def __init__(self, key, g, axis, lo, hi):
        self.key, self.g, self.axis, self.lo, self.hi = key, g, axis, lo, hi
        cols = g.shape[2] if axis == BLOCKS else g.shape[1] if axis == 0 else g.shape[1] // N_DEV
        self.recv_shape = jax.ShapeDtypeStruct((N_DEV, hi - lo, cols), g.dtype)
        self.nbytes = N_DEV * (hi - lo) * cols * g.dtype.itemsize


def _piece_sems(nj):
    return [pltpu.SemaphoreType.DMA((7 * nj,)), pltpu.SemaphoreType.DMA((7 * nj,)), pltpu.SemaphoreType.DMA((nj,))]


def _piece_copies(jobs, g_refs, recv_refs, sems):
    send_sems, recv_sems, local_sems = sems
    x, y, c = _my_place()
    me = 4 * x + 2 * y + c
    local, remote = [], []
    for t, (job, g, r) in enumerate(zip(jobs, g_refs, recv_refs)):
        rows = job.hi - job.lo

        def block(slot, job=job, g=g, r=r, rows=rows):
            if job.axis == BLOCKS:
                return g.at[slot, pl.ds(job.lo, rows), :]
            if job.axis == 0:
                start = pl.multiple_of(slot * (g.shape[0] // N_DEV) + job.lo, ROW_ALIGN)
                return g.at[pl.ds(start, rows), :]
            cols = r.shape[2]
            return g.at[pl.ds(job.lo, rows), pl.ds(pl.multiple_of(slot * cols, LANES), cols)]

        local.append(pltpu.make_async_copy(block(me), r.at[me], local_sems.at[t]))
        for rel in range(1, N_DEV):
            px, py, pc = x ^ ((rel >> 2) & 1), y ^ ((rel >> 1) & 1), c ^ (rel & 1)
            remote.append(pltpu.make_async_remote_copy(
                src_ref=block(4 * px + 2 * py + pc), dst_ref=r.at[me], send_sem=send_sems.at[7 * t + rel - 1],
                recv_sem=recv_sems.at[7 * t + rel - 1], device_id=(px, py, pc), device_id_type=MESH_ID))
    return local, remote


def _pieces_start(jobs, g_refs, recv_refs, sems):
    local, remote = _piece_copies(jobs, g_refs, recv_refs, sems)
    for cp in local + remote:
        cp.start()


def _pieces_wait(jobs, g_refs, recv_refs, sems):
    local, remote = _piece_copies(jobs, g_refs, recv_refs, sems)
    for cp in remote:
        cp.wait_recv()
    for cp in remote:
        cp.wait_send()
    for cp in local:
        cp.wait()


def _exchange_pieces(jobs, name):
    nj = len(jobs)

    def body(*refs):
        job_refs = (refs[:nj], refs[nj:2 * nj], refs[2 * nj:])
        _pieces_start(jobs, *job_refs)
        _pieces_wait(jobs, *job_refs)

    return pl.pallas_call(
        body, name=name, in_specs=[ANY] * nj, out_specs=[ANY] * nj, out_shape=[j.recv_shape for j in jobs],
        scratch_shapes=_piece_sems(nj),
    )(*[j.g for j in jobs])


class _GradExchange:
    def __init__(self):
        self.queue, self.recv = [], {}

    def put(self, name, layer, g):
        axis = SHARD_AXIS[name] - 1
        rows = g.shape[0] // N_DEV if axis == 0 else g.shape[0]
        if g.dtype != BF16 or g.ndim != 2:
            return False
        if rows % ROW_ALIGN or (axis == 1 and (g.shape[1] // N_DEV) % LANES):
            g = _to_blocks(g, axis)
            axis, rows = BLOCKS, g.shape[1]
        n_split = max(1, round(g.size * g.dtype.itemsize / PIECE_BYTES))
        while rows % (n_split * ROW_ALIGN):
            n_split -= 1
        for k in range(n_split):
            self.queue.append(_Piece((name, layer, k), g, axis, k * rows // n_split, (k + 1) * rows // n_split))
        return True

    def take(self, capacity):
        jobs, used = [], 0
        while capacity >= CARRIER_MIN_BYTES and self.queue and used + self.queue[0].nbytes <= CARRIER_OVERFILL * capacity:
            used += self.queue[0].nbytes
            jobs.append(self.queue.pop(0))
        return jobs

    def landed(self, jobs, recvs):
        for j, r in zip(jobs, recvs):
            self.recv[j.key] = r

    def flush(self):
        if self.queue:
            jobs, self.queue = self.queue, []
            self.landed(jobs, _exchange_pieces(jobs, "exchange_rest"))

    def pieces_of(self, name):
        return [self.recv[k] for k in sorted(k for k in self.recv if k[0] == name)]

    start = staticmethod(_pieces_start)
    wait = staticmethod(_pieces_wait)


GATHER_SPEEDUP = 2.0
GATHER_FIRST_OVERFILL = 2.2


class _WeightPiece:
    def __init__(self, key, local, axis):
        self.key, self.g, self.axis = key, local, axis
        r, c = local.shape
        self.recv_shape = jax.ShapeDtypeStruct(
            (N_DEV, r, c) if axis == BLOCKS else (r * N_DEV, c) if axis == 0 else (r, c * N_DEV), local.dtype)
        self.nbytes = N_DEV * r * c * local.dtype.itemsize


def _gather_copies(jobs, x_refs, full_refs, sems):
    send_sems, recv_sems, local_sems = sems
    x, y, c = _my_place()
    me, sibling = (x, y, c), (x, y, 1 - c)
    chips = [(1 - x, y), (x, 1 - y), (1 - x, 1 - y)]
    plans = []
    for t, (job, xr, fr) in enumerate(zip(jobs, x_refs, full_refs)):
        def blk(p, job=job, xr=xr, fr=fr):
            slot = 4 * p[0] + 2 * p[1] + p[2]
            if job.axis == BLOCKS:
                return fr.at[slot]
            if job.axis == 0:
                return fr.at[pl.ds(pl.multiple_of(slot * xr.shape[0], ROW_ALIGN), xr.shape[0]), :]
            return fr.at[:, pl.ds(pl.multiple_of(slot * xr.shape[1], LANES), xr.shape[1])]

        def copy(k, block, to, src=None, t=t, blk=blk):
            return pltpu.make_async_remote_copy(
                src_ref=blk(block) if src is None else src, dst_ref=blk(block), send_sem=send_sems.at[7 * t + k],
                recv_sem=recv_sems.at[7 * t + k], device_id=to, device_id_type=MESH_ID)

        plans.append(dict(
            mine=pltpu.make_async_copy(xr, blk(me), local_sems.at[t]),
            first=[copy(0, me, sibling, src=xr)] + [copy(1 + j, me, (*chip, c), src=xr) for j, chip in enumerate(chips)],
            landed=[copy(1 + j, (*chip, c), me) for j, chip in enumerate(chips)],
            passed=[copy(4 + j, (*chip, c), sibling) for j, chip in enumerate(chips)],
            from_sibling=[copy(0, sibling, me)] + [copy(4 + j, (*chip, 1 - c), me) for j, chip in enumerate(chips)]))
    return plans


def _gather_start(jobs, x_refs, full_refs, sems):
    for plan in _gather_copies(jobs, x_refs, full_refs, sems):
        plan["mine"].start()
        for cp in plan["first"]:
            cp.start()


def _gather_wait(jobs, x_refs, full_refs, sems):
    plans = _gather_copies(jobs, x_refs, full_refs, sems)
    for plan in plans:
        for landed, passed in zip(plan["landed"], plan["passed"]):
            landed.wait_recv()
            passed.start()
    for plan in plans:
        for cp in plan["from_sibling"]:
            cp.wait_recv()
        for cp in plan["first"] + plan["passed"]:
            cp.wait_send()
        plan["mine"].wait()


def _gather_pieces(jobs, name):
    nj = len(jobs)

    def body(*refs):
        job_refs = (refs[:nj], refs[nj:2 * nj], refs[2 * nj:])
        _gather_start(jobs, *job_refs)
        _gather_wait(jobs, *job_refs)

    return pl.pallas_call(
        body, name=name, in_specs=[ANY] * nj, out_specs=[ANY] * nj, out_shape=[j.recv_shape for j in jobs],
        scratch_shapes=_piece_sems(nj),
    )(*[j.g for j in jobs])


class _WeightGather:
    def __init__(self):
        self.queue, self.full = [], {}

    def add(self, name, layer, local, in_place):
        self.queue.append(_WeightPiece((name, layer), local, SHARD_AXIS[name] - 1 if in_place else BLOCKS))

    def take(self, capacity):
        capacity *= GATHER_SPEEDUP
        jobs, used = [], 0
        if capacity >= CARRIER_MIN_BYTES and self.queue and self.queue[0].nbytes <= GATHER_FIRST_OVERFILL * capacity:
            jobs.append(self.queue.pop(0))
            used = jobs[0].nbytes
            while self.queue and used + self.queue[0].nbytes <= CARRIER_OVERFILL * capacity:
                used += self.queue[0].nbytes
                jobs.append(self.queue.pop(0))
        return jobs

    def landed(self, jobs, fulls):
        for j, f in zip(jobs, fulls):
            self.full[j.key] = _to_full(f, SHARD_AXIS[j.key[0]] - 1) if j.axis == BLOCKS else f

    def get(self, name, layer):
        if (name, layer) not in self.full:
            at = [j.key for j in self.queue].index((name, layer))
            jobs, self.queue = self.queue[:at + 1], self.queue[at + 1:]
            self.landed(jobs, _gather_pieces(jobs, "gather_now"))
        return self.full[(name, layer)]

    start = staticmethod(_gather_start)
    wait = staticmethod(_gather_wait)


def _val(w):
    return w() if callable(w) else w


def _ffn_fwd(x, g, w_gu, w_d, tag, wg=None):
    n = _rms_fwd(x, g)
    gu = _mm(n, _val(w_gu), out_dtype=BF16, name=f"{tag}_gu", ex=wg)
    a = _swiglu_fwd(gu)
    return _mm(a, _val(w_d), alpha=0.5, res=x, name=f"{tag}_down", ex=wg), (n, gu, a)


def _put(ex, name, layer, g):
    if ex is not None:
        ex.put(name, layer, g)


def _ffn_bwd(dy, x, g, w_gu, w_d, saved, tag, ex=None, which="ffn1", layer=0):
    n, gu, a = saved
    da = _mm(dy, w_d, tb=True, alpha=0.5, out_dtype=BF16, name=f"{tag}_da", ex=ex)
    d_wd = _mm(a, dy, ta=True, alpha=0.5, out_dtype=BF16, name=f"{tag}_dwd", ex=ex)
    _put(ex, f"{which}_w_down", layer, d_wd)
    dgu = _swiglu_bwd(gu, da)
    dn = _mm(dgu, w_gu, tb=True, name=f"{tag}_dn", ex=ex)
    d_wgu = _mm(n, dgu, ta=True, out_dtype=BF16, name=f"{tag}_dwgu", ex=ex)
    _put(ex, f"{which}_w_gate_up", layer, d_wgu)
    dx, dg = _rms_bwd(x, g, dn, dy)
    return dx, dg, d_wgu, d_wd


def _sb_mixer_fwd(h, w_qkv, w_o, x, wg=None):
    qkv = _mm(h, _val(w_qkv), out_dtype=BF16, name="sb_qkv", ex=wg)
    o, ltot = _sb_fwd(qkv, qkv.shape[1] // (3 * LANES), wg)
    return _mm(o, _val(w_o), res=x, name="sb_out", ex=wg), (qkv, o, ltot)


def _sb_mixer_bwd(dy, h, w_qkv, w_o, saved, ex=None, layer=0):
    qkv, o, ltot = saved
    n_heads = w_o.shape[0] // LANES
    do = _mm(dy, w_o, tb=True, out_dtype=BF16, name="sb_do", ex=ex)
    d_wo = _mm(o, dy, ta=True, out_dtype=BF16, name="sb_dwo", ex=ex)
    _put(ex, "sb_w_o", layer, d_wo)
    dq, dk, dv = _sb_bwd(qkv, do, ltot, n_heads, ex)
    dqkv = jnp.concatenate([dq, dk.astype(BF16), dv.astype(BF16)], axis=1)
    dh = _mm(dqkv, w_qkv, tb=True, name="sb_dh", ex=ex)
    d_wqkv = _mm(h, dqkv, ta=True, out_dtype=BF16, name="sb_dwqkv", ex=ex)
    _put(ex, "sb_w_qkv", layer, d_wqkv)
    return dh, d_wqkv, d_wo


def _dil_cols(gi):
    ng = len(DIL_PATTERNS)
    return dict(q_blk=lambda n, st: n * 2 * ng + gi, k_blk=lambda n, st: n * 2 * ng + ng + gi,
                v_blk=lambda n, st: n * 3 * ng + 2 * ng + gi)


def _dil_mixer_fwd(h, w_qkv, w_o, x, tabs, wg=None):
    s = h.shape[0]
    qkv = _mm(h, _val(w_qkv), name="dil_qkv", ex=wg)
    n_all = qkv.shape[1] // (3 * LANES)
    hg = n_all // len(DIL_PATTERNS)
    qk = _rope(qkv, tabs[0], tabs[1], 2 * n_all, 16, "dil_rope")
    os_, lses = [], []
    for gi, (window, dil) in enumerate(DIL_PATTERNS):
        o, lse = _band_fwd(qk.reshape(s // dil, -1), qk.reshape(s // dil, -1), qkv.reshape(s // dil, -1),
                           n_cls=dil, n_steps=1, hpb=hg, group=1, **_dil_cols(gi),
                           max_dist=window // dil, scale=LANES ** -0.5, sinks=None, name=f"dil_fwd{gi}")
        os_.append(o.reshape(s, hg * LANES))
        lses.append(lse.reshape(s, hg * LANES))
    mixed = _dil_mix_fwd(os_, lses)
    return _mm(mixed, _val(w_o), res=x, name="dil_out", ex=wg), (qkv, qk, os_, lses, mixed)


def _dil_mixer_bwd(dy, h, w_qkv, w_o, saved, tabs_bwd, ex=None):
    qkv, qk, os_, lses, mixed = saved
    s = h.shape[0]
    n_all = w_o.shape[0] // LANES
    hg = n_all // len(DIL_PATTERNS)
    dmixed = _mm(dy, w_o, tb=True, out_dtype=BF16, name="dil_dmix", ex=ex)
    d_wo = _mm(mixed, dy, ta=True, out_dtype=BF16, name="dil_dwo", ex=ex)
    _put(ex, "dil_w_o", 0, d_wo)
    mix_out = _dil_mix_bwd(os_, lses, dmixed)
    dos, dlses = mix_out[:3], mix_out[3:]
    dqs, dks, dvs = [], [], []
    for gi, (window, dil) in enumerate(DIL_PATTERNS):
        length = s // dil
        dq, dkc, dkp, dvc, dvp = _band_bwd(
            qk.reshape(length, -1), qk.reshape(length, -1), qkv.reshape(length, -1), os_[gi].reshape(length, -1),
            dos[gi].reshape(length, -1), lses[gi].reshape(length, -1), dlses[gi].reshape(length, -1),
            n_cls=dil, n_steps=1, hpb=hg, group=1, **_dil_cols(gi), max_dist=window // dil,
            scale=LANES ** -0.5, sinks=None, name=f"dil_bwd{gi}")
        dqs.append(dq.reshape(s, -1))
        dks.append(_band_fold(dkc, dkp, n_cls=dil, n_heads=hg, group=1, name=f"dil_foldk{gi}").reshape(s, -1))
        dvs.append(_band_fold(dvc, dvp, n_cls=dil, n_heads=hg, group=1, name=f"dil_foldv{gi}").reshape(s, -1))
    dqk_rot = jnp.concatenate(dqs + dks, axis=1)
    dqk = _rope(dqk_rot, tabs_bwd[0], tabs_bwd[1], 2 * n_all, 16, "dil_rope_bwd")
    dqkv = jnp.concatenate([dqk] + [t.astype(BF16) for t in dvs], axis=1)
    dh = _mm(dqkv, w_qkv, tb=True, name="dil_dh", ex=ex)
    d_wqkv = _mm(h, dqkv, ta=True, out_dtype=BF16, name="dil_dwqkv", ex=ex)
    _put(ex, "dil_w_qkv", 0, d_wqkv)
    return dh, d_wqkv, d_wo


def _pad_heads(w, axis):
    shape = list(w.shape)
    n = shape[axis] // SWA_HEAD_DIM
    w = w.reshape(shape[:axis] + [n, SWA_HEAD_DIM] + shape[axis + 1:])
    pad = [(0, 0)] * w.ndim
    pad[axis + 1] = (0, LANES - SWA_HEAD_DIM)
    shape[axis] = n * LANES
    return jnp.pad(w, pad).reshape(shape)


def _unpad_heads(w, axis):
    shape = list(w.shape)
    n = shape[axis] // LANES
    w = w.reshape(shape[:axis] + [n, LANES] + shape[axis + 1:])
    w = lax.slice_in_dim(w, 0, SWA_HEAD_DIM, axis=axis + 1)
    shape[axis] = n * SWA_HEAD_DIM
    return w.reshape(shape)


def _swa_mixer_fwd(h, w_qkv_p, b_qkv_p, sinks_b, w_o_p, b_o, x, tabs, wg=None):
    nq = w_o_p.shape[0] // LANES
    nkv = nq // SWA_GROUP
    qkv = _mm(h, w_qkv_p, bias=b_qkv_p, name="swa_qkv", ex=wg)
    qk = _rope(qkv, tabs[0], tabs[1], nq + nkv, 8, "swa_rope")
    o, lse = _band_fwd(qk, qk, qkv, n_cls=1, n_steps=nkv, hpb=SWA_GROUP, group=SWA_GROUP, q_blk=lambda n, st: st,
                       k_blk=lambda n, st: nq + st, v_blk=lambda n, st: nq + nkv + st,
                       max_dist=SWA_WINDOW - 1, scale=SWA_HEAD_DIM ** -0.5, sinks=sinks_b, name="swa_fwd")
    return _mm(o, w_o_p, res=x, bias=b_o, name="swa_out"), (qkv, qk, o, lse)


def _swa_mixer_bwd(dy, h, w_qkv_p, sinks_b, w_o_p, saved, tabs_bwd, ex=None):
    qkv, qk, o, lse = saved
    nq = w_o_p.shape[0] // LANES
    nkv = nq // SWA_GROUP
    do = _mm(dy, w_o_p, tb=True, name="swa_do", ex=ex)
    d_wo_p = _mm(o, dy, ta=True, out_dtype=BF16, name="swa_dwo", ex=ex)
    d_bo = _colsum(dy, "swa_dbo")
    dq, dkc, dkp, dvc, dvp, dsink = _band_bwd(
        qk, qk, qkv, o, do, lse, None, n_cls=1, n_steps=nkv, hpb=SWA_GROUP, group=SWA_GROUP, q_blk=lambda n, st: st,
        k_blk=lambda n, st: nq + st, v_blk=lambda n, st: nq + nkv + st, max_dist=SWA_WINDOW - 1,
        scale=SWA_HEAD_DIM ** -0.5, sinks=sinks_b, name="swa_bwd")
    dk = _band_fold(dkc, dkp, n_cls=1, n_heads=nkv, group=1, name="swa_foldk")
    dv = _band_fold(dvc, dvp, n_cls=1, n_heads=nkv, group=1, name="swa_foldv")
    dqk = _rope(jnp.concatenate([dq, dk], axis=1), tabs_bwd[0], tabs_bwd[1], nq + nkv, 8, "swa_rope_bwd")
    dqkv = jnp.concatenate([dqk, dv.astype(BF16)], axis=1)
    d_bqkv_p = _colsum(dqkv, "swa_dbqkv")
    dh = _mm(dqkv, w_qkv_p, tb=True, name="swa_dh", ex=ex)
    d_wqkv_p = _mm(h, dqkv, ta=True, out_dtype=BF16, name="swa_dwqkv", ex=ex)
    return dh, d_wqkv_p, d_bqkv_p, dsink, d_wo_p, d_bo


def _xattn_layer_fwd(x, mem, g_x, g_m, w_q, w_kv, w_o):
    hq = _rms_fwd(x, g_x, "rms_fwd")
    hm = _rms_fwd(mem, g_m, "rms_mem_fwd")
    q = _mm(hq, _val(w_q), out_dtype=BF16, name="xa_q")
    kv = _mm(hm, _val(w_kv), out_dtype=BF16, name="xa_kv")
    o = _xattn_fwd(q, kv)
    return _mm(o, _val(w_o), res=x, name="xa_out"), (hq, hm, q, kv, o)


def _xattn_layer_bwd(dy, x, mem, g_x, g_m, w_q, w_kv, w_o, saved):
    hq, hm, q, kv, o = saved
    do = _mm(dy, w_o, tb=True, out_dtype=BF16, name="xa_do")
    d_wo = _mm(o, dy, ta=True, out_dtype=BF16, name="xa_dwo")
    dq, dk, dv = _xattn_bwd(q, kv, do)
    dkv = jnp.concatenate([dk, dv], axis=1).astype(BF16)
    dhq = _mm(dq, w_q, tb=True, name="xa_dhq")
    d_wq = _mm(hq, dq, ta=True, out_dtype=BF16, name="xa_dwq")
    dhm = _mm(dkv, w_kv, tb=True, name="xa_dhm")
    d_wkv = _mm(hm, dkv, ta=True, out_dtype=BF16, name="xa_dwkv")
    dx, dg_x = _rms_bwd(x, g_x, dhq, dy)
    _, dg_m = _rms_bwd(mem, g_m, dhm, None, "rms_mem_bwd")
    return dx, dg_x, dg_m, d_wq, d_wkv, d_wo


def _to_full(gathered, axis):
    t = jnp.moveaxis(gathered, 0, axis)
    shape = list(t.shape)
    return t.reshape(shape[:axis] + [shape[axis] * shape[axis + 1]] + shape[axis + 2:])


def _to_blocks(full, axis):
    shape = list(full.shape)
    t = full.reshape(shape[:axis] + [N_DEV, shape[axis] // N_DEV] + shape[axis + 1:])
    return jnp.moveaxis(t, axis, 0)


SHARD_AXIS = {
    "ffn1_w_gate_up": 2, "ffn1_w_down": 1, "sb_w_qkv": 2, "sb_w_o": 1, "dil_w_qkv": 2, "dil_w_o": 2,
    "swa_w_qkv": 2, "swa_b_qkv": 1, "swa_w_o": 1, "swa_b_o": 1, "xattn_w_q": 1, "xattn_w_kv": 1, "xattn_w_o": 2,
    "ffn2_w_gate_up": 2, "ffn2_w_down": 1,
}
SMALL = ("ffn1_norm", "mix_norm", "xattn_norm", "mem_norm", "ffn2_norm", "final_norm", "swa_sinks")
WEIGHTS = ("ffn1_norm", "ffn1_w_gate_up", "ffn1_w_down", "mix_norm", "sb_w_qkv", "sb_w_o", "dil_w_qkv", "dil_w_o",
           "swa_w_qkv", "swa_b_qkv", "swa_sinks", "swa_w_o", "swa_b_o", "xattn_norm", "mem_norm", "xattn_w_q",
           "xattn_w_kv", "xattn_w_o", "ffn2_norm", "ffn2_w_gate_up", "ffn2_w_down", "final_norm")


def _flat2(a):
    return a.reshape(-1, a.shape[-1])


def _pack_small(vals, d):
    rows = [vals[n].reshape(-1, d) for n in SMALL[:5]] + [vals["final_norm"].reshape(1, d)]
    sk = vals["swa_sinks"].reshape(1, -1)
    rows.append(jnp.pad(sk, ((0, 0), (0, d - sk.shape[1]))))
    rows.append(jnp.zeros((2, d), F32))
    return jnp.concatenate(rows, axis=0)


def _unpack_small(packed, like):
    out, r = {}, 0
    for n in SMALL[:5]:
        k = like[n].shape[0]
        out[n] = packed[r:r + k]
        r += k
    out["final_norm"] = packed[r]
    out["swa_sinks"] = packed[r + 1:r + 2, :like["swa_sinks"].shape[1]]
    return out


def _local_step(x0, mem0, positions, target, full, norm, ex=None, wg=None):
    d = x0.shape[1]
    names = list(SHARD_AXIS)
    sinks_b = jnp.repeat(norm["swa_sinks"], LANES, axis=1)
    tabs_dil, tabs_dil_bwd = _rope_tables(positions, 32), _rope_tables(positions, 32, -1.0)
    tabs_swa, tabs_swa_bwd = _rope_tables(positions, 16), _rope_tables(positions, 16, -1.0)

    def vec(name, i):
        return norm[name][i:i + 1]

    saved = []
    xc = x0
    for i in range(DEPTH):
        kind, j = i % 3, i // 3
        rec = {"x0": xc}
        xc, rec["ffn1"] = _ffn_fwd(xc, vec("ffn1_norm", i), lambda: full["ffn1_w_gate_up"][i],
                                   lambda: full["ffn1_w_down"][i], "ffn", wg)
        rec["x1"] = xc
        h = _rms_fwd(xc, vec("mix_norm", i))
        rec["h"] = h
        if kind == 0:
            xc, rec["mix"] = _sb_mixer_fwd(h, lambda: full["sb_w_qkv"][j], lambda: full["sb_w_o"][j], xc, wg)
        elif kind == 1:
            xc, rec["mix"] = _dil_mixer_fwd(h, lambda: full["dil_w_qkv"][j], lambda: full["dil_w_o"][j], xc, tabs_dil, wg)
        else:
            swa_w_qkv_p = _pad_heads(full["swa_w_qkv"][0], 1)
            swa_b_qkv_p = _pad_heads(full["swa_b_qkv"][0][None], 1)
            swa_w_o_p = _pad_heads(full["swa_w_o"][0], 0)
            xc, rec["mix"] = _swa_mixer_fwd(h, swa_w_qkv_p, swa_b_qkv_p, sinks_b, swa_w_o_p, full["swa_b_o"][0][None],
                                            xc, tabs_swa, wg)
        rec["x2"] = xc
        xc, rec["xa"] = _xattn_layer_fwd(xc, mem0, vec("xattn_norm", i), vec("mem_norm", i),
                                         lambda: full["xattn_w_q"][i], lambda: full["xattn_w_kv"][i],
                                         lambda: full["xattn_w_o"][i])
        rec["x3"] = xc
        xc, rec["ffn2"] = _ffn_fwd(xc, vec("ffn2_norm", i), lambda: full["ffn2_w_gate_up"][i],
                                   lambda: full["ffn2_w_down"][i], "ffn", wg)
        saved.append(rec)

    loss_part, dx, dg_final = _loss_head(xc, norm["final_norm"].reshape(1, d), target)

    gfull = {n: [None] * full[n].shape[0] for n in names}
    gsmall = {n: [None] * DEPTH for n in SMALL[:5]}
    gsmall["final_norm"] = dg_final
    gsmall["swa_sinks"] = jnp.zeros_like(norm["swa_sinks"])
    for i in reversed(range(DEPTH)):
        kind, j = i % 3, i // 3
        rec = saved[i]
        dx, gsmall["ffn2_norm"][i], gfull["ffn2_w_gate_up"][i], gfull["ffn2_w_down"][i] = _ffn_bwd(
            dx, rec["x3"], vec("ffn2_norm", i), full["ffn2_w_gate_up"][i], full["ffn2_w_down"][i], rec["ffn2"], "ffn",
            ex, "ffn2", i)
        (dx, gsmall["xattn_norm"][i], gsmall["mem_norm"][i], gfull["xattn_w_q"][i], gfull["xattn_w_kv"][i],
         gfull["xattn_w_o"][i]) = _xattn_layer_bwd(dx, rec["x2"], mem0, vec("xattn_norm", i), vec("mem_norm", i),
                                                   full["xattn_w_q"][i], full["xattn_w_kv"][i], full["xattn_w_o"][i],
                                                   rec["xa"])
        for n in ("xattn_w_q", "xattn_w_kv", "xattn_w_o"):
            _put(ex, n, i, gfull[n][i])
        if kind == 0:
            dh, gfull["sb_w_qkv"][j], gfull["sb_w_o"][j] = _sb_mixer_bwd(
                dx, rec["h"], full["sb_w_qkv"][j], full["sb_w_o"][j], rec["mix"], ex, j)
        elif kind == 1:
            dh, gfull["dil_w_qkv"][j], gfull["dil_w_o"][j] = _dil_mixer_bwd(
                dx, rec["h"], full["dil_w_qkv"][j], full["dil_w_o"][j], rec["mix"], tabs_dil_bwd, ex)
        else:
            dh, d_wqkv_p, d_bqkv_p, dsink, d_wo_p, d_bo = _swa_mixer_bwd(
                dx, rec["h"], swa_w_qkv_p, sinks_b, swa_w_o_p, rec["mix"], tabs_swa_bwd, ex)
            gfull["swa_w_qkv"][j] = _unpad_heads(d_wqkv_p, 1)
            gfull["swa_b_qkv"][j] = _unpad_heads(d_bqkv_p, 1)[0]
            gfull["swa_w_o"][j] = _unpad_heads(d_wo_p, 0)
            gfull["swa_b_o"][j] = d_bo[0]
            gsmall["swa_sinks"] = dsink.reshape(1, -1, LANES)[:, :, 0]
            _put(ex, "swa_w_o", j, gfull["swa_w_o"][j])
            _put(ex, "swa_w_qkv", j, gfull["swa_w_qkv"][j])
        dx, gsmall["mix_norm"][i] = _rms_bwd(rec["x1"], vec("mix_norm", i), dh, dx)
        dx, gsmall["ffn1_norm"][i], gfull["ffn1_w_gate_up"][i], gfull["ffn1_w_down"][i] = _ffn_bwd(
            dx, rec["x0"], vec("ffn1_norm", i), full["ffn1_w_gate_up"][i], full["ffn1_w_down"][i], rec["ffn1"], "ffn",
            ex, "ffn1", i)
    for n in SMALL[:5]:
        gsmall[n] = jnp.concatenate(gsmall[n], axis=0)
    return loss_part[0, 0], dx, gfull, gsmall


def _train_step(x, mem, positions, loss_target, w, m, v):
    d = x.shape[2]
    names = list(SHARD_AXIS)
    norm = {n: w[n] for n in SMALL}

    def in_place(n):
        return w[n].shape[2] % LANES == 0 if SHARD_AXIS[n] == 2 else w[n].shape[1] % ROW_ALIGN == 0

    first = [n for n in names if w[n].ndim != 3]
    gathered = _all_gather([_flat2(w[n]) for n in first], "gather_weights")
    stacked = {n: _to_full(g.reshape((N_DEV,) + w[n].shape), SHARD_AXIS[n]) for n, g in zip(first, gathered)}
    wg = _WeightGather()
    for i in range(DEPTH):
        mixer = (("sb_w_qkv", "sb_w_o"), ("dil_w_qkv", "dil_w_o"), ("swa_w_qkv", "swa_w_o"))[i % 3]
        for n in ("ffn1_w_gate_up", "ffn1_w_down") + mixer + ("xattn_w_q", "xattn_w_kv", "xattn_w_o",
                                                               "ffn2_w_gate_up", "ffn2_w_down"):
            layer = i // 3 if n in mixer else i
            wg.add(n, layer, w[n][layer].astype(BF16), in_place(n))

    class Layers:
        def __init__(self, n):
            self.n, self.shape = n, w[n].shape[:1]

        def __getitem__(self, layer):
            return wg.get(self.n, layer) if w[self.n].ndim == 3 else stacked[self.n][layer]

    full = {n: Layers(n) for n in names}
    ex = _GradExchange()
    loss_part, dx, gfull, gsmall = _local_step(x[0], mem[0], positions, loss_target[0], full, norm, ex, wg)
    loss = lax.psum(loss_part, MESH_AXES)
    grad_x = dx[None]

    grad, delta, new_m, new_v = {}, {}, {}, {}

    def update(n, parts, carrier):
        outs = _adamw(parts, _flat2(w[n]), _flat2(m[n]), _flat2(v[n]), "adamw", carrier)
        grad[n], delta[n], new_m[n], new_v[n] = (o.reshape(w[n].shape) for o in outs)

    taken = {k[0] for k in ex.recv} | {p.key[0] for p in ex.queue}
    late = {p.key[0] for p in ex.queue}
    for n in names:
        if n in taken and n not in late:
            update(n, ex.pieces_of(n), ex)
    ex.flush()
    for n in names:
        if n in late:
            update(n, ex.pieces_of(n), None)
    rest = [n for n in names if n not in taken]
    blocks = [_to_blocks(jnp.stack(gfull[n], axis=0), SHARD_AXIS[n]) for n in rest]
    blocks = [b.reshape(N_DEV, -1, b.shape[-1]) for b in blocks]
    for n, received in zip(rest, _exchange(blocks, "exchange_grads")):
        update(n, [received], None)

    small_parts = _all_gather([_pack_small(gsmall, d)], "gather_small_grads")[0]
    outs = _adamw([small_parts], _pack_small(norm, d), _pack_small({n: m[n] for n in SMALL}, d),
                  _pack_small({n: v[n] for n in SMALL}, d), "adamw_small")
    for res, o in zip((grad, delta, new_m, new_v), outs):
        res.update(_unpack_small(o, norm))
    return loss, grad_x, grad, delta, new_m, new_v


def kernel(x, mem, positions, ffn1_norm, ffn1_w_gate_up, ffn1_w_down, mix_norm, sb_w_qkv, sb_w_o, dil_w_qkv, dil_w_o, swa_w_qkv, swa_b_qkv, swa_sinks, swa_w_o, swa_b_o, xattn_norm, mem_norm, xattn_w_q, xattn_w_kv, xattn_w_o, ffn2_norm, ffn2_w_gate_up, ffn2_w_down, final_norm, loss_target, m_ffn1_norm, m_ffn1_w_gate_up, m_ffn1_w_down, m_mix_norm, m_sb_w_qkv, m_sb_w_o, m_dil_w_qkv, m_dil_w_o, m_swa_w_qkv, m_swa_b_qkv, m_swa_sinks, m_swa_w_o, m_swa_b_o, m_xattn_norm, m_mem_norm, m_xattn_w_q, m_xattn_w_kv, m_xattn_w_o, m_ffn2_norm, m_ffn2_w_gate_up, m_ffn2_w_down, m_final_norm, v_ffn1_norm, v_ffn1_w_gate_up, v_ffn1_w_down, v_mix_norm, v_sb_w_qkv, v_sb_w_o, v_dil_w_qkv, v_dil_w_o, v_swa_w_qkv, v_swa_b_qkv, v_swa_sinks, v_swa_w_o, v_swa_b_o, v_xattn_norm, v_mem_norm, v_xattn_w_q, v_xattn_w_kv, v_xattn_w_o, v_ffn2_norm, v_ffn2_w_gate_up, v_ffn2_w_down, v_final_norm):
    args = dict(locals())
    w = {n: args[n] for n in WEIGHTS}
    m = {n: args["m_" + n] for n in WEIGHTS}
    v = {n: args["v_" + n] for n in WEIGHTS}
    loss, grad_x, grad, delta, new_m, new_v = _train_step(x, mem, positions, loss_target, w, m, v)
    return (loss, grad_x, *[grad[n] for n in WEIGHTS], *[delta[n] for n in WEIGHTS],
            *[new_m[n] for n in WEIGHTS], *[new_v[n] for n in WEIGHTS])
```

```python
import jax
import jax.numpy as jnp
from jax import lax
from jax.experimental import pallas as pl
from jax.experimental.pallas import tpu as pltpu

F32 = jnp.float32
BF16 = jnp.bfloat16

N_DEV = 8
MESH_AXES = ("x", "y", "c")
LANES = 128
BAND = 128
NORM_EPS = 1e-6
ROPE_THETA = 500000.0
DIL_PATTERNS = ((128, 1), (512, 4), (2048, 16))
SWA_HEAD_DIM = 64
SWA_GROUP = 8
SWA_WINDOW = 128
XA_HEADS = 4
DEPTH = 4
ADAM_LR, ADAM_B1, ADAM_B2, ADAM_EPS, ADAM_WD, ADAM_STEP = 0.001, 0.9, 0.999, 1e-08, 0.01, 10
VMEM_LIMIT = 56 * 1024 * 1024
NEG = -1e30

NT = (((1,), (1,)), ((), ()))
TN = (((0,), (0,)), ((), ()))


def _tile(n, prefs):
    for p in prefs:
        if n % p == 0:
            return p
    return n


def _cp(sem):
    return pltpu.CompilerParams(dimension_semantics=sem, vmem_limit_bytes=VMEM_LIMIT)


MM_TILE_SIZES = (2816, 2048, 1408, 1024, 512, 256, 128)
MM_VMEM_BUDGET = 40 * 1024 * 1024
MM_STEP_BYTES = 1.2e6


def _mm_tiles(m, n, k, ea, eb, eo, has_res):
    best = None
    for bm in [c for c in MM_TILE_SIZES if m % c == 0] or [m]:
        for bn in [c for c in MM_TILE_SIZES if n % c == 0] or [n]:
            for bk in [c for c in MM_TILE_SIZES if k % c == 0] or [k]:
                vmem = 2 * (bm * bk * ea + bk * bn * eb) + bm * bn * 4 + 2 * bm * bn * (eo + (4 if has_res else 0))
                vmem += (bm * bk * 2 if ea == 4 else 0) + (bk * bn * 2 if eb == 4 else 0)
                if vmem > MM_VMEM_BUDGET:
                    continue
                ni, nj, nk = m // bm, n // bn, k // bk
                traffic = (m * k * ea * (nj if nk > 1 else 1) + k * n * eb * (ni if nk > 1 or nj > 1 else 1)
                           + m * n * (eo + (4 if has_res else 0)) + ni * nj * nk * MM_STEP_BYTES)
                if best is None or traffic < best[0]:
                    best = (traffic, (bm, bn, bk))
    return best[1]
def _mm(a, b, *, ta=False, tb=False, out_dtype=F32, alpha=1.0, res=None, bias=None, name, ex=None):
    kdim, m = a.shape if ta else a.shape[::-1]
    kdim2, n = b.shape[::-1] if tb else b.shape
    assert kdim == kdim2, (a.shape, b.shape, ta, tb)
    bm, bn, bk = _mm_tiles(m, n, kdim, a.dtype.itemsize, b.dtype.itemsize, jnp.dtype(out_dtype).itemsize, res is not None)
    nk = kdim // bk
    grid = (m // bm, n // bn, nk)
    dn = (((0 if ta else 1,), (1 if tb else 0,)), ((), ()))
    has_res, has_bias = res is not None, bias is not None
    jobs = ex.take(2.0 * m * n * kdim * EXCHANGE_BYTES_PER_FLOP) if ex is not None else []
    nj = len(jobs)
    n_in = 2 + has_res + has_bias

    def body(*refs):
        a_ref, b_ref = refs[0], refs[1]
        res_ref = refs[2] if has_res else None
        bias_ref = refs[2 + has_res] if has_bias else None
        o_ref, acc_ref = refs[n_in + nj], refs[n_in + 2 * nj + 1]
        k = pl.program_id(2)
        if nj:
            job_refs = (refs[n_in:n_in + nj], refs[n_in + nj + 1:n_in + 2 * nj + 1], refs[n_in + 2 * nj + 2:])
            ids = [pl.program_id(t) for t in range(3)]

            @pl.when((ids[0] == 0) & (ids[1] == 0) & (ids[2] == 0))
            def _():
                ex.start(jobs, *job_refs)

            @pl.when((ids[0] == grid[0] - 1) & (ids[1] == grid[1] - 1) & (ids[2] == grid[2] - 1))
            def _():
                ex.wait(jobs, *job_refs)

        def finish(r):
            if alpha != 1.0:
                r = r * alpha
            if has_bias:
                r = r + bias_ref[...]
            if has_res:
                r = r + res_ref[...]
            o_ref[...] = r.astype(o_ref.dtype)

        def product():
            return lax.dot_general(a_ref[...].astype(BF16), b_ref[...].astype(BF16), dn, preferred_element_type=F32)

        if nk == 1:
            finish(product())
        else:
            @pl.when(k == 0)
            def _():
                acc_ref[...] = jnp.zeros_like(acc_ref)

            acc_ref[...] += product()
            pl.when(k == nk - 1)(lambda: finish(acc_ref[...]))

    a_spec = pl.BlockSpec((bk, bm), lambda i, j, k: (k, i)) if ta else pl.BlockSpec((bm, bk), lambda i, j, k: (i, k))
    b_spec = pl.BlockSpec((bn, bk), lambda i, j, k: (j, k)) if tb else pl.BlockSpec((bk, bn), lambda i, j, k: (k, j))
    ins, specs = [a, b], [a_spec, b_spec]
    if has_res:
        ins.append(res)
        specs.append(pl.BlockSpec((bm, bn), lambda i, j, k: (i, j)))
    if has_bias:
        ins.append(bias)
        specs.append(pl.BlockSpec((1, bn), lambda i, j, k: (0, j)))
    out_spec = pl.BlockSpec((bm, bn), lambda i, j, k: (i, j))
    out_shape = jax.ShapeDtypeStruct((m, n), out_dtype)
    scratch = [pltpu.VMEM((bm, bn), F32)]
    if not nj:
        return pl.pallas_call(
            body, name=name, grid=grid, in_specs=specs, out_specs=out_spec, out_shape=out_shape,
            scratch_shapes=scratch, compiler_params=_cp(("parallel", "parallel", "arbitrary")),
        )(*ins)
    outs = pl.pallas_call(
        body, name=name + "_carry", grid=grid, in_specs=specs + [ANY] * nj, out_specs=[out_spec] + [ANY] * nj,
        out_shape=[out_shape] + [j.recv_shape for j in jobs], scratch_shapes=scratch + _piece_sems(nj),
        compiler_params=_cp(("arbitrary", "arbitrary", "arbitrary")),
    )(*ins, *[j.g for j in jobs])
    ex.landed(jobs, outs[1:])
    return outs[0]


def _rms_fwd(x, g, name="rms_fwd"):
    s, d = x.shape
    bm = _tile(s, (256, 128, 8))

    def body(x_ref, g_ref, o_ref):
        xv = x_ref[...]
        r = lax.rsqrt(jnp.mean(xv * xv, axis=-1, keepdims=True) + NORM_EPS)
        o_ref[...] = (xv * r * g_ref[...]).astype(o_ref.dtype)

    return pl.pallas_call(
        body, name=name, grid=(s // bm,),
        in_specs=[pl.BlockSpec((bm, d), lambda i: (i, 0)), pl.BlockSpec((1, d), lambda i: (0, 0))],
        out_specs=pl.BlockSpec((bm, d), lambda i: (i, 0)),
        out_shape=jax.ShapeDtypeStruct((s, d), BF16), compiler_params=_cp(("parallel",)),
    )(x, g)


def _rms_bwd(x, g, dn, dy=None, name="rms_bwd"):
    s, d = x.shape
    bm = _tile(s, (256, 128, 8))
    has_dy = dy is not None

    def body(*refs):
        x_ref, g_ref, dn_ref = refs[:3]
        dy_ref = refs[3] if has_dy else None
        dx_ref, dg_ref = refs[-2], refs[-1]

        @pl.when(pl.program_id(0) == 0)
        def _():
            dg_ref[...] = jnp.zeros_like(dg_ref)

        xv = x_ref[...]
        r = lax.rsqrt(jnp.mean(xv * xv, axis=-1, keepdims=True) + NORM_EPS)
        xh = xv * r
        dnv = dn_ref[...].astype(F32)
        dxh = dnv * g_ref[...]
        dx = r * (dxh - xh * jnp.mean(dxh * xh, axis=-1, keepdims=True))
        if has_dy:
            dx = dx + dy_ref[...]
        dx_ref[...] = dx
        dg_ref[...] += jnp.sum(dnv * xh, axis=0, keepdims=True)

    row = pl.BlockSpec((bm, d), lambda i: (i, 0))
    vec = pl.BlockSpec((1, d), lambda i: (0, 0))
    ins, specs = [x, g, dn], [row, vec, row]
    if has_dy:
        ins.append(dy)
        specs.append(row)
    return pl.pallas_call(
        body, name=name, grid=(s // bm,), in_specs=specs, out_specs=[row, vec],
        out_shape=[jax.ShapeDtypeStruct((s, d), F32), jax.ShapeDtypeStruct((1, d), F32)],
        compiler_params=_cp(("arbitrary",)),
    )(*ins)


def _loss_head(x, g, target):
    s, d = x.shape
    bm = _tile(s, (256, 128, 8))

    def body(x_ref, g_ref, t_ref, loss_ref, dx_ref, dg_ref):
        @pl.when(pl.program_id(0) == 0)
        def _():
            dg_ref[...] = jnp.zeros_like(dg_ref)
            loss_ref[...] = jnp.zeros_like(loss_ref)

        xv = x_ref[...]
        gv = g_ref[...]
        r = lax.rsqrt(jnp.mean(xv * xv, axis=-1, keepdims=True) + NORM_EPS)
        xh = xv * r
        err = xh * gv - t_ref[...]
        part = 0.5 * jnp.sum(jnp.mean(err * err, axis=-1, keepdims=True), axis=0, keepdims=True)
        loss_ref[...] += jnp.broadcast_to(part, loss_ref.shape)
        dyv = err * (1.0 / d)
        dxh = dyv * gv
        dx_ref[...] = r * (dxh - xh * jnp.mean(dxh * xh, axis=-1, keepdims=True))
        dg_ref[...] += jnp.sum(dyv * xh, axis=0, keepdims=True)

    row = pl.BlockSpec((bm, d), lambda i: (i, 0))
    vec = pl.BlockSpec((1, d), lambda i: (0, 0))
    return pl.pallas_call(
        body, name="loss_head", grid=(s // bm,), in_specs=[row, vec, row],
        out_specs=[pl.BlockSpec((1, LANES), lambda i: (0, 0)), row, vec],
        out_shape=[jax.ShapeDtypeStruct((1, LANES), F32), jax.ShapeDtypeStruct((s, d), F32),
                   jax.ShapeDtypeStruct((1, d), F32)],
        compiler_params=_cp(("arbitrary",)),
    )(x, g, target)


def _colsum(x, name="colsum"):
    s, n = x.shape
    bm = _tile(s, (256, 128, 8))

    def body(x_ref, o_ref):
        @pl.when(pl.program_id(0) == 0)
        def _():
            o_ref[...] = jnp.zeros_like(o_ref)

        o_ref[...] += jnp.sum(x_ref[...].astype(F32), axis=0, keepdims=True)

    return pl.pallas_call(
        body, name=name, grid=(s // bm,), in_specs=[pl.BlockSpec((bm, n), lambda i: (i, 0))],
        out_specs=pl.BlockSpec((1, n), lambda i: (0, 0)), out_shape=jax.ShapeDtypeStruct((1, n), F32),
        compiler_params=_cp(("arbitrary",)),
    )(x)


def _swiglu_fwd(gu):
    s, f2 = gu.shape
    f = f2 // 2
    bm, bf = _tile(s, (256, 128, 8)), _tile(f, (512, 256, 128))

    def body(gu_ref, o_ref):
        for c in range(0, f, bf):
            gv = gu_ref[:, c:c + bf].astype(F32)
            o_ref[:, c:c + bf] = (gv / (1.0 + jnp.exp(-gv)) * gu_ref[:, f + c:f + c + bf].astype(F32)).astype(o_ref.dtype)

    return pl.pallas_call(
        body, name="swiglu_fwd", grid=(s // bm,), in_specs=[pl.BlockSpec((bm, f2), lambda i: (i, 0))],
        out_specs=pl.BlockSpec((bm, f), lambda i: (i, 0)),
        out_shape=jax.ShapeDtypeStruct((s, f), BF16), compiler_params=_cp(("parallel",)),
    )(gu)


def _swiglu_bwd(gu, da):
    s, f2 = gu.shape
    f = f2 // 2
    bm, bf = _tile(s, (128, 8)), _tile(f, (512, 256, 128))

    def body(gu_ref, da_ref, o_ref):
        for c in range(0, f, bf):
            gv = gu_ref[:, c:c + bf].astype(F32)
            dav = da_ref[:, c:c + bf].astype(F32)
            sig = 1.0 / (1.0 + jnp.exp(-gv))
            o_ref[:, c:c + bf] = (dav * gu_ref[:, f + c:f + c + bf].astype(F32)
                                  * (sig * (1.0 + gv * (1.0 - sig)))).astype(o_ref.dtype)
            o_ref[:, f + c:f + c + bf] = (dav * gv * sig).astype(o_ref.dtype)

    return pl.pallas_call(
        body, name="swiglu_bwd", grid=(s // bm,),
        in_specs=[pl.BlockSpec((bm, f2), lambda i: (i, 0)), pl.BlockSpec((bm, f), lambda i: (i, 0))],
        out_specs=pl.BlockSpec((bm, f2), lambda i: (i, 0)),
        out_shape=jax.ShapeDtypeStruct((s, f2), BF16), compiler_params=_cp(("parallel",)),
    )(gu, da)


def _rope_tables(positions, rot, sign=1.0):
    half = rot // 2
    inv_freq = jnp.power(F32(ROPE_THETA), -jnp.arange(half, dtype=F32) * 2.0 / rot)
    ang = positions.reshape(-1).astype(F32)[:, None] * inv_freq
    cos, sin = jnp.cos(ang), jnp.sin(ang) * sign
    s = ang.shape[0]
    c_tab = jnp.concatenate([cos, cos, jnp.ones((s, LANES - rot), F32)], axis=1)
    s_tab = jnp.concatenate([-sin, sin, jnp.zeros((s, LANES - rot), F32)], axis=1)
    return c_tab, s_tab


def _rope(x, c_tab, s_tab, nblk, half, name):
    s = x.shape[0]
    bm = _tile(s, (256, 128, 8))
    hb = _tile(nblk, (12, 8, 6, 4, 3, 2))

    def body(x_ref, c_ref, s_ref, o_ref):
        lane = lax.broadcasted_iota(jnp.int32, (bm, LANES), 1)
        for h in range(hb):
            sl = slice(h * LANES, (h + 1) * LANES)
            xv = x_ref[:, sl].astype(F32)
            sw = jnp.where(lane < half, pltpu.roll(xv, LANES - half, 1), pltpu.roll(xv, half, 1))
            o_ref[:, sl] = (xv * c_ref[...] + sw * s_ref[...]).astype(o_ref.dtype)

    blk = pl.BlockSpec((bm, hb * LANES), lambda i, j: (i, j))
    tab = pl.BlockSpec((bm, LANES), lambda i, j: (i, 0))
    return pl.pallas_call(
        body, name=name, grid=(s // bm, nblk // hb), in_specs=[blk, tab, tab], out_specs=blk,
        out_shape=jax.ShapeDtypeStruct((s, nblk * LANES), BF16), compiler_params=_cp(("parallel", "parallel")),
    )(x, c_tab, s_tab)


def _sb_terms(q, kb, scale):
    z = lax.dot_general(q, kb, NT, preferred_element_type=F32) * scale
    u = jnp.log(1.0 + jnp.exp(-jnp.abs(z)))
    return jnp.minimum(-z, 0.0) - u, jnp.minimum(z, 0.0) - u


def _sb_heads_per_step(n_heads):
    return 2 if n_heads % 2 == 0 else 1


def _sb_fwd(qkv, n_heads, ex=None):
    s = qkv.shape[0]
    tq = _tile(s, (256, 128))
    hp = _sb_heads_per_step(n_heads)
    w = hp * LANES
    ng = n_heads // hp
    scale = LANES ** -0.5
    jobs = ex.take(4 * 4 * n_heads * s * s * LANES * EXCHANGE_BYTES_PER_FLOP) if ex is not None else []
    nj = len(jobs)

    def body(*refs):
        q_ref, k_ref, v_ref = refs[:3]
        o_ref, lt_ref = refs[3 + nj:5 + nj]
        i = pl.program_id(1)
        if nj:
            job_refs = (refs[3:3 + nj], refs[5 + nj:5 + 2 * nj], refs[5 + 2 * nj:])
            gi = pl.program_id(0)
            pl.when((gi == 0) & (i == 0))(lambda: ex.start(jobs, *job_refs))
            pl.when((gi == ng - 1) & (i == s // tq - 1))(lambda: ex.wait(jobs, *job_refs))
        row = lax.broadcasted_iota(jnp.int32, (tq, tq), 0)
        col = lax.broadcasted_iota(jnp.int32, (tq, tq), 1)
        below = col < row
        later = (row > col).astype(BF16)
        qs = [q_ref[:, h * LANES:(h + 1) * LANES] for h in range(hp)]

        def block(h, k0, c, acc, diag):
            sl = slice(h * LANES, (h + 1) * LANES)
            lk, logsig = _sb_terms(qs[h], k_ref[pl.ds(k0, tq), sl], scale)
            if diag:
                lk = jnp.where(below, lk, 0.0)
            a = jnp.exp(logsig + jnp.dot(lk.astype(BF16), later, preferred_element_type=F32) + c)
            if diag:
                a = jnp.where(below, a, 0.0)
            acc = acc + jnp.dot(a.astype(BF16), v_ref[pl.ds(k0, tq), sl], preferred_element_type=F32)
            return c + jnp.sum(lk, axis=1, keepdims=True), acc

        d0 = pl.multiple_of(i * tq, tq)
        carry = []
        for h in range(hp):
            carry += list(block(h, d0, jnp.zeros((tq, 1), F32), jnp.zeros((tq, LANES), F32), True))

        def step(t, carry):
            k0 = pl.multiple_of((i - 1 - t) * tq, tq)
            out = []
            for h in range(hp):
                out += list(block(h, k0, carry[2 * h], carry[2 * h + 1], False))
            return tuple(out)

        carry = lax.fori_loop(0, i, step, tuple(carry))
        for h in range(hp):
            sl = slice(h * LANES, (h + 1) * LANES)
            o_ref[:, sl] = carry[2 * h + 1].astype(o_ref.dtype)
            lt_ref[:, sl] = jnp.broadcast_to(carry[2 * h], (tq, LANES))

    blk = pl.BlockSpec((tq, w), lambda g, i: (i, g))
    outs = pl.pallas_call(
        body, name="sb_fwd_carry" if nj else "sb_fwd", grid=(ng, s // tq),
        in_specs=[blk, pl.BlockSpec((s, w), lambda g, i: (0, ng + g)), pl.BlockSpec((s, w), lambda g, i: (0, 2 * ng + g))]
        + [ANY] * nj,
        out_specs=[blk, blk] + [ANY] * nj,
        out_shape=[jax.ShapeDtypeStruct((s, n_heads * LANES), BF16), jax.ShapeDtypeStruct((s, n_heads * LANES), F32)]
        + [j.recv_shape for j in jobs],
        scratch_shapes=_piece_sems(nj) if nj else [],
        compiler_params=_cp(("arbitrary", "arbitrary")),
    )(qkv, qkv, qkv, *[j.g for j in jobs])
    if nj:
        ex.landed(jobs, outs[2:])
    return outs[:2]


def _sb_bwd(qkv, do, ltot, n_heads, ex=None):
    s = qkv.shape[0]
    tq = _tile(s, (256, 128))
    hp = _sb_heads_per_step(n_heads)
    w = hp * LANES
    ng = n_heads // hp
    scale = LANES ** -0.5
    jobs = ex.take(4 * 9 * n_heads * s * s * LANES * EXCHANGE_BYTES_PER_FLOP) if ex is not None else []
    nj = len(jobs)

    def body(*refs):
        q_ref, k_ref, v_ref, do_ref, lt_ref = refs[:5]
        dq_ref, dk_ref, dv_ref = refs[5 + nj:8 + nj]
        i = pl.program_id(1)
        if nj:
            job_refs = (refs[5:5 + nj], refs[8 + nj:8 + 2 * nj], refs[8 + 2 * nj:])
            gi = pl.program_id(0)
            pl.when((gi == 0) & (i == 0))(lambda: ex.start(jobs, *job_refs))
            pl.when((gi == ng - 1) & (i == s // tq - 1))(lambda: ex.wait(jobs, *job_refs))

        @pl.when(i == 0)
        def _():
            dk_ref[...] = jnp.zeros_like(dk_ref)
            dv_ref[...] = jnp.zeros_like(dv_ref)

        row = lax.broadcasted_iota(jnp.int32, (tq, tq), 0)
        col = lax.broadcasted_iota(jnp.int32, (tq, tq), 1)
        below = col < row
        later = (row > col).astype(BF16)
        before = (row < col).astype(BF16)
        qs = [q_ref[:, h * LANES:(h + 1) * LANES] for h in range(hp)]
        dos = [do_ref[:, h * LANES:(h + 1) * LANES] for h in range(hp)]
        lts = [lt_ref[:, h * LANES:h * LANES + 1] for h in range(hp)]

        def block(h, k0, cpre, ce, dq, diag):
            sl = slice(h * LANES, (h + 1) * LANES)
            kb = k_ref[pl.ds(k0, tq), sl]
            vb = v_ref[pl.ds(k0, tq), sl]
            lk, logsig = _sb_terms(qs[h], kb, scale)
            if diag:
                lk = jnp.where(below, lk, 0.0)
            cnext = cpre + jnp.sum(lk, axis=1, keepdims=True)
            a = jnp.exp(logsig + (lts[h] - cnext) + jnp.dot(lk.astype(BF16), later, preferred_element_type=F32))
            if diag:
                a = jnp.where(below, a, 0.0)
            e = a * lax.dot_general(dos[h], vb, NT, preferred_element_type=F32)
            e_before = ce + jnp.dot(e.astype(BF16), before, preferred_element_type=F32)
            sig = jnp.exp(logsig)
            dz = (e - sig * (e + e_before)) * scale
            if diag:
                dz = jnp.where(below, dz, 0.0)
            dzb = dz.astype(BF16)
            dq = dq + jnp.dot(dzb, kb, preferred_element_type=F32)
            dk_ref[pl.ds(k0, tq), sl] += lax.dot_general(dzb, qs[h], TN, preferred_element_type=F32)
            dv_ref[pl.ds(k0, tq), sl] += lax.dot_general(a.astype(BF16), dos[h], TN, preferred_element_type=F32)
            return cnext, ce + jnp.sum(e, axis=1, keepdims=True), dq

        def step(j, carry):
            k0 = pl.multiple_of(j * tq, tq)
            out = []
            for h in range(hp):
                out += list(block(h, k0, *carry[3 * h:3 * h + 3], False))
            return tuple(out)

        z1 = jnp.zeros((tq, 1), F32)
        carry = lax.fori_loop(0, i, step, (z1, z1, jnp.zeros((tq, LANES), F32)) * hp)
        d0 = pl.multiple_of(i * tq, tq)
        for h in range(hp):
            _, _, dq = block(h, d0, *carry[3 * h:3 * h + 3], True)
            dq_ref[:, h * LANES:(h + 1) * LANES] = dq.astype(dq_ref.dtype)

    blk = pl.BlockSpec((tq, w), lambda g, i: (i, g))
    full = pl.BlockSpec((s, w), lambda g, i: (0, g))
    wt = n_heads * LANES
    outs = pl.pallas_call(
        body, name="sb_bwd_carry" if nj else "sb_bwd", grid=(ng, s // tq),
        in_specs=[blk, pl.BlockSpec((s, w), lambda g, i: (0, ng + g)), pl.BlockSpec((s, w), lambda g, i: (0, 2 * ng + g)),
                  blk, blk] + [ANY] * nj,
        out_specs=[blk, full, full] + [ANY] * nj,
        out_shape=[jax.ShapeDtypeStruct((s, wt), BF16), jax.ShapeDtypeStruct((s, wt), F32),
                   jax.ShapeDtypeStruct((s, wt), F32)] + [j.recv_shape for j in jobs],
        scratch_shapes=_piece_sems(nj) if nj else [],
        compiler_params=_cp(("arbitrary", "arbitrary")),
    )(qkv, qkv, qkv, do, ltot, *[j.g for j in jobs])
    if nj:
        ex.landed(jobs, outs[3:])
    return outs[:3]


def _stack_heads(ref, n):
    return jnp.concatenate([ref[:, h * LANES:(h + 1) * LANES] for h in range(n)], axis=0)


def _stack_head_scalars(ref, n):
    return jnp.concatenate([jnp.broadcast_to(ref[:, h * LANES:h * LANES + 1], (BAND, 1)) for h in range(n)], axis=0)


def _band_masks(b, max_dist):
    qi = lax.broadcasted_iota(jnp.int32, (BAND, BAND), 0)
    kj = lax.broadcasted_iota(jnp.int32, (BAND, BAND), 1)
    dist = qi - kj
    return ((BAND + dist) <= max_dist) & (b > 0), (dist >= 0) & (dist <= max_dist)


def _band_fwd(qa, ka, va, *, n_cls, n_steps, hpb, group, q_blk, k_blk, v_blk, max_dist, scale, sinks, name):
    length = qa.shape[0]
    nb = length // BAND
    has_sink = sinks is not None
    qw, kw = hpb * LANES, (hpb // group) * LANES
    stacked = group > 1 and group == hpb

    def body(*refs):
        q_ref, kp_ref, kc_ref, vp_ref, vc_ref = refs[:5]
        o_ref, lse_ref = refs[-2], refs[-1]
        mask_p, mask_c = _band_masks(pl.program_id(2), max_dist)
        if stacked:
            q = _stack_heads(q_ref, hpb).astype(BF16)
            mask_p, mask_c = jnp.tile(mask_p, (hpb, 1)), jnp.tile(mask_c, (hpb, 1))
            s_p = lax.dot_general(q, kp_ref[...].astype(BF16), NT, preferred_element_type=F32) * scale
            s_c = lax.dot_general(q, kc_ref[...].astype(BF16), NT, preferred_element_type=F32) * scale
            s_p = jnp.where(mask_p, s_p, NEG)
            s_c = jnp.where(mask_c, s_c, NEG)
            m = jnp.maximum(jnp.max(s_p, axis=1, keepdims=True), jnp.max(s_c, axis=1, keepdims=True))
            l = jnp.sum(jnp.exp(s_p - m), axis=1, keepdims=True) + jnp.sum(jnp.exp(s_c - m), axis=1, keepdims=True)
            lse = m + jnp.log(l)
            if has_sink:
                sk = _stack_head_scalars(refs[5], hpb)
                lse = jnp.maximum(lse, sk) + jnp.log(1.0 + jnp.exp(-jnp.abs(lse - sk)))
            p_p = jnp.exp(s_p - lse).astype(BF16)
            p_c = jnp.exp(s_c - lse).astype(BF16)
            o = (jnp.dot(p_p, vp_ref[...].astype(BF16), preferred_element_type=F32)
                 + jnp.dot(p_c, vc_ref[...].astype(BF16), preferred_element_type=F32))
            for hh in range(hpb):
                rows = slice(hh * BAND, (hh + 1) * BAND)
                o_ref[:, hh * LANES:(hh + 1) * LANES] = o[rows]
                lse_ref[:, hh * LANES:(hh + 1) * LANES] = jnp.broadcast_to(lse[rows], (BAND, LANES))
            return
        for hh in range(hpb):
            qs = slice(hh * LANES, (hh + 1) * LANES)
            ks = slice((hh // group) * LANES, (hh // group + 1) * LANES)
            q = q_ref[:, qs].astype(BF16)
            s_p = lax.dot_general(q, kp_ref[:, ks].astype(BF16), NT, preferred_element_type=F32) * scale
            s_c = lax.dot_general(q, kc_ref[:, ks].astype(BF16), NT, preferred_element_type=F32) * scale
            s_p = jnp.where(mask_p, s_p, NEG)
            s_c = jnp.where(mask_c, s_c, NEG)
            m = jnp.maximum(jnp.max(s_p, axis=1, keepdims=True), jnp.max(s_c, axis=1, keepdims=True))
            l = jnp.sum(jnp.exp(s_p - m), axis=1, keepdims=True) + jnp.sum(jnp.exp(s_c - m), axis=1, keepdims=True)
            lse = m + jnp.log(l)
            if has_sink:
                sk = refs[5][:, hh * LANES:hh * LANES + 1]
                lse = jnp.maximum(lse, sk) + jnp.log(1.0 + jnp.exp(-jnp.abs(lse - sk)))
            p_p = jnp.exp(s_p - lse).astype(BF16)
            p_c = jnp.exp(s_c - lse).astype(BF16)
            o_ref[:, qs] = (jnp.dot(p_p, vp_ref[:, ks].astype(BF16), preferred_element_type=F32)
                            + jnp.dot(p_c, vc_ref[:, ks].astype(BF16), preferred_element_type=F32))
            lse_ref[:, qs] = jnp.broadcast_to(lse, (BAND, LANES))

    def prev(b):
        return jnp.maximum(b - 1, 0)

    specs = [pl.BlockSpec((BAND, qw), lambda n, st, b: (b, q_blk(n, st))),
             pl.BlockSpec((BAND, kw), lambda n, st, b: (prev(b), k_blk(n, st))),
             pl.BlockSpec((BAND, kw), lambda n, st, b: (b, k_blk(n, st))),
             pl.BlockSpec((BAND, kw), lambda n, st, b: (prev(b), v_blk(n, st))),
             pl.BlockSpec((BAND, kw), lambda n, st, b: (b, v_blk(n, st)))]
    ins = [qa, ka, ka, va, va]
    if has_sink:
        ins.append(sinks)
        specs.append(pl.BlockSpec((1, qw), lambda n, st, b: (0, st)))
    out = pl.BlockSpec((BAND, qw), lambda n, st, b: (b, n * n_steps + st))
    w = n_cls * n_steps * qw
    return pl.pallas_call(
        body, name=name, grid=(n_cls, n_steps, nb), in_specs=specs, out_specs=[out, out],
        out_shape=[jax.ShapeDtypeStruct((length, w), F32), jax.ShapeDtypeStruct((length, w), F32)],
        compiler_params=_cp(("parallel", "parallel", "parallel")),
    )(*ins)


def _band_bwd(qa, ka, va, o, do, lse, dlse, *, n_cls, n_steps, hpb, group, q_blk, k_blk, v_blk, max_dist, scale, sinks,
              name):
    length = qa.shape[0]
    nb = length // BAND
    has_sink, has_dlse = sinks is not None, dlse is not None
    qw, kw = hpb * LANES, (hpb // group) * LANES
    stacked = group > 1 and group == hpb

    def body(*refs):
        q_ref, kp_ref, kc_ref, vp_ref, vc_ref, o_ref, do_ref, lse_ref = refs[:8]
        pos = 8
        dlse_ref = refs[pos] if has_dlse else None
        pos += has_dlse
        sink_ref = refs[pos] if has_sink else None
        pos += has_sink
        dq_ref, dkc_ref, dkp_ref, dvc_ref, dvp_ref = refs[pos:pos + 5]
        b = pl.program_id(2)
        mask_p, mask_c = _band_masks(b, max_dist)
        if has_sink:
            dsink_ref = refs[pos + 5]

            @pl.when(b == 0)
            def _():
                dsink_ref[...] = jnp.zeros_like(dsink_ref)

        if stacked:
            q = _stack_heads(q_ref, hpb).astype(BF16)
            kp, kc = kp_ref[...].astype(BF16), kc_ref[...].astype(BF16)
            vp, vc = vp_ref[...].astype(BF16), vc_ref[...].astype(BF16)
            dov = _stack_heads(do_ref, hpb).astype(F32)
            dob = dov.astype(BF16)
            lse_v = _stack_head_scalars(lse_ref, hpb)
            mask_p, mask_c = jnp.tile(mask_p, (hpb, 1)), jnp.tile(mask_c, (hpb, 1))
            s_p = lax.dot_general(q, kp, NT, preferred_element_type=F32) * scale
            s_c = lax.dot_general(q, kc, NT, preferred_element_type=F32) * scale
            p_p = jnp.where(mask_p, jnp.exp(jnp.where(mask_p, s_p, NEG) - lse_v), 0.0)
            p_c = jnp.where(mask_c, jnp.exp(jnp.where(mask_c, s_c, NEG) - lse_v), 0.0)
            delta = jnp.sum(dov * _stack_heads(o_ref, hpb), axis=1, keepdims=True)
            shift = -delta
            if has_dlse:
                shift = shift + _stack_head_scalars(dlse_ref, hpb)
            ds_p = (p_p * (lax.dot_general(dob, vp, NT, preferred_element_type=F32) + shift) * scale).astype(BF16)
            ds_c = (p_c * (lax.dot_general(dob, vc, NT, preferred_element_type=F32) + shift) * scale).astype(BF16)
            dq = jnp.dot(ds_p, kp, preferred_element_type=F32) + jnp.dot(ds_c, kc, preferred_element_type=F32)
            for hh in range(hpb):
                dq_ref[:, hh * LANES:(hh + 1) * LANES] = dq[hh * BAND:(hh + 1) * BAND]
            dkc_ref[...] = lax.dot_general(ds_c, q, TN, preferred_element_type=F32)
            dkp_ref[...] = lax.dot_general(ds_p, q, TN, preferred_element_type=F32)
            dvc_ref[...] = lax.dot_general(p_c.astype(BF16), dob, TN, preferred_element_type=F32)
            dvp_ref[...] = lax.dot_general(p_p.astype(BF16), dob, TN, preferred_element_type=F32)
            if has_sink:
                lost = -jnp.exp(_stack_head_scalars(sink_ref, hpb) - lse_v) * delta
                for hh in range(hpb):
                    dsink_ref[:, hh * LANES:(hh + 1) * LANES] += jnp.broadcast_to(
                        jnp.sum(lost[hh * BAND:(hh + 1) * BAND], axis=0, keepdims=True), (1, LANES))
            return

        for hh in range(hpb):
            qs = slice(hh * LANES, (hh + 1) * LANES)
            ks = slice((hh // group) * LANES, (hh // group + 1) * LANES)
            q = q_ref[:, qs].astype(BF16)
            kp, kc = kp_ref[:, ks].astype(BF16), kc_ref[:, ks].astype(BF16)
            vp, vc = vp_ref[:, ks].astype(BF16), vc_ref[:, ks].astype(BF16)
            dov = do_ref[:, qs].astype(F32)
            dob = dov.astype(BF16)
            lse_v = lse_ref[:, hh * LANES:hh * LANES + 1]
            s_p = lax.dot_general(q, kp, NT, preferred_element_type=F32) * scale
            s_c = lax.dot_general(q, kc, NT, preferred_element_type=F32) * scale
            p_p = jnp.where(mask_p, jnp.exp(jnp.where(mask_p, s_p, NEG) - lse_v), 0.0)
            p_c = jnp.where(mask_c, jnp.exp(jnp.where(mask_c, s_c, NEG) - lse_v), 0.0)
            delta = jnp.sum(dov * o_ref[:, qs], axis=1, keepdims=True)
            shift = -delta
            if has_dlse:
                shift = shift + dlse_ref[:, hh * LANES:hh * LANES + 1]
            dp_p = lax.dot_general(dob, vp, NT, preferred_element_type=F32)
            dp_c = lax.dot_general(dob, vc, NT, preferred_element_type=F32)
            ds_p = (p_p * (dp_p + shift) * scale).astype(BF16)
            ds_c = (p_c * (dp_c + shift) * scale).astype(BF16)
            dq_ref[:, qs] = (jnp.dot(ds_p, kp, preferred_element_type=F32)
                             + jnp.dot(ds_c, kc, preferred_element_type=F32))
            parts = (lax.dot_general(ds_c, q, TN, preferred_element_type=F32),
                     lax.dot_general(ds_p, q, TN, preferred_element_type=F32),
                     lax.dot_general(p_c.astype(BF16), dob, TN, preferred_element_type=F32),
                     lax.dot_general(p_p.astype(BF16), dob, TN, preferred_element_type=F32))
            for ref, part in zip((dkc_ref, dkp_ref, dvc_ref, dvp_ref), parts):
                if hh % group == 0:
                    ref[:, ks] = part
                else:
                    ref[:, ks] += part
            if has_sink:
                p_sink = jnp.exp(sink_ref[:, hh * LANES:hh * LANES + 1] - lse_v)
                dsink_ref[:, qs] += jnp.broadcast_to(jnp.sum(-p_sink * delta, axis=0, keepdims=True), (1, LANES))

    def prev(b):
        return jnp.maximum(b - 1, 0)

    per_q = pl.BlockSpec((BAND, qw), lambda n, st, b: (b, n * n_steps + st))
    per_k = pl.BlockSpec((BAND, kw), lambda n, st, b: (b, n * n_steps + st))
    specs = [pl.BlockSpec((BAND, qw), lambda n, st, b: (b, q_blk(n, st))),
             pl.BlockSpec((BAND, kw), lambda n, st, b: (prev(b), k_blk(n, st))),
             pl.BlockSpec((BAND, kw), lambda n, st, b: (b, k_blk(n, st))),
             pl.BlockSpec((BAND, kw), lambda n, st, b: (prev(b), v_blk(n, st))),
             pl.BlockSpec((BAND, kw), lambda n, st, b: (b, v_blk(n, st))),
             per_q, per_q, per_q]
    ins = [qa, ka, ka, va, va, o, do, lse]
    if has_dlse:
        ins.append(dlse)
        specs.append(per_q)
    out_specs = [per_q] + [per_k] * 4
    out_shape = ([jax.ShapeDtypeStruct((length, n_cls * n_steps * qw), F32)]
                 + [jax.ShapeDtypeStruct((length, n_cls * n_steps * kw), F32)] * 4)
    if has_sink:
        ins.append(sinks)
        specs.append(pl.BlockSpec((1, qw), lambda n, st, b: (0, st)))
        out_specs = out_specs + [pl.BlockSpec((1, qw), lambda n, st, b: (0, st))]
        out_shape = out_shape + [jax.ShapeDtypeStruct((1, n_steps * qw), F32)]
    return pl.pallas_call(
        body, name=name, grid=(n_cls, n_steps, nb), in_specs=specs, out_specs=out_specs, out_shape=out_shape,
        compiler_params=_cp(("parallel", "parallel", "arbitrary")),
    )(*ins)


def _band_fold(cur, prv, *, n_cls, n_heads, group, name):
    length = cur.shape[0]
    nb = length // BAND
    n_kv = n_heads // group

    def body(c_ref, p_ref, o_ref):
        b, gq = pl.program_id(2), pl.program_id(3)

        @pl.when(gq == 0)
        def _():
            o_ref[...] = jnp.zeros_like(o_ref)

        o_ref[...] += c_ref[...] + jnp.where(b + 1 < nb, p_ref[...], 0.0)

    blk = (BAND, LANES)
    return pl.pallas_call(
        body, name=name, grid=(n_cls, n_kv, nb, group),
        in_specs=[pl.BlockSpec(blk, lambda n, h, b, gq: (b, n * n_heads + h * group + gq)),
                  pl.BlockSpec(blk, lambda n, h, b, gq: (jnp.minimum(b + 1, nb - 1), n * n_heads + h * group + gq))],
        out_specs=pl.BlockSpec(blk, lambda n, h, b, gq: (b, n * n_kv + h)),
        out_shape=jax.ShapeDtypeStruct((length, n_cls * n_kv * LANES), F32),
        compiler_params=_cp(("parallel", "parallel", "parallel", "arbitrary")),
    )(cur, prv)


def _dil_mix_fwd(os_, lses):
    s, w = os_[0].shape
    bm = _tile(s, (256, 128, 8))

    def body(o0, o1, o2, l0, l1, l2, out_ref):
        ls = [l0[...], l1[...], l2[...]]
        m = jnp.maximum(jnp.maximum(ls[0], ls[1]), ls[2])
        es = [jnp.exp(v - m) for v in ls]
        inv = 1.0 / (es[0] + es[1] + es[2])
        for gi, o_ref in enumerate((o0, o1, o2)):
            out_ref[:, gi * w:(gi + 1) * w] = (o_ref[...] * (es[gi] * inv)).astype(out_ref.dtype)

    blk = pl.BlockSpec((bm, w), lambda i: (i, 0))
    return pl.pallas_call(
        body, name="dil_mix_fwd", grid=(s // bm,), in_specs=[blk] * 6,
        out_specs=pl.BlockSpec((bm, 3 * w), lambda i: (i, 0)),
        out_shape=jax.ShapeDtypeStruct((s, 3 * w), BF16), compiler_params=_cp(("parallel",)),
    )(*os_, *lses)


def _dil_mix_bwd(os_, lses, dmixed):
    s, w = os_[0].shape
    bm = _tile(s, (256, 128, 8))
    hg = w // LANES

    def body(o0, o1, o2, l0, l1, l2, dm_ref, do0, do1, do2, dl0, dl1, dl2):
        ls = [l0[...], l1[...], l2[...]]
        m = jnp.maximum(jnp.maximum(ls[0], ls[1]), ls[2])
        es = [jnp.exp(v - m) for v in ls]
        inv = 1.0 / (es[0] + es[1] + es[2])
        alphas = [e * inv for e in es]
        dalphas = []
        for gi, (o_ref, do_ref) in enumerate(((o0, do0), (o1, do1), (o2, do2))):
            dm = dm_ref[:, gi * w:(gi + 1) * w].astype(F32)
            do_ref[...] = (dm * alphas[gi]).astype(do_ref.dtype)
            prod = dm * o_ref[...]
            parts = [jnp.broadcast_to(jnp.sum(prod[:, j * LANES:(j + 1) * LANES], axis=1, keepdims=True), (bm, LANES))
                     for j in range(hg)]
            dalphas.append(jnp.concatenate(parts, axis=1) if hg > 1 else parts[0])
        mean = alphas[0] * dalphas[0] + alphas[1] * dalphas[1] + alphas[2] * dalphas[2]
        for gi, dl_ref in enumerate((dl0, dl1, dl2)):
            dl_ref[...] = alphas[gi] * (dalphas[gi] - mean)

    blk = pl.BlockSpec((bm, w), lambda i: (i, 0))
    return pl.pallas_call(
        body, name="dil_mix_bwd", grid=(s // bm,), in_specs=[blk] * 6 + [pl.BlockSpec((bm, 3 * w), lambda i: (i, 0))],
        out_specs=[blk] * 6,
        out_shape=[jax.ShapeDtypeStruct((s, w), BF16)] * 3 + [jax.ShapeDtypeStruct((s, w), F32)] * 3,
        compiler_params=_cp(("parallel",)),
    )(*os_, *lses, dmixed)


def _xattn_fwd(q, kv):
    s, w = q.shape
    mlen = kv.shape[0]
    tq = _tile(s, (512, 256, 128))
    scale = LANES ** -0.5

    def body(q_ref, k_ref, v_ref, o_ref):
        for h in range(XA_HEADS):
            sl = slice(h * LANES, (h + 1) * LANES)
            sc = lax.dot_general(q_ref[:, sl], k_ref[:, sl], NT, preferred_element_type=F32) * scale
            m = jnp.max(sc, axis=1, keepdims=True)
            e = jnp.exp(sc - m)
            p = e / jnp.sum(e, axis=1, keepdims=True)
            o_ref[:, sl] = jnp.dot(p.astype(BF16), v_ref[:, sl], preferred_element_type=F32).astype(o_ref.dtype)

    return pl.pallas_call(
        body, name="xattn_fwd", grid=(s // tq,),
        in_specs=[pl.BlockSpec((tq, w), lambda i: (i, 0)), pl.BlockSpec((mlen, w), lambda i: (0, 0)),
                  pl.BlockSpec((mlen, w), lambda i: (0, 1))],
        out_specs=pl.BlockSpec((tq, w), lambda i: (i, 0)),
        out_shape=jax.ShapeDtypeStruct((s, w), BF16), compiler_params=_cp(("parallel",)),
    )(q, kv, kv)


def _xattn_bwd(q, kv, do):
    s, w = q.shape
    mlen = kv.shape[0]
    tq = _tile(s, (512, 256, 128))
    scale = LANES ** -0.5

    def body(q_ref, k_ref, v_ref, do_ref, dq_ref, dk_ref, dv_ref):
        @pl.when(pl.program_id(0) == 0)
        def _():
            dk_ref[...] = jnp.zeros_like(dk_ref)
            dv_ref[...] = jnp.zeros_like(dv_ref)

        for h in range(XA_HEADS):
            sl = slice(h * LANES, (h + 1) * LANES)
            qh, kh, vh, doh = q_ref[:, sl], k_ref[:, sl], v_ref[:, sl], do_ref[:, sl]
            sc = lax.dot_general(qh, kh, NT, preferred_element_type=F32) * scale
            m = jnp.max(sc, axis=1, keepdims=True)
            e = jnp.exp(sc - m)
            p = e / jnp.sum(e, axis=1, keepdims=True)
            dp = lax.dot_general(doh, vh, NT, preferred_element_type=F32)
            ds = (p * (dp - jnp.sum(p * dp, axis=1, keepdims=True)) * scale).astype(BF16)
            dq_ref[:, sl] = jnp.dot(ds, kh, preferred_element_type=F32).astype(dq_ref.dtype)
            dk_ref[:, sl] += lax.dot_general(ds, qh, TN, preferred_element_type=F32)
            dv_ref[:, sl] += lax.dot_general(p.astype(BF16), doh, TN, preferred_element_type=F32)

    row = pl.BlockSpec((tq, w), lambda i: (i, 0))
    acc = pl.BlockSpec((mlen, w), lambda i: (0, 0))
    return pl.pallas_call(
        body, name="xattn_bwd", grid=(s // tq,),
        in_specs=[row, acc, pl.BlockSpec((mlen, w), lambda i: (0, 1)), row],
        out_specs=[row, acc, acc],
        out_shape=[jax.ShapeDtypeStruct((s, w), BF16), jax.ShapeDtypeStruct((mlen, w), F32),
                   jax.ShapeDtypeStruct((mlen, w), F32)],
        compiler_params=_cp(("arbitrary",)),
    )(q, kv, kv, do)


def _adamw(parts, w, m, v, name, ex=None):
    r, c = w.shape
    npc = len(parts)
    rp = r // npc
    row_bytes = c * (2 * N_DEV * npc * parts[0].dtype.itemsize + 2 * 7 * 4)
    br = _tile(rp, tuple(p for p in (256, 128, 64, 32, 16, 8) if p * c * 4 <= 1024 * 1024 and p * row_bytes <= MM_VMEM_BUDGET))
    steps = rp // br
    jobs = ex.take(r * c * ADAMW_EXCHANGE_BYTES_PER_PARAM) if ex is not None else []
    nj = len(jobs)

    def body(*refs):
        w_ref, m_ref, v_ref = refs[npc:npc + 3]
        g_ref, d_ref, nm_ref, nv_ref = refs[npc + 3 + nj:npc + 7 + nj]
        if nj:
            job_refs = (refs[npc + 3:npc + 3 + nj], refs[npc + 7 + nj:npc + 7 + 2 * nj], refs[npc + 7 + 2 * nj:])
            p_id, i_id = pl.program_id(0), pl.program_id(1)
            pl.when((p_id == 0) & (i_id == 0))(lambda: ex.start(jobs, *job_refs))
            pl.when((p_id == npc - 1) & (i_id == steps - 1))(lambda: ex.wait(jobs, *job_refs))

        def update(p_ref):
            g = p_ref[0].astype(F32)
            for t in range(1, N_DEV):
                g = g + p_ref[t].astype(F32)
            nm = ADAM_B1 * m_ref[...] + (1.0 - ADAM_B1) * g
            nv = ADAM_B2 * v_ref[...] + (1.0 - ADAM_B2) * (g * g)
            m_hat = nm / (1.0 - ADAM_B1 ** ADAM_STEP)
            v_hat = nv / (1.0 - ADAM_B2 ** ADAM_STEP)
            g_ref[...] = g
            d_ref[...] = -ADAM_LR * (m_hat / (jnp.sqrt(v_hat) + ADAM_EPS) + ADAM_WD * w_ref[...])
            nm_ref[...] = nm
            nv_ref[...] = nv

        for k in range(npc):
            pl.when(pl.program_id(0) == k)(lambda k=k: update(refs[k]))

    blk = pl.BlockSpec((br, c), lambda p, i: (p * steps + i, 0))
    part_specs = [pl.BlockSpec((N_DEV, br, c), lambda p, i, k=k: (0, jnp.where(p == k, i, 0), 0)) for k in range(npc)]
    outs = pl.pallas_call(
        body, name=name + "_carry" if nj else name, grid=(npc, steps),
        in_specs=part_specs + [blk, blk, blk] + [ANY] * nj, out_specs=[blk] * 4 + [ANY] * nj,
        out_shape=[jax.ShapeDtypeStruct((r, c), F32)] * 4 + [j.recv_shape for j in jobs],
        scratch_shapes=_piece_sems(nj) if nj else [],
        compiler_params=_cp(("arbitrary", "arbitrary") if nj else ("parallel", "parallel")),
    )(*parts, w, m, v, *[j.g for j in jobs])
    if nj:
        ex.landed(jobs, outs[4:])
    return outs[:4]


MESH_ID = pl.DeviceIdType.MESH
ANY = pl.BlockSpec(memory_space=pl.ANY)


def _my_place():
    return lax.axis_index("x"), lax.axis_index("y"), lax.axis_index("c")


def _all_gather(xs, name):
    nt = len(xs)

    def body(*refs):
        x_refs, out_refs = refs[:nt], refs[nt:2 * nt]
        send_sems, recv_sems, local_sems = refs[2 * nt:]
        x, y, c = _my_place()
        me, sibling = (x, y, c), (x, y, 1 - c)
        chips = [(1 - x, y), (x, 1 - y), (1 - x, 1 - y)]

        def slot(t, p):
            return out_refs[t].at[4 * p[0] + 2 * p[1] + p[2]]

        def copy(t, k, block, to, src=None):
            return pltpu.make_async_remote_copy(
                src_ref=slot(t, block) if src is None else src, dst_ref=slot(t, block),
                send_sem=send_sems.at[7 * t + k], recv_sem=recv_sems.at[7 * t + k], device_id=to,
                device_id_type=MESH_ID)

        mine, first, passed = [], [], []
        for t in range(nt):
            cp = pltpu.make_async_copy(x_refs[t], slot(t, me), local_sems.at[t])
            cp.start()
            mine.append(cp)
            group = [copy(t, 0, me, sibling, src=x_refs[t])]
            group += [copy(t, 1 + j, me, (*chip, c), src=x_refs[t]) for j, chip in enumerate(chips)]
            for cp in group:
                cp.start()
            first += group
        for t in range(nt):
            for j, chip in enumerate(chips):
                copy(t, 1 + j, (*chip, c), me).wait_recv()
                fw = copy(t, 4 + j, (*chip, c), sibling)
                fw.start()
                passed.append(fw)
        for t in range(nt):
            copy(t, 0, sibling, me).wait_recv()
            for j, chip in enumerate(chips):
                copy(t, 4 + j, (*chip, 1 - c), me).wait_recv()
        for cp in first + passed:
            cp.wait_send()
        for cp in mine:
            cp.wait()

    return pl.pallas_call(
        body, name=name, in_specs=[ANY] * nt, out_specs=[ANY] * nt,
        out_shape=[jax.ShapeDtypeStruct((N_DEV,) + tuple(v.shape), v.dtype) for v in xs],
        scratch_shapes=[pltpu.SemaphoreType.DMA((7 * nt,)), pltpu.SemaphoreType.DMA((7 * nt,)),
                        pltpu.SemaphoreType.DMA((nt,))],
    )(*xs)


def _exchange(gs, name):
    nt = len(gs)

    def body(*refs):
        g_refs, out_refs = refs[:nt], refs[nt:2 * nt]
        send_sems, recv_sems, local_sems = refs[2 * nt:]
        x, y, c = _my_place()
        my_slot = 4 * x + 2 * y + c
        mine, sent = [], []
        for t in range(nt):
            cp = pltpu.make_async_copy(g_refs[t].at[my_slot], out_refs[t].at[my_slot], local_sems.at[t])
            cp.start()
            mine.append(cp)
            for rel in range(1, N_DEV):
                px, py, pc = x ^ ((rel >> 2) & 1), y ^ ((rel >> 1) & 1), c ^ (rel & 1)
                cp = pltpu.make_async_remote_copy(
                    src_ref=g_refs[t].at[4 * px + 2 * py + pc], dst_ref=out_refs[t].at[my_slot],
                    send_sem=send_sems.at[7 * t + rel - 1], recv_sem=recv_sems.at[7 * t + rel - 1],
                    device_id=(px, py, pc), device_id_type=MESH_ID)
                cp.start()
                sent.append(cp)
        for cp in sent:
            cp.wait_recv()
        for cp in sent:
            cp.wait_send()
        for cp in mine:
            cp.wait()

    return pl.pallas_call(
        body, name=name, in_specs=[ANY] * nt, out_specs=[ANY] * nt,
        out_shape=[jax.ShapeDtypeStruct(tuple(v.shape), v.dtype) for v in gs],
        scratch_shapes=[pltpu.SemaphoreType.DMA((7 * nt,)), pltpu.SemaphoreType.DMA((7 * nt,)),
                        pltpu.SemaphoreType.DMA((nt,))],
    )(*gs)


EXCHANGE_BYTES_PER_FLOP = 1.1e-4
CARRIER_OVERFILL = 1.15
ADAMW_EXCHANGE_BYTES_PER_PARAM = 2.0
PIECE_BYTES = 8 * 1024 * 1024
CARRIER_MIN_BYTES = 6 * 1024 * 1024
ROW_ALIGN = 16
BLOCKS = -1


class _Piece:
    def __init__(self, key, g, axis, lo, hi):
        self.key, self.g, self.axis, self.lo, self.hi = key, g, axis, lo, hi
        cols = g.shape[2] if axis == BLOCKS else g.shape[1] if axis == 0 else g.shape[1] // N_DEV
        self.recv_shape = jax.ShapeDtypeStruct((N_DEV, hi - lo, cols), g.dtype)
        self.nbytes = N_DEV * (hi - lo) * cols * g.dtype.itemsize


def _piece_sems(nj):
    return [pltpu.SemaphoreType.DMA((7 * nj,)), pltpu.SemaphoreType.DMA((7 * nj,)), pltpu.SemaphoreType.DMA((nj,))]


def _piece_copies(jobs, g_refs, recv_refs, sems):
    send_sems, recv_sems, local_sems = sems
    x, y, c = _my_place()
    me = 4 * x + 2 * y + c
    local, remote = [], []
    for t, (job, g, r) in enumerate(zip(jobs, g_refs, recv_refs)):
        rows = job.hi - job.lo

        def block(slot, job=job, g=g, r=r, rows=rows):
            if job.axis == BLOCKS:
                return g.at[slot, pl.ds(job.lo, rows), :]
            if job.axis == 0:
                start = pl.multiple_of(slot * (g.shape[0] // N_DEV) + job.lo, ROW_ALIGN)
                return g.at[pl.ds(start, rows), :]
            cols = r.shape[2]
            return g.at[pl.ds(job.lo, rows), pl.ds(pl.multiple_of(slot * cols, LANES), cols)]

        local.append(pltpu.make_async_copy(block(me), r.at[me], local_sems.at[t]))
        for rel in range(1, N_DEV):
            px, py, pc = x ^ ((rel >> 2) & 1), y ^ ((rel >> 1) & 1), c ^ (rel & 1)
            remote.append(pltpu.make_async_remote_copy(
                src_ref=block(4 * px + 2 * py + pc), dst_ref=r.at[me], send_sem=send_sems.at[7 * t + rel - 1],
                recv_sem=recv_sems.at[7 * t + rel - 1], device_id=(px, py, pc), device_id_type=MESH_ID))
    return local, remote


def _pieces_start(jobs, g_refs, recv_refs, sems):
    local, remote = _piece_copies(jobs, g_refs, recv_refs, sems)
    for cp in local + remote:
        cp.start()


def _pieces_wait(jobs, g_refs, recv_refs, sems):
    local, remote = _piece_copies(jobs, g_refs, recv_refs, sems)
    for cp in remote:
        cp.wait_recv()
    for cp in remote:
        cp.wait_send()
    for cp in local:
        cp.wait()


def _exchange_pieces(jobs, name):
    nj = len(jobs)

    def body(*refs):
        job_refs = (refs[:nj], refs[nj:2 * nj], refs[2 * nj:])
        _pieces_start(jobs, *job_refs)
        _pieces_wait(jobs, *job_refs)

    return pl.pallas_call(
        body, name=name, in_specs=[ANY] * nj, out_specs=[ANY] * nj, out_shape=[j.recv_shape for j in jobs],
        scratch_shapes=_piece_sems(nj),
    )(*[j.g for j in jobs])


class _GradExchange:
    def __init__(self):
        self.queue, self.recv = [], {}

    def put(self, name, layer, g):
        axis = SHARD_AXIS[name] - 1
        rows = g.shape[0] // N_DEV if axis == 0 else g.shape[0]
        if g.dtype != BF16 or g.ndim != 2:
            return False
        if rows % ROW_ALIGN or (axis == 1 and (g.shape[1] // N_DEV) % LANES):
            g = _to_blocks(g, axis)
            axis, rows = BLOCKS, g.shape[1]
        n_split = max(1, round(g.size * g.dtype.itemsize / PIECE_BYTES))
        while rows % (n_split * ROW_ALIGN):
            n_split -= 1
        for k in range(n_split):
            self.queue.append(_Piece((name, layer, k), g, axis, k * rows // n_split, (k + 1) * rows // n_split))
        return True

    def take(self, capacity):
        jobs, used = [], 0
        while capacity >= CARRIER_MIN_BYTES and self.queue and used + self.queue[0].nbytes <= CARRIER_OVERFILL * capacity:
            used += self.queue[0].nbytes
            jobs.append(self.queue.pop(0))
        return jobs

    def landed(self, jobs, recvs):
        for j, r in zip(jobs, recvs):
            self.recv[j.key] = r

    def flush(self):
        if self.queue:
            jobs, self.queue = self.queue, []
            self.landed(jobs, _exchange_pieces(jobs, "exchange_rest"))

    def pieces_of(self, name):
        return [self.recv[k] for k in sorted(k for k in self.recv if k[0] == name)]

    start = staticmethod(_pieces_start)
    wait = staticmethod(_pieces_wait)


GATHER_SPEEDUP = 2.0
GATHER_FIRST_OVERFILL = 2.2


class _WeightPiece:
    def __init__(self, key, local, axis):
        self.key, self.g, self.axis = key, local, axis
        r, c = local.shape
        self.recv_shape = jax.ShapeDtypeStruct(
            (N_DEV, r, c) if axis == BLOCKS else (r * N_DEV, c) if axis == 0 else (r, c * N_DEV), local.dtype)
        self.nbytes = N_DEV * r * c * local.dtype.itemsize


def _gather_copies(jobs, x_refs, full_refs, sems):
    send_sems, recv_sems, local_sems = sems
    x, y, c = _my_place()
    me, sibling = (x, y, c), (x, y, 1 - c)
    chips = [(1 - x, y), (x, 1 - y), (1 - x, 1 - y)]
    plans = []
    for t, (job, xr, fr) in enumerate(zip(jobs, x_refs, full_refs)):
        def blk(p, job=job, xr=xr, fr=fr):
            slot = 4 * p[0] + 2 * p[1] + p[2]
            if job.axis == BLOCKS:
                return fr.at[slot]
            if job.axis == 0:
                return fr.at[pl.ds(pl.multiple_of(slot * xr.shape[0], ROW_ALIGN), xr.shape[0]), :]
            return fr.at[:, pl.ds(pl.multiple_of(slot * xr.shape[1], LANES), xr.shape[1])]

        def copy(k, block, to, src=None, t=t, blk=blk):
            return pltpu.make_async_remote_copy(
                src_ref=blk(block) if src is None else src, dst_ref=blk(block), send_sem=send_sems.at[7 * t + k],
                recv_sem=recv_sems.at[7 * t + k], device_id=to, device_id_type=MESH_ID)

        plans.append(dict(
            mine=pltpu.make_async_copy(xr, blk(me), local_sems.at[t]),
            first=[copy(0, me, sibling, src=xr)] + [copy(1 + j, me, (*chip, c), src=xr) for j, chip in enumerate(chips)],
            landed=[copy(1 + j, (*chip, c), me) for j, chip in enumerate(chips)],
            passed=[copy(4 + j, (*chip, c), sibling) for j, chip in enumerate(chips)],
            from_sibling=[copy(0, sibling, me)] + [copy(4 + j, (*chip, 1 - c), me) for j, chip in enumerate(chips)]))
    return plans


def _gather_start(jobs, x_refs, full_refs, sems):
    for plan in _gather_copies(jobs, x_refs, full_refs, sems):
        plan["mine"].start()
        for cp in plan["first"]:
            cp.start()


def _gather_wait(jobs, x_refs, full_refs, sems):
    plans = _gather_copies(jobs, x_refs, full_refs, sems)
    for plan in plans:
        for landed, passed in zip(plan["landed"], plan["passed"]):
            landed.wait_recv()
            passed.start()
    for plan in plans:
        for cp in plan["from_sibling"]:
            cp.wait_recv()
        for cp in plan["first"] + plan["passed"]:
            cp.wait_send()
        plan["mine"].wait()


def _gather_pieces(jobs, name):
    nj = len(jobs)

    def body(*refs):
        job_refs = (refs[:nj], refs[nj:2 * nj], refs[2 * nj:])
        _gather_start(jobs, *job_refs)
        _gather_wait(jobs, *job_refs)

    return pl.pallas_call(
        body, name=name, in_specs=[ANY] * nj, out_specs=[ANY] * nj, out_shape=[j.recv_shape for j in jobs],
        scratch_shapes=_piece_sems(nj),
    )(*[j.g for j in jobs])


class _WeightGather:
    def __init__(self):
        self.queue, self.full = [], {}

    def add(self, name, layer, local, in_place):
        self.queue.append(_WeightPiece((name, layer), local, SHARD_AXIS[name] - 1 if in_place else BLOCKS))

    def take(self, capacity):
        capacity *= GATHER_SPEEDUP
        jobs, used = [], 0
        if capacity >= CARRIER_MIN_BYTES and self.queue and self.queue[0].nbytes <= GATHER_FIRST_OVERFILL * capacity:
            jobs.append(self.queue.pop(0))
            used = jobs[0].nbytes
            while self.queue and used + self.queue[0].nbytes <= CARRIER_OVERFILL * capacity:
                used += self.queue[0].nbytes
                jobs.append(self.queue.pop(0))
        return jobs

    def landed(self, jobs, fulls):
        for j, f in zip(jobs, fulls):
            self.full[j.key] = _to_full(f, SHARD_AXIS[j.key[0]] - 1) if j.axis == BLOCKS else f

    def get(self, name, layer):
        if (name, layer) not in self.full:
            at = [j.key for j in self.queue].index((name, layer))
            jobs, self.queue = self.queue[:at + 1], self.queue[at + 1:]
            self.landed(jobs, _gather_pieces(jobs, "gather_now"))
        return self.full[(name, layer)]

    start = staticmethod(_gather_start)
    wait = staticmethod(_gather_wait)


def _val(w):
    return w() if callable(w) else w


def _ffn_fwd(x, g, w_gu, w_d, tag, wg=None):
    n = _rms_fwd(x, g)
    gu = _mm(n, _val(w_gu), out_dtype=BF16, name=f"{tag}_gu", ex=wg)
    a = _swiglu_fwd(gu)
    return _mm(a, _val(w_d), alpha=0.5, res=x, name=f"{tag}_down", ex=wg), (n, gu, a)


def _put(ex, name, layer, g):
    if ex is not None:
        ex.put(name, layer, g)


def _ffn_bwd(dy, x, g, w_gu, w_d, saved, tag, ex=None, which="ffn1", layer=0):
    n, gu, a = saved
    da = _mm(dy, w_d, tb=True, alpha=0.5, out_dtype=BF16, name=f"{tag}_da", ex=ex)
    d_wd = _mm(a, dy, ta=True, alpha=0.5, out_dtype=BF16, name=f"{tag}_dwd", ex=ex)
    _put(ex, f"{which}_w_down", layer, d_wd)
    dgu = _swiglu_bwd(gu, da)
    dn = _mm(dgu, w_gu, tb=True, name=f"{tag}_dn", ex=ex)
    d_wgu = _mm(n, dgu, ta=True, out_dtype=BF16, name=f"{tag}_dwgu", ex=ex)
    _put(ex, f"{which}_w_gate_up", layer, d_wgu)
    dx, dg = _rms_bwd(x, g, dn, dy)
    return dx, dg, d_wgu, d_wd


def _sb_mixer_fwd(h, w_qkv, w_o, x, wg=None):
    qkv = _mm(h, _val(w_qkv), out_dtype=BF16, name="sb_qkv", ex=wg)
    o, ltot = _sb_fwd(qkv, qkv.shape[1] // (3 * LANES), wg)
    return _mm(o, _val(w_o), res=x, name="sb_out", ex=wg), (qkv, o, ltot)


def _sb_mixer_bwd(dy, h, w_qkv, w_o, saved, ex=None, layer=0):
    qkv, o, ltot = saved
    n_heads = w_o.shape[0] // LANES
    do = _mm(dy, w_o, tb=True, out_dtype=BF16, name="sb_do", ex=ex)
    d_wo = _mm(o, dy, ta=True, out_dtype=BF16, name="sb_dwo", ex=ex)
    _put(ex, "sb_w_o", layer, d_wo)
    dq, dk, dv = _sb_bwd(qkv, do, ltot, n_heads, ex)
    dqkv = jnp.concatenate([dq, dk.astype(BF16), dv.astype(BF16)], axis=1)
    dh = _mm(dqkv, w_qkv, tb=True, name="sb_dh", ex=ex)
    d_wqkv = _mm(h, dqkv, ta=True, out_dtype=BF16, name="sb_dwqkv", ex=ex)
    _put(ex, "sb_w_qkv", layer, d_wqkv)
    return dh, d_wqkv, d_wo


def _dil_cols(gi):
    ng = len(DIL_PATTERNS)
    return dict(q_blk=lambda n, st: n * 2 * ng + gi, k_blk=lambda n, st: n * 2 * ng + ng + gi,
                v_blk=lambda n, st: n * 3 * ng + 2 * ng + gi)


def _dil_mixer_fwd(h, w_qkv, w_o, x, tabs, wg=None):
    s = h.shape[0]
    qkv = _mm(h, _val(w_qkv), name="dil_qkv", ex=wg)
    n_all = qkv.shape[1] // (3 * LANES)
    hg = n_all // len(DIL_PATTERNS)
    qk = _rope(qkv, tabs[0], tabs[1], 2 * n_all, 16, "dil_rope")
    os_, lses = [], []
    for gi, (window, dil) in enumerate(DIL_PATTERNS):
        o, lse = _band_fwd(qk.reshape(s // dil, -1), qk.reshape(s // dil, -1), qkv.reshape(s // dil, -1),
                           n_cls=dil, n_steps=1, hpb=hg, group=1, **_dil_cols(gi),
                           max_dist=window // dil, scale=LANES ** -0.5, sinks=None, name=f"dil_fwd{gi}")
        os_.append(o.reshape(s, hg * LANES))
        lses.append(lse.reshape(s, hg * LANES))
    mixed = _dil_mix_fwd(os_, lses)
    return _mm(mixed, _val(w_o), res=x, name="dil_out", ex=wg), (qkv, qk, os_, lses, mixed)


def _dil_mixer_bwd(dy, h, w_qkv, w_o, saved, tabs_bwd, ex=None):
    qkv, qk, os_, lses, mixed = saved
    s = h.shape[0]
    n_all = w_o.shape[0] // LANES
    hg = n_all // len(DIL_PATTERNS)
    dmixed = _mm(dy, w_o, tb=True, out_dtype=BF16, name="dil_dmix", ex=ex)
    d_wo = _mm(mixed, dy, ta=True, out_dtype=BF16, name="dil_dwo", ex=ex)
    _put(ex, "dil_w_o", 0, d_wo)
    mix_out = _dil_mix_bwd(os_, lses, dmixed)
    dos, dlses = mix_out[:3], mix_out[3:]
    dqs, dks, dvs = [], [], []
    for gi, (window, dil) in enumerate(DIL_PATTERNS):
        length = s // dil
        dq, dkc, dkp, dvc, dvp = _band_bwd(
            qk.reshape(length, -1), qk.reshape(length, -1), qkv.reshape(length, -1), os_[gi].reshape(length, -1),
            dos[gi].reshape(length, -1), lses[gi].reshape(length, -1), dlses[gi].reshape(length, -1),
            n_cls=dil, n_steps=1, hpb=hg, group=1, **_dil_cols(gi), max_dist=window // dil,
            scale=LANES ** -0.5, sinks=None, name=f"dil_bwd{gi}")
        dqs.append(dq.reshape(s, -1))
        dks.append(_band_fold(dkc, dkp, n_cls=dil, n_heads=hg, group=1, name=f"dil_foldk{gi}").reshape(s, -1))
        dvs.append(_band_fold(dvc, dvp, n_cls=dil, n_heads=hg, group=1, name=f"dil_foldv{gi}").reshape(s, -1))
    dqk_rot = jnp.concatenate(dqs + dks, axis=1)
    dqk = _rope(dqk_rot, tabs_bwd[0], tabs_bwd[1], 2 * n_all, 16, "dil_rope_bwd")
    dqkv = jnp.concatenate([dqk] + [t.astype(BF16) for t in dvs], axis=1)
    dh = _mm(dqkv, w_qkv, tb=True, name="dil_dh", ex=ex)
    d_wqkv = _mm(h, dqkv, ta=True, out_dtype=BF16, name="dil_dwqkv", ex=ex)
    _put(ex, "dil_w_qkv", 0, d_wqkv)
    return dh, d_wqkv, d_wo


def _pad_heads(w, axis):
    shape = list(w.shape)
    n = shape[axis] // SWA_HEAD_DIM
    w = w.reshape(shape[:axis] + [n, SWA_HEAD_DIM] + shape[axis + 1:])
    pad = [(0, 0)] * w.ndim
    pad[axis + 1] = (0, LANES - SWA_HEAD_DIM)
    shape[axis] = n * LANES
    return jnp.pad(w, pad).reshape(shape)


def _unpad_heads(w, axis):
    shape = list(w.shape)
    n = shape[axis] // LANES
    w = w.reshape(shape[:axis] + [n, LANES] + shape[axis + 1:])
    w = lax.slice_in_dim(w, 0, SWA_HEAD_DIM, axis=axis + 1)
    shape[axis] = n * SWA_HEAD_DIM
    return w.reshape(shape)


def _swa_mixer_fwd(h, w_qkv_p, b_qkv_p, sinks_b, w_o_p, b_o, x, tabs, wg=None):
    nq = w_o_p.shape[0] // LANES
    nkv = nq // SWA_GROUP
    qkv = _mm(h, w_qkv_p, bias=b_qkv_p, name="swa_qkv", ex=wg)
    qk = _rope(qkv, tabs[0], tabs[1], nq + nkv, 8, "swa_rope")
    o, lse = _band_fwd(qk, qk, qkv, n_cls=1, n_steps=nkv, hpb=SWA_GROUP, group=SWA_GROUP, q_blk=lambda n, st: st,
                       k_blk=lambda n, st: nq + st, v_blk=lambda n, st: nq + nkv + st,
                       max_dist=SWA_WINDOW - 1, scale=SWA_HEAD_DIM ** -0.5, sinks=sinks_b, name="swa_fwd")
    return _mm(o, w_o_p, res=x, bias=b_o, name="swa_out"), (qkv, qk, o, lse)


def _swa_mixer_bwd(dy, h, w_qkv_p, sinks_b, w_o_p, saved, tabs_bwd, ex=None):
    qkv, qk, o, lse = saved
    nq = w_o_p.shape[0] // LANES
    nkv = nq // SWA_GROUP
    do = _mm(dy, w_o_p, tb=True, name="swa_do", ex=ex)
    d_wo_p = _mm(o, dy, ta=True, out_dtype=BF16, name="swa_dwo", ex=ex)
    d_bo = _colsum(dy, "swa_dbo")
    dq, dkc, dkp, dvc, dvp, dsink = _band_bwd(
        qk, qk, qkv, o, do, lse, None, n_cls=1, n_steps=nkv, hpb=SWA_GROUP, group=SWA_GROUP, q_blk=lambda n, st: st,
        k_blk=lambda n, st: nq + st, v_blk=lambda n, st: nq + nkv + st, max_dist=SWA_WINDOW - 1,
        scale=SWA_HEAD_DIM ** -0.5, sinks=sinks_b, name="swa_bwd")
    dk = _band_fold(dkc, dkp, n_cls=1, n_heads=nkv, group=1, name="swa_foldk")
    dv = _band_fold(dvc, dvp, n_cls=1, n_heads=nkv, group=1, name="swa_foldv")
    dqk = _rope(jnp.concatenate([dq, dk], axis=1), tabs_bwd[0], tabs_bwd[1], nq + nkv, 8, "swa_rope_bwd")
    dqkv = jnp.concatenate([dqk, dv.astype(BF16)], axis=1)
    d_bqkv_p = _colsum(dqkv, "swa_dbqkv")
    dh = _mm(dqkv, w_qkv_p, tb=True, name="swa_dh", ex=ex)
    d_wqkv_p = _mm(h, dqkv, ta=True, out_dtype=BF16, name="swa_dwqkv", ex=ex)
    return dh, d_wqkv_p, d_bqkv_p, dsink, d_wo_p, d_bo


def _xattn_layer_fwd(x, mem, g_x, g_m, w_q, w_kv, w_o):
    hq = _rms_fwd(x, g_x, "rms_fwd")
    hm = _rms_fwd(mem, g_m, "rms_mem_fwd")
    q = _mm(hq, _val(w_q), out_dtype=BF16, name="xa_q")
    kv = _mm(hm, _val(w_kv), out_dtype=BF16, name="xa_kv")
    o = _xattn_fwd(q, kv)
    return _mm(o, _val(w_o), res=x, name="xa_out"), (hq, hm, q, kv, o)


def _xattn_layer_bwd(dy, x, mem, g_x, g_m, w_q, w_kv, w_o, saved):
    hq, hm, q, kv, o = saved
    do = _mm(dy, w_o, tb=True, out_dtype=BF16, name="xa_do")
    d_wo = _mm(o, dy, ta=True, out_dtype=BF16, name="xa_dwo")
    dq, dk, dv = _xattn_bwd(q, kv, do)
    dkv = jnp.concatenate([dk, dv], axis=1).astype(BF16)
    dhq = _mm(dq, w_q, tb=True, name="xa_dhq")
    d_wq = _mm(hq, dq, ta=True, out_dtype=BF16, name="xa_dwq")
    dhm = _mm(dkv, w_kv, tb=True, name="xa_dhm")
    d_wkv = _mm(hm, dkv, ta=True, out_dtype=BF16, name="xa_dwkv")
    dx, dg_x = _rms_bwd(x, g_x, dhq, dy)
    _, dg_m = _rms_bwd(mem, g_m, dhm, None, "rms_mem_bwd")
    return dx, dg_x, dg_m, d_wq, d_wkv, d_wo


def _to_full(gathered, axis):
    t = jnp.moveaxis(gathered, 0, axis)
    shape = list(t.shape)
    return t.reshape(shape[:axis] + [shape[axis] * shape[axis + 1]] + shape[axis + 2:])


def _to_blocks(full, axis):
    shape = list(full.shape)
    t = full.reshape(shape[:axis] + [N_DEV, shape[axis] // N_DEV] + shape[axis + 1:])
    return jnp.moveaxis(t, axis, 0)


SHARD_AXIS = {
    "ffn1_w_gate_up": 2, "ffn1_w_down": 1, "sb_w_qkv": 2, "sb_w_o": 1, "dil_w_qkv": 2, "dil_w_o": 2,
    "swa_w_qkv": 2, "swa_b_qkv": 1, "swa_w_o": 1, "swa_b_o": 1, "xattn_w_q": 1, "xattn_w_kv": 1, "xattn_w_o": 2,
    "ffn2_w_gate_up": 2, "ffn2_w_down": 1,
}
SMALL = ("ffn1_norm", "mix_norm", "xattn_norm", "mem_norm", "ffn2_norm", "final_norm", "swa_sinks")
WEIGHTS = ("ffn1_norm", "ffn1_w_gate_up", "ffn1_w_down", "mix_norm", "sb_w_qkv", "sb_w_o", "dil_w_qkv", "dil_w_o",
           "swa_w_qkv", "swa_b_qkv", "swa_sinks", "swa_w_o", "swa_b_o", "xattn_norm", "mem_norm", "xattn_w_q",
           "xattn_w_kv", "xattn_w_o", "ffn2_norm", "ffn2_w_gate_up", "ffn2_w_down", "final_norm")


def _flat2(a):
    return a.reshape(-1, a.shape[-1])


def _pack_small(vals, d):
    rows = [vals[n].reshape(-1, d) for n in SMALL[:5]] + [vals["final_norm"].reshape(1, d)]
    sk = vals["swa_sinks"].reshape(1, -1)
    rows.append(jnp.pad(sk, ((0, 0), (0, d - sk.shape[1]))))
    rows.append(jnp.zeros((2, d), F32))
    return jnp.concatenate(rows, axis=0)


def _unpack_small(packed, like):
    out, r = {}, 0
    for n in SMALL[:5]:
        k = like[n].shape[0]
        out[n] = packed[r:r + k]
        r += k
    out["final_norm"] = packed[r]
    out["swa_sinks"] = packed[r + 1:r + 2, :like["swa_sinks"].shape[1]]
    return out


def _local_step(x0, mem0, positions, target, full, norm, ex=None, wg=None):
    d = x0.shape[1]
    names = list(SHARD_AXIS)
    sinks_b = jnp.repeat(norm["swa_sinks"], LANES, axis=1)
    tabs_dil, tabs_dil_bwd = _rope_tables(positions, 32), _rope_tables(positions, 32, -1.0)
    tabs_swa, tabs_swa_bwd = _rope_tables(positions, 16), _rope_tables(positions, 16, -1.0)

    def vec(name, i):
        return norm[name][i:i + 1]

    saved = []
    xc = x0
    for i in range(DEPTH):
        kind, j = i % 3, i // 3
        rec = {"x0": xc}
        xc, rec["ffn1"] = _ffn_fwd(xc, vec("ffn1_norm", i), lambda: full["ffn1_w_gate_up"][i],
                                   lambda: full["ffn1_w_down"][i], "ffn", wg)
        rec["x1"] = xc
        h = _rms_fwd(xc, vec("mix_norm", i))
        rec["h"] = h
        if kind == 0:
            xc, rec["mix"] = _sb_mixer_fwd(h, lambda: full["sb_w_qkv"][j], lambda: full["sb_w_o"][j], xc, wg)
        elif kind == 1:
            xc, rec["mix"] = _dil_mixer_fwd(h, lambda: full["dil_w_qkv"][j], lambda: full["dil_w_o"][j], xc, tabs_dil, wg)
        else:
            swa_w_qkv_p = _pad_heads(full["swa_w_qkv"][0], 1)
            swa_b_qkv_p = _pad_heads(full["swa_b_qkv"][0][None], 1)
            swa_w_o_p = _pad_heads(full["swa_w_o"][0], 0)
            xc, rec["mix"] = _swa_mixer_fwd(h, swa_w_qkv_p, swa_b_qkv_p, sinks_b, swa_w_o_p, full["swa_b_o"][0][None],
                                            xc, tabs_swa, wg)
        rec["x2"] = xc
        xc, rec["xa"] = _xattn_layer_fwd(xc, mem0, vec("xattn_norm", i), vec("mem_norm", i),
                                         lambda: full["xattn_w_q"][i], lambda: full["xattn_w_kv"][i],
                                         lambda: full["xattn_w_o"][i])
        rec["x3"] = xc
        xc, rec["ffn2"] = _ffn_fwd(xc, vec("ffn2_norm", i), lambda: full["ffn2_w_gate_up"][i],
                                   lambda: full["ffn2_w_down"][i], "ffn", wg)
        saved.append(rec)

    loss_part, dx, dg_final = _loss_head(xc, norm["final_norm"].reshape(1, d), target)

    gfull = {n: [None] * full[n].shape[0] for n in names}
    gsmall = {n: [None] * DEPTH for n in SMALL[:5]}
    gsmall["final_norm"] = dg_final
    gsmall["swa_sinks"] = jnp.zeros_like(norm["swa_sinks"])
    for i in reversed(range(DEPTH)):
        kind, j = i % 3, i // 3
        rec = saved[i]
        dx, gsmall["ffn2_norm"][i], gfull["ffn2_w_gate_up"][i], gfull["ffn2_w_down"][i] = _ffn_bwd(
            dx, rec["x3"], vec("ffn2_norm", i), full["ffn2_w_gate_up"][i], full["ffn2_w_down"][i], rec["ffn2"], "ffn",
            ex, "ffn2", i)
        (dx, gsmall["xattn_norm"][i], gsmall["mem_norm"][i], gfull["xattn_w_q"][i], gfull["xattn_w_kv"][i],
         gfull["xattn_w_o"][i]) = _xattn_layer_bwd(dx, rec["x2"], mem0, vec("xattn_norm", i), vec("mem_norm", i),
                                                   full["xattn_w_q"][i], full["xattn_w_kv"][i], full["xattn_w_o"][i],
                                                   rec["xa"])
        for n in ("xattn_w_q", "xattn_w_kv", "xattn_w_o"):
            _put(ex, n, i, gfull[n][i])
        if kind == 0:
            dh, gfull["sb_w_qkv"][j], gfull["sb_w_o"][j] = _sb_mixer_bwd(
                dx, rec["h"], full["sb_w_qkv"][j], full["sb_w_o"][j], rec["mix"], ex, j)
        elif kind == 1:
            dh, gfull["dil_w_qkv"][j], gfull["dil_w_o"][j] = _dil_mixer_bwd(
                dx, rec["h"], full["dil_w_qkv"][j], full["dil_w_o"][j], rec["mix"], tabs_dil_bwd, ex)
        else:
            dh, d_wqkv_p, d_bqkv_p, dsink, d_wo_p, d_bo = _swa_mixer_bwd(
                dx, rec["h"], swa_w_qkv_p, sinks_b, swa_w_o_p, rec["mix"], tabs_swa_bwd, ex)
            gfull["swa_w_qkv"][j] = _unpad_heads(d_wqkv_p, 1)
            gfull["swa_b_qkv"][j] = _unpad_heads(d_bqkv_p, 1)[0]
            gfull["swa_w_o"][j] = _unpad_heads(d_wo_p, 0)
            gfull["swa_b_o"][j] = d_bo[0]
            gsmall["swa_sinks"] = dsink.reshape(1, -1, LANES)[:, :, 0]
            _put(ex, "swa_w_o", j, gfull["swa_w_o"][j])
            _put(ex, "swa_w_qkv", j, gfull["swa_w_qkv"][j])
        dx, gsmall["mix_norm"][i] = _rms_bwd(rec["x1"], vec("mix_norm", i), dh, dx)
        dx, gsmall["ffn1_norm"][i], gfull["ffn1_w_gate_up"][i], gfull["ffn1_w_down"][i] = _ffn_bwd(
            dx, rec["x0"], vec("ffn1_norm", i), full["ffn1_w_gate_up"][i], full["ffn1_w_down"][i], rec["ffn1"], "ffn",
            ex, "ffn1", i)
    for n in SMALL[:5]:
        gsmall[n] = jnp.concatenate(gsmall[n], axis=0)
    return loss_part[0, 0], dx, gfull, gsmall


def _train_step(x, mem, positions, loss_target, w, m, v):
    d = x.shape[2]
    names = list(SHARD_AXIS)
    norm = {n: w[n] for n in SMALL}

    def in_place(n):
        return w[n].shape[2] % LANES == 0 if SHARD_AXIS[n] == 2 else w[n].shape[1] % ROW_ALIGN == 0

    first = [n for n in names if w[n].ndim != 3]
    gathered = _all_gather([_flat2(w[n]) for n in first], "gather_weights")
    stacked = {n: _to_full(g.reshape((N_DEV,) + w[n].shape), SHARD_AXIS[n]) for n, g in zip(first, gathered)}
    wg = _WeightGather()
    for i in range(DEPTH):
        mixer = (("sb_w_qkv", "sb_w_o"), ("dil_w_qkv", "dil_w_o"), ("swa_w_qkv", "swa_w_o"))[i % 3]
        for n in ("ffn1_w_gate_up", "ffn1_w_down") + mixer + ("xattn_w_q", "xattn_w_kv", "xattn_w_o",
                                                               "ffn2_w_gate_up", "ffn2_w_down"):
            layer = i // 3 if n in mixer else i
            wg.add(n, layer, w[n][layer].astype(BF16), in_place(n))

    class Layers:
        def __init__(self, n):
            self.n, self.shape = n, w[n].shape[:1]

        def __getitem__(self, layer):
            return wg.get(self.n, layer) if w[self.n].ndim == 3 else stacked[self.n][layer]

    full = {n: Layers(n) for n in names}
    ex = _GradExchange()
    loss_part, dx, gfull, gsmall = _local_step(x[0], mem[0], positions, loss_target[0], full, norm, ex, wg)
    loss = lax.psum(loss_part, MESH_AXES)
    grad_x = dx[None]

    grad, delta, new_m, new_v = {}, {}, {}, {}

    def update(n, parts, carrier):
        outs = _adamw(parts, _flat2(w[n]), _flat2(m[n]), _flat2(v[n]), "adamw", carrier)
        grad[n], delta[n], new_m[n], new_v[n] = (o.reshape(w[n].shape) for o in outs)

    taken = {k[0] for k in ex.recv} | {p.key[0] for p in ex.queue}
    late = {p.key[0] for p in ex.queue}
    for n in names:
        if n in taken and n not in late:
            update(n, ex.pieces_of(n), ex)
    ex.flush()
    for n in names:
        if n in late:
            update(n, ex.pieces_of(n), None)
    rest = [n for n in names if n not in taken]
    blocks = [_to_blocks(jnp.stack(gfull[n], axis=0), SHARD_AXIS[n]) for n in rest]
    blocks = [b.reshape(N_DEV, -1, b.shape[-1]) for b in blocks]
    for n, received in zip(rest, _exchange(blocks, "exchange_grads")):
        update(n, [received], None)

    small_parts = _all_gather([_pack_small(gsmall, d)], "gather_small_grads")[0]
    outs = _adamw([small_parts], _pack_small(norm, d), _pack_small({n: m[n] for n in SMALL}, d),
                  _pack_small({n: v[n] for n in SMALL}, d), "adamw_small")
    for res, o in zip((grad, delta, new_m, new_v), outs):
        res.update(_unpack_small(o, norm))
    return loss, grad_x, grad, delta, new_m, new_v


def kernel(x, mem, positions, ffn1_norm, ffn1_w_gate_up, ffn1_w_down, mix_norm, sb_w_qkv, sb_w_o, dil_w_qkv, dil_w_o, swa_w_qkv, swa_b_qkv, swa_sinks, swa_w_o, swa_b_o, xattn_norm, mem_norm, xattn_w_q, xattn_w_kv, xattn_w_o, ffn2_norm, ffn2_w_gate_up, ffn2_w_down, final_norm, loss_target, m_ffn1_norm, m_ffn1_w_gate_up, m_ffn1_w_down, m_mix_norm, m_sb_w_qkv, m_sb_w_o, m_dil_w_qkv, m_dil_w_o, m_swa_w_qkv, m_swa_b_qkv, m_swa_sinks, m_swa_w_o, m_swa_b_o, m_xattn_norm, m_mem_norm, m_xattn_w_q, m_xattn_w_kv, m_xattn_w_o, m_ffn2_norm, m_ffn2_w_gate_up, m_ffn2_w_down, m_final_norm, v_ffn1_norm, v_ffn1_w_gate_up, v_ffn1_w_down, v_mix_norm, v_sb_w_qkv, v_sb_w_o, v_dil_w_qkv, v_dil_w_o, v_swa_w_qkv, v_swa_b_qkv, v_swa_sinks, v_swa_w_o, v_swa_b_o, v_xattn_norm, v_mem_norm, v_xattn_w_q, v_xattn_w_kv, v_xattn_w_o, v_ffn2_norm, v_ffn2_w_gate_up, v_ffn2_w_down, v_final_norm):
    args = dict(locals())
    w = {n: args[n] for n in WEIGHTS}
    m = {n: args["m_" + n] for n in WEIGHTS}
    v = {n: args["v_" + n] for n in WEIGHTS}
    loss, grad_x, grad, delta, new_m, new_v = _train_step(x, mem, positions, loss_target, w, m, v)
    return (loss, grad_x, *[grad[n] for n in WEIGHTS], *[delta[n] for n in WEIGHTS],
            *[new_m[n] for n in WEIGHTS], *[new_v[n] for n in WEIGHTS])
```
